```python
import math
import jax
import jax.numpy as jnp
from jax import lax
import numpy as np

D_MODEL = 1024
BATCH = 4
SEQ = 4096
DEPTH = 2

HEAD_DIM = 64
A_HEADS = 6
A_WIDTH = A_HEADS * HEAD_DIM
B_Q_HEADS = 6
B_KV_HEADS = 2
B_WIDTH = B_Q_HEADS * HEAD_DIM
B_KV_WIDTH = B_KV_HEADS * HEAD_DIM
C_HEADS = 4
C_WIDTH = C_HEADS * HEAD_DIM
D_MIX = A_WIDTH + B_WIDTH + C_WIDTH
IN_SPLITS = (3 * A_WIDTH, A_WIDTH, 2 * A_HEADS, 2 * A_HEADS,
             B_WIDTH, B_KV_WIDTH, B_KV_WIDTH,
             2 * C_WIDTH, C_WIDTH, C_WIDTH, C_WIDTH)
D_IN = sum(IN_SPLITS)
CONV_K = 5
GDN_CHUNK = 64
HGRN_CHUNK = 64
Q_BLOCK = 128
GRID_W = 64
ROPE_AXIS_DIM = HEAD_DIM // 2
ROPE_THETA = 10000.0
MEM_TOKENS = 256
X_HEADS = 4
X_HEAD_DIM = D_MODEL // X_HEADS
D_FF = 2816
N_EXPERTS = 8
TOP_K = 2
D_FF_EXPERT = 3584
MOE_BLOCK = 128
N_DENSE = (DEPTH + 1) // 2
N_MOE = DEPTH // 2
DN_ALPHA = (2 * DEPTH) ** 0.25
DN_BETA = (8 * DEPTH) ** -0.25
LN_EPS = 1e-5
RMS_EPS = 1e-6

kernel_name = 'hybrid_parallel_heads_deepnorm_encoder'


def layer_norm(x, g, b):
    xf = x.astype(jnp.float32)
    xc = xf - jnp.mean(xf, axis=-1, keepdims=True)
    var = jnp.mean(xc * xc, axis=-1, keepdims=True)
    return (xc * lax.rsqrt(var + LN_EPS) * g + b).astype(x.dtype)


def rms_norm(x, w):
    xf = x.astype(jnp.float32)
    return (xf * lax.rsqrt(jnp.mean(xf * xf, axis=-1, keepdims=True) + RMS_EPS) * w).astype(x.dtype)


def l2_norm(x):
    xf = x.astype(jnp.float32)
    return xf * lax.rsqrt(jnp.sum(xf * xf, axis=-1, keepdims=True) + RMS_EPS)


def split_last(x, sizes):
    idx = [int(i) for i in np.cumsum(sizes)[:-1]]
    return jnp.split(x, idx, axis=-1)


def to_heads(t, n_heads):
    b, s, w = t.shape
    return t.reshape(b, s, n_heads, w // n_heads).transpose(0, 2, 1, 3)


def flip_seq(t):
    return jnp.flip(t, axis=2)


def centered_depthwise_conv(x, w):
    k, c = w.shape
    return lax.conv_general_dilated(
        x, w[:, None, :].astype(x.dtype), window_strides=(1,), padding=[(k // 2, k // 2)],
        dimension_numbers=('NWC', 'WIO', 'NWC'), feature_group_count=c)


def axial_rope_tables(seq_len):
    rows = seq_len // GRID_W
    row = jnp.repeat(jnp.arange(rows, dtype=jnp.float32), GRID_W)
    col = jnp.tile(jnp.arange(GRID_W, dtype=jnp.float32), rows)
    inv_freq = ROPE_THETA ** (-jnp.arange(0, ROPE_AXIS_DIM, 2, dtype=jnp.float32) / ROPE_AXIS_DIM)
    ang = jnp.stack([row[:, None] * inv_freq, col[:, None] * inv_freq], axis=1)
    return jnp.cos(ang), jnp.sin(ang)


def apply_axial_rope(x, cos, sin):
    xs = x.astype(jnp.float32).reshape(*x.shape[:-1], 2, 2, ROPE_AXIS_DIM // 2)
    x1, x2 = xs[..., 0, :], xs[..., 1, :]
    out = jnp.stack([x1 * cos - x2 * sin, x2 * cos + x1 * sin], axis=-2)
    return out.reshape(x.shape).astype(x.dtype)


def gated_delta_rule(q, k, v, g, beta):
    f32 = jnp.float32
    b, h, s, dk = q.shape
    dv = v.shape[-1]
    c = GDN_CHUNK
    n = s // c
    q = q.astype(f32).reshape(b, h, n, c, dk)
    k = k.astype(f32).reshape(b, h, n, c, dk)
    v = v.astype(f32).reshape(b, h, n, c, dv)
    g = jnp.cumsum(g.astype(f32).reshape(b, h, n, c), axis=-1)
    beta = beta.astype(f32).reshape(b, h, n, c, 1)
    pos = jnp.arange(c)
    incl = pos[:, None] >= pos[None, :]
    strict = pos[:, None] > pos[None, :]
    decay = jnp.exp(jnp.where(incl, g[..., :, None] - g[..., None, :], -jnp.inf))
    k_beta = k * beta
    lower = jnp.where(strict, jnp.einsum('bhnid,bhnjd->bhnij', k_beta, k) * decay, 0.0)
    lhs = lower + jnp.eye(c, dtype=f32)
    rhs = jnp.concatenate([v * beta, k_beta * jnp.exp(g)[..., None]], axis=-1)
    sol = lax.linalg.triangular_solve(lhs, rhs, left_side=True, lower=True)
    u, w = sol[..., :dv], sol[..., dv:]
    attn = jnp.einsum('bhnid,bhnjd->bhnij', q, k) * decay
    q_dec = q * jnp.exp(g)[..., None]
    k_dec = k * jnp.exp(g[..., -1:] - g)[..., None]
    chunk_decay = jnp.exp(g[..., -1])

    def step(state, xs):
        u_n, w_n, attn_n, qd_n, kd_n, cd_n = xs
        v_new = u_n - jnp.einsum('bhcd,bhde->bhce', w_n, state)
        out = (jnp.einsum('bhcd,bhde->bhce', qd_n, state)
               + jnp.einsum('bhij,bhje->bhie', attn_n, v_new))
        state = state * cd_n[..., None, None] + jnp.einsum('bhcd,bhce->bhde', kd_n, v_new)
        return state, out

    xs = tuple(jnp.moveaxis(t, 2, 0) for t in (u, w, attn, q_dec, k_dec, chunk_decay))
    _, out = lax.scan(step, jnp.zeros((b, h, dk, dv), f32), xs)
    return jnp.moveaxis(out, 0, 2).reshape(b, h, s, dv)


def hgrn2_chunked(q, k, v, log_f):
    f32 = jnp.float32
    b, h, s, dk = q.shape
    dv = v.shape[-1]
    c = HGRN_CHUNK
    n = s // c
    q = q.astype(f32).reshape(b, h, n, c, dk)
    k = k.astype(f32).reshape(b, h, n, c, dk)
    v = v.astype(f32).reshape(b, h, n, c, dv)
    cum = jnp.cumsum(log_f.astype(f32).reshape(b, h, n, c, dk), axis=-2)
    q_dec = q * jnp.exp(cum)
    k_dec = k * jnp.exp(cum[..., -1:, :] - cum)
    chunk_decay = jnp.exp(cum[..., -1, :])
    pos = jnp.arange(c)
    incl = (pos[:, None] >= pos[None, :])[:, :, None]

    def step(state, xs):
        q_n, k_n, v_n, cum_n, qd_n, kd_n, cd_n = xs
        rel = jnp.exp(jnp.where(incl, cum_n[..., :, None, :] - cum_n[..., None, :, :], -jnp.inf))
        a = jnp.einsum('bhid,bhjd,bhijd->bhij', q_n, k_n, rel)
        out = (jnp.einsum('bhid,bhde->bhie', qd_n, state)
               + jnp.einsum('bhij,bhje->bhie', a, v_n))
        state = state * cd_n[..., :, None] + jnp.einsum('bhjd,bhje->bhde', kd_n, v_n)
        return state, out

    xs = tuple(jnp.moveaxis(t, 2, 0) for t in (q, k, v, cum, q_dec, k_dec, chunk_decay))
    _, out = lax.scan(step, jnp.zeros((b, h, dk, dv), f32), xs)
    return jnp.moveaxis(out, 0, 2).reshape(b, h, s, dv)


def hgrn_lower_bounds(logits):
    p = jax.nn.softmax(logits.astype(jnp.float32), axis=0)
    cum = jnp.cumsum(p, axis=0)
    return cum - cum[0]


def blocked_bidirectional_gqa(q, k, v):
    b, hq, s, d = q.shape
    hkv = k.shape[1]
    grp = hq // hkv
    nb = s // Q_BLOCK
    qb = q.reshape(b, hkv, grp, nb, Q_BLOCK, d).transpose(3, 0, 1, 2, 4, 5)
    scale = d ** -0.5

    def one_block(qi):
        sc = jnp.einsum('bkgqd,bksd->bkgqs', qi, k).astype(jnp.float32) * scale
        p = jax.nn.softmax(sc, axis=-1).astype(v.dtype)
        return jnp.einsum('bkgqs,bksd->bkgqd', p, v)

    o = lax.map(one_block, qb)
    return o.transpose(1, 0, 4, 2, 3, 5).reshape(b, s, hq * d)


def hybrid_token_mixer(x, rope_cos, rope_sin, w_in, conv_w, a_log, dt_bias, gdn_norm_w,
                       q_norm_w, k_norm_w, lower_bound, hgrn_norm_w, w_out):
    f32 = jnp.float32
    b, s, _ = x.shape
    (a_qkv, a_z, a_beta, a_decay, b_q, b_k, b_v,
     c_f, c_i, c_q, c_g) = split_last(x @ w_in, IN_SPLITS)

    def dir_heads(t):
        return t.astype(f32).reshape(b, s, 2, A_HEADS).transpose(2, 0, 3, 1)

    qkv = jax.nn.silu(centered_depthwise_conv(a_qkv, conv_w))
    aq, ak, av = jnp.split(qkv, 3, axis=-1)
    aq = l2_norm(to_heads(aq, A_HEADS)) * (HEAD_DIM ** -0.5)
    ak = l2_norm(to_heads(ak, A_HEADS))
    av = to_heads(av, A_HEADS)
    beta = jax.nn.sigmoid(dir_heads(a_beta))
    log_decay = (-jnp.exp(a_log.astype(f32))[:, None, :, None]
                 * jax.nn.softplus(dir_heads(a_decay) + dt_bias.astype(f32)[:, None, :, None]))
    oa = gated_delta_rule(aq, ak, av, log_decay[0], beta[0])
    oa = oa + flip_seq(gated_delta_rule(flip_seq(aq), flip_seq(ak), flip_seq(av),
                                        flip_seq(log_decay[1]), flip_seq(beta[1])))
    oa = rms_norm(oa.transpose(0, 2, 1, 3), gdn_norm_w) * jax.nn.silu(
        a_z.astype(f32).reshape(b, s, A_HEADS, HEAD_DIM))
    oa = oa.reshape(b, s, A_WIDTH)

    bq = rms_norm(b_q.reshape(b, s, B_Q_HEADS, HEAD_DIM), q_norm_w).transpose(0, 2, 1, 3)
    bk = rms_norm(b_k.reshape(b, s, B_KV_HEADS, HEAD_DIM), k_norm_w).transpose(0, 2, 1, 3)
    bv = to_heads(b_v, B_KV_HEADS)
    bq = apply_axial_rope(bq, rope_cos, rope_sin)
    bk = apply_axial_rope(bk, rope_cos, rope_sin)
    ob = blocked_bidirectional_gqa(bq, bk, bv)

    lb = lower_bound.astype(f32)
    f_gate = lb + (1.0 - lb) * jax.nn.sigmoid(c_f.astype(f32).reshape(b, s, 2, C_WIDTH))
    log_f = jnp.log(f_gate)
    k_in = 1.0 - f_gate
    cq = to_heads(jax.nn.silu(c_q.astype(f32)), C_HEADS) * (HEAD_DIM ** -0.5)
    ci = to_heads(c_i.astype(f32), C_HEADS)
    oc = hgrn2_chunked(cq, to_heads(k_in[:, :, 0], C_HEADS), ci, to_heads(log_f[:, :, 0], C_HEADS))
    oc = oc + flip_seq(hgrn2_chunked(flip_seq(cq), flip_seq(to_heads(k_in[:, :, 1], C_HEADS)),
                                     flip_seq(ci), flip_seq(to_heads(log_f[:, :, 1], C_HEADS))))
    oc = rms_norm(oc.transpose(0, 2, 1, 3), hgrn_norm_w) * jax.nn.sigmoid(
        c_g.astype(f32).reshape(b, s, C_HEADS, HEAD_DIM))
    oc = oc.reshape(b, s, C_WIDTH)

    mixed = jnp.concatenate([oa.astype(x.dtype), ob.astype(x.dtype), oc.astype(x.dtype)], axis=-1)
    return mixed @ w_out


def memory_cross_attention(x, mem, wq, wk, wv, wo):
    b, s, _ = x.shape
    m = mem.shape[1]
    q = (x @ wq).reshape(b, s, X_HEADS, X_HEAD_DIM)
    k = (mem @ wk).reshape(b, m, X_HEADS, X_HEAD_DIM)
    v = (mem @ wv).reshape(b, m, X_HEADS, X_HEAD_DIM)
    sc = jnp.einsum('bshd,bmhd->bhsm', q, k).astype(jnp.float32) * (X_HEAD_DIM ** -0.5)
    p = jax.nn.softmax(sc, axis=-1).astype(v.dtype)
    o = jnp.einsum('bhsm,bmhd->bshd', p, v).reshape(b, s, X_HEADS * X_HEAD_DIM)
    return o @ wo


def swiglu(x, wg, wu, wd):
    return (jax.nn.silu(x @ wg) * (x @ wu)) @ wd


def moe_swiglu(x, w_router, w_gate, w_up, w_down):
    b, s, d = x.shape
    n = b * s
    xt = x.reshape(n, d)
    logits = (xt @ w_router).astype(jnp.float32)
    top_logit, top_e = lax.top_k(logits, TOP_K)
    top_w = jax.nn.softmax(top_logit, axis=-1)
    flat_e = top_e.reshape(-1)
    flat_tok = jnp.repeat(jnp.arange(n, dtype=jnp.int32), TOP_K)
    flat_w = top_w.reshape(-1)
    order = jnp.argsort(flat_e)
    e_sorted = flat_e[order]
    counts = jnp.bincount(flat_e, length=N_EXPERTS)
    padded = (counts + MOE_BLOCK - 1) // MOE_BLOCK * MOE_BLOCK
    pad_end = jnp.cumsum(padded)
    pad_start = pad_end - padded
    start = jnp.cumsum(counts) - counts
    dest = pad_start[e_sorted] + jnp.arange(n * TOP_K, dtype=jnp.int32) - start[e_sorted]
    n_rows = n * TOP_K + N_EXPERTS * MOE_BLOCK
    n_blocks = n_rows // MOE_BLOCK
    row_tok = jnp.zeros((n_rows,), jnp.int32).at[dest].set(flat_tok[order])
    row_w = jnp.zeros((n_rows,), jnp.float32).at[dest].set(flat_w[order])
    block_e = jnp.minimum(
        jnp.searchsorted(pad_end, jnp.arange(n_blocks, dtype=jnp.int32) * MOE_BLOCK, side='right'),
        N_EXPERTS - 1)
    xb = xt[row_tok].reshape(n_blocks, MOE_BLOCK, d)

    def expert_block(args):
        xs, e = args
        hdn = jax.nn.silu(xs @ w_gate[e]) * (xs @ w_up[e])
        return hdn @ w_down[e]

    yb = lax.map(expert_block, (xb, block_e)).reshape(n_rows, d)
    y = jnp.zeros((n, d), jnp.float32).at[row_tok].add(yb.astype(jnp.float32) * row_w[:, None])
    return y.astype(x.dtype).reshape(b, s, d)


def setup_inputs(seed: int = 0) -> dict:
    key = jax.random.key(seed)
    keys = iter(jax.random.split(key, 40))
    f32 = jnp.float32
    L = DEPTH

    def normal(shape, scale):
        return jax.random.normal(next(keys), shape, f32) * scale

    def gain(shape):
        return 1.0 + normal(shape, 0.02)

    x = normal((BATCH, SEQ, D_MODEL), 1.0)
    mem = normal((BATCH, MEM_TOKENS, D_MODEL), 1.0)
    w_in = normal((L, D_MODEL, D_IN), D_MODEL ** -0.5)
    conv_w = normal((L, CONV_K, 3 * A_WIDTH), CONV_K ** -0.5)
    gdn_a_log = jnp.log(jax.random.uniform(next(keys), (L, 2, A_HEADS), f32, 1.0, 16.0))
    dt = jnp.exp(jax.random.uniform(next(keys), (L, 2, A_HEADS), f32, math.log(1e-3), math.log(1e-1)))
    gdn_dt_bias = dt + jnp.log(-jnp.expm1(-dt))
    gdn_norm_w = gain((L, HEAD_DIM))
    q_norm_w = gain((L, HEAD_DIM))
    k_norm_w = gain((L, HEAD_DIM))
    hgrn_lb_logits = normal((L, C_WIDTH), 0.1)
    hgrn_norm_w = gain((L, HEAD_DIM))
    w_out = normal((L, D_MIX, D_MODEL), D_MIX ** -0.5 * DN_BETA)
    ln1_g = gain((L, D_MODEL))
    ln1_b = normal((L, D_MODEL), 0.02)
    xq = normal((L, D_MODEL, X_HEADS * X_HEAD_DIM), D_MODEL ** -0.5)
    xk = normal((L, D_MODEL, X_HEADS * X_HEAD_DIM), D_MODEL ** -0.5)
    xv = normal((L, D_MODEL, X_HEADS * X_HEAD_DIM), D_MODEL ** -0.5)
    xo = normal((L, X_HEADS * X_HEAD_DIM, D_MODEL), (X_HEADS * X_HEAD_DIM) ** -0.5 * DN_BETA)
    ln2_g = gain((L, D_MODEL))
    ln2_b = normal((L, D_MODEL), 0.02)
    ffn_wg = normal((N_DENSE, D_MODEL, D_FF), D_MODEL ** -0.5)
    ffn_wu = normal((N_DENSE, D_MODEL, D_FF), D_MODEL ** -0.5)
    ffn_wd = normal((N_DENSE, D_FF, D_MODEL), D_FF ** -0.5 * DN_BETA)
    moe_router = normal((N_MOE, D_MODEL, N_EXPERTS), D_MODEL ** -0.5)
    moe_wg = normal((N_MOE, N_EXPERTS, D_MODEL, D_FF_EXPERT), D_MODEL ** -0.5)
    moe_wu = normal((N_MOE, N_EXPERTS, D_MODEL, D_FF_EXPERT), D_MODEL ** -0.5)
    moe_wd = normal((N_MOE, N_EXPERTS, D_FF_EXPERT, D_MODEL), D_FF_EXPERT ** -0.5 * DN_BETA)
    ln3_g = gain((L, D_MODEL))
    ln3_b = normal((L, D_MODEL), 0.02)
    return {'x': x, 'mem': mem, 'w_in': w_in, 'conv_w': conv_w, 'gdn_a_log': gdn_a_log,
            'gdn_dt_bias': gdn_dt_bias, 'gdn_norm_w': gdn_norm_w, 'q_norm_w': q_norm_w,
            'k_norm_w': k_norm_w, 'hgrn_lb_logits': hgrn_lb_logits, 'hgrn_norm_w': hgrn_norm_w,
            'w_out': w_out, 'ln1_g': ln1_g, 'ln1_b': ln1_b, 'xq': xq, 'xk': xk, 'xv': xv, 'xo': xo,
            'ln2_g': ln2_g, 'ln2_b': ln2_b, 'ffn_wg': ffn_wg, 'ffn_wu': ffn_wu, 'ffn_wd': ffn_wd,
            'moe_router': moe_router, 'moe_wg': moe_wg, 'moe_wu': moe_wu, 'moe_wd': moe_wd,
            'ln3_g': ln3_g, 'ln3_b': ln3_b}


def reference(x, mem, w_in, conv_w, gdn_a_log, gdn_dt_bias, gdn_norm_w, q_norm_w, k_norm_w,
              hgrn_lb_logits, hgrn_norm_w, w_out, ln1_g, ln1_b, xq, xk, xv, xo, ln2_g, ln2_b,
              ffn_wg, ffn_wu, ffn_wd, moe_router, moe_wg, moe_wu, moe_wd, ln3_g, ln3_b):
    seq_len = x.shape[1]
    rope_cos, rope_sin = axial_rope_tables(seq_len)
    lower_bounds = hgrn_lower_bounds(hgrn_lb_logits)
    for l in range(DEPTH):
        h = hybrid_token_mixer(x, rope_cos, rope_sin, w_in[l], conv_w[l], gdn_a_log[l], gdn_dt_bias[l],
                               gdn_norm_w[l], q_norm_w[l], k_norm_w[l], lower_bounds[l],
                               hgrn_norm_w[l], w_out[l])
        x = layer_norm(DN_ALPHA * x + h, ln1_g[l], ln1_b[l])
        c = memory_cross_attention(x, mem, xq[l], xk[l], xv[l], xo[l])
        x = layer_norm(DN_ALPHA * x + c, ln2_g[l], ln2_b[l])
        if l % 2 == 0:
            f = swiglu(x, ffn_wg[l // 2], ffn_wu[l // 2], ffn_wd[l // 2])
        else:
            f = moe_swiglu(x, moe_router[l // 2], moe_wg[l // 2], moe_wu[l // 2], moe_wd[l // 2])
        x = layer_norm(DN_ALPHA * x + f, ln3_g[l], ln3_b[l])
    return x
```

```python
import functools
import math

import numpy as np
import jax
import jax.numpy as jnp
from jax import lax
from jax.experimental import pallas as pl
from jax.experimental.pallas import tpu as pltpu

F32 = jnp.float32
BF16 = jnp.bfloat16

D_MODEL = 1024
DEPTH = 2
HEAD_DIM = 64
A_HEADS = 6
A_WIDTH = A_HEADS * HEAD_DIM
B_Q_HEADS = 6
B_KV_HEADS = 2
B_WIDTH = B_Q_HEADS * HEAD_DIM
B_KV_WIDTH = B_KV_HEADS * HEAD_DIM
C_HEADS = 4
C_WIDTH = C_HEADS * HEAD_DIM
D_MIX = A_WIDTH + B_WIDTH + C_WIDTH
CONV_K = 5
CHUNK = 64
GRID_W = 64
ROPE_AXIS_DIM = HEAD_DIM // 2
ROPE_THETA = 10000.0
X_HEADS = 4
X_HEAD_DIM = D_MODEL // X_HEADS
D_FF = 2816
N_EXPERTS = 8
TOP_K = 2
D_FF_EXPERT = 3584
DN_ALPHA = (2 * DEPTH) ** 0.25
LN_EPS = 1e-5
RMS_EPS = 1e-6

LANES = 128
GROUP = 256
CHUNKS_PER_GROUP = GROUP // CHUNK
VMEM_LIMIT = 56 * 1024 * 1024

COL_A_QKV = 0
COL_A_Z = 1152
COL_B_Q = 1536
COL_B_K = 1920
COL_C_G = 2048
COL_B_V = 2304
COL_C_F = 2432
COL_C_I = 2944
COL_C_Q = 3200
COL_GATES = 3456
D_IN_P = 3840


def _cparams(sem):
    return pltpu.CompilerParams(dimension_semantics=sem, vmem_limit_bytes=VMEM_LIMIT)


def _sigmoid(x):
    return 1.0 / (1.0 + jnp.exp(-x))


def _softplus(x):
    return jnp.maximum(x, 0.0) + jnp.log1p(jnp.exp(-jnp.abs(x)))


def _dot(a, b):
    return jnp.dot(a.astype(BF16), b.astype(BF16), preferred_element_type=F32)


def _dot_nt(a, b):
    return lax.dot_general(a.astype(BF16), b.astype(BF16), (((1,), (1,)), ((), ())),
                           preferred_element_type=F32)


def _dot_sel(m01, x):
    m = jnp.where(m01, 1.0, 0.0).astype(BF16)
    hi = x.astype(BF16)
    r1 = x - hi.astype(F32)
    mid = r1.astype(BF16)
    lo = (r1 - mid.astype(F32)).astype(BF16)
    out = jnp.dot(m, hi, preferred_element_type=F32)
    out = out + jnp.dot(m, mid, preferred_element_type=F32)
    return out + jnp.dot(m, lo, preferred_element_type=F32)


def _layer_norm_rows(y, g, b):
    mu = jnp.mean(y, axis=-1, keepdims=True)
    yc = y - mu
    var = jnp.mean(yc * yc, axis=-1, keepdims=True)
    return yc * lax.rsqrt(var + LN_EPS) * g + b


def _iota2(shape, dim):
    return lax.broadcasted_iota(jnp.int32, shape, dim)


def _head_sum(x, first_head):
    s0 = jnp.sum(jnp.where(first_head, x, 0.0), axis=-1, keepdims=True)
    s1 = jnp.sum(jnp.where(first_head, 0.0, x), axis=-1, keepdims=True)
    return jnp.where(first_head, s0, s1)


def _chunk_last_rows(t):
    parts = [jnp.broadcast_to(t[c * CHUNK + CHUNK - 1:c * CHUNK + CHUNK, :], (CHUNK, t.shape[1]))
             for c in range(CHUNKS_PER_GROUP)]
    return jnp.concatenate(parts, axis=0)


def _block_rows(t, block, row_in_block):
    parts = [jnp.broadcast_to(t[b * block + row_in_block:b * block + row_in_block + 1, :], (block, t.shape[1]))
             for b in range(GROUP // block)]
    return jnp.concatenate(parts, axis=0)


def _mm_kernel(x_ref, w_ref, o_ref):
    o_ref[...] = jnp.dot(x_ref[...], w_ref[...], preferred_element_type=F32).astype(o_ref.dtype)


def matmul(x, w, out_dtype, tm, tn):
    m, k = x.shape
    n = w.shape[1]
    return pl.pallas_call(
        _mm_kernel,
        out_shape=jax.ShapeDtypeStruct((m, n), out_dtype),
        grid=(m // tm, n // tn),
        in_specs=[pl.BlockSpec((tm, k), lambda i, j: (i, 0)),
                  pl.BlockSpec((k, tn), lambda i, j: (0, j))],
        out_specs=pl.BlockSpec((tm, tn), lambda i, j: (i, j)),
        compiler_params=_cparams(("parallel", "arbitrary")),
        name="matmul",
    )(x, w)


def _gdn_kernel(q_ref, k_ref, v_ref, cwq_ref, cwk_ref, cwv_ref, g_ref, prm_ref, o_ref, s_scr, *, seq_len):
    p = pl.program_id(1)
    n = pl.program_id(2)
    n_groups = pl.num_programs(2)

    @pl.when(n == 0)
    def _():
        s_scr[...] = jnp.zeros_like(s_scr)

    row0 = pl.multiple_of(n * GROUP, GROUP)
    prev0 = pl.multiple_of(jnp.maximum(row0 - 8, 0), 8)
    next0 = pl.multiple_of(jnp.minimum(row0 + GROUP, seq_len - 8), 8)

    def conv_silu(x_ref, cw_ref):
        main = x_ref[0, pl.ds(row0, GROUP), :]
        prev = jnp.where(n > 0, x_ref[0, pl.ds(prev0, 8), :], 0.0)
        nxt = jnp.where(n < n_groups - 1, x_ref[0, pl.ds(next0, 8), :], 0.0)
        ext = jnp.concatenate([prev, main, nxt], axis=0)
        cw = cw_ref[0]
        acc = ext[6:6 + GROUP, :] * cw[0:1, :]
        for t in range(1, CONV_K):
            acc = acc + ext[6 + t:6 + t + GROUP, :] * cw[t:t + 1, :]
        return acc * _sigmoid(acc)

    q = conv_silu(q_ref, cwq_ref)
    k = conv_silu(k_ref, cwk_ref)
    v = conv_silu(v_ref, cwv_ref)

    lane = _iota2((GROUP, LANES), 1)
    h0 = lane < HEAD_DIM
    q = q * lax.rsqrt(_head_sum(q * q, h0) + RMS_EPS) * (HEAD_DIM ** -0.5)
    k = k * lax.rsqrt(_head_sum(k * k, h0) + RMS_EPS)

    gt = g_ref[0]
    base = 4 * p

    def gate_col(j):
        return jnp.sum(jnp.where(lane == base + j, gt, 0.0), axis=-1, keepdims=True)

    prm = prm_ref[0]
    beta0 = _sigmoid(gate_col(0))
    beta1 = _sigmoid(gate_col(1))
    g0 = -jnp.exp(prm[0:1, 0:1]) * _softplus(gate_col(2) + prm[1:2, 0:1])
    g1 = -jnp.exp(prm[0:1, 1:2]) * _softplus(gate_col(3) + prm[1:2, 1:2])
    beta_t = jnp.where(h0, beta0, beta1)
    g_t = jnp.where(h0, g0, g1)

    ri = _iota2((GROUP, GROUP), 0)
    ci = _iota2((GROUP, GROUP), 1)
    same_chunk = (ri >> 6) == (ci >> 6)
    incl = same_chunk & (ri >= ci)
    strict = same_chunk & (ri > ci)

    gc = _dot_sel(incl, g_t)
    gc_t = gc.T
    glast = _chunk_last_rows(gc)
    exp_g = jnp.exp(gc)

    kb = k * beta_t
    rhs = jnp.concatenate([v * beta_t, kb * exp_g], axis=1).astype(BF16)
    k16 = k.astype(BF16)

    sr = _iota2((CHUNK, GROUP), 0)
    sc = _iota2((CHUNK, GROUP), 1)
    eye_side = jnp.where(sr == (sc & (CHUNK - 1)), 1.0, 0.0)

    def block_diag(side):
        return jnp.where(same_chunk, jnp.concatenate([side] * CHUNKS_PER_GROUP, axis=0), 0.0)

    sols, attns = [], []
    for h in range(2):
        mh = h0 if h == 0 else jnp.logical_not(h0)
        col = gc[:, 0:1] if h == 0 else gc[:, HEAD_DIM:HEAD_DIM + 1]
        row = gc_t[0:1, :] if h == 0 else gc_t[HEAD_DIM:HEAD_DIM + 1, :]
        decay = jnp.exp(jnp.where(incl, col - row, -1e30))
        kk = _dot_nt(jnp.where(mh, kb, 0.0), k16)
        low = jnp.where(strict, kk * decay, 0.0)
        a = -(low[0:CHUNK] + low[CHUNK:2 * CHUNK] + low[2 * CHUNK:3 * CHUNK] + low[3 * CHUNK:4 * CHUNK])
        x = eye_side + a
        pw = a
        for _ in range(5):
            pw = _dot(pw, block_diag(pw))
            x = x + _dot(x, block_diag(pw))
        sols.append(jnp.dot(block_diag(x).astype(BF16), rhs, preferred_element_type=F32))
        attns.append(jnp.where(incl, _dot_nt(jnp.where(mh, q, 0.0), k16) * decay, 0.0).astype(BF16))

    u = jnp.where(h0, sols[0][:, :LANES], sols[1][:, :LANES])
    w = jnp.where(h0, sols[0][:, LANES:], sols[1][:, LANES:])
    qd = q * exp_g
    kd = k * jnp.exp(glast - gc)
    kd_t = kd.T

    tl = _iota2((LANES, GROUP), 1)
    br = _iota2((LANES, LANES), 0)
    bc = _iota2((LANES, LANES), 1)
    bd128 = (br >> 6) == (bc >> 6)

    h0c = _iota2((CHUNK, LANES), 1) < HEAD_DIM
    state = s_scr[...]
    zeros_c = jnp.zeros((CHUNK, LANES), F32)
    v_new_parts = [zeros_c] * CHUNKS_PER_GROUP
    for c in range(CHUNKS_PER_GROUP):
        rs = slice(c * CHUNK, (c + 1) * CHUNK)
        r = _dot(jnp.concatenate([qd[rs], w[rs]], axis=0), state)
        v_new = u[rs] - r[CHUNK:]
        v_new_parts[c] = v_new
        vn32 = jnp.concatenate(v_new_parts, axis=0)
        vn = vn32.astype(BF16)
        o0 = jnp.dot(attns[0][rs, :], vn, preferred_element_type=F32)
        o1 = jnp.dot(attns[1][rs, :], vn, preferred_element_type=F32)
        o_ref[0, rs, :] = r[:CHUNK] + jnp.where(h0c, o0, o1)
        upd = jnp.dot(jnp.where((tl >> 6) == c, kd_t, 0.0), vn32, preferred_element_type=F32)
        cd = jnp.exp(glast[c * CHUNK:c * CHUNK + 1, :])
        state = state * cd + jnp.where(bd128, upd, 0.0)
    s_scr[...] = state


def gdn_direction(src, col_blk0, gate_blk, conv_w, prm, seq_len):
    b = src.shape[0]
    n_groups = seq_len // GROUP
    pairs = A_HEADS // 2
    seq_spec = lambda off: pl.BlockSpec((1, seq_len, LANES), lambda i, p, n: (i, 0, col_blk0 + off + p))
    cw_spec = lambda off: pl.BlockSpec((1, CONV_K, LANES), lambda i, p, n: (0, 0, off + p))
    cw = conv_w[None]
    return pl.pallas_call(
        functools.partial(_gdn_kernel, seq_len=seq_len),
        out_shape=jax.ShapeDtypeStruct((b, seq_len, A_WIDTH), F32),
        grid=(b, pairs, n_groups),
        in_specs=[seq_spec(0), seq_spec(pairs), seq_spec(2 * pairs),
                  cw_spec(0), cw_spec(pairs), cw_spec(2 * pairs),
                  pl.BlockSpec((1, GROUP, LANES), lambda i, p, n: (i, n, gate_blk)),
                  pl.BlockSpec((1, 8, LANES), lambda i, p, n: (p, 0, 0))],
        out_specs=pl.BlockSpec((1, GROUP, LANES), lambda i, p, n: (i, n, p)),
        scratch_shapes=[pltpu.VMEM((LANES, LANES), F32)],
        compiler_params=_cparams(("parallel", "parallel", "arbitrary")),
        name="gdn_scan",
    )(src, src, src, cw, cw, cw, src, prm)


EXP_CLAMP = 60.0


def _hgrn_kernel(f_ref, i_ref, q_ref, lb_ref, o_ref, s_scr):
    n = pl.program_id(2)

    @pl.when(n == 0)
    def _():
        s_scr[...] = jnp.zeros_like(s_scr)

    lb = lb_ref[0]
    f = lb + (1.0 - lb) * _sigmoid(f_ref[0])
    lf = jnp.log(f)
    kk = 1.0 - f
    qr = q_ref[0]
    q = qr * _sigmoid(qr) * (HEAD_DIM ** -0.5)
    v = i_ref[0]

    lane = _iota2((GROUP, LANES), 1)
    h0 = lane < HEAD_DIM
    ri = _iota2((GROUP, GROUP), 0)
    ci = _iota2((GROUP, GROUP), 1)
    same_chunk = (ri >> 6) == (ci >> 6)
    incl = same_chunk & (ri >= ci)

    cum = _dot_sel(incl, lf)
    clast = _chunk_last_rows(cum)
    qd = q * jnp.exp(cum)
    kd = kk * jnp.exp(clast - cum)

    m32 = same_chunk & ((ri & 63) >= 32) & ((ci & 63) < 32)
    m16 = ((ri >> 5) == (ci >> 5)) & ((ri & 31) >= 16) & ((ci & 31) < 16)
    mdg = ((ri >> 4) == (ci >> 4)) & (ri >= ci)
    ref32 = _block_rows(cum, 64, 31)
    ref16 = _block_rows(cum, 32, 15)
    refdg = _block_rows(cum, 16, 7)
    qs32 = q * jnp.exp(jnp.minimum(cum - ref32, 0.0))
    ks32 = (kk * jnp.exp(jnp.minimum(ref32 - cum, 0.0))).astype(BF16)
    qs16 = q * jnp.exp(jnp.minimum(cum - ref16, 0.0))
    ks16 = (kk * jnp.exp(jnp.minimum(ref16 - cum, 0.0))).astype(BF16)
    qsdg = q * jnp.exp(jnp.clip(cum - refdg, -EXP_CLAMP, EXP_CLAMP))
    ksdg = (kk * jnp.exp(jnp.clip(refdg - cum, -EXP_CLAMP, EXP_CLAMP))).astype(BF16)

    v16 = v.astype(BF16)
    intra = []
    for h in range(2):
        mh = h0 if h == 0 else jnp.logical_not(h0)
        a = jnp.where(m32, _dot_nt(jnp.where(mh, qs32, 0.0), ks32),
                      jnp.where(m16, _dot_nt(jnp.where(mh, qs16, 0.0), ks16),
                                jnp.where(mdg, _dot_nt(jnp.where(mh, qsdg, 0.0), ksdg), 0.0)))
        intra.append(jnp.dot(a.astype(BF16), v16, preferred_element_type=F32))
    intra = jnp.where(h0, intra[0], intra[1])

    v_t = v.T
    tl = _iota2((LANES, GROUP), 1)
    br = _iota2((LANES, LANES), 0)
    bc = _iota2((LANES, LANES), 1)
    bd128 = (br >> 6) == (bc >> 6)

    state_t = s_scr[...]
    for c in range(CHUNKS_PER_GROUP):
        rs = slice(c * CHUNK, (c + 1) * CHUNK)
        o_ref[0, rs, :] = _dot_nt(qd[rs], state_t) + intra[rs]
        upd = jnp.dot(jnp.where((tl >> 6) == c, v_t, 0.0), kd, preferred_element_type=F32)
        cd = jnp.exp(clast[c * CHUNK:c * CHUNK + 1, :])
        state_t = state_t * cd + jnp.where(bd128, upd, 0.0)
    s_scr[...] = state_t


def hgrn_direction(src, f_blk0, i_blk0, q_blk0, lb, seq_len):
    b = src.shape[0]
    n_groups = seq_len // GROUP
    pairs = C_HEADS // 2
    spec = lambda blk0: pl.BlockSpec((1, GROUP, LANES), lambda i, p, n: (i, n, blk0 + p))
    return pl.pallas_call(
        _hgrn_kernel,
        out_shape=jax.ShapeDtypeStruct((b, seq_len, C_WIDTH), F32),
        grid=(b, pairs, n_groups),
        in_specs=[spec(f_blk0), spec(i_blk0), spec(q_blk0),
                  pl.BlockSpec((1, 1, LANES), lambda i, p, n: (p, 0, 0))],
        out_specs=pl.BlockSpec((1, GROUP, LANES), lambda i, p, n: (i, n, p)),
        scratch_shapes=[pltpu.VMEM((LANES, LANES), F32)],
        compiler_params=_cparams(("parallel", "parallel", "arbitrary")),
        name="hgrn_scan",
    )(src, src, src, lb)


ATT_TQ = 256


def _norm_rope(x, w, cos, sin_signed, first_half16):
    lane = _iota2(x.shape, 1)
    h0 = lane < HEAD_DIM
    ms = _head_sum(x * x, h0) * (1.0 / HEAD_DIM)
    xn = x * lax.rsqrt(ms + RMS_EPS) * w
    partner = jnp.where(first_half16, pltpu.roll(xn, LANES - 16, 1), pltpu.roll(xn, 16, 1))
    return xn * cos + partner * sin_signed


def _attn_kernel(q_ref, k_ref, v_ref, cosq_ref, sinq_ref, cosk_ref, sink_ref, qw_ref, kw_ref, o_ref,
                 k_scr, v_scr):
    qi = pl.program_id(1)

    @pl.when(qi == 0)
    def _():
        kx = k_ref[0]
        lane_k = _iota2(kx.shape, 1)
        k_scr[...] = _norm_rope(kx, kw_ref[...], cosk_ref[...], sink_ref[...], (lane_k & 31) < 16).astype(BF16)
        v_scr[...] = v_ref[0].astype(BF16)

    lane = _iota2((ATT_TQ, LANES), 1)
    h0 = lane < HEAD_DIM
    fh = (lane & 31) < 16
    cos = cosq_ref[...]
    sin = sinq_ref[...]
    kmat = k_scr[...]
    vmat = v_scr[...]
    for j in range(B_Q_HEADS // 2):
        qn = _norm_rope(q_ref[0, :, j * LANES:(j + 1) * LANES], qw_ref[...], cos, sin, fh) * (HEAD_DIM ** -0.5)
        outs = []
        for half in range(2):
            mh = h0 if half == 0 else jnp.logical_not(h0)
            s = _dot_nt(jnp.where(mh, qn, 0.0), kmat)
            m = jnp.max(s, axis=-1, keepdims=True)
            e = jnp.exp(s - m)
            l = jnp.sum(e, axis=-1, keepdims=True)
            pv = jnp.dot(e.astype(BF16), vmat, preferred_element_type=F32)
            outs.append(pv / l)
        o_ref[0, :, j * LANES:(j + 1) * LANES] = jnp.where(h0, outs[0], outs[1])


def gqa_attention(proj, cos_t, sin_t, q_w, k_w, seq_len):
    b = proj.shape[0]
    qb, kb, vb = COL_B_Q // LANES, COL_B_K // LANES, COL_B_V // LANES
    return pl.pallas_call(
        _attn_kernel,
        out_shape=jax.ShapeDtypeStruct((b, seq_len, B_WIDTH), F32),
        grid=(b, seq_len // ATT_TQ),
        in_specs=[pl.BlockSpec((1, ATT_TQ, B_WIDTH), lambda i, t: (i, t, qb // 3)),
                  pl.BlockSpec((1, seq_len, LANES), lambda i, t: (i, 0, kb)),
                  pl.BlockSpec((1, seq_len, LANES), lambda i, t: (i, 0, vb)),
                  pl.BlockSpec((ATT_TQ, LANES), lambda i, t: (t, 0)),
                  pl.BlockSpec((ATT_TQ, LANES), lambda i, t: (t, 0)),
                  pl.BlockSpec((seq_len, LANES), lambda i, t: (0, 0)),
                  pl.BlockSpec((seq_len, LANES), lambda i, t: (0, 0)),
                  pl.BlockSpec((1, LANES), lambda i, t: (0, 0)),
                  pl.BlockSpec((1, LANES), lambda i, t: (0, 0))],
        out_specs=pl.BlockSpec((1, ATT_TQ, B_WIDTH), lambda i, t: (i, t, 0)),
        scratch_shapes=[pltpu.VMEM((seq_len, LANES), BF16), pltpu.VMEM((seq_len, LANES), BF16)],
        compiler_params=_cparams(("parallel", "arbitrary")),
        name="gqa_attention",
    )(proj, proj, proj, cos_t, sin_t, cos_t, sin_t, q_w, k_w)


def rope_tables(seq_len):
    rows = seq_len // GRID_W
    row = jnp.repeat(jnp.arange(rows, dtype=F32), GRID_W)
    col = jnp.tile(jnp.arange(GRID_W, dtype=F32), rows)
    inv_freq = ROPE_THETA ** (-jnp.arange(0, ROPE_AXIS_DIM, 2, dtype=F32) / ROPE_AXIS_DIM)
    ang_r = row[:, None] * inv_freq
    ang_c = col[:, None] * inv_freq
    cos64 = jnp.concatenate([jnp.cos(ang_r), jnp.cos(ang_r), jnp.cos(ang_c), jnp.cos(ang_c)], axis=1)
    sin64 = jnp.concatenate([-jnp.sin(ang_r), jnp.sin(ang_r), -jnp.sin(ang_c), jnp.sin(ang_c)], axis=1)
    return jnp.tile(cos64, (1, 2)), jnp.tile(sin64, (1, 2))


MIX_TM = 256


def _mixout_kernel(oaf_ref, oab_ref, z_ref, ob_ref, ocf_ref, ocb_ref, g_ref, x_ref, wa_ref, wc_ref,
                   wo_ref, lg_ref, lb_ref, xo_ref, xb_ref):
    lane = _iota2((MIX_TM, LANES), 1)
    h0 = lane < HEAD_DIM
    parts = []
    for j in range(A_WIDTH // LANES):
        sl = slice(j * LANES, (j + 1) * LANES)
        o = oaf_ref[:, sl] + oab_ref[:, sl]
        ms = _head_sum(o * o, h0) * (1.0 / HEAD_DIM)
        z = z_ref[:, sl]
        parts.append((o * lax.rsqrt(ms + RMS_EPS) * wa_ref[...] * (z * _sigmoid(z))).astype(BF16))
    parts.append(ob_ref[...].astype(BF16))
    for j in range(C_WIDTH // LANES):
        sl = slice(j * LANES, (j + 1) * LANES)
        o = ocf_ref[:, sl] + ocb_ref[:, sl]
        ms = _head_sum(o * o, h0) * (1.0 / HEAD_DIM)
        parts.append((o * lax.rsqrt(ms + RMS_EPS) * wc_ref[...] * _sigmoid(g_ref[:, sl])).astype(BF16))
    mixed = jnp.concatenate(parts, axis=1)
    h = jnp.dot(mixed, wo_ref[...], preferred_element_type=F32)
    y = _layer_norm_rows(DN_ALPHA * x_ref[...] + h, lg_ref[...], lb_ref[...])
    xo_ref[...] = y
    xb_ref[...] = y.astype(BF16)


def mixer_output(oa_f, oa_b, proj2, ob, oc_f, oc_b, x2, gdn_w, hgrn_w, w_out, ln_g, ln_b):
    m = x2.shape[0]
    row = lambda w: pl.BlockSpec((MIX_TM, w), lambda i: (i, 0))
    const = lambda r, c: pl.BlockSpec((r, c), lambda i: (0, 0))
    return pl.pallas_call(
        _mixout_kernel,
        out_shape=(jax.ShapeDtypeStruct((m, D_MODEL), F32), jax.ShapeDtypeStruct((m, D_MODEL), BF16)),
        grid=(m // MIX_TM,),
        in_specs=[row(A_WIDTH), row(A_WIDTH),
                  pl.BlockSpec((MIX_TM, A_WIDTH), lambda i: (i, COL_A_Z // A_WIDTH)),
                  row(B_WIDTH), row(C_WIDTH), row(C_WIDTH),
                  pl.BlockSpec((MIX_TM, C_WIDTH), lambda i: (i, COL_C_G // C_WIDTH)),
                  row(D_MODEL), const(1, LANES), const(1, LANES),
                  const(D_MIX, D_MODEL), const(1, D_MODEL), const(1, D_MODEL)],
        out_specs=(row(D_MODEL), row(D_MODEL)),
        compiler_params=_cparams(("parallel",)),
        name="mixer_output",
    )(oa_f, oa_b, proj2, ob, oc_f, oc_b, proj2, x2, gdn_w, hgrn_w, w_out, ln_g, ln_b)


XATT_TM = 512


def _xattn_kernel(xb_ref, x_ref, k_ref, v_ref, wq_ref, wo_ref, lg_ref, lb_ref, xo_ref, xob_ref):
    q = jnp.dot(xb_ref[0], wq_ref[...], preferred_element_type=F32) * (X_HEAD_DIM ** -0.5)
    outs = []
    for h in range(X_HEADS):
        sl = slice(h * X_HEAD_DIM, (h + 1) * X_HEAD_DIM)
        s = _dot_nt(q[:, sl], k_ref[0, :, sl])
        m = jnp.max(s, axis=-1, keepdims=True)
        e = jnp.exp(s - m)
        l = jnp.sum(e, axis=-1, keepdims=True)
        outs.append((jnp.dot(e.astype(BF16), v_ref[0, :, sl], preferred_element_type=F32) / l).astype(BF16))
    o = jnp.concatenate(outs, axis=1)
    c = jnp.dot(o, wo_ref[...], preferred_element_type=F32)
    y = _layer_norm_rows(DN_ALPHA * x_ref[0] + c, lg_ref[...], lb_ref[...])
    xo_ref[0] = y
    xob_ref[0] = y.astype(BF16)


def cross_attention(xb3, x3, k3, v3, wq, wo, ln_g, ln_b):
    b, s, _ = x3.shape
    mem = k3.shape[1]
    row = pl.BlockSpec((1, XATT_TM, D_MODEL), lambda i, t: (i, t, 0))
    const = lambda r, c: pl.BlockSpec((r, c), lambda i, t: (0, 0))
    kv = pl.BlockSpec((1, mem, D_MODEL), lambda i, t: (i, 0, 0))
    return pl.pallas_call(
        _xattn_kernel,
        out_shape=(jax.ShapeDtypeStruct((b, s, D_MODEL), F32), jax.ShapeDtypeStruct((b, s, D_MODEL), BF16)),
        grid=(b, s // XATT_TM),
        in_specs=[row, row, kv, kv, const(D_MODEL, D_MODEL), const(D_MODEL, D_MODEL),
                  const(1, D_MODEL), const(1, D_MODEL)],
        out_specs=(row, row),
        compiler_params=_cparams(("parallel", "parallel")),
        name="cross_attention",
    )(xb3, x3, k3, v3, wq, wo, ln_g, ln_b)


FFN_TM = 512
FFN_TF = 1408


def _ffn_kernel(xb_ref, x_ref, wg_ref, wu_ref, wd_ref, lg_ref, lb_ref, xo_ref, xob_ref, acc_ref):
    j = pl.program_id(1)

    @pl.when(j == 0)
    def _():
        acc_ref[...] = jnp.zeros_like(acc_ref)

    xb = xb_ref[...]
    g = jnp.dot(xb, wg_ref[...], preferred_element_type=F32)
    u = jnp.dot(xb, wu_ref[...], preferred_element_type=F32)
    h = (g * _sigmoid(g) * u).astype(BF16)
    acc_ref[...] += jnp.dot(h, wd_ref[...], preferred_element_type=F32)

    @pl.when(j == pl.num_programs(1) - 1)
    def _():
        y = _layer_norm_rows(DN_ALPHA * x_ref[...] + acc_ref[...], lg_ref[...], lb_ref[...])
        xo_ref[...] = y
        xob_ref[...] = y.astype(BF16)


def dense_ffn(xb2, x2, wg, wu, wd, ln_g, ln_b):
    m = x2.shape[0]
    row = pl.BlockSpec((FFN_TM, D_MODEL), lambda i, j: (i, 0))
    const = pl.BlockSpec((1, D_MODEL), lambda i, j: (0, 0))
    return pl.pallas_call(
        _ffn_kernel,
        out_shape=(jax.ShapeDtypeStruct((m, D_MODEL), F32), jax.ShapeDtypeStruct((m, D_MODEL), BF16)),
        grid=(m // FFN_TM, D_FF // FFN_TF),
        in_specs=[row, row,
                  pl.BlockSpec((D_MODEL, FFN_TF), lambda i, j: (0, j)),
                  pl.BlockSpec((D_MODEL, FFN_TF), lambda i, j: (0, j)),
                  pl.BlockSpec((FFN_TF, D_MODEL), lambda i, j: (j, 0)),
                  const, const],
        out_specs=(row, row),
        scratch_shapes=[pltpu.VMEM((FFN_TM, D_MODEL), F32)],
        compiler_params=_cparams(("parallel", "arbitrary")),
        name="dense_ffn",
    )(xb2, x2, wg, wu, wd, ln_g, ln_b)


MOE_TM = 512
MOE_WIN = 512
MOE_TF = 512
ROUTER_TM = 512


def _router_kernel(xb_ref, wr_ref, e_ref, w_ref):
    logits = jnp.dot(xb_ref[...], wr_ref[...], preferred_element_type=F32)
    lane = _iota2(logits.shape, 1)
    neg = jnp.float32(-jnp.inf)
    lane_f = lane.astype(F32)
    lg = jnp.where(lane < N_EXPERTS, logits, neg)
    m1 = jnp.max(lg, axis=-1, keepdims=True)
    i1 = jnp.min(jnp.where(lg == m1, lane_f, float(LANES)), axis=-1, keepdims=True)
    lg2 = jnp.where(lane_f == i1, neg, lg)
    m2 = jnp.max(lg2, axis=-1, keepdims=True)
    i2 = jnp.min(jnp.where(lg2 == m2, lane_f, float(LANES)), axis=-1, keepdims=True)
    t = jnp.exp(m2 - m1)
    w1 = 1.0 / (1.0 + t)
    w2 = t / (1.0 + t)
    e_ref[...] = jnp.where(lane == 0, i1, jnp.where(lane == 1, i2, 0.0)).astype(jnp.int32)
    w_ref[...] = jnp.where(lane == 0, w1, jnp.where(lane == 1, w2, 0.0))


def moe_router(xb2, w_router_p):
    m = xb2.shape[0]
    return pl.pallas_call(
        _router_kernel,
        out_shape=(jax.ShapeDtypeStruct((m, LANES), jnp.int32), jax.ShapeDtypeStruct((m, LANES), F32)),
        grid=(m // ROUTER_TM,),
        in_specs=[pl.BlockSpec((ROUTER_TM, D_MODEL), lambda i: (i, 0)),
                  pl.BlockSpec((D_MODEL, LANES), lambda i: (0, 0))],
        out_specs=(pl.BlockSpec((ROUTER_TM, LANES), lambda i: (i, 0)),
                   pl.BlockSpec((ROUTER_TM, LANES), lambda i: (i, 0))),
        compiler_params=_cparams(("parallel",)),
        name="moe_router",
    )(xb2, w_router_p)


def _moe_gather_kernel(rb_ref, win_ref, flag_ref, tok_ref, x_ref, o_ref):
    p = pl.program_id(0)
    flag = flag_ref[p]

    @pl.when((flag & 2) != 0)
    def _():
        o_ref[...] = jnp.zeros_like(o_ref)

    @pl.when((flag & 1) != 0)
    def _():
        tok = tok_ref[...]
        cols = _iota2((MOE_TM, MOE_WIN), 1) + win_ref[p] * MOE_WIN
        onehot = jnp.where(tok == cols, 1.0, 0.0).astype(BF16)
        o_ref[...] += jnp.dot(onehot, x_ref[...], preferred_element_type=F32).astype(BF16)


def moe_gather(pair_rb, pair_win, pair_flag, row_tok_col, xb2, n_rows):
    n_pairs = pair_rb.shape[0]
    grid_spec = pltpu.PrefetchScalarGridSpec(
        num_scalar_prefetch=3,
        grid=(n_pairs,),
        in_specs=[pl.BlockSpec((MOE_TM, 1), lambda p, rb, win, fl: (rb[p], 0)),
                  pl.BlockSpec((MOE_WIN, D_MODEL), lambda p, rb, win, fl: (win[p], 0))],
        out_specs=pl.BlockSpec((MOE_TM, D_MODEL), lambda p, rb, win, fl: (rb[p], 0)),
    )
    return pl.pallas_call(
        _moe_gather_kernel,
        out_shape=jax.ShapeDtypeStruct((n_rows, D_MODEL), BF16),
        grid_spec=grid_spec,
        compiler_params=_cparams(("arbitrary",)),
        name="moe_gather",
    )(pair_rb, pair_win, pair_flag, row_tok_col, xb2)


def _moe_ffn_kernel(be_ref, bv_ref, x_ref, wg_ref, wu_ref, wd_ref, rw_ref, o_ref, acc_ref):
    i = pl.program_id(0)
    j = pl.program_id(1)
    valid = bv_ref[i] != 0

    @pl.when(j == 0)
    def _():
        acc_ref[...] = jnp.zeros_like(acc_ref)

    @pl.when(valid)
    def _():
        xb = x_ref[...]
        g = jnp.dot(xb, wg_ref[0], preferred_element_type=F32)
        u = jnp.dot(xb, wu_ref[0], preferred_element_type=F32)
        h = (g * _sigmoid(g) * u).astype(BF16)
        acc_ref[...] += jnp.dot(h, wd_ref[0], preferred_element_type=F32)

    @pl.when(j == pl.num_programs(1) - 1)
    def _():
        o_ref[...] = (acc_ref[...] * rw_ref[...]).astype(BF16)


def moe_expert_ffn(block_e, block_valid, xrows, wg, wu, wd, row_w_col):
    n_rows = xrows.shape[0]
    grid_spec = pltpu.PrefetchScalarGridSpec(
        num_scalar_prefetch=2,
        grid=(n_rows // MOE_TM, D_FF_EXPERT // MOE_TF),
        in_specs=[pl.BlockSpec((MOE_TM, D_MODEL), lambda i, j, be, bv: (i, 0)),
                  pl.BlockSpec((1, D_MODEL, MOE_TF), lambda i, j, be, bv: (be[i], 0, j)),
                  pl.BlockSpec((1, D_MODEL, MOE_TF), lambda i, j, be, bv: (be[i], 0, j)),
                  pl.BlockSpec((1, MOE_TF, D_MODEL), lambda i, j, be, bv: (be[i], j, 0)),
                  pl.BlockSpec((MOE_TM, 1), lambda i, j, be, bv: (i, 0))],
        out_specs=pl.BlockSpec((MOE_TM, D_MODEL), lambda i, j, be, bv: (i, 0)),
        scratch_shapes=[pltpu.VMEM((MOE_TM, D_MODEL), F32)],
    )
    return pl.pallas_call(
        _moe_ffn_kernel,
        out_shape=jax.ShapeDtypeStruct((n_rows, D_MODEL), BF16),
        grid_spec=grid_spec,
        compiler_params=_cparams(("arbitrary", "arbitrary")),
        name="moe_expert_ffn",
    )(block_e, block_valid, xrows, wg, wu, wd, row_w_col)


def _moe_combine_kernel(rb_ref, win_ref, flag_ref, tok_ref, y_ref, x_ref, lg_ref, lb_ref, xo_ref, acc_ref):
    p = pl.program_id(0)
    flag = flag_ref[p]

    @pl.when((flag & 2) != 0)
    def _():
        acc_ref[...] = jnp.zeros_like(acc_ref)

    @pl.when((flag & 1) != 0)
    def _():
        tok = tok_ref[0]
        rows = _iota2((MOE_WIN, MOE_TM), 0) + win_ref[p] * MOE_WIN
        onehot_t = jnp.where(tok == rows, 1.0, 0.0).astype(BF16)
        acc_ref[...] += jnp.dot(onehot_t, y_ref[...], preferred_element_type=F32)

    @pl.when((flag & 4) != 0)
    def _():
        xo_ref[...] = _layer_norm_rows(DN_ALPHA * x_ref[...] + acc_ref[...], lg_ref[...], lb_ref[...])


def moe_combine(pair_rb, pair_win, pair_flag, row_tok_row, yrows, x2, ln_g, ln_b):
    n_pairs = pair_rb.shape[0]
    m = x2.shape[0]
    grid_spec = pltpu.PrefetchScalarGridSpec(
        num_scalar_prefetch=3,
        grid=(n_pairs,),
        in_specs=[pl.BlockSpec((1, 1, MOE_TM), lambda p, rb, win, fl: (rb[p], 0, 0)),
                  pl.BlockSpec((MOE_TM, D_MODEL), lambda p, rb, win, fl: (rb[p], 0)),
                  pl.BlockSpec((MOE_WIN, D_MODEL), lambda p, rb, win, fl: (win[p], 0)),
                  pl.BlockSpec((1, D_MODEL), lambda p, rb, win, fl: (0, 0)),
                  pl.BlockSpec((1, D_MODEL), lambda p, rb, win, fl: (0, 0))],
        out_specs=pl.BlockSpec((MOE_WIN, D_MODEL), lambda p, rb, win, fl: (win[p], 0)),
        scratch_shapes=[pltpu.VMEM((MOE_WIN, D_MODEL), F32)],
    )
    return pl.pallas_call(
        _moe_combine_kernel,
        out_shape=jax.ShapeDtypeStruct((m, D_MODEL), F32),
        grid_spec=grid_spec,
        compiler_params=_cparams(("arbitrary",)),
        name="moe_combine",
    )(pair_rb, pair_win, pair_flag, row_tok_row, yrows, x2, ln_g, ln_b)


def moe_layer(xb2, x2, w_router, wg, wu, wd, ln_g, ln_b):
    n = x2.shape[0]
    n_win = n // MOE_WIN
    n_rows = n * TOP_K + N_EXPERTS * MOE_TM
    n_blocks = n_rows // MOE_TM
    n_pairs = n_blocks + N_EXPERTS * n_win

    wr = jnp.pad(w_router, ((0, 0), (0, LANES - N_EXPERTS))).astype(BF16)
    top_e_t, top_w_t = moe_router(xb2, wr)
    flat_e = top_e_t[:, :TOP_K].reshape(-1)
    flat_w = top_w_t[:, :TOP_K].reshape(-1)
    flat_tok = jnp.repeat(jnp.arange(n, dtype=jnp.int32), TOP_K)

    onehot = (flat_e[:, None] == jnp.arange(N_EXPERTS, dtype=jnp.int32)[None, :]).astype(jnp.int32)
    csum = jnp.cumsum(onehot, axis=0)
    counts = csum[-1]
    rank = jnp.take_along_axis(csum, flat_e[:, None], axis=1)[:, 0] - 1
    padded = (counts + MOE_TM - 1) // MOE_TM * MOE_TM
    pad_end = jnp.cumsum(padded)
    pad_start = pad_end - padded
    dest = pad_start[flat_e] + rank
    row_tok = jnp.full((n_rows,), -1, jnp.int32).at[dest].set(flat_tok)
    row_w = jnp.zeros((n_rows,), F32).at[dest].set(flat_w)

    blk_row0 = jnp.arange(n_blocks, dtype=jnp.int32) * MOE_TM
    block_e = jnp.minimum(jnp.searchsorted(pad_end, blk_row0, side='right'), N_EXPERTS - 1).astype(jnp.int32)
    blk_cnt = jnp.clip(counts[block_e] - (blk_row0 - pad_start[block_e]), 0, MOE_TM)
    blk_cnt = jnp.where(blk_row0 < pad_end[-1], blk_cnt, 0)
    block_valid = (blk_cnt > 0).astype(jnp.int32)
    tok_first = row_tok[blk_row0]
    tok_last = row_tok[blk_row0 + jnp.maximum(blk_cnt - 1, 0)]
    win_lo = jnp.where(blk_cnt > 0, tok_first // MOE_WIN, 0)
    win_hi = jnp.where(blk_cnt > 0, tok_last // MOE_WIN, 0)
    blk_pairs = win_hi - win_lo + 1
    pair_end = jnp.cumsum(blk_pairs)
    pair_start = pair_end - blk_pairs
    total = pair_end[-1]
    pidx = jnp.arange(n_pairs, dtype=jnp.int32)
    p_rb = jnp.minimum(jnp.searchsorted(pair_end, pidx, side='right'), n_blocks - 1).astype(jnp.int32)
    p_valid = pidx < total
    p_win = jnp.clip(win_lo[p_rb] + pidx - pair_start[p_rb], 0, n_win - 1).astype(jnp.int32)
    p_first = p_valid & (pidx == pair_start[p_rb])
    p_real = p_valid & (blk_cnt[p_rb] > 0)
    g_flag = p_real.astype(jnp.int32) + 2 * p_first.astype(jnp.int32)

    xrows = moe_gather(p_rb, p_win, g_flag, row_tok[:, None], xb2, n_rows)
    yrows = moe_expert_ffn(block_e, block_valid, xrows, wg, wu, wd, row_w[:, None])

    key = jnp.where(p_real, p_win, n_win)
    order = jnp.argsort(key, stable=True)
    c_real = p_real[order]
    c_win = jnp.where(c_real, p_win[order], n_win - 1).astype(jnp.int32)
    c_rb = jnp.where(c_real, p_rb[order], 0).astype(jnp.int32)
    prev_win = jnp.concatenate([jnp.full((1,), -1, jnp.int32), c_win[:-1]])
    next_win = jnp.concatenate([c_win[1:], jnp.full((1,), -1, jnp.int32)])
    next_real = jnp.concatenate([c_real[1:], jnp.zeros((1,), bool)])
    c_first = c_real & (c_win != prev_win)
    c_last = c_real & ((c_win != next_win) | jnp.logical_not(next_real))
    c_flag = c_real.astype(jnp.int32) + 2 * c_first.astype(jnp.int32) + 4 * c_last.astype(jnp.int32)
    row_tok_row = row_tok.reshape(n_blocks, 1, MOE_TM)
    return moe_combine(c_rb, c_win, c_flag, row_tok_row, yrows, x2, ln_g, ln_b)


def _w_in_perm():
    zero = 3480
    perm = np.full((D_IN_P,), zero, np.int64)
    o_qkv, o_z, o_beta, o_decay, o_bq, o_bk, o_bv, o_cf, o_ci, o_cq, o_cg = (
        0, 1152, 1536, 1548, 1560, 1944, 2072, 2200, 2712, 2968, 3224)
    perm[COL_A_QKV:COL_A_QKV + 1152] = np.arange(o_qkv, o_qkv + 1152)
    perm[COL_A_Z:COL_A_Z + 384] = np.arange(o_z, o_z + 384)
    for j in range(3):
        for a in range(2):
            dst = COL_B_Q + j * LANES + a * HEAD_DIM
            src = o_bq + (j + 3 * a) * HEAD_DIM
            perm[dst:dst + HEAD_DIM] = np.arange(src, src + HEAD_DIM)
    perm[COL_B_K:COL_B_K + 128] = np.arange(o_bk, o_bk + 128)
    perm[COL_C_G:COL_C_G + 256] = np.arange(o_cg, o_cg + 256)
    perm[COL_B_V:COL_B_V + 128] = np.arange(o_bv, o_bv + 128)
    perm[COL_C_F:COL_C_F + 512] = np.arange(o_cf, o_cf + 512)
    perm[COL_C_I:COL_C_I + 256] = np.arange(o_ci, o_ci + 256)
    perm[COL_C_Q:COL_C_Q + 256] = np.arange(o_cq, o_cq + 256)
    for d in range(2):
        for p in range(A_HEADS // 2):
            dst = COL_GATES + d * LANES + 4 * p
            perm[dst + 0] = o_beta + d * A_HEADS + 2 * p
            perm[dst + 1] = o_beta + d * A_HEADS + 2 * p + 1
            perm[dst + 2] = o_decay + d * A_HEADS + 2 * p
            perm[dst + 3] = o_decay + d * A_HEADS + 2 * p + 1
    return perm


def _w_out_perm():
    perm = np.arange(D_MIX)
    for j in range(3):
        for a in range(2):
            dst = A_WIDTH + j * LANES + a * HEAD_DIM
            src = A_WIDTH + (j + 3 * a) * HEAD_DIM
            perm[dst:dst + HEAD_DIM] = np.arange(src, src + HEAD_DIM)
    return perm


def _gdn_params(a_log_d, dt_bias_d):
    pairs = A_HEADS // 2
    prm = jnp.zeros((pairs, 8, LANES), F32)
    prm = prm.at[:, 0, 0:2].set(a_log_d.reshape(pairs, 2).astype(F32))
    return prm.at[:, 1, 0:2].set(dt_bias_d.reshape(pairs, 2).astype(F32))


def kernel(x, mem, w_in, conv_w, gdn_a_log, gdn_dt_bias, gdn_norm_w, q_norm_w, k_norm_w, hgrn_lb_logits,
           hgrn_norm_w, w_out, ln1_g, ln1_b, xq, xk, xv, xo, ln2_g, ln2_b, ffn_wg, ffn_wu, ffn_wd,
           moe_router, moe_wg, moe_wu, moe_wd, ln3_g, ln3_b):
    b, s, d = x.shape
    m = b * s
    cos_t, sin_t = rope_tables(s)
    lb_p = jax.nn.softmax(hgrn_lb_logits.astype(F32), axis=0)
    lb_c = jnp.cumsum(lb_p, axis=0)
    lower_bounds = lb_c - lb_c[0]
    in_perm = _w_in_perm()
    out_perm = _w_out_perm()
    mem_b = mem.astype(BF16).reshape(b * mem.shape[1], d)
    tile2 = lambda w: jnp.tile(w.astype(F32), 2)[None, :]
    row = lambda v: v.astype(F32)[None, :]

    x2 = x.reshape(m, d)
    xb2 = x2.astype(BF16)
    for l in range(DEPTH):
        w_in_p = jnp.take(jnp.pad(w_in[l], ((0, 0), (0, 1))), in_perm, axis=1).astype(BF16)
        proj2 = matmul(xb2, w_in_p, F32, 512, 768)
        proj3 = proj2.reshape(b, s, D_IN_P)

        oa_f = gdn_direction(proj3, 0, COL_GATES // LANES, conv_w[l],
                             _gdn_params(gdn_a_log[l, 0], gdn_dt_bias[l, 0]), s)
        src_b = jnp.flip(jnp.concatenate([proj3[..., :3 * A_WIDTH],
                                          proj3[..., COL_GATES + LANES:COL_GATES + 2 * LANES]], axis=-1), axis=1)
        oa_b = jnp.flip(gdn_direction(src_b, 0, 3 * A_WIDTH // LANES, conv_w[l][::-1],
                                      _gdn_params(gdn_a_log[l, 1], gdn_dt_bias[l, 1]), s), axis=1)

        ob = gqa_attention(proj3, cos_t, sin_t, tile2(q_norm_w[l]), tile2(k_norm_w[l]), s)

        lb = lower_bounds[l].reshape(C_HEADS // 2, 1, LANES)
        oc_f = hgrn_direction(proj3, COL_C_F // LANES, COL_C_I // LANES, COL_C_Q // LANES, lb, s)
        src_c = jnp.flip(proj3[..., COL_C_F + C_WIDTH:COL_C_Q + C_WIDTH], axis=1)
        oc_b = jnp.flip(hgrn_direction(src_c, 0, 2, 4, lb, s), axis=1)

        w_out_p = jnp.take(w_out[l], out_perm, axis=0).astype(BF16)
        x2, xb2 = mixer_output(oa_f.reshape(m, A_WIDTH), oa_b.reshape(m, A_WIDTH), proj2,
                               ob.reshape(m, B_WIDTH), oc_f.reshape(m, C_WIDTH), oc_b.reshape(m, C_WIDTH),
                               x2, tile2(gdn_norm_w[l]), tile2(hgrn_norm_w[l]), w_out_p,
                               row(ln1_g[l]), row(ln1_b[l]))

        k3 = matmul(mem_b, xk[l].astype(BF16), BF16, 256, 512).reshape(b, -1, d)
        v3 = matmul(mem_b, xv[l].astype(BF16), BF16, 256, 512).reshape(b, -1, d)
        x3, xb3 = cross_attention(xb2.reshape(b, s, d), x2.reshape(b, s, d), k3, v3,
                                  xq[l].astype(BF16), xo[l].astype(BF16), row(ln2_g[l]), row(ln2_b[l]))
        x2, xb2 = x3.reshape(m, d), xb3.reshape(m, d)

        if l % 2 == 0:
            x2, xb2 = dense_ffn(xb2, x2, ffn_wg[l // 2].astype(BF16), ffn_wu[l // 2].astype(BF16),
                                ffn_wd[l // 2].astype(BF16), row(ln3_g[l]), row(ln3_b[l]))
        else:
            x2 = moe_layer(xb2, x2, moe_router[l // 2], moe_wg[l // 2].astype(BF16),
                           moe_wu[l // 2].astype(BF16), moe_wd[l // 2].astype(BF16),
                           row(ln3_g[l]), row(ln3_b[l]))
            xb2 = x2.astype(BF16)
    return x2.reshape(b, s, d)
```

```python
import functools

import numpy as np
import jax
import jax.numpy as jnp
from jax import lax
from jax.experimental import pallas as pl
from jax.experimental.pallas import tpu as pltpu

F32 = jnp.float32
BF16 = jnp.bfloat16

D_MODEL = 1024
DEPTH = 2
HEAD_DIM = 64
A_HEADS = 6
A_WIDTH = A_HEADS * HEAD_DIM
B_Q_HEADS = 6
B_KV_HEADS = 2
B_WIDTH = B_Q_HEADS * HEAD_DIM
C_HEADS = 4
C_WIDTH = C_HEADS * HEAD_DIM
D_MIX = A_WIDTH + B_WIDTH + C_WIDTH
CONV_K = 5
CHUNK = 64
GRID_W = 64
ROPE_AXIS_DIM = HEAD_DIM // 2
ROPE_THETA = 10000.0
X_HEADS = 4
X_HEAD_DIM = D_MODEL // X_HEADS
D_FF = 2816
N_EXPERTS = 8
TOP_K = 2
D_FF_EXPERT = 3584
DN_ALPHA = (2 * DEPTH) ** 0.25
LN_EPS = 1e-5
RMS_EPS = 1e-6

LANES = 128
SUBLANES = 8
GROUP = 256
CHUNKS_PER_GROUP = GROUP // CHUNK
VMEM_LIMIT = 56 * 1024 * 1024

COL_A_QKV = 0
COL_A_Z = 1152
COL_B_Q = 1536
COL_B_K = 1920
COL_C_G = 2048
COL_B_V = 2304
COL_GATES0 = 2432
COL_C_F = 2560
COL_C_I = 3072
COL_C_Q = 3328
COL_GATES1 = 3584
D_IN_P = 3840


def _cparams(sem):
    return pltpu.CompilerParams(dimension_semantics=sem, vmem_limit_bytes=VMEM_LIMIT)


def _sigmoid(x):
    return 1.0 / (1.0 + jnp.exp(-x))


def _softplus(x):
    return jnp.maximum(x, 0.0) + jnp.log1p(jnp.exp(-jnp.abs(x)))


def _dot(a, b):
    return jnp.dot(a.astype(BF16), b.astype(BF16), preferred_element_type=F32)


def _dot_nt(a, b):
    return lax.dot_general(a.astype(BF16), b.astype(BF16), (((1,), (1,)), ((), ())),
                           preferred_element_type=F32)


def _dot_sel(m01, x):
    m = jnp.where(m01, 1.0, 0.0).astype(BF16)
    hi = x.astype(BF16)
    r1 = x - hi.astype(F32)
    mid = r1.astype(BF16)
    lo = (r1 - mid.astype(F32)).astype(BF16)
    out = jnp.dot(m, hi, preferred_element_type=F32)
    out = out + jnp.dot(m, mid, preferred_element_type=F32)
    return out + jnp.dot(m, lo, preferred_element_type=F32)


def _layer_norm_rows(y, g, b):
    mu = jnp.mean(y, axis=-1, keepdims=True)
    yc = y - mu
    var = jnp.mean(yc * yc, axis=-1, keepdims=True)
    return yc * lax.rsqrt(var + LN_EPS) * g + b


def _iota2(shape, dim):
    return lax.broadcasted_iota(jnp.int32, shape, dim)


def _head_sum(x, first_head):
    s0 = jnp.sum(jnp.where(first_head, x, 0.0), axis=-1, keepdims=True)
    s1 = jnp.sum(jnp.where(first_head, 0.0, x), axis=-1, keepdims=True)
    return jnp.where(first_head, s0, s1)


def _block_rows(t, block, row_in_block):
    parts = [jnp.broadcast_to(t[b * block + row_in_block:b * block + row_in_block + 1, :], (block, t.shape[1]))
             for b in range(GROUP // block)]
    return jnp.concatenate(parts, axis=0)


def _mm_kernel(x_ref, w_ref, o_ref):
    o_ref[...] = jnp.dot(x_ref[...], w_ref[...], preferred_element_type=F32).astype(o_ref.dtype)


def matmul(x, w, out_dtype, tm, tn):
    m, k = x.shape
    n = w.shape[1]
    return pl.pallas_call(
        _mm_kernel,
        out_shape=jax.ShapeDtypeStruct((m, n), out_dtype),
        grid=(m // tm, n // tn),
        in_specs=[pl.BlockSpec((tm, k), lambda i, j: (i, 0)),
                  pl.BlockSpec((k, tn), lambda i, j: (0, j))],
        out_specs=pl.BlockSpec((tm, tn), lambda i, j: (i, j)),
        compiler_params=_cparams(("parallel", "arbitrary")),
        name="matmul",
    )(x, w)


def _scan_masks(reverse):
    ri = _iota2((GROUP, GROUP), 0)
    ci = _iota2((GROUP, GROUP), 1)
    same_chunk = (ri >> 6) == (ci >> 6)
    if reverse:
        return ri, ci, same_chunk, same_chunk & (ri <= ci), same_chunk & (ri < ci)
    return ri, ci, same_chunk, same_chunk & (ri >= ci), same_chunk & (ri > ci)


def _gdn_pair(q, k, v, beta_t, g_t, state, masks, reverse):
    _, _, same_chunk, incl, strict = masks
    lane = _iota2((GROUP, LANES), 1)
    h0 = lane < HEAD_DIM
    q = q * lax.rsqrt(_head_sum(q * q, h0) + RMS_EPS) * (HEAD_DIM ** -0.5)
    k = k * lax.rsqrt(_head_sum(k * k, h0) + RMS_EPS)

    gc = _dot_sel(incl, g_t)
    gc_t = gc.T
    gedge = _block_rows(gc, CHUNK, 0 if reverse else CHUNK - 1)
    exp_g = jnp.exp(gc)

    kb = k * beta_t
    rhs = jnp.concatenate([v * beta_t, kb * exp_g], axis=1).astype(BF16)
    k16 = k.astype(BF16)

    sr = _iota2((CHUNK, GROUP), 0)
    sc = _iota2((CHUNK, GROUP), 1)
    eye_side = jnp.where(sr == (sc & (CHUNK - 1)), 1.0, 0.0)

    def block_diag(side):
        return jnp.where(same_chunk, jnp.concatenate([side] * CHUNKS_PER_GROUP, axis=0), 0.0)

    sols, attns = [], []
    for h in range(2):
        mh = h0 if h == 0 else jnp.logical_not(h0)
        col = gc[:, 0:1] if h == 0 else gc[:, HEAD_DIM:HEAD_DIM + 1]
        row = gc_t[0:1, :] if h == 0 else gc_t[HEAD_DIM:HEAD_DIM + 1, :]
        decay = jnp.exp(jnp.where(incl, col - row, -1e30))
        kk = _dot_nt(jnp.where(mh, kb, 0.0), k16)
        low = jnp.where(strict, kk * decay, 0.0)
        a = -(low[0:CHUNK] + low[CHUNK:2 * CHUNK] + low[2 * CHUNK:3 * CHUNK] + low[3 * CHUNK:4 * CHUNK])
        x = eye_side + a
        pw = a
        for _ in range(5):
            pw = _dot(pw, block_diag(pw))
            x = x + _dot(x, block_diag(pw))
        sols.append(jnp.dot(block_diag(x).astype(BF16), rhs, preferred_element_type=F32))
        attns.append(jnp.where(incl, _dot_nt(jnp.where(mh, q, 0.0), k16) * decay, 0.0).astype(BF16))

    u = jnp.where(h0, sols[0][:, :LANES], sols[1][:, :LANES])
    w = jnp.where(h0, sols[0][:, LANES:], sols[1][:, LANES:])
    qd = q * exp_g
    kd = k * jnp.exp(gedge - gc)
    kd_t = kd.T

    tl = _iota2((LANES, GROUP), 1)
    br = _iota2((LANES, LANES), 0)
    bc = _iota2((LANES, LANES), 1)
    bd128 = (br >> 6) == (bc >> 6)
    h0c = _iota2((CHUNK, LANES), 1) < HEAD_DIM

    v_new_parts = [jnp.zeros((CHUNK, LANES), F32)] * CHUNKS_PER_GROUP
    outs = [None] * CHUNKS_PER_GROUP
    order = range(CHUNKS_PER_GROUP - 1, -1, -1) if reverse else range(CHUNKS_PER_GROUP)
    for c in order:
        rs = slice(c * CHUNK, (c + 1) * CHUNK)
        r = _dot(jnp.concatenate([qd[rs], w[rs]], axis=0), state)
        v_new_parts[c] = u[rs] - r[CHUNK:]
        vn32 = jnp.concatenate(v_new_parts, axis=0)
        vn = vn32.astype(BF16)
        o0 = jnp.dot(attns[0][rs, :], vn, preferred_element_type=F32)
        o1 = jnp.dot(attns[1][rs, :], vn, preferred_element_type=F32)
        outs[c] = r[:CHUNK] + jnp.where(h0c, o0, o1)
        upd = jnp.dot(jnp.where((tl >> 6) == c, kd_t, 0.0), vn32, preferred_element_type=F32)
        cd = jnp.exp(gedge[c * CHUNK:c * CHUNK + 1, :])
        state = state * cd + jnp.where(bd128, upd, 0.0)
    return jnp.concatenate(outs, axis=0), state


def _gdn_kernel(x_ref, xp_ref, xn_ref, cw_ref, g_ref, prm_ref, o_ref, s_scr, *, reverse):
    n = pl.program_id(1)
    n_groups = pl.num_programs(1)
    gi = n_groups - 1 - n if reverse else n

    @pl.when(n == 0)
    def _():
        s_scr[...] = jnp.zeros_like(s_scr)

    prev = jnp.where(gi > 0, xp_ref[0], 0.0)
    nxt = jnp.where(gi < n_groups - 1, xn_ref[0], 0.0)
    ext = jnp.concatenate([prev, x_ref[0], nxt], axis=0)
    cw = cw_ref[...]
    base = SUBLANES - CONV_K // 2
    acc = ext[base:base + GROUP, :] * cw[0:1, :]
    for t in range(1, CONV_K):
        acc = acc + ext[base + t:base + t + GROUP, :] * cw[t:t + 1, :]
    qkv = acc * _sigmoid(acc)

    gt = g_ref[0]
    prm = prm_ref[...]
    lane = _iota2((GROUP, LANES), 1)
    h0 = lane < HEAD_DIM
    masks = _scan_masks(reverse)

    def gate_col(j):
        return jnp.sum(jnp.where(lane == j, gt, 0.0), axis=-1, keepdims=True)

    for p in range(A_HEADS // 2):
        sl = lambda part: slice(part * A_WIDTH + p * LANES, part * A_WIDTH + (p + 1) * LANES)
        beta_t = jnp.where(h0, _sigmoid(gate_col(4 * p)), _sigmoid(gate_col(4 * p + 1)))
        g0 = -jnp.exp(prm[0:1, 2 * p:2 * p + 1]) * _softplus(gate_col(4 * p + 2) + prm[1:2, 2 * p:2 * p + 1])
        g1 = -jnp.exp(prm[0:1, 2 * p + 1:2 * p + 2]) * _softplus(gate_col(4 * p + 3) + prm[1:2, 2 * p + 1:2 * p + 2])
        out, state = _gdn_pair(qkv[:, sl(0)], qkv[:, sl(1)], qkv[:, sl(2)], beta_t, jnp.where(h0, g0, g1),
                               s_scr[p], masks, reverse)
        o_ref[0, :, p * LANES:(p + 1) * LANES] = out
        s_scr[p] = state


def gdn_direction(proj3, gate_blk, conv_w, prm, reverse):
    b, seq_len, _ = proj3.shape
    n_groups = seq_len // GROUP
    halo_per_group = GROUP // SUBLANES
    n_halo = seq_len // SUBLANES
    width = 3 * A_WIDTH
    gidx = (lambda n: n_groups - 1 - n) if reverse else (lambda n: n)
    return pl.pallas_call(
        functools.partial(_gdn_kernel, reverse=reverse),
        out_shape=jax.ShapeDtypeStruct((b, seq_len, A_WIDTH), F32),
        grid=(b, n_groups),
        in_specs=[pl.BlockSpec((1, GROUP, width), lambda i, n: (i, gidx(n), 0)),
                  pl.BlockSpec((1, SUBLANES, width),
                               lambda i, n: (i, jnp.maximum(gidx(n) * halo_per_group - 1, 0), 0)),
                  pl.BlockSpec((1, SUBLANES, width),
                               lambda i, n: (i, jnp.minimum((gidx(n) + 1) * halo_per_group, n_halo - 1), 0)),
                  pl.BlockSpec((CONV_K, width), lambda i, n: (0, 0)),
                  pl.BlockSpec((1, GROUP, LANES), lambda i, n: (i, gidx(n), gate_blk)),
                  pl.BlockSpec((SUBLANES, LANES), lambda i, n: (0, 0))],
        out_specs=pl.BlockSpec((1, GROUP, A_WIDTH), lambda i, n: (i, gidx(n), 0)),
        scratch_shapes=[pltpu.VMEM((A_HEADS // 2, LANES, LANES), F32)],
        compiler_params=_cparams(("parallel", "arbitrary")),
        name="gdn_scan",
    )(proj3, proj3, proj3, conv_w, proj3, prm)


EXP_CLAMP = 60.0


def _hgrn_pair(f_raw, v, q_raw, lb, state_t, masks, reverse):
    ri, ci, same_chunk, incl, _ = masks
    f = lb + (1.0 - lb) * _sigmoid(f_raw)
    lf = jnp.log(f)
    kk = 1.0 - f
    q = q_raw * _sigmoid(q_raw) * (HEAD_DIM ** -0.5)
    lane = _iota2((GROUP, LANES), 1)
    h0 = lane < HEAD_DIM

    cum = _dot_sel(incl, lf)
    cedge = _block_rows(cum, CHUNK, 0 if reverse else CHUNK - 1)
    qd = q * jnp.exp(cum)
    kd = kk * jnp.exp(cedge - cum)

    if reverse:
        m32 = same_chunk & ((ri & 63) < 32) & ((ci & 63) >= 32)
        m16 = ((ri >> 5) == (ci >> 5)) & ((ri & 31) < 16) & ((ci & 31) >= 16)
        mdg = ((ri >> 4) == (ci >> 4)) & (ri <= ci)
        ref32, ref16, refdg = _block_rows(cum, 64, 32), _block_rows(cum, 32, 16), _block_rows(cum, 16, 8)
    else:
        m32 = same_chunk & ((ri & 63) >= 32) & ((ci & 63) < 32)
        m16 = ((ri >> 5) == (ci >> 5)) & ((ri & 31) >= 16) & ((ci & 31) < 16)
        mdg = ((ri >> 4) == (ci >> 4)) & (ri >= ci)
        ref32, ref16, refdg = _block_rows(cum, 64, 31), _block_rows(cum, 32, 15), _block_rows(cum, 16, 7)
    qs32 = q * jnp.exp(jnp.minimum(cum - ref32, 0.0))
    ks32 = (kk * jnp.exp(jnp.minimum(ref32 - cum, 0.0))).astype(BF16)
    qs16 = q * jnp.exp(jnp.minimum(cum - ref16, 0.0))
    ks16 = (kk * jnp.exp(jnp.minimum(ref16 - cum, 0.0))).astype(BF16)
    qsdg = q * jnp.exp(jnp.clip(cum - refdg, -EXP_CLAMP, EXP_CLAMP))
    ksdg = (kk * jnp.exp(jnp.clip(refdg - cum, -EXP_CLAMP, EXP_CLAMP))).astype(BF16)

    v16 = v.astype(BF16)
    intra = []
    for h in range(2):
        mh = h0 if h == 0 else jnp.logical_not(h0)
        a = jnp.where(m32, _dot_nt(jnp.where(mh, qs32, 0.0), ks32),
                      jnp.where(m16, _dot_nt(jnp.where(mh, qs16, 0.0), ks16),
                                jnp.where(mdg, _dot_nt(jnp.where(mh, qsdg, 0.0), ksdg), 0.0)))
        intra.append(jnp.dot(a.astype(BF16), v16, preferred_element_type=F32))
    intra = jnp.where(h0, intra[0], intra[1])

    v_t = v.T
    tl = _iota2((LANES, GROUP), 1)
    br = _iota2((LANES, LANES), 0)
    bc = _iota2((LANES, LANES), 1)
    bd128 = (br >> 6) == (bc >> 6)

    outs = [None] * CHUNKS_PER_GROUP
    order = range(CHUNKS_PER_GROUP - 1, -1, -1) if reverse else range(CHUNKS_PER_GROUP)
    for c in order:
        rs = slice(c * CHUNK, (c + 1) * CHUNK)
        outs[c] = _dot_nt(qd[rs], state_t) + intra[rs]
        upd = jnp.dot(jnp.where((tl >> 6) == c, v_t, 0.0), kd, preferred_element_type=F32)
        cd = jnp.exp(cedge[c * CHUNK:c * CHUNK + 1, :])
        state_t = state_t * cd + jnp.where(bd128, upd, 0.0)
    return jnp.concatenate(outs, axis=0), state_t


def _hgrn_kernel(f_ref, i_ref, q_ref, lb_ref, o_ref, s_scr, *, reverse):
    n = pl.program_id(1)

    @pl.when(n == 0)
    def _():
        s_scr[...] = jnp.zeros_like(s_scr)

    masks = _scan_masks(reverse)
    for p in range(C_HEADS // 2):
        sl = slice(p * LANES, (p + 1) * LANES)
        out, state_t = _hgrn_pair(f_ref[0, :, sl], i_ref[0, :, sl], q_ref[0, :, sl], lb_ref[:, sl],
                                  s_scr[p], masks, reverse)
        o_ref[0, :, sl] = out
        s_scr[p] = state_t


def hgrn_direction(proj3, f_blk, lb, reverse):
    b, seq_len, _ = proj3.shape
    n_groups = seq_len // GROUP
    gidx = (lambda n: n_groups - 1 - n) if reverse else (lambda n: n)
    spec = lambda blk: pl.BlockSpec((1, GROUP, C_WIDTH), lambda i, n: (i, gidx(n), blk))
    return pl.pallas_call(
        functools.partial(_hgrn_kernel, reverse=reverse),
        out_shape=jax.ShapeDtypeStruct((b, seq_len, C_WIDTH), F32),
        grid=(b, n_groups),
        in_specs=[spec(f_blk), spec(COL_C_I // C_WIDTH), spec(COL_C_Q // C_WIDTH),
                  pl.BlockSpec((1, C_WIDTH), lambda i, n: (0, 0))],
        out_specs=pl.BlockSpec((1, GROUP, C_WIDTH), lambda i, n: (i, gidx(n), 0)),
        scratch_shapes=[pltpu.VMEM((C_HEADS // 2, LANES, LANES), F32)],
        compiler_params=_cparams(("parallel", "arbitrary")),
        name="hgrn_scan",
    )(proj3, proj3, proj3, lb)


ATT_TQ = 256


def _norm_rope(x, w, cos, sin_signed, first_half16):
    lane = _iota2(x.shape, 1)
    h0 = lane < HEAD_DIM
    ms = _head_sum(x * x, h0) * (1.0 / HEAD_DIM)
    xn = x * lax.rsqrt(ms + RMS_EPS) * w
    partner = jnp.where(first_half16, pltpu.roll(xn, LANES - 16, 1), pltpu.roll(xn, 16, 1))
    return xn * cos + partner * sin_signed


def _attn_kernel(q_ref, k_ref, v_ref, cosq_ref, sinq_ref, cosk_ref, sink_ref, qw_ref, kw_ref, o_ref,
                 k_scr, v_scr):
    qi = pl.program_id(1)

    @pl.when(qi == 0)
    def _():
        kx = k_ref[0]
        lane_k = _iota2(kx.shape, 1)
        k_scr[...] = _norm_rope(kx, kw_ref[...], cosk_ref[...], sink_ref[...], (lane_k & 31) < 16).astype(BF16)
        v_scr[...] = v_ref[0].astype(BF16)

    lane = _iota2((ATT_TQ, LANES), 1)
    h0 = lane < HEAD_DIM
    fh = (lane & 31) < 16
    cos = cosq_ref[...]
    sin = sinq_ref[...]
    kmat = k_scr[...]
    vmat = v_scr[...]
    for j in range(B_Q_HEADS // 2):
        qn = _norm_rope(q_ref[0, :, j * LANES:(j + 1) * LANES], qw_ref[...], cos, sin, fh) * (HEAD_DIM ** -0.5)
        outs = []
        for half in range(2):
            mh = h0 if half == 0 else jnp.logical_not(h0)
            s = _dot_nt(jnp.where(mh, qn, 0.0), kmat)
            m = jnp.max(s, axis=-1, keepdims=True)
            e = jnp.exp(s - m)
            l = jnp.sum(e, axis=-1, keepdims=True)
            pv = jnp.dot(e.astype(BF16), vmat, preferred_element_type=F32)
            outs.append(pv / l)
        o_ref[0, :, j * LANES:(j + 1) * LANES] = jnp.where(h0, outs[0], outs[1])


def gqa_attention(proj, cos_t, sin_t, q_w, k_w, seq_len):
    b = proj.shape[0]
    return pl.pallas_call(
        _attn_kernel,
        out_shape=jax.ShapeDtypeStruct((b, seq_len, B_WIDTH), F32),
        grid=(b, seq_len // ATT_TQ),
        in_specs=[pl.BlockSpec((1, ATT_TQ, B_WIDTH), lambda i, t: (i, t, COL_B_Q // B_WIDTH)),
                  pl.BlockSpec((1, seq_len, LANES), lambda i, t: (i, 0, COL_B_K // LANES)),
                  pl.BlockSpec((1, seq_len, LANES), lambda i, t: (i, 0, COL_B_V // LANES)),
                  pl.BlockSpec((ATT_TQ, LANES), lambda i, t: (t, 0)),
                  pl.BlockSpec((ATT_TQ, LANES), lambda i, t: (t, 0)),
                  pl.BlockSpec((seq_len, LANES), lambda i, t: (0, 0)),
                  pl.BlockSpec((seq_len, LANES), lambda i, t: (0, 0)),
                  pl.BlockSpec((1, LANES), lambda i, t: (0, 0)),
                  pl.BlockSpec((1, LANES), lambda i, t: (0, 0))],
        out_specs=pl.BlockSpec((1, ATT_TQ, B_WIDTH), lambda i, t: (i, t, 0)),
        scratch_shapes=[pltpu.VMEM((seq_len, LANES), BF16), pltpu.VMEM((seq_len, LANES), BF16)],
        compiler_params=_cparams(("parallel", "arbitrary")),
        name="gqa_attention",
    )(proj, proj, proj, cos_t, sin_t, cos_t, sin_t, q_w, k_w)


def rope_tables(seq_len):
    rows = seq_len // GRID_W
    row = jnp.repeat(jnp.arange(rows, dtype=F32), GRID_W)
    col = jnp.tile(jnp.arange(GRID_W, dtype=F32), rows)
    inv_freq = ROPE_THETA ** (-jnp.arange(0, ROPE_AXIS_DIM, 2, dtype=F32) / ROPE_AXIS_DIM)
    ang_r = row[:, None] * inv_freq
    ang_c = col[:, None] * inv_freq
    cos64 = jnp.concatenate([jnp.cos(ang_r), jnp.cos(ang_r), jnp.cos(ang_c), jnp.cos(ang_c)], axis=1)
    sin64 = jnp.concatenate([-jnp.sin(ang_r), jnp.sin(ang_r), -jnp.sin(ang_c), jnp.sin(ang_c)], axis=1)
    return jnp.tile(cos64, (1, 2)), jnp.tile(sin64, (1, 2))


MIX_TM = 256


def _mixout_kernel(oaf_ref, oab_ref, z_ref, ob_ref, ocf_ref, ocb_ref, g_ref, x_ref, wa_ref, wc_ref,
                   wo_ref, lg_ref, lb_ref, xo_ref, xb_ref):
    lane = _iota2((MIX_TM, LANES), 1)
    h0 = lane < HEAD_DIM
    parts = []
    for j in range(A_WIDTH // LANES):
        sl = slice(j * LANES, (j + 1) * LANES)
        o = oaf_ref[:, sl] + oab_ref[:, sl]
        ms = _head_sum(o * o, h0) * (1.0 / HEAD_DIM)
        z = z_ref[:, sl]
        parts.append((o * lax.rsqrt(ms + RMS_EPS) * wa_ref[...] * (z * _sigmoid(z))).astype(BF16))
    parts.append(ob_ref[...].astype(BF16))
    for j in range(C_WIDTH // LANES):
        sl = slice(j * LANES, (j + 1) * LANES)
        o = ocf_ref[:, sl] + ocb_ref[:, sl]
        ms = _head_sum(o * o, h0) * (1.0 / HEAD_DIM)
        parts.append((o * lax.rsqrt(ms + RMS_EPS) * wc_ref[...] * _sigmoid(g_ref[:, sl])).astype(BF16))
    mixed = jnp.concatenate(parts, axis=1)
    h = jnp.dot(mixed, wo_ref[...], preferred_element_type=F32)
    y = _layer_norm_rows(DN_ALPHA * x_ref[...] + h, lg_ref[...], lb_ref[...])
    xo_ref[...] = y
    xb_ref[...] = y.astype(BF16)


def mixer_output(oa_f, oa_b, proj2, ob, oc_f, oc_b, x2, gdn_w, hgrn_w, w_out, ln_g, ln_b):
    m = x2.shape[0]
    row = lambda w: pl.BlockSpec((MIX_TM, w), lambda i: (i, 0))
    const = lambda r, c: pl.BlockSpec((r, c), lambda i: (0, 0))
    return pl.pallas_call(
        _mixout_kernel,
        out_shape=(jax.ShapeDtypeStruct((m, D_MODEL), F32), jax.ShapeDtypeStruct((m, D_MODEL), BF16)),
        grid=(m // MIX_TM,),
        in_specs=[row(A_WIDTH), row(A_WIDTH),
                  pl.BlockSpec((MIX_TM, A_WIDTH), lambda i: (i, COL_A_Z // A_WIDTH)),
                  row(B_WIDTH), row(C_WIDTH), row(C_WIDTH),
                  pl.BlockSpec((MIX_TM, C_WIDTH), lambda i: (i, COL_C_G // C_WIDTH)),
                  row(D_MODEL), const(1, LANES), const(1, LANES),
                  const(D_MIX, D_MODEL), const(1, D_MODEL), const(1, D_MODEL)],
        out_specs=(row(D_MODEL), row(D_MODEL)),
        compiler_params=_cparams(("parallel",)),
        name="mixer_output",
    )(oa_f, oa_b, proj2, ob, oc_f, oc_b, proj2, x2, gdn_w, hgrn_w, w_out, ln_g, ln_b)


XATT_TM = 512


def _xattn_kernel(xb_ref, x_ref, k_ref, v_ref, wq_ref, wo_ref, lg_ref, lb_ref, xo_ref, xob_ref):
    q = jnp.dot(xb_ref[0], wq_ref[...], preferred_element_type=F32) * (X_HEAD_DIM ** -0.5)
    outs = []
    for h in range(X_HEADS):
        sl = slice(h * X_HEAD_DIM, (h + 1) * X_HEAD_DIM)
        s = _dot_nt(q[:, sl], k_ref[0, :, sl])
        m = jnp.max(s, axis=-1, keepdims=True)
        e = jnp.exp(s - m)
        l = jnp.sum(e, axis=-1, keepdims=True)
        outs.append((jnp.dot(e.astype(BF16), v_ref[0, :, sl], preferred_element_type=F32) / l).astype(BF16))
    o = jnp.concatenate(outs, axis=1)
    c = jnp.dot(o, wo_ref[...], preferred_element_type=F32)
    y = _layer_norm_rows(DN_ALPHA * x_ref[0] + c, lg_ref[...], lb_ref[...])
    xo_ref[0] = y
    xob_ref[0] = y.astype(BF16)


def cross_attention(xb3, x3, k3, v3, wq, wo, ln_g, ln_b):
    b, s, _ = x3.shape
    mem = k3.shape[1]
    row = pl.BlockSpec((1, XATT_TM, D_MODEL), lambda i, t: (i, t, 0))
    const = lambda r, c: pl.BlockSpec((r, c), lambda i, t: (0, 0))
    kv = pl.BlockSpec((1, mem, D_MODEL), lambda i, t: (i, 0, 0))
    return pl.pallas_call(
        _xattn_kernel,
        out_shape=(jax.ShapeDtypeStruct((b, s, D_MODEL), F32), jax.ShapeDtypeStruct((b, s, D_MODEL), BF16)),
        grid=(b, s // XATT_TM),
        in_specs=[row, row, kv, kv, const(D_MODEL, D_MODEL), const(D_MODEL, D_MODEL),
                  const(1, D_MODEL), const(1, D_MODEL)],
        out_specs=(row, row),
        compiler_params=_cparams(("parallel", "parallel")),
        name="cross_attention",
    )(xb3, x3, k3, v3, wq, wo, ln_g, ln_b)


FFN_TM = 512
FFN_TF = 1408


def _ffn_kernel(xb_ref, x_ref, wg_ref, wu_ref, wd_ref, lg_ref, lb_ref, xo_ref, xob_ref, acc_ref):
    j = pl.program_id(1)

    @pl.when(j == 0)
    def _():
        acc_ref[...] = jnp.zeros_like(acc_ref)

    xb = xb_ref[...]
    g = jnp.dot(xb, wg_ref[...], preferred_element_type=F32)
    u = jnp.dot(xb, wu_ref[...], preferred_element_type=F32)
    h = (g * _sigmoid(g) * u).astype(BF16)
    acc_ref[...] += jnp.dot(h, wd_ref[...], preferred_element_type=F32)

    @pl.when(j == pl.num_programs(1) - 1)
    def _():
        y = _layer_norm_rows(DN_ALPHA * x_ref[...] + acc_ref[...], lg_ref[...], lb_ref[...])
        xo_ref[...] = y
        xob_ref[...] = y.astype(BF16)


def dense_ffn(xb2, x2, wg, wu, wd, ln_g, ln_b):
    m = x2.shape[0]
    row = pl.BlockSpec((FFN_TM, D_MODEL), lambda i, j: (i, 0))
    const = pl.BlockSpec((1, D_MODEL), lambda i, j: (0, 0))
    return pl.pallas_call(
        _ffn_kernel,
        out_shape=(jax.ShapeDtypeStruct((m, D_MODEL), F32), jax.ShapeDtypeStruct((m, D_MODEL), BF16)),
        grid=(m // FFN_TM, D_FF // FFN_TF),
        in_specs=[row, row,
                  pl.BlockSpec((D_MODEL, FFN_TF), lambda i, j: (0, j)),
                  pl.BlockSpec((D_MODEL, FFN_TF), lambda i, j: (0, j)),
                  pl.BlockSpec((FFN_TF, D_MODEL), lambda i, j: (j, 0)),
                  const, const],
        out_specs=(row, row),
        scratch_shapes=[pltpu.VMEM((FFN_TM, D_MODEL), F32)],
        compiler_params=_cparams(("parallel", "arbitrary")),
        name="dense_ffn",
    )(xb2, x2, wg, wu, wd, ln_g, ln_b)


MOE_TM = 512
MOE_WIN = 512
MOE_TF = 512
ROUTER_TM = 512


def _router_kernel(xb_ref, wr_ref, e_ref, w_ref):
    logits = jnp.dot(xb_ref[...], wr_ref[...], preferred_element_type=F32)
    lane = _iota2(logits.shape, 1)
    neg = jnp.float32(-jnp.inf)
    lane_f = lane.astype(F32)
    lg = jnp.where(lane < N_EXPERTS, logits, neg)
    m1 = jnp.max(lg, axis=-1, keepdims=True)
    i1 = jnp.min(jnp.where(lg == m1, lane_f, float(LANES)), axis=-1, keepdims=True)
    lg2 = jnp.where(lane_f == i1, neg, lg)
    m2 = jnp.max(lg2, axis=-1, keepdims=True)
    i2 = jnp.min(jnp.where(lg2 == m2, lane_f, float(LANES)), axis=-1, keepdims=True)
    t = jnp.exp(m2 - m1)
    w1 = 1.0 / (1.0 + t)
    w2 = t / (1.0 + t)
    e_ref[...] = jnp.where(lane == 0, i1, jnp.where(lane == 1, i2, 0.0)).astype(jnp.int32)
    w_ref[...] = jnp.where(lane == 0, w1, jnp.where(lane == 1, w2, 0.0))


def moe_router(xb2, w_router_p):
    m = xb2.shape[0]
    return pl.pallas_call(
        _router_kernel,
        out_shape=(jax.ShapeDtypeStruct((m, LANES), jnp.int32), jax.ShapeDtypeStruct((m, LANES), F32)),
        grid=(m // ROUTER_TM,),
        in_specs=[pl.BlockSpec((ROUTER_TM, D_MODEL), lambda i: (i, 0)),
                  pl.BlockSpec((D_MODEL, LANES), lambda i: (0, 0))],
        out_specs=(pl.BlockSpec((ROUTER_TM, LANES), lambda i: (i, 0)),
                   pl.BlockSpec((ROUTER_TM, LANES), lambda i: (i, 0))),
        compiler_params=_cparams(("parallel",)),
        name="moe_router",
    )(xb2, w_router_p)


def _moe_gather_kernel(rb_ref, win_ref, flag_ref, dest_ref, x_ref, o_ref):
    p = pl.program_id(0)
    flag = flag_ref[p]

    @pl.when((flag & 2) != 0)
    def _():
        o_ref[...] = jnp.zeros_like(o_ref)

    @pl.when((flag & 1) != 0)
    def _():
        d = dest_ref[0]
        rows = _iota2((MOE_TM, MOE_WIN), 0) + rb_ref[p] * MOE_TM
        onehot = jnp.where((d[0:1, :] == rows) | (d[1:2, :] == rows), 1.0, 0.0).astype(BF16)
        o_ref[...] += jnp.dot(onehot, x_ref[...], preferred_element_type=F32).astype(BF16)


def moe_gather(pair_rb, pair_win, pair_flag, dest_rows, xb2, n_rows):
    n_pairs = pair_rb.shape[0]
    grid_spec = pltpu.PrefetchScalarGridSpec(
        num_scalar_prefetch=3,
        grid=(n_pairs,),
        in_specs=[pl.BlockSpec((1, TOP_K, MOE_WIN), lambda p, rb, win, fl: (win[p], 0, 0)),
                  pl.BlockSpec((MOE_WIN, D_MODEL), lambda p, rb, win, fl: (win[p], 0))],
        out_specs=pl.BlockSpec((MOE_TM, D_MODEL), lambda p, rb, win, fl: (rb[p], 0)),
    )
    return pl.pallas_call(
        _moe_gather_kernel,
        out_shape=jax.ShapeDtypeStruct((n_rows, D_MODEL), BF16),
        grid_spec=grid_spec,
        compiler_params=_cparams(("arbitrary",)),
        name="moe_gather",
    )(pair_rb, pair_win, pair_flag, dest_rows, xb2)


def _moe_ffn_kernel(be_ref, bv_ref, x_ref, wg_ref, wu_ref, wd_ref, o_ref, acc_ref):
    i = pl.program_id(0)
    j = pl.program_id(1)
    valid = bv_ref[i] != 0

    @pl.when(j == 0)
    def _():
        acc_ref[...] = jnp.zeros_like(acc_ref)

    @pl.when(valid)
    def _():
        xb = x_ref[...]
        g = jnp.dot(xb, wg_ref[0], preferred_element_type=F32)
        u = jnp.dot(xb, wu_ref[0], preferred_element_type=F32)
        h = (g * _sigmoid(g) * u).astype(BF16)
        acc_ref[...] += jnp.dot(h, wd_ref[0], preferred_element_type=F32)

    @pl.when(j == pl.num_programs(1) - 1)
    def _():
        o_ref[...] = acc_ref[...].astype(BF16)


def moe_expert_ffn(block_e, block_valid, xrows, wg, wu, wd):
    n_rows = xrows.shape[0]
    grid_spec = pltpu.PrefetchScalarGridSpec(
        num_scalar_prefetch=2,
        grid=(n_rows // MOE_TM, D_FF_EXPERT // MOE_TF),
        in_specs=[pl.BlockSpec((MOE_TM, D_MODEL), lambda i, j, be, bv: (i * bv[i], 0)),
                  pl.BlockSpec((1, D_MODEL, MOE_TF), lambda i, j, be, bv: (be[i], 0, j)),
                  pl.BlockSpec((1, D_MODEL, MOE_TF), lambda i, j, be, bv: (be[i], 0, j)),
                  pl.BlockSpec((1, MOE_TF, D_MODEL), lambda i, j, be, bv: (be[i], j, 0))],
        out_specs=pl.BlockSpec((MOE_TM, D_MODEL), lambda i, j, be, bv: (i, 0)),
        scratch_shapes=[pltpu.VMEM((MOE_TM, D_MODEL), F32)],
    )
    return pl.pallas_call(
        _moe_ffn_kernel,
        out_shape=jax.ShapeDtypeStruct((n_rows, D_MODEL), BF16),
        grid_spec=grid_spec,
        compiler_params=_cparams(("arbitrary", "arbitrary")),
        name="moe_expert_ffn",
    )(block_e, block_valid, xrows, wg, wu, wd)


def _moe_combine_kernel(rb_ref, win_ref, flag_ref, dest_ref, w_ref, y_ref, x_ref, lg_ref, lb_ref, xo_ref, acc_ref):
    p = pl.program_id(0)
    flag = flag_ref[p]

    @pl.when((flag & 2) != 0)
    def _():
        acc_ref[...] = jnp.zeros_like(acc_ref)

    @pl.when((flag & 1) != 0)
    def _():
        d = dest_ref[...]
        wt = w_ref[...]
        cols = _iota2((MOE_WIN, MOE_TM), 1) + rb_ref[p] * MOE_TM
        sel = (jnp.where(d[:, 0:1] == cols, wt[:, 0:1], 0.0)
               + jnp.where(d[:, 1:2] == cols, wt[:, 1:2], 0.0)).astype(BF16)
        acc_ref[...] += jnp.dot(sel, y_ref[...], preferred_element_type=F32)

    @pl.when((flag & 4) != 0)
    def _():
        xo_ref[...] = _layer_norm_rows(DN_ALPHA * x_ref[...] + acc_ref[...], lg_ref[...], lb_ref[...])


def moe_combine(pair_rb, pair_win, pair_flag, dest_cols, w_cols, yrows, x2, ln_g, ln_b):
    n_pairs = pair_rb.shape[0]
    m = x2.shape[0]
    tok = lambda p, rb, win, fl: (win[p], 0)
    grid_spec = pltpu.PrefetchScalarGridSpec(
        num_scalar_prefetch=3,
        grid=(n_pairs,),
        in_specs=[pl.BlockSpec((MOE_WIN, TOP_K), tok),
                  pl.BlockSpec((MOE_WIN, TOP_K), tok),
                  pl.BlockSpec((MOE_TM, D_MODEL), lambda p, rb, win, fl: (rb[p], 0)),
                  pl.BlockSpec((MOE_WIN, D_MODEL), tok),
                  pl.BlockSpec((1, D_MODEL), lambda p, rb, win, fl: (0, 0)),
                  pl.BlockSpec((1, D_MODEL), lambda p, rb, win, fl: (0, 0))],
        out_specs=pl.BlockSpec((MOE_WIN, D_MODEL), tok),
        scratch_shapes=[pltpu.VMEM((MOE_WIN, D_MODEL), F32)],
    )
    return pl.pallas_call(
        _moe_combine_kernel,
        out_shape=jax.ShapeDtypeStruct((m, D_MODEL), F32),
        grid_spec=grid_spec,
        compiler_params=_cparams(("arbitrary",)),
        name="moe_combine",
    )(pair_rb, pair_win, pair_flag, dest_cols, w_cols, yrows, x2, ln_g, ln_b)


def _pair_tables(n_blk, blk_lo, n_pairs):
    cell_end = jnp.cumsum(n_blk)
    cell_start = cell_end - n_blk
    pidx = jnp.arange(n_pairs, dtype=jnp.int32)
    valid = pidx < cell_end[-1]
    cell = jnp.sum((cell_end[None, :] <= pidx[:, None]).astype(jnp.int32), axis=1)
    cell = jnp.minimum(cell, n_blk.shape[0] - 1)
    rb = blk_lo[cell] + pidx - cell_start[cell]
    return cell, rb, valid


def moe_layer(xb2, x2, w_router, wg, wu, wd, ln_g, ln_b):
    n = x2.shape[0]
    n_win = n // MOE_WIN
    n_rows = n * TOP_K + N_EXPERTS * MOE_TM
    n_blocks = n_rows // MOE_TM
    n_pairs = n_blocks + N_EXPERTS * n_win

    wr = jnp.pad(w_router, ((0, 0), (0, LANES - N_EXPERTS))).astype(BF16)
    top_e_t, top_w_t = moe_router(xb2, wr)
    top_e = top_e_t[:, :TOP_K]
    top_w = top_w_t[:, :TOP_K]

    experts = jnp.arange(N_EXPERTS, dtype=jnp.int32)
    tok_onehot = ((top_e[:, 0:1] == experts[None, :]) | (top_e[:, 1:2] == experts[None, :])).astype(jnp.int32)
    csum = jnp.cumsum(tok_onehot, axis=0)
    counts = csum[-1]
    rank = csum - tok_onehot
    padded = (counts + MOE_TM - 1) // MOE_TM * MOE_TM
    pad_end = jnp.cumsum(padded)
    pad_start = pad_end - padded
    dest = pad_start[top_e] + jnp.take_along_axis(rank, top_e, axis=1)

    blk_row0 = jnp.arange(n_blocks, dtype=jnp.int32) * MOE_TM
    block_e = jnp.minimum(jnp.sum((pad_end[None, :] <= blk_row0[:, None]).astype(jnp.int32), axis=1),
                          N_EXPERTS - 1).astype(jnp.int32)
    block_valid = (blk_row0 < pad_end[-1]).astype(jnp.int32)

    cell_cnt = tok_onehot.reshape(n_win, MOE_WIN, N_EXPERTS).sum(axis=1)
    cell_lo = pad_start[None, :] + jnp.cumsum(cell_cnt, axis=0) - cell_cnt
    blk_lo = cell_lo // MOE_TM
    n_blk = jnp.where(cell_cnt > 0, (cell_lo + cell_cnt - 1) // MOE_TM - blk_lo + 1, 0)

    cell, g_rb, g_valid = _pair_tables(n_blk.T.reshape(-1), blk_lo.T.reshape(-1), n_pairs)
    g_win = cell % n_win
    last_rb = jnp.max(jnp.where(g_valid, g_rb, 0))
    g_rb = jnp.where(g_valid, g_rb, last_rb).astype(jnp.int32)
    g_win = jnp.where(g_valid, g_win, 0).astype(jnp.int32)
    prev_rb = jnp.concatenate([jnp.full((1,), -1, jnp.int32), g_rb[:-1]])
    g_flag = g_valid.astype(jnp.int32) + 2 * (g_valid & (g_rb != prev_rb)).astype(jnp.int32)
    dest_rows = dest.reshape(n_win, MOE_WIN, TOP_K).transpose(0, 2, 1)
    xrows = moe_gather(g_rb, g_win, g_flag, dest_rows, xb2, n_rows)

    yrows = moe_expert_ffn(block_e, block_valid, xrows, wg, wu, wd)

    cell, c_rb, c_valid = _pair_tables(n_blk.reshape(-1), blk_lo.reshape(-1), n_pairs)
    c_win = jnp.where(c_valid, cell // N_EXPERTS, n_win - 1).astype(jnp.int32)
    c_rb = jnp.where(c_valid, c_rb, 0).astype(jnp.int32)
    prev_win = jnp.concatenate([jnp.full((1,), -1, jnp.int32), c_win[:-1]])
    next_win = jnp.concatenate([c_win[1:], jnp.full((1,), -1, jnp.int32)])
    next_valid = jnp.concatenate([c_valid[1:], jnp.zeros((1,), bool)])
    c_first = c_valid & (c_win != prev_win)
    c_last = c_valid & ((c_win != next_win) | jnp.logical_not(next_valid))
    c_flag = c_valid.astype(jnp.int32) + 2 * c_first.astype(jnp.int32) + 4 * c_last.astype(jnp.int32)
    return moe_combine(c_rb, c_win, c_flag, dest, top_w, yrows, x2, ln_g, ln_b)


def _w_in_perm():
    zero = 3480
    perm = np.full((D_IN_P,), zero, np.int64)
    o_qkv, o_z, o_beta, o_decay, o_bq, o_bk, o_bv, o_cf, o_ci, o_cq, o_cg = (
        0, 1152, 1536, 1548, 1560, 1944, 2072, 2200, 2712, 2968, 3224)
    perm[COL_A_QKV:COL_A_QKV + 1152] = np.arange(o_qkv, o_qkv + 1152)
    perm[COL_A_Z:COL_A_Z + 384] = np.arange(o_z, o_z + 384)
    for j in range(3):
        for a in range(2):
            dst = COL_B_Q + j * LANES + a * HEAD_DIM
            src = o_bq + (j + 3 * a) * HEAD_DIM
            perm[dst:dst + HEAD_DIM] = np.arange(src, src + HEAD_DIM)
    perm[COL_B_K:COL_B_K + 128] = np.arange(o_bk, o_bk + 128)
    perm[COL_C_G:COL_C_G + 256] = np.arange(o_cg, o_cg + 256)
    perm[COL_B_V:COL_B_V + 128] = np.arange(o_bv, o_bv + 128)
    perm[COL_C_F:COL_C_F + 512] = np.arange(o_cf, o_cf + 512)
    perm[COL_C_I:COL_C_I + 256] = np.arange(o_ci, o_ci + 256)
    perm[COL_C_Q:COL_C_Q + 256] = np.arange(o_cq, o_cq + 256)
    for d, col in enumerate((COL_GATES0, COL_GATES1)):
        for p in range(A_HEADS // 2):
            dst = col + 4 * p
            perm[dst + 0] = o_beta + d * A_HEADS + 2 * p
            perm[dst + 1] = o_beta + d * A_HEADS + 2 * p + 1
            perm[dst + 2] = o_decay + d * A_HEADS + 2 * p
            perm[dst + 3] = o_decay + d * A_HEADS + 2 * p + 1
    return perm


def _w_out_perm():
    perm = np.arange(D_MIX)
    for j in range(3):
        for a in range(2):
            dst = A_WIDTH + j * LANES + a * HEAD_DIM
            src = A_WIDTH + (j + 3 * a) * HEAD_DIM
            perm[dst:dst + HEAD_DIM] = np.arange(src, src + HEAD_DIM)
    return perm


def _gdn_params(a_log_d, dt_bias_d):
    prm = jnp.zeros((SUBLANES, LANES), F32)
    prm = prm.at[0, :A_HEADS].set(a_log_d.astype(F32))
    return prm.at[1, :A_HEADS].set(dt_bias_d.astype(F32))


def kernel(x, mem, w_in, conv_w, gdn_a_log, gdn_dt_bias, gdn_norm_w, q_norm_w, k_norm_w, hgrn_lb_logits,
           hgrn_norm_w, w_out, ln1_g, ln1_b, xq, xk, xv, xo, ln2_g, ln2_b, ffn_wg, ffn_wu, ffn_wd,
           moe_router, moe_wg, moe_wu, moe_wd, ln3_g, ln3_b):
    b, s, d = x.shape
    m = b * s
    cos_t, sin_t = rope_tables(s)
    lb_p = jax.nn.softmax(hgrn_lb_logits.astype(F32), axis=0)
    lb_c = jnp.cumsum(lb_p, axis=0)
    lower_bounds = lb_c - lb_c[0]
    in_perm = _w_in_perm()
    out_perm = _w_out_perm()
    mem_b = mem.astype(BF16).reshape(b * mem.shape[1], d)
    tile2 = lambda w: jnp.tile(w.astype(F32), 2)[None, :]
    row = lambda v: v.astype(F32)[None, :]

    x2 = x.reshape(m, d)
    xb2 = x2.astype(BF16)
    for l in range(DEPTH):
        w_in_p = jnp.take(jnp.pad(w_in[l], ((0, 0), (0, 1))), in_perm, axis=1).astype(BF16)
        proj2 = matmul(xb2, w_in_p, F32, 512, 768)
        proj3 = proj2.reshape(b, s, D_IN_P)

        oa_f = gdn_direction(proj3, COL_GATES0 // LANES, conv_w[l],
                             _gdn_params(gdn_a_log[l, 0], gdn_dt_bias[l, 0]), False)
        oa_b = gdn_direction(proj3, COL_GATES1 // LANES, conv_w[l],
                             _gdn_params(gdn_a_log[l, 1], gdn_dt_bias[l, 1]), True)
        ob = gqa_attention(proj3, cos_t, sin_t, tile2(q_norm_w[l]), tile2(k_norm_w[l]), s)
        lb = lower_bounds[l][None, :]
        oc_f = hgrn_direction(proj3, COL_C_F // C_WIDTH, lb, False)
        oc_b = hgrn_direction(proj3, COL_C_F // C_WIDTH + 1, lb, True)

        w_out_p = jnp.take(w_out[l], out_perm, axis=0).astype(BF16)
        x2, xb2 = mixer_output(oa_f.reshape(m, A_WIDTH), oa_b.reshape(m, A_WIDTH), proj2,
                               ob.reshape(m, B_WIDTH), oc_f.reshape(m, C_WIDTH), oc_b.reshape(m, C_WIDTH),
                               x2, tile2(gdn_norm_w[l]), tile2(hgrn_norm_w[l]), w_out_p,
                               row(ln1_g[l]), row(ln1_b[l]))

        k3 = matmul(mem_b, xk[l].astype(BF16), BF16, 256, 512).reshape(b, -1, d)
        v3 = matmul(mem_b, xv[l].astype(BF16), BF16, 256, 512).reshape(b, -1, d)
        x3, xb3 = cross_attention(xb2.reshape(b, s, d), x2.reshape(b, s, d), k3, v3,
                                  xq[l].astype(BF16), xo[l].astype(BF16), row(ln2_g[l]), row(ln2_b[l]))
        x2, xb2 = x3.reshape(m, d), xb3.reshape(m, d)

        if l % 2 == 0:
            x2, xb2 = dense_ffn(xb2, x2, ffn_wg[l // 2].astype(BF16), ffn_wu[l // 2].astype(BF16),
                                ffn_wd[l // 2].astype(BF16), row(ln3_g[l]), row(ln3_b[l]))
        else:
            x2 = moe_layer(xb2, x2, moe_router[l // 2], moe_wg[l // 2].astype(BF16),
                           moe_wu[l // 2].astype(BF16), moe_wd[l // 2].astype(BF16),
                           row(ln3_g[l]), row(ln3_b[l]))
            xb2 = x2.astype(BF16)
    return x2.reshape(b, s, d)
```

```python
import functools
import math

import numpy as np
import jax
import jax.numpy as jnp
from jax import lax
from jax.experimental import pallas as pl
from jax.experimental.pallas import tpu as pltpu

F32 = jnp.float32
BF16 = jnp.bfloat16

D_MODEL = 1024
DEPTH = 2
HEAD_DIM = 64
A_HEADS = 6
A_WIDTH = A_HEADS * HEAD_DIM
B_Q_HEADS = 6
B_KV_HEADS = 2
B_WIDTH = B_Q_HEADS * HEAD_DIM
C_HEADS = 4
C_WIDTH = C_HEADS * HEAD_DIM
D_MIX = A_WIDTH + B_WIDTH + C_WIDTH
CONV_K = 5
CHUNK = 64
GRID_W = 64
ROPE_AXIS_DIM = HEAD_DIM // 2
ROPE_THETA = 10000.0
X_HEADS = 4
X_HEAD_DIM = D_MODEL // X_HEADS
D_FF = 2816
N_EXPERTS = 8
TOP_K = 2
D_FF_EXPERT = 3584
DN_ALPHA = (2 * DEPTH) ** 0.25
LN_EPS = 1e-5
RMS_EPS = 1e-6

LANES = 128
SUBLANES = 8
GROUP = 256
CHUNKS_PER_GROUP = GROUP // CHUNK
VMEM_LIMIT = 56 * 1024 * 1024

COL_A_QKV = 0
COL_A_Z = 1152
COL_B_Q = 1536
COL_B_K = 1920
COL_C_G = 2048
COL_B_V = 2304
COL_GATES0 = 2432
COL_C_F = 2560
COL_C_I = 3072
COL_C_Q = 3328
COL_GATES1 = 3584
D_IN_P = 3840


def _cparams(sem):
    return pltpu.CompilerParams(dimension_semantics=sem, vmem_limit_bytes=VMEM_LIMIT)


def _sigmoid(x):
    return 1.0 / (1.0 + jnp.exp(-x))


def _softplus(x):
    return jnp.maximum(x, 0.0) + jnp.log1p(jnp.exp(-jnp.abs(x)))


def _dot(a, b):
    return jnp.dot(a.astype(BF16), b.astype(BF16), preferred_element_type=F32)


def _dot_nt(a, b):
    return lax.dot_general(a.astype(BF16), b.astype(BF16), (((1,), (1,)), ((), ())),
                           preferred_element_type=F32)


def _dot_sel(m01, x):
    m = jnp.where(m01, 1.0, 0.0).astype(BF16)
    hi = x.astype(BF16)
    r1 = x - hi.astype(F32)
    mid = r1.astype(BF16)
    lo = (r1 - mid.astype(F32)).astype(BF16)
    out = jnp.dot(m, hi, preferred_element_type=F32)
    out = out + jnp.dot(m, mid, preferred_element_type=F32)
    return out + jnp.dot(m, lo, preferred_element_type=F32)


def _dot_sel_right(x, m01):
    m = jnp.where(m01, 1.0, 0.0).astype(BF16)
    hi = x.astype(BF16)
    r1 = x - hi.astype(F32)
    mid = r1.astype(BF16)
    lo = (r1 - mid.astype(F32)).astype(BF16)
    out = jnp.dot(hi, m, preferred_element_type=F32)
    out = out + jnp.dot(mid, m, preferred_element_type=F32)
    return out + jnp.dot(lo, m, preferred_element_type=F32)


def _layer_norm_rows(y, g, b):
    mu = jnp.mean(y, axis=-1, keepdims=True)
    yc = y - mu
    var = jnp.mean(yc * yc, axis=-1, keepdims=True)
    return yc * lax.rsqrt(var + LN_EPS) * g + b


def _iota2(shape, dim):
    return lax.broadcasted_iota(jnp.int32, shape, dim)


def _head_sum(x, first_head):
    s0 = jnp.sum(jnp.where(first_head, x, 0.0), axis=-1, keepdims=True)
    s1 = jnp.sum(jnp.where(first_head, 0.0, x), axis=-1, keepdims=True)
    return jnp.where(first_head, s0, s1)


def _block_rows(t, block, row_in_block):
    parts = [jnp.broadcast_to(t[b * block + row_in_block:b * block + row_in_block + 1, :], (block, t.shape[1]))
             for b in range(GROUP // block)]
    return jnp.concatenate(parts, axis=0)


def _mm_kernel(x_ref, w_ref, o_ref):
    o_ref[...] = jnp.dot(x_ref[...], w_ref[...], preferred_element_type=F32).astype(o_ref.dtype)


def matmul(x, w, out_dtype, tm, tn):
    m, k = x.shape
    n = w.shape[1]
    return pl.pallas_call(
        _mm_kernel,
        out_shape=jax.ShapeDtypeStruct((m, n), out_dtype),
        grid=(n // tn, m // tm),
        in_specs=[pl.BlockSpec((tm, k), lambda j, i: (i, 0)),
                  pl.BlockSpec((k, tn), lambda j, i: (0, j))],
        out_specs=pl.BlockSpec((tm, tn), lambda j, i: (i, j)),
        compiler_params=_cparams(("parallel", "arbitrary")),
        name="matmul",
    )(x, w)


def _scan_masks(reverse):
    ri = _iota2((GROUP, GROUP), 0)
    ci = _iota2((GROUP, GROUP), 1)
    same_chunk = (ri >> 6) == (ci >> 6)
    if reverse:
        return ri, ci, same_chunk, same_chunk & (ri <= ci), same_chunk & (ri < ci)
    return ri, ci, same_chunk, same_chunk & (ri >= ci), same_chunk & (ri > ci)


GATE_ROWS = 16


class _GdnProblem:
    def __init__(self, **kw):
        self.__dict__.update(kw)


def _gdn_group(probs, same_chunk, h0):
    nh0 = jnp.logical_not(h0)
    sr = _iota2((CHUNK, GROUP), 0)
    sc = _iota2((CHUNK, GROUP), 1)
    eye_side = jnp.where(sr == (sc & (CHUNK - 1)), 1.0, 0.0)
    tl = _iota2((LANES, GROUP), 1)
    br = _iota2((LANES, LANES), 0)
    bc = _iota2((LANES, LANES), 1)
    bd128 = (br >> 6) == (bc >> 6)

    def block_diag(side):
        return jnp.where(same_chunk, jnp.concatenate([side] * CHUNKS_PER_GROUP, axis=0), 0.0).astype(BF16)

    for pr in probs:
        pr.q = pr.q * lax.rsqrt(_head_sum(pr.q * pr.q, h0) + RMS_EPS) * (HEAD_DIM ** -0.5)
        pr.k = pr.k * lax.rsqrt(_head_sum(pr.k * pr.k, h0) + RMS_EPS)
        pr.gedge = _block_rows(pr.gc, CHUNK, 0 if pr.reverse else CHUNK - 1)
        pr.exp_g = jnp.exp(pr.gc)
        pr.kb = pr.k * pr.beta_t
        pr.rhs = jnp.concatenate([pr.v * pr.beta_t, pr.kb * pr.exp_g], axis=1).astype(BF16)
        pr.k16 = pr.k.astype(BF16)
        pr.decay, pr.x, pr.pw, pr.bd = [], [], [], []
        for h in range(2):
            decay = jnp.exp(jnp.where(pr.incl, pr.g_col[h] - pr.g_row[h], -1e30))
            kk = _dot_nt(jnp.where(h0 if h == 0 else nh0, pr.kb, 0.0), pr.k16)
            low = jnp.where(pr.strict, kk * decay, 0.0)
            a = -(low[0:CHUNK] + low[CHUNK:2 * CHUNK] + low[2 * CHUNK:3 * CHUNK] + low[3 * CHUNK:4 * CHUNK])
            pr.decay.append(decay)
            pr.x.append(eye_side + a)
            pr.pw.append(a)
            pr.bd.append(block_diag(a))

    for _ in range(5):
        for pr in probs:
            for h in range(2):
                pr.pw[h] = jnp.dot(pr.pw[h].astype(BF16), pr.bd[h], preferred_element_type=F32)
        for pr in probs:
            for h in range(2):
                pr.bd[h] = block_diag(pr.pw[h])
        for pr in probs:
            for h in range(2):
                pr.x[h] = pr.x[h] + jnp.dot(pr.x[h].astype(BF16), pr.bd[h], preferred_element_type=F32)

    for pr in probs:
        sols = [jnp.dot(block_diag(pr.x[h]), pr.rhs, preferred_element_type=F32) for h in range(2)]
        u = jnp.where(h0, sols[0][:, :LANES], sols[1][:, :LANES])
        w = jnp.where(h0, sols[0][:, LANES:], sols[1][:, LANES:])
        wu = jnp.concatenate([w, u], axis=1)
        wu16 = wu.astype(BF16)
        aw = []
        for h in range(2):
            attn = jnp.where(pr.incl, _dot_nt(jnp.where(h0 if h == 0 else nh0, pr.q, 0.0), pr.k16) * pr.decay[h], 0.0)
            aw.append(jnp.dot(attn.astype(BF16), wu16, preferred_element_type=F32))
        pr.qp = (pr.q * pr.exp_g - jnp.where(h0, aw[0][:, :LANES], aw[1][:, :LANES])).astype(BF16)
        pr.op = jnp.where(h0, aw[0][:, LANES:], aw[1][:, LANES:])
        kd_t = (pr.k * jnp.exp(pr.gedge - pr.gc)).T
        pr.kmat, pr.nmat = [], []
        for c in range(CHUNKS_PER_GROUP):
            km = jnp.dot(jnp.where((tl >> 6) == c, kd_t, 0.0), wu, preferred_element_type=F32)
            pr.kmat.append(jnp.where(bd128, km[:, :LANES], 0.0).astype(BF16))
            pr.nmat.append(jnp.where(bd128, km[:, LANES:], 0.0))
        pr.out = [None] * CHUNKS_PER_GROUP

    for t in range(CHUNKS_PER_GROUP):
        for pr in probs:
            c = CHUNKS_PER_GROUP - 1 - t if pr.reverse else t
            rs = slice(c * CHUNK, (c + 1) * CHUNK)
            s16 = pr.state.astype(BF16)
            pr.out[c] = jnp.dot(pr.qp[rs], s16, preferred_element_type=F32) + pr.op[rs]
            cd = jnp.exp(pr.gedge[c * CHUNK:c * CHUNK + 1, :])
            pr.state = pr.state * cd - jnp.dot(pr.kmat[c], s16, preferred_element_type=F32) + pr.nmat[c]
    return [jnp.concatenate(pr.out, axis=0) for pr in probs]


def _gdn_kernel(xf_ref, xfp_ref, xfn_ref, xb_ref, xbp_ref, xbn_ref, cw_ref, gf_ref, gb_ref, prm_ref,
                of_ref, ob_ref, s_scr):
    n = pl.program_id(1)
    n_groups = pl.num_programs(1)

    @pl.when(n == 0)
    def _():
        s_scr[...] = jnp.zeros_like(s_scr)

    lane = _iota2((GROUP, LANES), 1)
    h0 = lane < HEAD_DIM
    ri = _iota2((GROUP, GROUP), 0)
    ci = _iota2((GROUP, GROUP), 1)
    same_chunk = (ri >> 6) == (ci >> 6)
    cw = cw_ref[...]
    base = SUBLANES - CONV_K // 2
    pairs = A_HEADS // 2

    probs = []
    for d, (gi, x_ref, xp_ref, xn_ref, g_ref) in enumerate(((n, xf_ref, xfp_ref, xfn_ref, gf_ref),
                                                            (n_groups - 1 - n, xb_ref, xbp_ref, xbn_ref, gb_ref))):
        reverse = d == 1
        lower, upper = same_chunk & (ri >= ci), same_chunk & (ri <= ci)
        incl = upper if reverse else lower
        strict = same_chunk & ((ri < ci) if reverse else (ri > ci))

        prev = jnp.where(gi > 0, xp_ref[0], 0.0)
        nxt = jnp.where(gi < n_groups - 1, xn_ref[0], 0.0)
        ext = jnp.concatenate([prev, x_ref[0], nxt], axis=0)
        acc = ext[base:base + GROUP, :] * cw[0:1, :]
        for t in range(1, CONV_K):
            acc = acc + ext[base + t:base + t + GROUP, :] * cw[t:t + 1, :]
        qkv = acc * _sigmoid(acc)

        gt_t = g_ref[0].T[0:GATE_ROWS, :]
        prm = prm_ref[d]
        e_rows = jnp.where(prm[:, 2:3] > 0.5,
                           -jnp.exp(prm[:, 0:1]) * _softplus(gt_t + prm[:, 1:2]), _sigmoid(gt_t))
        g_rows = _dot_sel_right(e_rows, lower if reverse else upper)
        slab = jnp.concatenate([e_rows, g_rows, jnp.zeros((LANES - 2 * GATE_ROWS, GROUP), F32)], axis=0)
        cols = slab.T
        for p in range(pairs):
            sl = lambda part: slice(part * A_WIDTH + p * LANES, part * A_WIDTH + (p + 1) * LANES)
            col = lambda r: cols[:, r:r + 1]
            probs.append(_GdnProblem(
                reverse=reverse, incl=incl, strict=strict, slot=d * pairs + p,
                q=qkv[:, sl(0)], k=qkv[:, sl(1)], v=qkv[:, sl(2)],
                beta_t=jnp.where(h0, col(4 * p), col(4 * p + 1)),
                gc=jnp.where(h0, col(GATE_ROWS + 4 * p + 2), col(GATE_ROWS + 4 * p + 3)),
                g_col=(col(GATE_ROWS + 4 * p + 2), col(GATE_ROWS + 4 * p + 3)),
                g_row=(g_rows[4 * p + 2:4 * p + 3, :], g_rows[4 * p + 3:4 * p + 4, :]),
                state=s_scr[d * pairs + p]))

    outs = _gdn_group(probs, same_chunk, h0)
    for pr, out in zip(probs, outs):
        o_ref = ob_ref if pr.reverse else of_ref
        p = pr.slot % pairs
        o_ref[0, :, p * LANES:(p + 1) * LANES] = out
        s_scr[pr.slot] = pr.state


def gdn_scan(proj3, conv_w, prm):
    b, seq_len, _ = proj3.shape
    n_groups = seq_len // GROUP
    halo_per_group = GROUP // SUBLANES
    n_halo = seq_len // SUBLANES
    width = 3 * A_WIDTH
    fwd = lambda n: n
    bwd = lambda n: n_groups - 1 - n

    def x_specs(gidx):
        return [pl.BlockSpec((1, GROUP, width), lambda i, n: (i, gidx(n), 0)),
                pl.BlockSpec((1, SUBLANES, width),
                             lambda i, n: (i, jnp.maximum(gidx(n) * halo_per_group - 1, 0), 0)),
                pl.BlockSpec((1, SUBLANES, width),
                             lambda i, n: (i, jnp.minimum((gidx(n) + 1) * halo_per_group, n_halo - 1), 0))]

    out_sds = jax.ShapeDtypeStruct((b, seq_len, A_WIDTH), F32)
    return pl.pallas_call(
        _gdn_kernel,
        out_shape=(out_sds, out_sds),
        grid=(b, n_groups),
        in_specs=x_specs(fwd) + x_specs(bwd) + [
            pl.BlockSpec((CONV_K, width), lambda i, n: (0, 0)),
            pl.BlockSpec((1, GROUP, LANES), lambda i, n: (i, fwd(n), COL_GATES0 // LANES)),
            pl.BlockSpec((1, GROUP, LANES), lambda i, n: (i, bwd(n), COL_GATES1 // LANES)),
            pl.BlockSpec((2, GATE_ROWS, LANES), lambda i, n: (0, 0, 0))],
        out_specs=(pl.BlockSpec((1, GROUP, A_WIDTH), lambda i, n: (i, fwd(n), 0)),
                   pl.BlockSpec((1, GROUP, A_WIDTH), lambda i, n: (i, bwd(n), 0))),
        scratch_shapes=[pltpu.VMEM((A_HEADS, LANES, LANES), F32)],
        compiler_params=_cparams(("parallel", "arbitrary")),
        name="gdn_scan",
    )(proj3, proj3, proj3, proj3, proj3, proj3, conv_w, proj3, proj3, prm)


EXP_CLAMP = 60.0


def _hgrn_pair(f_raw, v, q_raw, lb, state_t, masks, reverse):
    ri, ci, same_chunk, incl, _ = masks
    f = lb + (1.0 - lb) * _sigmoid(f_raw)
    lf = jnp.log(f)
    kk = 1.0 - f
    q = q_raw * _sigmoid(q_raw) * (HEAD_DIM ** -0.5)
    lane = _iota2((GROUP, LANES), 1)
    h0 = lane < HEAD_DIM

    cum = _dot_sel(incl, lf)
    cedge = _block_rows(cum, CHUNK, 0 if reverse else CHUNK - 1)
    qd = q * jnp.exp(cum)
    kd = kk * jnp.exp(cedge - cum)

    if reverse:
        m32 = same_chunk & ((ri & 63) < 32) & ((ci & 63) >= 32)
        m16 = ((ri >> 5) == (ci >> 5)) & ((ri & 31) < 16) & ((ci & 31) >= 16)
        mdg = ((ri >> 4) == (ci >> 4)) & (ri <= ci)
        ref32, ref16, refdg = _block_rows(cum, 64, 32), _block_rows(cum, 32, 16), _block_rows(cum, 16, 8)
    else:
        m32 = same_chunk & ((ri & 63) >= 32) & ((ci & 63) < 32)
        m16 = ((ri >> 5) == (ci >> 5)) & ((ri & 31) >= 16) & ((ci & 31) < 16)
        mdg = ((ri >> 4) == (ci >> 4)) & (ri >= ci)
        ref32, ref16, refdg = _block_rows(cum, 64, 31), _block_rows(cum, 32, 15), _block_rows(cum, 16, 7)
    qs32 = q * jnp.exp(jnp.minimum(cum - ref32, 0.0))
    ks32 = (kk * jnp.exp(jnp.minimum(ref32 - cum, 0.0))).astype(BF16)
    qs16 = q * jnp.exp(jnp.minimum(cum - ref16, 0.0))
    ks16 = (kk * jnp.exp(jnp.minimum(ref16 - cum, 0.0))).astype(BF16)
    qsdg = q * jnp.exp(jnp.clip(cum - refdg, -EXP_CLAMP, EXP_CLAMP))
    ksdg = (kk * jnp.exp(jnp.clip(refdg - cum, -EXP_CLAMP, EXP_CLAMP))).astype(BF16)

    v16 = v.astype(BF16)
    intra = []
    for h in range(2):
        mh = h0 if h == 0 else jnp.logical_not(h0)
        a = jnp.where(m32, _dot_nt(jnp.where(mh, qs32, 0.0), ks32),
                      jnp.where(m16, _dot_nt(jnp.where(mh, qs16, 0.0), ks16),
                                jnp.where(mdg, _dot_nt(jnp.where(mh, qsdg, 0.0), ksdg), 0.0)))
        intra.append(jnp.dot(a.astype(BF16), v16, preferred_element_type=F32))
    intra = jnp.where(h0, intra[0], intra[1])

    v_t = v.T
    tl = _iota2((LANES, GROUP), 1)
    br = _iota2((LANES, LANES), 0)
    bc = _iota2((LANES, LANES), 1)
    bd128 = (br >> 6) == (bc >> 6)

    outs = [None] * CHUNKS_PER_GROUP
    order = range(CHUNKS_PER_GROUP - 1, -1, -1) if reverse else range(CHUNKS_PER_GROUP)
    for c in order:
        rs = slice(c * CHUNK, (c + 1) * CHUNK)
        outs[c] = _dot_nt(qd[rs], state_t) + intra[rs]
        upd = jnp.dot(jnp.where((tl >> 6) == c, v_t, 0.0), kd, preferred_element_type=F32)
        cd = jnp.exp(cedge[c * CHUNK:c * CHUNK + 1, :])
        state_t = state_t * cd + jnp.where(bd128, upd, 0.0)
    return jnp.concatenate(outs, axis=0), state_t


def _hgrn_kernel(f_ref, i_ref, q_ref, lb_ref, o_ref, s_scr, *, reverse):
    n = pl.program_id(1)

    @pl.when(n == 0)
    def _():
        s_scr[...] = jnp.zeros_like(s_scr)

    masks = _scan_masks(reverse)
    for p in range(C_HEADS // 2):
        sl = slice(p * LANES, (p + 1) * LANES)
        out, state_t = _hgrn_pair(f_ref[0, :, sl], i_ref[0, :, sl], q_ref[0, :, sl], lb_ref[:, sl],
                                  s_scr[p], masks, reverse)
        o_ref[0, :, sl] = out
        s_scr[p] = state_t


def hgrn_direction(proj3, f_blk, lb, reverse):
    b, seq_len, _ = proj3.shape
    n_groups = seq_len // GROUP
    gidx = (lambda n: n_groups - 1 - n) if reverse else (lambda n: n)
    spec = lambda blk: pl.BlockSpec((1, GROUP, C_WIDTH), lambda i, n: (i, gidx(n), blk))
    return pl.pallas_call(
        functools.partial(_hgrn_kernel, reverse=reverse),
        out_shape=jax.ShapeDtypeStruct((b, seq_len, C_WIDTH), F32),
        grid=(b, n_groups),
        in_specs=[spec(f_blk), spec(COL_C_I // C_WIDTH), spec(COL_C_Q // C_WIDTH),
                  pl.BlockSpec((1, C_WIDTH), lambda i, n: (0, 0))],
        out_specs=pl.BlockSpec((1, GROUP, C_WIDTH), lambda i, n: (i, gidx(n), 0)),
        scratch_shapes=[pltpu.VMEM((C_HEADS // 2, LANES, LANES), F32)],
        compiler_params=_cparams(("parallel", "arbitrary")),
        name="hgrn_scan",
    )(proj3, proj3, proj3, lb)


ATT_TQ = 256


def _norm_rope(x, w, cos, sin_signed, first_half16):
    lane = _iota2(x.shape, 1)
    h0 = lane < HEAD_DIM
    ms = _head_sum(x * x, h0) * (1.0 / HEAD_DIM)
    xn = x * lax.rsqrt(ms + RMS_EPS) * w
    partner = jnp.where(first_half16, pltpu.roll(xn, LANES - 16, 1), pltpu.roll(xn, 16, 1))
    return xn * cos + partner * sin_signed


def _attn_kernel(q_ref, k_ref, v_ref, cosq_ref, sinq_ref, cosk_ref, sink_ref, qw_ref, kw_ref, o_ref,
                 k_scr, v_scr):
    qi = pl.program_id(1)

    @pl.when(qi == 0)
    def _():
        kx = k_ref[0]
        lane_k = _iota2(kx.shape, 1)
        k_scr[...] = _norm_rope(kx, kw_ref[...], cosk_ref[...], sink_ref[...], (lane_k & 31) < 16).astype(BF16)
        v_scr[:, :LANES] = v_ref[0].astype(BF16)
        v_scr[:, LANES:] = jnp.ones((v_scr.shape[0], LANES), BF16)

    lane = _iota2((ATT_TQ, LANES), 1)
    h0 = lane < HEAD_DIM
    fh = (lane & 31) < 16
    cos = cosq_ref[...]
    sin = sinq_ref[...]
    kmat = k_scr[...]
    vmat = v_scr[...]
    q_scale = (HEAD_DIM ** -0.5) * math.log2(math.e)
    for j in range(B_Q_HEADS // 2):
        qn = _norm_rope(q_ref[0, :, j * LANES:(j + 1) * LANES], qw_ref[...], cos, sin, fh) * q_scale
        outs = []
        for half in range(2):
            mh = h0 if half == 0 else jnp.logical_not(h0)
            s = _dot_nt(jnp.where(mh, qn, 0.0), kmat)
            m = jnp.max(s, axis=-1, keepdims=True)
            e = jnp.exp2((s - m).astype(BF16))
            pvl = jnp.dot(e, vmat, preferred_element_type=F32)
            outs.append(pvl[:, :LANES] / pvl[:, LANES:LANES + 1])
        o_ref[0, :, j * LANES:(j + 1) * LANES] = jnp.where(h0, outs[0], outs[1])


def gqa_attention(proj, cos_t, sin_t, q_w, k_w, seq_len):
    b = proj.shape[0]
    return pl.pallas_call(
        _attn_kernel,
        out_shape=jax.ShapeDtypeStruct((b, seq_len, B_WIDTH), F32),
        grid=(b, seq_len // ATT_TQ),
        in_specs=[pl.BlockSpec((1, ATT_TQ, B_WIDTH), lambda i, t: (i, t, COL_B_Q // B_WIDTH)),
                  pl.BlockSpec((1, seq_len, LANES), lambda i, t: (i, 0, COL_B_K // LANES)),
                  pl.BlockSpec((1, seq_len, LANES), lambda i, t: (i, 0, COL_B_V // LANES)),
                  pl.BlockSpec((ATT_TQ, LANES), lambda i, t: (t, 0)),
                  pl.BlockSpec((ATT_TQ, LANES), lambda i, t: (t, 0)),
                  pl.BlockSpec((seq_len, LANES), lambda i, t: (0, 0)),
                  pl.BlockSpec((seq_len, LANES), lambda i, t: (0, 0)),
                  pl.BlockSpec((1, LANES), lambda i, t: (0, 0)),
                  pl.BlockSpec((1, LANES), lambda i, t: (0, 0))],
        out_specs=pl.BlockSpec((1, ATT_TQ, B_WIDTH), lambda i, t: (i, t, 0)),
        scratch_shapes=[pltpu.VMEM((seq_len, LANES), BF16), pltpu.VMEM((seq_len, 2 * LANES), BF16)],
        compiler_params=_cparams(("parallel", "arbitrary")),
        name="gqa_attention",
    )(proj, proj, proj, cos_t, sin_t, cos_t, sin_t, q_w, k_w)


def rope_tables(seq_len):
    rows = seq_len // GRID_W
    row = jnp.repeat(jnp.arange(rows, dtype=F32), GRID_W)
    col = jnp.tile(jnp.arange(GRID_W, dtype=F32), rows)
    inv_freq = ROPE_THETA ** (-jnp.arange(0, ROPE_AXIS_DIM, 2, dtype=F32) / ROPE_AXIS_DIM)
    ang_r = row[:, None] * inv_freq
    ang_c = col[:, None] * inv_freq
    cos64 = jnp.concatenate([jnp.cos(ang_r), jnp.cos(ang_r), jnp.cos(ang_c), jnp.cos(ang_c)], axis=1)
    sin64 = jnp.concatenate([-jnp.sin(ang_r), jnp.sin(ang_r), -jnp.sin(ang_c), jnp.sin(ang_c)], axis=1)
    return jnp.tile(cos64, (1, 2)), jnp.tile(sin64, (1, 2))


MIX_TM = 256


def _mixout_kernel(oaf_ref, oab_ref, z_ref, ob_ref, ocf_ref, ocb_ref, g_ref, x_ref, wa_ref, wc_ref,
                   wo_ref, lg_ref, lb_ref, xo_ref, xb_ref):
    lane = _iota2((MIX_TM, LANES), 1)
    h0 = lane < HEAD_DIM
    parts = []
    for j in range(A_WIDTH // LANES):
        sl = slice(j * LANES, (j + 1) * LANES)
        o = oaf_ref[:, sl] + oab_ref[:, sl]
        ms = _head_sum(o * o, h0) * (1.0 / HEAD_DIM)
        z = z_ref[:, sl]
        parts.append((o * lax.rsqrt(ms + RMS_EPS) * wa_ref[...] * (z * _sigmoid(z))).astype(BF16))
    parts.append(ob_ref[...].astype(BF16))
    for j in range(C_WIDTH // LANES):
        sl = slice(j * LANES, (j + 1) * LANES)
        o = ocf_ref[:, sl] + ocb_ref[:, sl]
        ms = _head_sum(o * o, h0) * (1.0 / HEAD_DIM)
        parts.append((o * lax.rsqrt(ms + RMS_EPS) * wc_ref[...] * _sigmoid(g_ref[:, sl])).astype(BF16))
    mixed = jnp.concatenate(parts, axis=1)
    h = jnp.dot(mixed, wo_ref[...], preferred_element_type=F32)
    y = _layer_norm_rows(DN_ALPHA * x_ref[...] + h, lg_ref[...], lb_ref[...])
    xo_ref[...] = y
    xb_ref[...] = y.astype(BF16)


def mixer_output(oa_f, oa_b, proj2, ob, oc_f, oc_b, x2, gdn_w, hgrn_w, w_out, ln_g, ln_b):
    m = x2.shape[0]
    row = lambda w: pl.BlockSpec((MIX_TM, w), lambda i: (i, 0))
    const = lambda r, c: pl.BlockSpec((r, c), lambda i: (0, 0))
    return pl.pallas_call(
        _mixout_kernel,
        out_shape=(jax.ShapeDtypeStruct((m, D_MODEL), F32), jax.ShapeDtypeStruct((m, D_MODEL), BF16)),
        grid=(m // MIX_TM,),
        in_specs=[row(A_WIDTH), row(A_WIDTH),
                  pl.BlockSpec((MIX_TM, A_WIDTH), lambda i: (i, COL_A_Z // A_WIDTH)),
                  row(B_WIDTH), row(C_WIDTH), row(C_WIDTH),
                  pl.BlockSpec((MIX_TM, C_WIDTH), lambda i: (i, COL_C_G // C_WIDTH)),
                  row(D_MODEL), const(1, LANES), const(1, LANES),
                  const(D_MIX, D_MODEL), const(1, D_MODEL), const(1, D_MODEL)],
        out_specs=(row(D_MODEL), row(D_MODEL)),
        compiler_params=_cparams(("parallel",)),
        name="mixer_output",
    )(oa_f, oa_b, proj2, ob, oc_f, oc_b, proj2, x2, gdn_w, hgrn_w, w_out, ln_g, ln_b)


XATT_TM = 512


def _xattn_kernel(xb_ref, x_ref, k_ref, v_ref, wq_ref, wo_ref, lg_ref, lb_ref, xo_ref, xob_ref):
    q = jnp.dot(xb_ref[0], wq_ref[...], preferred_element_type=F32) * (X_HEAD_DIM ** -0.5)
    outs = []
    for h in range(X_HEADS):
        sl = slice(h * X_HEAD_DIM, (h + 1) * X_HEAD_DIM)
        s = _dot_nt(q[:, sl], k_ref[0, :, sl])
        m = jnp.max(s, axis=-1, keepdims=True)
        e = jnp.exp(s - m)
        l = jnp.sum(e, axis=-1, keepdims=True)
        outs.append((jnp.dot(e.astype(BF16), v_ref[0, :, sl], preferred_element_type=F32) / l).astype(BF16))
    o = jnp.concatenate(outs, axis=1)
    c = jnp.dot(o, wo_ref[...], preferred_element_type=F32)
    y = _layer_norm_rows(DN_ALPHA * x_ref[0] + c, lg_ref[...], lb_ref[...])
    xo_ref[0] = y
    xob_ref[0] = y.astype(BF16)


def cross_attention(xb3, x3, k3, v3, wq, wo, ln_g, ln_b):
    b, s, _ = x3.shape
    mem = k3.shape[1]
    row = pl.BlockSpec((1, XATT_TM, D_MODEL), lambda i, t: (i, t, 0))
    const = lambda r, c: pl.BlockSpec((r, c), lambda i, t: (0, 0))
    kv = pl.BlockSpec((1, mem, D_MODEL), lambda i, t: (i, 0, 0))
    return pl.pallas_call(
        _xattn_kernel,
        out_shape=(jax.ShapeDtypeStruct((b, s, D_MODEL), F32), jax.ShapeDtypeStruct((b, s, D_MODEL), BF16)),
        grid=(b, s // XATT_TM),
        in_specs=[row, row, kv, kv, const(D_MODEL, D_MODEL), const(D_MODEL, D_MODEL),
                  const(1, D_MODEL), const(1, D_MODEL)],
        out_specs=(row, row),
        compiler_params=_cparams(("parallel", "parallel")),
        name="cross_attention",
    )(xb3, x3, k3, v3, wq, wo, ln_g, ln_b)


FFN_TM = 512
FFN_TF = 1408


def _ffn_kernel(xb_ref, x_ref, wg_ref, wu_ref, wd_ref, lg_ref, lb_ref, xo_ref, xob_ref, acc_ref):
    j = pl.program_id(1)

    @pl.when(j == 0)
    def _():
        acc_ref[...] = jnp.zeros_like(acc_ref)

    xb = xb_ref[...]
    g = jnp.dot(xb, wg_ref[...], preferred_element_type=F32)
    u = jnp.dot(xb, wu_ref[...], preferred_element_type=F32)
    h = (g * _sigmoid(g) * u).astype(BF16)
    acc_ref[...] += jnp.dot(h, wd_ref[...], preferred_element_type=F32)

    @pl.when(j == pl.num_programs(1) - 1)
    def _():
        y = _layer_norm_rows(DN_ALPHA * x_ref[...] + acc_ref[...], lg_ref[...], lb_ref[...])
        xo_ref[...] = y
        xob_ref[...] = y.astype(BF16)


def dense_ffn(xb2, x2, wg, wu, wd, ln_g, ln_b):
    m = x2.shape[0]
    row = pl.BlockSpec((FFN_TM, D_MODEL), lambda i, j: (i, 0))
    const = pl.BlockSpec((1, D_MODEL), lambda i, j: (0, 0))
    return pl.pallas_call(
        _ffn_kernel,
        out_shape=(jax.ShapeDtypeStruct((m, D_MODEL), F32), jax.ShapeDtypeStruct((m, D_MODEL), BF16)),
        grid=(m // FFN_TM, D_FF // FFN_TF),
        in_specs=[row, row,
                  pl.BlockSpec((D_MODEL, FFN_TF), lambda i, j: (0, j)),
                  pl.BlockSpec((D_MODEL, FFN_TF), lambda i, j: (0, j)),
                  pl.BlockSpec((FFN_TF, D_MODEL), lambda i, j: (j, 0)),
                  const, const],
        out_specs=(row, row),
        scratch_shapes=[pltpu.VMEM((FFN_TM, D_MODEL), F32)],
        compiler_params=_cparams(("parallel", "arbitrary")),
        name="dense_ffn",
    )(xb2, x2, wg, wu, wd, ln_g, ln_b)


MOE_TM = 512
MOE_WIN = 512
MOE_TF = 512
ROUTER_TM = 512


def _router_kernel(xb_ref, wr_ref, e_ref, w_ref):
    logits = jnp.dot(xb_ref[...], wr_ref[...], preferred_element_type=F32)
    lane = _iota2(logits.shape, 1)
    neg = jnp.float32(-jnp.inf)
    lane_f = lane.astype(F32)
    lg = jnp.where(lane < N_EXPERTS, logits, neg)
    m1 = jnp.max(lg, axis=-1, keepdims=True)
    i1 = jnp.min(jnp.where(lg == m1, lane_f, float(LANES)), axis=-1, keepdims=True)
    lg2 = jnp.where(lane_f == i1, neg, lg)
    m2 = jnp.max(lg2, axis=-1, keepdims=True)
    i2 = jnp.min(jnp.where(lg2 == m2, lane_f, float(LANES)), axis=-1, keepdims=True)
    t = jnp.exp(m2 - m1)
    w1 = 1.0 / (1.0 + t)
    w2 = t / (1.0 + t)
    e_ref[...] = jnp.where(lane == 0, i1, jnp.where(lane == 1, i2, 0.0)).astype(jnp.int32)
    w_ref[...] = jnp.where(lane == 0, w1, jnp.where(lane == 1, w2, 0.0))


def moe_router(xb2, w_router_p):
    m = xb2.shape[0]
    return pl.pallas_call(
        _router_kernel,
        out_shape=(jax.ShapeDtypeStruct((m, LANES), jnp.int32), jax.ShapeDtypeStruct((m, LANES), F32)),
        grid=(m // ROUTER_TM,),
        in_specs=[pl.BlockSpec((ROUTER_TM, D_MODEL), lambda i: (i, 0)),
                  pl.BlockSpec((D_MODEL, LANES), lambda i: (0, 0))],
        out_specs=(pl.BlockSpec((ROUTER_TM, LANES), lambda i: (i, 0)),
                   pl.BlockSpec((ROUTER_TM, LANES), lambda i: (i, 0))),
        compiler_params=_cparams(("parallel",)),
        name="moe_router",
    )(xb2, w_router_p)


def _moe_gather_kernel(rb_ref, win_ref, flag_ref, dest_ref, x_ref, o_ref):
    p = pl.program_id(0)
    flag = flag_ref[p]

    @pl.when((flag & 2) != 0)
    def _():
        o_ref[...] = jnp.zeros_like(o_ref)

    @pl.when((flag & 1) != 0)
    def _():
        d = dest_ref[0]
        rows = _iota2((MOE_TM, MOE_WIN), 0) + rb_ref[p] * MOE_TM
        onehot = jnp.where((d[0:1, :] == rows) | (d[1:2, :] == rows), 1.0, 0.0).astype(BF16)
        o_ref[...] += jnp.dot(onehot, x_ref[...], preferred_element_type=F32).astype(BF16)


def moe_gather(pair_rb, pair_win, pair_flag, dest_rows, xb2, n_rows):
    n_pairs = pair_rb.shape[0]
    grid_spec = pltpu.PrefetchScalarGridSpec(
        num_scalar_prefetch=3,
        grid=(n_pairs,),
        in_specs=[pl.BlockSpec((1, TOP_K, MOE_WIN), lambda p, rb, win, fl: (win[p], 0, 0)),
                  pl.BlockSpec((MOE_WIN, D_MODEL), lambda p, rb, win, fl: (win[p], 0))],
        out_specs=pl.BlockSpec((MOE_TM, D_MODEL), lambda p, rb, win, fl: (rb[p], 0)),
    )
    return pl.pallas_call(
        _moe_gather_kernel,
        out_shape=jax.ShapeDtypeStruct((n_rows, D_MODEL), BF16),
        grid_spec=grid_spec,
        compiler_params=_cparams(("arbitrary",)),
        name="moe_gather",
    )(pair_rb, pair_win, pair_flag, dest_rows, xb2)


def _moe_ffn_kernel(be_ref, bv_ref, x_ref, wg_ref, wu_ref, wd_ref, o_ref, acc_ref):
    i = pl.program_id(0)
    j = pl.program_id(1)
    valid = bv_ref[i] != 0

    @pl.when(j == 0)
    def _():
        acc_ref[...] = jnp.zeros_like(acc_ref)

    @pl.when(valid)
    def _():
        xb = x_ref[...]
        g = jnp.dot(xb, wg_ref[0], preferred_element_type=F32)
        u = jnp.dot(xb, wu_ref[0], preferred_element_type=F32)
        h = (g * _sigmoid(g) * u).astype(BF16)
        acc_ref[...] += jnp.dot(h, wd_ref[0], preferred_element_type=F32)

    @pl.when(j == pl.num_programs(1) - 1)
    def _():
        o_ref[...] = acc_ref[...].astype(BF16)


def moe_expert_ffn(block_e, block_valid, xrows, wg, wu, wd):
    n_rows = xrows.shape[0]
    grid_spec = pltpu.PrefetchScalarGridSpec(
        num_scalar_prefetch=2,
        grid=(n_rows // MOE_TM, D_FF_EXPERT // MOE_TF),
        in_specs=[pl.BlockSpec((MOE_TM, D_MODEL), lambda i, j, be, bv: (i * bv[i], 0)),
                  pl.BlockSpec((1, D_MODEL, MOE_TF), lambda i, j, be, bv: (be[i], 0, j)),
                  pl.BlockSpec((1, D_MODEL, MOE_TF), lambda i, j, be, bv: (be[i], 0, j)),
                  pl.BlockSpec((1, MOE_TF, D_MODEL), lambda i, j, be, bv: (be[i], j, 0))],
        out_specs=pl.BlockSpec((MOE_TM, D_MODEL), lambda i, j, be, bv: (i, 0)),
        scratch_shapes=[pltpu.VMEM((MOE_TM, D_MODEL), F32)],
    )
    return pl.pallas_call(
        _moe_ffn_kernel,
        out_shape=jax.ShapeDtypeStruct((n_rows, D_MODEL), BF16),
        grid_spec=grid_spec,
        compiler_params=_cparams(("arbitrary", "arbitrary")),
        name="moe_expert_ffn",
    )(block_e, block_valid, xrows, wg, wu, wd)


def _moe_combine_kernel(rb_ref, win_ref, flag_ref, dest_ref, w_ref, y_ref, x_ref, lg_ref, lb_ref, xo_ref, acc_ref):
    p = pl.program_id(0)
    flag = flag_ref[p]

    @pl.when((flag & 2) != 0)
    def _():
        acc_ref[...] = jnp.zeros_like(acc_ref)

    @pl.when((flag & 1) != 0)
    def _():
        d = dest_ref[...]
        wt = w_ref[...]
        cols = _iota2((MOE_WIN, MOE_TM), 1) + rb_ref[p] * MOE_TM
        sel = (jnp.where(d[:, 0:1] == cols, wt[:, 0:1], 0.0)
               + jnp.where(d[:, 1:2] == cols, wt[:, 1:2], 0.0)).astype(BF16)
        acc_ref[...] += jnp.dot(sel, y_ref[...], preferred_element_type=F32)

    @pl.when((flag & 4) != 0)
    def _():
        xo_ref[...] = _layer_norm_rows(DN_ALPHA * x_ref[...] + acc_ref[...], lg_ref[...], lb_ref[...])


def moe_combine(pair_rb, pair_win, pair_flag, dest_cols, w_cols, yrows, x2, ln_g, ln_b):
    n_pairs = pair_rb.shape[0]
    m = x2.shape[0]
    tok = lambda p, rb, win, fl: (win[p], 0)
    grid_spec = pltpu.PrefetchScalarGridSpec(
        num_scalar_prefetch=3,
        grid=(n_pairs,),
        in_specs=[pl.BlockSpec((MOE_WIN, TOP_K), tok),
                  pl.BlockSpec((MOE_WIN, TOP_K), tok),
                  pl.BlockSpec((MOE_TM, D_MODEL), lambda p, rb, win, fl: (rb[p], 0)),
                  pl.BlockSpec((MOE_WIN, D_MODEL), tok),
                  pl.BlockSpec((1, D_MODEL), lambda p, rb, win, fl: (0, 0)),
                  pl.BlockSpec((1, D_MODEL), lambda p, rb, win, fl: (0, 0))],
        out_specs=pl.BlockSpec((MOE_WIN, D_MODEL), tok),
        scratch_shapes=[pltpu.VMEM((MOE_WIN, D_MODEL), F32)],
    )
    return pl.pallas_call(
        _moe_combine_kernel,
        out_shape=jax.ShapeDtypeStruct((m, D_MODEL), F32),
        grid_spec=grid_spec,
        compiler_params=_cparams(("arbitrary",)),
        name="moe_combine",
    )(pair_rb, pair_win, pair_flag, dest_cols, w_cols, yrows, x2, ln_g, ln_b)


def _pair_tables(n_blk, blk_lo, n_pairs):
    cell_end = jnp.cumsum(n_blk)
    cell_start = cell_end - n_blk
    pidx = jnp.arange(n_pairs, dtype=jnp.int32)
    valid = pidx < cell_end[-1]
    cell = jnp.sum((cell_end[None, :] <= pidx[:, None]).astype(jnp.int32), axis=1)
    cell = jnp.minimum(cell, n_blk.shape[0] - 1)
    rb = blk_lo[cell] + pidx - cell_start[cell]
    return cell, rb, valid


def moe_layer(xb2, x2, w_router, wg, wu, wd, ln_g, ln_b):
    n = x2.shape[0]
    n_win = n // MOE_WIN
    n_rows = n * TOP_K + N_EXPERTS * MOE_TM
    n_blocks = n_rows // MOE_TM
    n_pairs = n_blocks + N_EXPERTS * n_win

    wr = jnp.pad(w_router, ((0, 0), (0, LANES - N_EXPERTS))).astype(BF16)
    top_e_t, top_w_t = moe_router(xb2, wr)
    top_e = top_e_t[:, :TOP_K]
    top_w = top_w_t[:, :TOP_K]

    experts = jnp.arange(N_EXPERTS, dtype=jnp.int32)
    tok_onehot = ((top_e[:, 0:1] == experts[None, :]) | (top_e[:, 1:2] == experts[None, :])).astype(jnp.int32)
    csum = jnp.cumsum(tok_onehot, axis=0)
    counts = csum[-1]
    rank = csum - tok_onehot
    padded = (counts + MOE_TM - 1) // MOE_TM * MOE_TM
    pad_end = jnp.cumsum(padded)
    pad_start = pad_end - padded
    dest = pad_start[top_e] + jnp.take_along_axis(rank, top_e, axis=1)

    blk_row0 = jnp.arange(n_blocks, dtype=jnp.int32) * MOE_TM
    block_e = jnp.minimum(jnp.sum((pad_end[None, :] <= blk_row0[:, None]).astype(jnp.int32), axis=1),
                          N_EXPERTS - 1).astype(jnp.int32)
    block_valid = (blk_row0 < pad_end[-1]).astype(jnp.int32)

    cell_cnt = tok_onehot.reshape(n_win, MOE_WIN, N_EXPERTS).sum(axis=1)
    cell_lo = pad_start[None, :] + jnp.cumsum(cell_cnt, axis=0) - cell_cnt
    blk_lo = cell_lo // MOE_TM
    n_blk = jnp.where(cell_cnt > 0, (cell_lo + cell_cnt - 1) // MOE_TM - blk_lo + 1, 0)

    cell, g_rb, g_valid = _pair_tables(n_blk.T.reshape(-1), blk_lo.T.reshape(-1), n_pairs)
    g_win = cell % n_win
    last_rb = jnp.max(jnp.where(g_valid, g_rb, 0))
    g_rb = jnp.where(g_valid, g_rb, last_rb).astype(jnp.int32)
    g_win = jnp.where(g_valid, g_win, 0).astype(jnp.int32)
    prev_rb = jnp.concatenate([jnp.full((1,), -1, jnp.int32), g_rb[:-1]])
    g_flag = g_valid.astype(jnp.int32) + 2 * (g_valid & (g_rb != prev_rb)).astype(jnp.int32)
    dest_rows = dest.reshape(n_win, MOE_WIN, TOP_K).transpose(0, 2, 1)
    xrows = moe_gather(g_rb, g_win, g_flag, dest_rows, xb2, n_rows)

    yrows = moe_expert_ffn(block_e, block_valid, xrows, wg, wu, wd)

    cell, c_rb, c_valid = _pair_tables(n_blk.reshape(-1), blk_lo.reshape(-1), n_pairs)
    c_win = jnp.where(c_valid, cell // N_EXPERTS, n_win - 1).astype(jnp.int32)
    c_rb = jnp.where(c_valid, c_rb, 0).astype(jnp.int32)
    prev_win = jnp.concatenate([jnp.full((1,), -1, jnp.int32), c_win[:-1]])
    next_win = jnp.concatenate([c_win[1:], jnp.full((1,), -1, jnp.int32)])
    next_valid = jnp.concatenate([c_valid[1:], jnp.zeros((1,), bool)])
    c_first = c_valid & (c_win != prev_win)
    c_last = c_valid & ((c_win != next_win) | jnp.logical_not(next_valid))
    c_flag = c_valid.astype(jnp.int32) + 2 * c_first.astype(jnp.int32) + 4 * c_last.astype(jnp.int32)
    return moe_combine(c_rb, c_win, c_flag, dest, top_w, yrows, x2, ln_g, ln_b)


def _w_in_perm():
    zero = 3480
    perm = np.full((D_IN_P,), zero, np.int64)
    o_qkv, o_z, o_beta, o_decay, o_bq, o_bk, o_bv, o_cf, o_ci, o_cq, o_cg = (
        0, 1152, 1536, 1548, 1560, 1944, 2072, 2200, 2712, 2968, 3224)
    perm[COL_A_QKV:COL_A_QKV + 1152] = np.arange(o_qkv, o_qkv + 1152)
    perm[COL_A_Z:COL_A_Z + 384] = np.arange(o_z, o_z + 384)
    for j in range(3):
        for a in range(2):
            dst = COL_B_Q + j * LANES + a * HEAD_DIM
            src = o_bq + (j + 3 * a) * HEAD_DIM
            perm[dst:dst + HEAD_DIM] = np.arange(src, src + HEAD_DIM)
    perm[COL_B_K:COL_B_K + 128] = np.arange(o_bk, o_bk + 128)
    perm[COL_C_G:COL_C_G + 256] = np.arange(o_cg, o_cg + 256)
    perm[COL_B_V:COL_B_V + 128] = np.arange(o_bv, o_bv + 128)
    perm[COL_C_F:COL_C_F + 512] = np.arange(o_cf, o_cf + 512)
    perm[COL_C_I:COL_C_I + 256] = np.arange(o_ci, o_ci + 256)
    perm[COL_C_Q:COL_C_Q + 256] = np.arange(o_cq, o_cq + 256)
    for d, col in enumerate((COL_GATES0, COL_GATES1)):
        for p in range(A_HEADS // 2):
            dst = col + 4 * p
            perm[dst + 0] = o_beta + d * A_HEADS + 2 * p
            perm[dst + 1] = o_beta + d * A_HEADS + 2 * p + 1
            perm[dst + 2] = o_decay + d * A_HEADS + 2 * p
            perm[dst + 3] = o_decay + d * A_HEADS + 2 * p + 1
    return perm


def _w_out_perm():
    perm = np.arange(D_MIX)
    for j in range(3):
        for a in range(2):
            dst = A_WIDTH + j * LANES + a * HEAD_DIM
            src = A_WIDTH + (j + 3 * a) * HEAD_DIM
            perm[dst:dst + HEAD_DIM] = np.arange(src, src + HEAD_DIM)
    return perm


def _gdn_params(a_log, dt_bias):
    rows = np.array([4 * (h // 2) + 2 + h % 2 for h in range(A_HEADS)])
    prm = jnp.zeros((2, GATE_ROWS, LANES), F32)
    prm = prm.at[:, rows, 0].set(a_log.astype(F32))
    prm = prm.at[:, rows, 1].set(dt_bias.astype(F32))
    return prm.at[:, rows, 2].set(1.0)


def kernel(x, mem, w_in, conv_w, gdn_a_log, gdn_dt_bias, gdn_norm_w, q_norm_w, k_norm_w, hgrn_lb_logits,
           hgrn_norm_w, w_out, ln1_g, ln1_b, xq, xk, xv, xo, ln2_g, ln2_b, ffn_wg, ffn_wu, ffn_wd,
           moe_router, moe_wg, moe_wu, moe_wd, ln3_g, ln3_b):
    b, s, d = x.shape
    m = b * s
    cos_t, sin_t = rope_tables(s)
    lb_p = jax.nn.softmax(hgrn_lb_logits.astype(F32), axis=0)
    lb_c = jnp.cumsum(lb_p, axis=0)
    lower_bounds = lb_c - lb_c[0]
    in_perm = _w_in_perm()
    out_perm = _w_out_perm()
    mem_b = mem.astype(BF16).reshape(b * mem.shape[1], d)
    tile2 = lambda w: jnp.tile(w.astype(F32), 2)[None, :]
    row = lambda v: v.astype(F32)[None, :]

    x2 = x.reshape(m, d)
    xb2 = x2.astype(BF16)
    for l in range(DEPTH):
        w_in_p = jnp.take(jnp.pad(w_in[l], ((0, 0), (0, 1))), in_perm, axis=1).astype(BF16)
        proj2 = matmul(xb2, w_in_p, F32, min(1024, m), 768)
        proj3 = proj2.reshape(b, s, D_IN_P)

        oa_f, oa_b = gdn_scan(proj3, conv_w[l], _gdn_params(gdn_a_log[l], gdn_dt_bias[l]))
        ob = gqa_attention(proj3, cos_t, sin_t, tile2(q_norm_w[l]), tile2(k_norm_w[l]), s)
        lb = lower_bounds[l][None, :]
        oc_f = hgrn_direction(proj3, COL_C_F // C_WIDTH, lb, False)
        oc_b = hgrn_direction(proj3, COL_C_F // C_WIDTH + 1, lb, True)

        w_out_p = jnp.take(w_out[l], out_perm, axis=0).astype(BF16)
        x2, xb2 = mixer_output(oa_f.reshape(m, A_WIDTH), oa_b.reshape(m, A_WIDTH), proj2,
                               ob.reshape(m, B_WIDTH), oc_f.reshape(m, C_WIDTH), oc_b.reshape(m, C_WIDTH),
                               x2, tile2(gdn_norm_w[l]), tile2(hgrn_norm_w[l]), w_out_p,
                               row(ln1_g[l]), row(ln1_b[l]))

        k3 = matmul(mem_b, xk[l].astype(BF16), BF16, 256, 512).reshape(b, -1, d)
        v3 = matmul(mem_b, xv[l].astype(BF16), BF16, 256, 512).reshape(b, -1, d)
        x3, xb3 = cross_attention(xb2.reshape(b, s, d), x2.reshape(b, s, d), k3, v3,
                                  xq[l].astype(BF16), xo[l].astype(BF16), row(ln2_g[l]), row(ln2_b[l]))
        x2, xb2 = x3.reshape(m, d), xb3.reshape(m, d)

        if l % 2 == 0:
            x2, xb2 = dense_ffn(xb2, x2, ffn_wg[l // 2].astype(BF16), ffn_wu[l // 2].astype(BF16),
                                ffn_wd[l // 2].astype(BF16), row(ln3_g[l]), row(ln3_b[l]))
        else:
            x2 = moe_layer(xb2, x2, moe_router[l // 2], moe_wg[l // 2].astype(BF16),
                           moe_wu[l // 2].astype(BF16), moe_wd[l // 2].astype(BF16),
                           row(ln3_g[l]), row(ln3_b[l]))
            xb2 = x2.astype(BF16)
    return x2.reshape(b, s, d)
```

```python
import functools
import math

import numpy as np
import jax
import jax.numpy as jnp
from jax import lax
from jax.experimental import pallas as pl
from jax.experimental.pallas import tpu as pltpu

F32 = jnp.float32
BF16 = jnp.bfloat16

D_MODEL = 1024
DEPTH = 2
HEAD_DIM = 64
A_HEADS = 6
A_WIDTH = A_HEADS * HEAD_DIM
B_Q_HEADS = 6
B_KV_HEADS = 2
B_WIDTH = B_Q_HEADS * HEAD_DIM
C_HEADS = 4
C_WIDTH = C_HEADS * HEAD_DIM
D_MIX = A_WIDTH + B_WIDTH + C_WIDTH
CONV_K = 5
CHUNK = 64
GRID_W = 64
ROPE_AXIS_DIM = HEAD_DIM // 2
ROPE_THETA = 10000.0
X_HEADS = 4
X_HEAD_DIM = D_MODEL // X_HEADS
D_FF = 2816
N_EXPERTS = 8
TOP_K = 2
D_FF_EXPERT = 3584
DN_ALPHA = (2 * DEPTH) ** 0.25
LN_EPS = 1e-5
RMS_EPS = 1e-6

LANES = 128
SUBLANES = 8
HALO = 16
GROUP = 256
CHUNKS_PER_GROUP = GROUP // CHUNK
VMEM_LIMIT = 56 * 1024 * 1024

COL_A_QKV = 0
COL_A_Z = 1152
COL_B_Q = 1536
COL_B_K = 1920
COL_C_G = 2048
COL_B_V = 2304
COL_C_I = 2560
COL_C_Q = 2816
D_WIDE = 3072
COL_C_F = D_WIDE
COL_GATES0 = D_WIDE + 512
COL_GATES1 = D_WIDE + 640
D_IN_P = 3840


def _cparams(sem):
    return pltpu.CompilerParams(dimension_semantics=sem, vmem_limit_bytes=VMEM_LIMIT)


def _sigmoid(x):
    return 1.0 / (1.0 + jnp.exp(-x))


def _softplus(x):
    return jnp.maximum(x, 0.0) + jnp.log1p(jnp.exp(-jnp.abs(x)))


def _dot(a, b):
    return jnp.dot(a.astype(BF16), b.astype(BF16), preferred_element_type=F32)


def _dot_nt(a, b):
    return lax.dot_general(a.astype(BF16), b.astype(BF16), (((1,), (1,)), ((), ())),
                           preferred_element_type=F32)


def _dot_sel(m01, x):
    m = jnp.where(m01, 1.0, 0.0).astype(BF16)
    hi = x.astype(BF16)
    r1 = x - hi.astype(F32)
    mid = r1.astype(BF16)
    lo = (r1 - mid.astype(F32)).astype(BF16)
    out = jnp.dot(m, hi, preferred_element_type=F32)
    out = out + jnp.dot(m, mid, preferred_element_type=F32)
    return out + jnp.dot(m, lo, preferred_element_type=F32)


def _dot_sel_right(x, m01):
    m = jnp.where(m01, 1.0, 0.0).astype(BF16)
    hi = x.astype(BF16)
    r1 = x - hi.astype(F32)
    mid = r1.astype(BF16)
    lo = (r1 - mid.astype(F32)).astype(BF16)
    out = jnp.dot(hi, m, preferred_element_type=F32)
    out = out + jnp.dot(mid, m, preferred_element_type=F32)
    return out + jnp.dot(lo, m, preferred_element_type=F32)


def _layer_norm_rows(y, g, b):
    mu = jnp.mean(y, axis=-1, keepdims=True)
    yc = y - mu
    var = jnp.mean(yc * yc, axis=-1, keepdims=True)
    return yc * lax.rsqrt(var + LN_EPS) * g + b


def _iota2(shape, dim):
    return lax.broadcasted_iota(jnp.int32, shape, dim)


def _head_sum(x, first_head):
    s0 = jnp.sum(jnp.where(first_head, x, 0.0), axis=-1, keepdims=True)
    s1 = jnp.sum(jnp.where(first_head, 0.0, x), axis=-1, keepdims=True)
    return jnp.where(first_head, s0, s1)


def _block_rows(t, block, row_in_block):
    parts = [jnp.broadcast_to(t[b * block + row_in_block:b * block + row_in_block + 1, :], (block, t.shape[1]))
             for b in range(GROUP // block)]
    return jnp.concatenate(parts, axis=0)


def _mm_kernel(x_ref, w_ref, o_ref):
    o_ref[...] = jnp.dot(x_ref[...], w_ref[...], preferred_element_type=F32).astype(o_ref.dtype)


def matmul(x, w, out_dtype, tm, tn):
    m, k = x.shape
    n = w.shape[1]
    return pl.pallas_call(
        _mm_kernel,
        out_shape=jax.ShapeDtypeStruct((m, n), out_dtype),
        grid=(n // tn, m // tm),
        in_specs=[pl.BlockSpec((tm, k), lambda j, i: (i, 0)),
                  pl.BlockSpec((k, tn), lambda j, i: (0, j))],
        out_specs=pl.BlockSpec((tm, tn), lambda j, i: (i, j)),
        compiler_params=_cparams(("parallel", "arbitrary")),
        name="matmul",
    )(x, w)


def _scan_masks(reverse):
    ri = _iota2((GROUP, GROUP), 0)
    ci = _iota2((GROUP, GROUP), 1)
    same_chunk = (ri >> 6) == (ci >> 6)
    if reverse:
        return ri, ci, same_chunk, same_chunk & (ri <= ci), same_chunk & (ri < ci)
    return ri, ci, same_chunk, same_chunk & (ri >= ci), same_chunk & (ri > ci)


GATE_ROWS = 16


class _GdnProblem:
    def __init__(self, **kw):
        self.__dict__.update(kw)


def _gdn_group(probs, same_chunk, h0):
    nh0 = jnp.logical_not(h0)
    sr = _iota2((CHUNK, GROUP), 0)
    sc = _iota2((CHUNK, GROUP), 1)
    eye_side = jnp.where(sr == (sc & (CHUNK - 1)), 1.0, 0.0)
    tl = _iota2((LANES, GROUP), 1)
    br = _iota2((LANES, LANES), 0)
    bc = _iota2((LANES, LANES), 1)
    bd128 = (br >> 6) == (bc >> 6)

    def block_diag(side):
        return jnp.where(same_chunk, jnp.concatenate([side] * CHUNKS_PER_GROUP, axis=0), 0.0).astype(BF16)

    for pr in probs:
        pr.q = pr.q * lax.rsqrt(_head_sum(pr.q * pr.q, h0) + RMS_EPS) * (HEAD_DIM ** -0.5)
        pr.k = pr.k * lax.rsqrt(_head_sum(pr.k * pr.k, h0) + RMS_EPS)
        pr.gedge = _block_rows(pr.gc, CHUNK, 0 if pr.reverse else CHUNK - 1)
        pr.exp_g = jnp.exp(pr.gc)
        pr.kb = pr.k * pr.beta_t
        pr.rhs = jnp.concatenate([pr.v * pr.beta_t, pr.kb * pr.exp_g], axis=1).astype(BF16)
        pr.k16 = pr.k.astype(BF16)
        pr.decay, pr.x, pr.pw, pr.bd = [], [], [], []
        for h in range(2):
            decay = jnp.exp(jnp.where(pr.incl, pr.g_col[h] - pr.g_row[h], -1e30))
            kk = _dot_nt(jnp.where(h0 if h == 0 else nh0, pr.kb, 0.0), pr.k16)
            low = jnp.where(pr.strict, kk * decay, 0.0)
            a = -(low[0:CHUNK] + low[CHUNK:2 * CHUNK] + low[2 * CHUNK:3 * CHUNK] + low[3 * CHUNK:4 * CHUNK])
            pr.decay.append(decay)
            pr.x.append(eye_side + a)
            pr.pw.append(a)
            pr.bd.append(block_diag(a))

    for _ in range(5):
        for pr in probs:
            for h in range(2):
                pr.pw[h] = jnp.dot(pr.pw[h].astype(BF16), pr.bd[h], preferred_element_type=F32)
        for pr in probs:
            for h in range(2):
                pr.bd[h] = block_diag(pr.pw[h])
        for pr in probs:
            for h in range(2):
                pr.x[h] = pr.x[h] + jnp.dot(pr.x[h].astype(BF16), pr.bd[h], preferred_element_type=F32)

    for pr in probs:
        sols = [jnp.dot(block_diag(pr.x[h]), pr.rhs, preferred_element_type=F32) for h in range(2)]
        u = jnp.where(h0, sols[0][:, :LANES], sols[1][:, :LANES])
        w = jnp.where(h0, sols[0][:, LANES:], sols[1][:, LANES:])
        wu = jnp.concatenate([w, u], axis=1)
        wu16 = wu.astype(BF16)
        aw = []
        for h in range(2):
            attn = jnp.where(pr.incl, _dot_nt(jnp.where(h0 if h == 0 else nh0, pr.q, 0.0), pr.k16) * pr.decay[h], 0.0)
            aw.append(jnp.dot(attn.astype(BF16), wu16, preferred_element_type=F32))
        pr.qp = (pr.q * pr.exp_g - jnp.where(h0, aw[0][:, :LANES], aw[1][:, :LANES])).astype(BF16)
        pr.op = jnp.where(h0, aw[0][:, LANES:], aw[1][:, LANES:])
        kd_t = (pr.k * jnp.exp(pr.gedge - pr.gc)).T
        pr.kmat, pr.nmat = [], []
        for c in range(CHUNKS_PER_GROUP):
            km = jnp.dot(jnp.where((tl >> 6) == c, kd_t, 0.0), wu, preferred_element_type=F32)
            pr.kmat.append(jnp.where(bd128, km[:, :LANES], 0.0).astype(BF16))
            pr.nmat.append(jnp.where(bd128, km[:, LANES:], 0.0))
        pr.out = [None] * CHUNKS_PER_GROUP

    for t in range(CHUNKS_PER_GROUP):
        for pr in probs:
            c = CHUNKS_PER_GROUP - 1 - t if pr.reverse else t
            rs = slice(c * CHUNK, (c + 1) * CHUNK)
            s16 = pr.state.astype(BF16)
            pr.out[c] = jnp.dot(pr.qp[rs], s16, preferred_element_type=F32) + pr.op[rs]
            cd = jnp.exp(pr.gedge[c * CHUNK:c * CHUNK + 1, :])
            pr.state = pr.state * cd - jnp.dot(pr.kmat[c], s16, preferred_element_type=F32) + pr.nmat[c]
    return [jnp.concatenate(pr.out, axis=0) for pr in probs]


def _gdn_kernel(xf_ref, xfp_ref, xfn_ref, xb_ref, xbp_ref, xbn_ref, cw_ref, gf_ref, gb_ref, prm_ref,
                of_ref, ob_ref, s_scr):
    n = pl.program_id(1)
    n_groups = pl.num_programs(1)

    @pl.when(n == 0)
    def _():
        s_scr[...] = jnp.zeros_like(s_scr)

    lane = _iota2((GROUP, LANES), 1)
    h0 = lane < HEAD_DIM
    ri = _iota2((GROUP, GROUP), 0)
    ci = _iota2((GROUP, GROUP), 1)
    same_chunk = (ri >> 6) == (ci >> 6)
    cw = cw_ref[...]
    base = SUBLANES - CONV_K // 2
    pairs = A_HEADS // 2

    probs = []
    for d, (gi, x_ref, xp_ref, xn_ref, g_ref) in enumerate(((n, xf_ref, xfp_ref, xfn_ref, gf_ref),
                                                            (n_groups - 1 - n, xb_ref, xbp_ref, xbn_ref, gb_ref))):
        reverse = d == 1
        lower, upper = same_chunk & (ri >= ci), same_chunk & (ri <= ci)
        incl = upper if reverse else lower
        strict = same_chunk & ((ri < ci) if reverse else (ri > ci))

        prev = jnp.where(gi > 0, xp_ref[0].astype(F32)[HALO - SUBLANES:], 0.0)
        nxt = jnp.where(gi < n_groups - 1, xn_ref[0].astype(F32)[:SUBLANES], 0.0)
        ext = jnp.concatenate([prev, x_ref[0].astype(F32), nxt], axis=0)
        acc = ext[base:base + GROUP, :] * cw[0:1, :]
        for t in range(1, CONV_K):
            acc = acc + ext[base + t:base + t + GROUP, :] * cw[t:t + 1, :]
        qkv = acc * _sigmoid(acc)

        gt_t = g_ref[0].T[0:GATE_ROWS, :]
        prm = prm_ref[d]
        e_rows = jnp.where(prm[:, 2:3] > 0.5,
                           -jnp.exp(prm[:, 0:1]) * _softplus(gt_t + prm[:, 1:2]), _sigmoid(gt_t))
        g_rows = _dot_sel_right(e_rows, lower if reverse else upper)
        slab = jnp.concatenate([e_rows, g_rows, jnp.zeros((LANES - 2 * GATE_ROWS, GROUP), F32)], axis=0)
        cols = slab.T
        for p in range(pairs):
            sl = lambda part: slice(part * A_WIDTH + p * LANES, part * A_WIDTH + (p + 1) * LANES)
            col = lambda r: cols[:, r:r + 1]
            probs.append(_GdnProblem(
                reverse=reverse, incl=incl, strict=strict, slot=d * pairs + p,
                q=qkv[:, sl(0)], k=qkv[:, sl(1)], v=qkv[:, sl(2)],
                beta_t=jnp.where(h0, col(4 * p), col(4 * p + 1)),
                gc=jnp.where(h0, col(GATE_ROWS + 4 * p + 2), col(GATE_ROWS + 4 * p + 3)),
                g_col=(col(GATE_ROWS + 4 * p + 2), col(GATE_ROWS + 4 * p + 3)),
                g_row=(g_rows[4 * p + 2:4 * p + 3, :], g_rows[4 * p + 3:4 * p + 4, :]),
                state=s_scr[d * pairs + p]))

    outs = _gdn_group(probs, same_chunk, h0)
    for pr, out in zip(probs, outs):
        o_ref = ob_ref if pr.reverse else of_ref
        p = pr.slot % pairs
        o_ref[0, :, p * LANES:(p + 1) * LANES] = out.astype(BF16)
        s_scr[pr.slot] = pr.state


def gdn_scan(wide3, gates3, conv_w, prm):
    b, seq_len, _ = wide3.shape
    n_groups = seq_len // GROUP
    halo_per_group = GROUP // HALO
    n_halo = seq_len // HALO
    width = 3 * A_WIDTH
    fwd = lambda n: n
    bwd = lambda n: n_groups - 1 - n

    def x_specs(gidx):
        return [pl.BlockSpec((1, GROUP, width), lambda i, n: (i, gidx(n), 0)),
                pl.BlockSpec((1, HALO, width),
                             lambda i, n: (i, jnp.maximum(gidx(n) * halo_per_group - 1, 0), 0)),
                pl.BlockSpec((1, HALO, width),
                             lambda i, n: (i, jnp.minimum((gidx(n) + 1) * halo_per_group, n_halo - 1), 0))]

    out_sds = jax.ShapeDtypeStruct((b, seq_len, A_WIDTH), BF16)
    return pl.pallas_call(
        _gdn_kernel,
        out_shape=(out_sds, out_sds),
        grid=(b, n_groups),
        in_specs=x_specs(fwd) + x_specs(bwd) + [
            pl.BlockSpec((CONV_K, width), lambda i, n: (0, 0)),
            pl.BlockSpec((1, GROUP, LANES), lambda i, n: (i, fwd(n), (COL_GATES0 - D_WIDE) // LANES)),
            pl.BlockSpec((1, GROUP, LANES), lambda i, n: (i, bwd(n), (COL_GATES1 - D_WIDE) // LANES)),
            pl.BlockSpec((2, GATE_ROWS, LANES), lambda i, n: (0, 0, 0))],
        out_specs=(pl.BlockSpec((1, GROUP, A_WIDTH), lambda i, n: (i, fwd(n), 0)),
                   pl.BlockSpec((1, GROUP, A_WIDTH), lambda i, n: (i, bwd(n), 0))),
        scratch_shapes=[pltpu.VMEM((A_HEADS, LANES, LANES), F32)],
        compiler_params=_cparams(("parallel", "arbitrary")),
        name="gdn_scan",
    )(wide3, wide3, wide3, wide3, wide3, wide3, conv_w, gates3, gates3, prm)


EXP_CLAMP = 60.0


def _hgrn_pair(f_raw, v, q_raw, lb, state_t, masks, reverse):
    ri, ci, same_chunk, incl, _ = masks
    f = lb + (1.0 - lb) * _sigmoid(f_raw)
    lf = jnp.log(f)
    kk = 1.0 - f
    q = q_raw * _sigmoid(q_raw) * (HEAD_DIM ** -0.5)
    lane = _iota2((GROUP, LANES), 1)
    h0 = lane < HEAD_DIM

    cum = _dot_sel(incl, lf)
    cedge = _block_rows(cum, CHUNK, 0 if reverse else CHUNK - 1)
    qd = q * jnp.exp(cum)
    kd = kk * jnp.exp(cedge - cum)

    if reverse:
        m32 = same_chunk & ((ri & 63) < 32) & ((ci & 63) >= 32)
        m16 = ((ri >> 5) == (ci >> 5)) & ((ri & 31) < 16) & ((ci & 31) >= 16)
        mdg = ((ri >> 4) == (ci >> 4)) & (ri <= ci)
        ref32, ref16, refdg = _block_rows(cum, 64, 32), _block_rows(cum, 32, 16), _block_rows(cum, 16, 8)
    else:
        m32 = same_chunk & ((ri & 63) >= 32) & ((ci & 63) < 32)
        m16 = ((ri >> 5) == (ci >> 5)) & ((ri & 31) >= 16) & ((ci & 31) < 16)
        mdg = ((ri >> 4) == (ci >> 4)) & (ri >= ci)
        ref32, ref16, refdg = _block_rows(cum, 64, 31), _block_rows(cum, 32, 15), _block_rows(cum, 16, 7)
    qs32 = q * jnp.exp(jnp.minimum(cum - ref32, 0.0))
    ks32 = (kk * jnp.exp(jnp.minimum(ref32 - cum, 0.0))).astype(BF16)
    qs16 = q * jnp.exp(jnp.minimum(cum - ref16, 0.0))
    ks16 = (kk * jnp.exp(jnp.minimum(ref16 - cum, 0.0))).astype(BF16)
    qsdg = q * jnp.exp(jnp.clip(cum - refdg, -EXP_CLAMP, EXP_CLAMP))
    ksdg = (kk * jnp.exp(jnp.clip(refdg - cum, -EXP_CLAMP, EXP_CLAMP))).astype(BF16)

    v16 = v.astype(BF16)
    intra = []
    for h in range(2):
        mh = h0 if h == 0 else jnp.logical_not(h0)
        a = jnp.where(m32, _dot_nt(jnp.where(mh, qs32, 0.0), ks32),
                      jnp.where(m16, _dot_nt(jnp.where(mh, qs16, 0.0), ks16),
                                jnp.where(mdg, _dot_nt(jnp.where(mh, qsdg, 0.0), ksdg), 0.0)))
        intra.append(jnp.dot(a.astype(BF16), v16, preferred_element_type=F32))
    intra = jnp.where(h0, intra[0], intra[1])

    v_t = v.T
    tl = _iota2((LANES, GROUP), 1)
    br = _iota2((LANES, LANES), 0)
    bc = _iota2((LANES, LANES), 1)
    bd128 = (br >> 6) == (bc >> 6)

    outs = [None] * CHUNKS_PER_GROUP
    order = range(CHUNKS_PER_GROUP - 1, -1, -1) if reverse else range(CHUNKS_PER_GROUP)
    for c in order:
        rs = slice(c * CHUNK, (c + 1) * CHUNK)
        outs[c] = _dot_nt(qd[rs], state_t) + intra[rs]
        upd = jnp.dot(jnp.where((tl >> 6) == c, v_t, 0.0), kd, preferred_element_type=F32)
        cd = jnp.exp(cedge[c * CHUNK:c * CHUNK + 1, :])
        state_t = state_t * cd + jnp.where(bd128, upd, 0.0)
    return jnp.concatenate(outs, axis=0), state_t


def _hgrn_kernel(f_ref, i_ref, q_ref, lb_ref, o_ref, s_scr, *, reverse):
    n = pl.program_id(1)

    @pl.when(n == 0)
    def _():
        s_scr[...] = jnp.zeros_like(s_scr)

    masks = _scan_masks(reverse)
    for p in range(C_HEADS // 2):
        sl = slice(p * LANES, (p + 1) * LANES)
        out, state_t = _hgrn_pair(f_ref[0, :, sl], i_ref[0, :, sl].astype(F32), q_ref[0, :, sl].astype(F32),
                                  lb_ref[:, sl], s_scr[p], masks, reverse)
        o_ref[0, :, sl] = out.astype(BF16)
        s_scr[p] = state_t


def hgrn_direction(wide3, gates3, direction, lb):
    b, seq_len, _ = wide3.shape
    n_groups = seq_len // GROUP
    reverse = direction == 1
    gidx = (lambda n: n_groups - 1 - n) if reverse else (lambda n: n)
    spec = lambda blk: pl.BlockSpec((1, GROUP, C_WIDTH), lambda i, n: (i, gidx(n), blk))
    return pl.pallas_call(
        functools.partial(_hgrn_kernel, reverse=reverse),
        out_shape=jax.ShapeDtypeStruct((b, seq_len, C_WIDTH), BF16),
        grid=(b, n_groups),
        in_specs=[spec((COL_C_F - D_WIDE) // C_WIDTH + direction), spec(COL_C_I // C_WIDTH),
                  spec(COL_C_Q // C_WIDTH), pl.BlockSpec((1, C_WIDTH), lambda i, n: (0, 0))],
        out_specs=pl.BlockSpec((1, GROUP, C_WIDTH), lambda i, n: (i, gidx(n), 0)),
        scratch_shapes=[pltpu.VMEM((C_HEADS // 2, LANES, LANES), F32)],
        compiler_params=_cparams(("parallel", "arbitrary")),
        name="hgrn_scan",
    )(gates3, wide3, wide3, lb)


ATT_TQ = 256


def _norm_rope(x, w, cos, sin_signed, first_half16):
    lane = _iota2(x.shape, 1)
    h0 = lane < HEAD_DIM
    ms = _head_sum(x * x, h0) * (1.0 / HEAD_DIM)
    xn = x * lax.rsqrt(ms + RMS_EPS) * w
    partner = jnp.where(first_half16, pltpu.roll(xn, LANES - 16, 1), pltpu.roll(xn, 16, 1))
    return xn * cos + partner * sin_signed


def _attn_kernel(q_ref, k_ref, v_ref, cosq_ref, sinq_ref, cosk_ref, sink_ref, qw_ref, kw_ref, o_ref,
                 k_scr, v_scr):
    qi = pl.program_id(1)

    @pl.when(qi == 0)
    def _():
        kx = k_ref[0].astype(F32)
        lane_k = _iota2(kx.shape, 1)
        k_scr[...] = _norm_rope(kx, kw_ref[...], cosk_ref[...], sink_ref[...], (lane_k & 31) < 16).astype(BF16)
        v_scr[:, :LANES] = v_ref[0]
        v_scr[:, LANES:] = jnp.ones((v_scr.shape[0], LANES), BF16)

    lane = _iota2((ATT_TQ, LANES), 1)
    h0 = lane < HEAD_DIM
    fh = (lane & 31) < 16
    cos = cosq_ref[...]
    sin = sinq_ref[...]
    kmat = k_scr[...]
    vmat = v_scr[...]
    q_scale = (HEAD_DIM ** -0.5) * math.log2(math.e)
    for j in range(B_Q_HEADS // 2):
        qn = _norm_rope(q_ref[0, :, j * LANES:(j + 1) * LANES].astype(F32), qw_ref[...], cos, sin, fh) * q_scale
        outs = []
        for half in range(2):
            mh = h0 if half == 0 else jnp.logical_not(h0)
            s = _dot_nt(jnp.where(mh, qn, 0.0), kmat)
            m = jnp.max(s, axis=-1, keepdims=True)
            e = jnp.exp2((s - m).astype(BF16))
            pvl = jnp.dot(e, vmat, preferred_element_type=F32)
            outs.append(pvl[:, :LANES] / pvl[:, LANES:LANES + 1])
        o_ref[0, :, j * LANES:(j + 1) * LANES] = jnp.where(h0, outs[0], outs[1]).astype(BF16)


def gqa_attention(proj, cos_t, sin_t, q_w, k_w, seq_len):
    b = proj.shape[0]
    return pl.pallas_call(
        _attn_kernel,
        out_shape=jax.ShapeDtypeStruct((b, seq_len, B_WIDTH), BF16),
        grid=(b, seq_len // ATT_TQ),
        in_specs=[pl.BlockSpec((1, ATT_TQ, B_WIDTH), lambda i, t: (i, t, COL_B_Q // B_WIDTH)),
                  pl.BlockSpec((1, seq_len, LANES), lambda i, t: (i, 0, COL_B_K // LANES)),
                  pl.BlockSpec((1, seq_len, LANES), lambda i, t: (i, 0, COL_B_V // LANES)),
                  pl.BlockSpec((ATT_TQ, LANES), lambda i, t: (t, 0)),
                  pl.BlockSpec((ATT_TQ, LANES), lambda i, t: (t, 0)),
                  pl.BlockSpec((seq_len, LANES), lambda i, t: (0, 0)),
                  pl.BlockSpec((seq_len, LANES), lambda i, t: (0, 0)),
                  pl.BlockSpec((1, LANES), lambda i, t: (0, 0)),
                  pl.BlockSpec((1, LANES), lambda i, t: (0, 0))],
        out_specs=pl.BlockSpec((1, ATT_TQ, B_WIDTH), lambda i, t: (i, t, 0)),
        scratch_shapes=[pltpu.VMEM((seq_len, LANES), BF16), pltpu.VMEM((seq_len, 2 * LANES), BF16)],
        compiler_params=_cparams(("parallel", "arbitrary")),
        name="gqa_attention",
    )(proj, proj, proj, cos_t, sin_t, cos_t, sin_t, q_w, k_w)


def rope_tables(seq_len):
    rows = seq_len // GRID_W
    row = jnp.repeat(jnp.arange(rows, dtype=F32), GRID_W)
    col = jnp.tile(jnp.arange(GRID_W, dtype=F32), rows)
    inv_freq = ROPE_THETA ** (-jnp.arange(0, ROPE_AXIS_DIM, 2, dtype=F32) / ROPE_AXIS_DIM)
    ang_r = row[:, None] * inv_freq
    ang_c = col[:, None] * inv_freq
    cos64 = jnp.concatenate([jnp.cos(ang_r), jnp.cos(ang_r), jnp.cos(ang_c), jnp.cos(ang_c)], axis=1)
    sin64 = jnp.concatenate([-jnp.sin(ang_r), jnp.sin(ang_r), -jnp.sin(ang_c), jnp.sin(ang_c)], axis=1)
    return jnp.tile(cos64, (1, 2)), jnp.tile(sin64, (1, 2))


MIX_TM = 256


def _mixout_kernel(oaf_ref, oab_ref, z_ref, ob_ref, ocf_ref, ocb_ref, g_ref, x_ref, wa_ref, wc_ref,
                   wo_ref, lg_ref, lb_ref, xo_ref, xb_ref):
    lane = _iota2((MIX_TM, LANES), 1)
    h0 = lane < HEAD_DIM
    parts = []
    for j in range(A_WIDTH // LANES):
        sl = slice(j * LANES, (j + 1) * LANES)
        o = oaf_ref[:, sl].astype(F32) + oab_ref[:, sl].astype(F32)
        ms = _head_sum(o * o, h0) * (1.0 / HEAD_DIM)
        z = z_ref[:, sl].astype(F32)
        parts.append((o * lax.rsqrt(ms + RMS_EPS) * wa_ref[...] * (z * _sigmoid(z))).astype(BF16))
    parts.append(ob_ref[...])
    for j in range(C_WIDTH // LANES):
        sl = slice(j * LANES, (j + 1) * LANES)
        o = ocf_ref[:, sl].astype(F32) + ocb_ref[:, sl].astype(F32)
        ms = _head_sum(o * o, h0) * (1.0 / HEAD_DIM)
        parts.append((o * lax.rsqrt(ms + RMS_EPS) * wc_ref[...]
                      * _sigmoid(g_ref[:, sl].astype(F32))).astype(BF16))
    mixed = jnp.concatenate(parts, axis=1)
    h = jnp.dot(mixed, wo_ref[...], preferred_element_type=F32)
    y = _layer_norm_rows(DN_ALPHA * x_ref[...] + h, lg_ref[...], lb_ref[...])
    xo_ref[...] = y
    xb_ref[...] = y.astype(BF16)


def mixer_output(oa_f, oa_b, proj2, ob, oc_f, oc_b, x2, gdn_w, hgrn_w, w_out, ln_g, ln_b):
    m = x2.shape[0]
    row = lambda w: pl.BlockSpec((MIX_TM, w), lambda i: (i, 0))
    const = lambda r, c: pl.BlockSpec((r, c), lambda i: (0, 0))
    return pl.pallas_call(
        _mixout_kernel,
        out_shape=(jax.ShapeDtypeStruct((m, D_MODEL), F32), jax.ShapeDtypeStruct((m, D_MODEL), BF16)),
        grid=(m // MIX_TM,),
        in_specs=[row(A_WIDTH), row(A_WIDTH),
                  pl.BlockSpec((MIX_TM, A_WIDTH), lambda i: (i, COL_A_Z // A_WIDTH)),
                  row(B_WIDTH), row(C_WIDTH), row(C_WIDTH),
                  pl.BlockSpec((MIX_TM, C_WIDTH), lambda i: (i, COL_C_G // C_WIDTH)),
                  row(D_MODEL), const(1, LANES), const(1, LANES),
                  const(D_MIX, D_MODEL), const(1, D_MODEL), const(1, D_MODEL)],
        out_specs=(row(D_MODEL), row(D_MODEL)),
        compiler_params=_cparams(("parallel",)),
        name="mixer_output",
    )(oa_f, oa_b, proj2, ob, oc_f, oc_b, proj2, x2, gdn_w, hgrn_w, w_out, ln_g, ln_b)


XATT_TM = 512


def _xattn_kernel(xb_ref, x_ref, k_ref, v_ref, wq_ref, wo_ref, lg_ref, lb_ref, xo_ref, xob_ref):
    q = jnp.dot(xb_ref[0], wq_ref[...], preferred_element_type=F32) * (X_HEAD_DIM ** -0.5)
    outs = []
    for h in range(X_HEADS):
        sl = slice(h * X_HEAD_DIM, (h + 1) * X_HEAD_DIM)
        s = _dot_nt(q[:, sl], k_ref[0, :, sl])
        m = jnp.max(s, axis=-1, keepdims=True)
        e = jnp.exp(s - m)
        l = jnp.sum(e, axis=-1, keepdims=True)
        outs.append((jnp.dot(e.astype(BF16), v_ref[0, :, sl], preferred_element_type=F32) / l).astype(BF16))
    o = jnp.concatenate(outs, axis=1)
    c = jnp.dot(o, wo_ref[...], preferred_element_type=F32)
    y = _layer_norm_rows(DN_ALPHA * x_ref[0] + c, lg_ref[...], lb_ref[...])
    xo_ref[0] = y
    xob_ref[0] = y.astype(BF16)


def cross_attention(xb3, x3, k3, v3, wq, wo, ln_g, ln_b):
    b, s, _ = x3.shape
    mem = k3.shape[1]
    row = pl.BlockSpec((1, XATT_TM, D_MODEL), lambda i, t: (i, t, 0))
    const = lambda r, c: pl.BlockSpec((r, c), lambda i, t: (0, 0))
    kv = pl.BlockSpec((1, mem, D_MODEL), lambda i, t: (i, 0, 0))
    return pl.pallas_call(
        _xattn_kernel,
        out_shape=(jax.ShapeDtypeStruct((b, s, D_MODEL), F32), jax.ShapeDtypeStruct((b, s, D_MODEL), BF16)),
        grid=(b, s // XATT_TM),
        in_specs=[row, row, kv, kv, const(D_MODEL, D_MODEL), const(D_MODEL, D_MODEL),
                  const(1, D_MODEL), const(1, D_MODEL)],
        out_specs=(row, row),
        compiler_params=_cparams(("parallel", "parallel")),
        name="cross_attention",
    )(xb3, x3, k3, v3, wq, wo, ln_g, ln_b)


FFN_TM = 512
FFN_TF = 1408


def _ffn_kernel(xb_ref, x_ref, wg_ref, wu_ref, wd_ref, lg_ref, lb_ref, xo_ref, xob_ref, acc_ref):
    j = pl.program_id(1)

    @pl.when(j == 0)
    def _():
        acc_ref[...] = jnp.zeros_like(acc_ref)

    xb = xb_ref[...]
    g = jnp.dot(xb, wg_ref[...], preferred_element_type=F32)
    u = jnp.dot(xb, wu_ref[...], preferred_element_type=F32)
    h = (g * _sigmoid(g) * u).astype(BF16)
    acc_ref[...] += jnp.dot(h, wd_ref[...], preferred_element_type=F32)

    @pl.when(j == pl.num_programs(1) - 1)
    def _():
        y = _layer_norm_rows(DN_ALPHA * x_ref[...] + acc_ref[...], lg_ref[...], lb_ref[...])
        xo_ref[...] = y
        xob_ref[...] = y.astype(BF16)


def dense_ffn(xb2, x2, wg, wu, wd, ln_g, ln_b):
    m = x2.shape[0]
    row = pl.BlockSpec((FFN_TM, D_MODEL), lambda i, j: (i, 0))
    const = pl.BlockSpec((1, D_MODEL), lambda i, j: (0, 0))
    return pl.pallas_call(
        _ffn_kernel,
        out_shape=(jax.ShapeDtypeStruct((m, D_MODEL), F32), jax.ShapeDtypeStruct((m, D_MODEL), BF16)),
        grid=(m // FFN_TM, D_FF // FFN_TF),
        in_specs=[row, row,
                  pl.BlockSpec((D_MODEL, FFN_TF), lambda i, j: (0, j)),
                  pl.BlockSpec((D_MODEL, FFN_TF), lambda i, j: (0, j)),
                  pl.BlockSpec((FFN_TF, D_MODEL), lambda i, j: (j, 0)),
                  const, const],
        out_specs=(row, row),
        scratch_shapes=[pltpu.VMEM((FFN_TM, D_MODEL), F32)],
        compiler_params=_cparams(("parallel", "arbitrary")),
        name="dense_ffn",
    )(xb2, x2, wg, wu, wd, ln_g, ln_b)


MOE_TM = 512
MOE_WIN = 512
MOE_TF = 1792
ROUTER_TM = 512


def _router_kernel(xb_ref, wr_ref, e_ref, w_ref):
    logits = jnp.dot(xb_ref[...], wr_ref[...], preferred_element_type=F32)
    lane = _iota2(logits.shape, 1)
    neg = jnp.float32(-jnp.inf)
    lane_f = lane.astype(F32)
    lg = jnp.where(lane < N_EXPERTS, logits, neg)
    m1 = jnp.max(lg, axis=-1, keepdims=True)
    i1 = jnp.min(jnp.where(lg == m1, lane_f, float(LANES)), axis=-1, keepdims=True)
    lg2 = jnp.where(lane_f == i1, neg, lg)
    m2 = jnp.max(lg2, axis=-1, keepdims=True)
    i2 = jnp.min(jnp.where(lg2 == m2, lane_f, float(LANES)), axis=-1, keepdims=True)
    t = jnp.exp(m2 - m1)
    w1 = 1.0 / (1.0 + t)
    w2 = t / (1.0 + t)
    e_ref[...] = jnp.where(lane == 0, i1, jnp.where(lane == 1, i2, 0.0)).astype(jnp.int32)
    w_ref[...] = jnp.where(lane == 0, w1, jnp.where(lane == 1, w2, 0.0))


def moe_router(xb2, w_router_p):
    m = xb2.shape[0]
    return pl.pallas_call(
        _router_kernel,
        out_shape=(jax.ShapeDtypeStruct((m, LANES), jnp.int32), jax.ShapeDtypeStruct((m, LANES), F32)),
        grid=(m // ROUTER_TM,),
        in_specs=[pl.BlockSpec((ROUTER_TM, D_MODEL), lambda i: (i, 0)),
                  pl.BlockSpec((D_MODEL, LANES), lambda i: (0, 0))],
        out_specs=(pl.BlockSpec((ROUTER_TM, LANES), lambda i: (i, 0)),
                   pl.BlockSpec((ROUTER_TM, LANES), lambda i: (i, 0))),
        compiler_params=_cparams(("parallel",)),
        name="moe_router",
    )(xb2, w_router_p)


def _moe_gather_kernel(rb_ref, win_ref, flag_ref, dest_ref, x_ref, o_ref):
    p = pl.program_id(0)
    flag = flag_ref[p]

    @pl.when((flag & 2) != 0)
    def _():
        o_ref[...] = jnp.zeros_like(o_ref)

    @pl.when((flag & 1) != 0)
    def _():
        d = dest_ref[0]
        rows = _iota2((MOE_TM, MOE_WIN), 0) + rb_ref[p] * MOE_TM
        onehot = jnp.where((d[0:1, :] == rows) | (d[1:2, :] == rows), 1.0, 0.0).astype(BF16)
        o_ref[...] += jnp.dot(onehot, x_ref[...], preferred_element_type=F32).astype(BF16)


def moe_gather(pair_rb, pair_win, pair_flag, dest_rows, xb2, n_rows):
    n_pairs = pair_rb.shape[0]
    grid_spec = pltpu.PrefetchScalarGridSpec(
        num_scalar_prefetch=3,
        grid=(n_pairs,),
        in_specs=[pl.BlockSpec((1, TOP_K, MOE_WIN), lambda p, rb, win, fl: (win[p], 0, 0)),
                  pl.BlockSpec((MOE_WIN, D_MODEL), lambda p, rb, win, fl: (win[p], 0))],
        out_specs=pl.BlockSpec((MOE_TM, D_MODEL), lambda p, rb, win, fl: (rb[p], 0)),
    )
    return pl.pallas_call(
        _moe_gather_kernel,
        out_shape=jax.ShapeDtypeStruct((n_rows, D_MODEL), BF16),
        grid_spec=grid_spec,
        compiler_params=_cparams(("arbitrary",)),
        name="moe_gather",
    )(pair_rb, pair_win, pair_flag, dest_rows, xb2)


def _moe_ffn_kernel(be_ref, bv_ref, x_ref, wg_ref, wu_ref, wd_ref, o_ref, acc_ref):
    i = pl.program_id(0)
    j = pl.program_id(1)
    valid = bv_ref[i] != 0

    @pl.when(j == 0)
    def _():
        acc_ref[...] = jnp.zeros_like(acc_ref)

    @pl.when(valid)
    def _():
        xb = x_ref[...]
        g = jnp.dot(xb, wg_ref[0], preferred_element_type=F32)
        u = jnp.dot(xb, wu_ref[0], preferred_element_type=F32)
        h = (g * _sigmoid(g) * u).astype(BF16)
        acc_ref[...] += jnp.dot(h, wd_ref[0], preferred_element_type=F32)

    @pl.when(j == pl.num_programs(1) - 1)
    def _():
        o_ref[...] = acc_ref[...].astype(BF16)


def moe_expert_ffn(block_e, block_valid, xrows, wg, wu, wd):
    n_rows = xrows.shape[0]
    grid_spec = pltpu.PrefetchScalarGridSpec(
        num_scalar_prefetch=2,
        grid=(n_rows // MOE_TM, D_FF_EXPERT // MOE_TF),
        in_specs=[pl.BlockSpec((MOE_TM, D_MODEL), lambda i, j, be, bv: (i * bv[i], 0)),
                  pl.BlockSpec((1, D_MODEL, MOE_TF), lambda i, j, be, bv: (be[i], 0, j)),
                  pl.BlockSpec((1, D_MODEL, MOE_TF), lambda i, j, be, bv: (be[i], 0, j)),
                  pl.BlockSpec((1, MOE_TF, D_MODEL), lambda i, j, be, bv: (be[i], j, 0))],
        out_specs=pl.BlockSpec((MOE_TM, D_MODEL), lambda i, j, be, bv: (i, 0)),
        scratch_shapes=[pltpu.VMEM((MOE_TM, D_MODEL), F32)],
    )
    return pl.pallas_call(
        _moe_ffn_kernel,
        out_shape=jax.ShapeDtypeStruct((n_rows, D_MODEL), BF16),
        grid_spec=grid_spec,
        compiler_params=_cparams(("arbitrary", "arbitrary")),
        name="moe_expert_ffn",
    )(block_e, block_valid, xrows, wg, wu, wd)


def _moe_combine_kernel(rb_ref, win_ref, flag_ref, dest_ref, w_ref, y_ref, x_ref, lg_ref, lb_ref, xo_ref, acc_ref):
    p = pl.program_id(0)
    flag = flag_ref[p]

    @pl.when((flag & 2) != 0)
    def _():
        acc_ref[...] = jnp.zeros_like(acc_ref)

    @pl.when((flag & 1) != 0)
    def _():
        d = dest_ref[...]
        wt = w_ref[...]
        cols = _iota2((MOE_WIN, MOE_TM), 1) + rb_ref[p] * MOE_TM
        sel = (jnp.where(d[:, 0:1] == cols, wt[:, 0:1], 0.0)
               + jnp.where(d[:, 1:2] == cols, wt[:, 1:2], 0.0)).astype(BF16)
        acc_ref[...] += jnp.dot(sel, y_ref[...], preferred_element_type=F32)

    @pl.when((flag & 4) != 0)
    def _():
        xo_ref[...] = _layer_norm_rows(DN_ALPHA * x_ref[...] + acc_ref[...], lg_ref[...], lb_ref[...])


def moe_combine(pair_rb, pair_win, pair_flag, dest_cols, w_cols, yrows, x2, ln_g, ln_b):
    n_pairs = pair_rb.shape[0]
    m = x2.shape[0]
    tok = lambda p, rb, win, fl: (win[p], 0)
    grid_spec = pltpu.PrefetchScalarGridSpec(
        num_scalar_prefetch=3,
        grid=(n_pairs,),
        in_specs=[pl.BlockSpec((MOE_WIN, TOP_K), tok),
                  pl.BlockSpec((MOE_WIN, TOP_K), tok),
                  pl.BlockSpec((MOE_TM, D_MODEL), lambda p, rb, win, fl: (rb[p], 0)),
                  pl.BlockSpec((MOE_WIN, D_MODEL), tok),
                  pl.BlockSpec((1, D_MODEL), lambda p, rb, win, fl: (0, 0)),
                  pl.BlockSpec((1, D_MODEL), lambda p, rb, win, fl: (0, 0))],
        out_specs=pl.BlockSpec((MOE_WIN, D_MODEL), tok),
        scratch_shapes=[pltpu.VMEM((MOE_WIN, D_MODEL), F32)],
    )
    return pl.pallas_call(
        _moe_combine_kernel,
        out_shape=jax.ShapeDtypeStruct((m, D_MODEL), F32),
        grid_spec=grid_spec,
        compiler_params=_cparams(("arbitrary",)),
        name="moe_combine",
    )(pair_rb, pair_win, pair_flag, dest_cols, w_cols, yrows, x2, ln_g, ln_b)


def _pair_tables(n_blk, blk_lo, n_pairs):
    cell_end = jnp.cumsum(n_blk)
    cell_start = cell_end - n_blk
    pidx = jnp.arange(n_pairs, dtype=jnp.int32)
    valid = pidx < cell_end[-1]
    cell = jnp.sum((cell_end[None, :] <= pidx[:, None]).astype(jnp.int32), axis=1)
    cell = jnp.minimum(cell, n_blk.shape[0] - 1)
    rb = blk_lo[cell] + pidx - cell_start[cell]
    return cell, rb, valid


def moe_layer(xb2, x2, w_router, wg, wu, wd, ln_g, ln_b):
    n = x2.shape[0]
    n_win = n // MOE_WIN
    n_rows = n * TOP_K + N_EXPERTS * MOE_TM
    n_blocks = n_rows // MOE_TM
    n_pairs = n_blocks + N_EXPERTS * n_win

    wr = jnp.pad(w_router, ((0, 0), (0, LANES - N_EXPERTS))).astype(BF16)
    top_e_t, top_w_t = moe_router(xb2, wr)
    top_e = top_e_t[:, :TOP_K]
    top_w = top_w_t[:, :TOP_K]

    experts = jnp.arange(N_EXPERTS, dtype=jnp.int32)
    tok_onehot = ((top_e[:, 0:1] == experts[None, :]) | (top_e[:, 1:2] == experts[None, :])).astype(jnp.int32)
    csum = jnp.cumsum(tok_onehot, axis=0)
    counts = csum[-1]
    rank = csum - tok_onehot
    padded = (counts + MOE_TM - 1) // MOE_TM * MOE_TM
    pad_end = jnp.cumsum(padded)
    pad_start = pad_end - padded
    dest = pad_start[top_e] + jnp.take_along_axis(rank, top_e, axis=1)

    blk_row0 = jnp.arange(n_blocks, dtype=jnp.int32) * MOE_TM
    block_e = jnp.minimum(jnp.sum((pad_end[None, :] <= blk_row0[:, None]).astype(jnp.int32), axis=1),
                          N_EXPERTS - 1).astype(jnp.int32)
    block_valid = (blk_row0 < pad_end[-1]).astype(jnp.int32)

    cell_cnt = tok_onehot.reshape(n_win, MOE_WIN, N_EXPERTS).sum(axis=1)
    cell_lo = pad_start[None, :] + jnp.cumsum(cell_cnt, axis=0) - cell_cnt
    blk_lo = cell_lo // MOE_TM
    n_blk = jnp.where(cell_cnt > 0, (cell_lo + cell_cnt - 1) // MOE_TM - blk_lo + 1, 0)

    cell, g_rb, g_valid = _pair_tables(n_blk.T.reshape(-1), blk_lo.T.reshape(-1), n_pairs)
    g_win = cell % n_win
    last_rb = jnp.max(jnp.where(g_valid, g_rb, 0))
    g_rb = jnp.where(g_valid, g_rb, last_rb).astype(jnp.int32)
    g_win = jnp.where(g_valid, g_win, 0).astype(jnp.int32)
    prev_rb = jnp.concatenate([jnp.full((1,), -1, jnp.int32), g_rb[:-1]])
    g_flag = g_valid.astype(jnp.int32) + 2 * (g_valid & (g_rb != prev_rb)).astype(jnp.int32)
    dest_rows = dest.reshape(n_win, MOE_WIN, TOP_K).transpose(0, 2, 1)
    xrows = moe_gather(g_rb, g_win, g_flag, dest_rows, xb2, n_rows)

    yrows = moe_expert_ffn(block_e, block_valid, xrows, wg, wu, wd)

    cell, c_rb, c_valid = _pair_tables(n_blk.reshape(-1), blk_lo.reshape(-1), n_pairs)
    c_win = jnp.where(c_valid, cell // N_EXPERTS, n_win - 1).astype(jnp.int32)
    c_rb = jnp.where(c_valid, c_rb, 0).astype(jnp.int32)
    prev_win = jnp.concatenate([jnp.full((1,), -1, jnp.int32), c_win[:-1]])
    next_win = jnp.concatenate([c_win[1:], jnp.full((1,), -1, jnp.int32)])
    next_valid = jnp.concatenate([c_valid[1:], jnp.zeros((1,), bool)])
    c_first = c_valid & (c_win != prev_win)
    c_last = c_valid & ((c_win != next_win) | jnp.logical_not(next_valid))
    c_flag = c_valid.astype(jnp.int32) + 2 * c_first.astype(jnp.int32) + 4 * c_last.astype(jnp.int32)
    return moe_combine(c_rb, c_win, c_flag, dest, top_w, yrows, x2, ln_g, ln_b)


def _w_in_perm():
    zero = 3480
    perm = np.full((D_IN_P,), zero, np.int64)
    o_qkv, o_z, o_beta, o_decay, o_bq, o_bk, o_bv, o_cf, o_ci, o_cq, o_cg = (
        0, 1152, 1536, 1548, 1560, 1944, 2072, 2200, 2712, 2968, 3224)
    perm[COL_A_QKV:COL_A_QKV + 1152] = np.arange(o_qkv, o_qkv + 1152)
    perm[COL_A_Z:COL_A_Z + 384] = np.arange(o_z, o_z + 384)
    for j in range(3):
        for a in range(2):
            dst = COL_B_Q + j * LANES + a * HEAD_DIM
            src = o_bq + (j + 3 * a) * HEAD_DIM
            perm[dst:dst + HEAD_DIM] = np.arange(src, src + HEAD_DIM)
    perm[COL_B_K:COL_B_K + 128] = np.arange(o_bk, o_bk + 128)
    perm[COL_C_G:COL_C_G + 256] = np.arange(o_cg, o_cg + 256)
    perm[COL_B_V:COL_B_V + 128] = np.arange(o_bv, o_bv + 128)
    perm[COL_C_F:COL_C_F + 512] = np.arange(o_cf, o_cf + 512)
    perm[COL_C_I:COL_C_I + 256] = np.arange(o_ci, o_ci + 256)
    perm[COL_C_Q:COL_C_Q + 256] = np.arange(o_cq, o_cq + 256)
    for d, col in enumerate((COL_GATES0, COL_GATES1)):
        for p in range(A_HEADS // 2):
            dst = col + 4 * p
            perm[dst + 0] = o_beta + d * A_HEADS + 2 * p
            perm[dst + 1] = o_beta + d * A_HEADS + 2 * p + 1
            perm[dst + 2] = o_decay + d * A_HEADS + 2 * p
            perm[dst + 3] = o_decay + d * A_HEADS + 2 * p + 1
    return perm


def _w_out_perm():
    perm = np.arange(D_MIX)
    for j in range(3):
        for a in range(2):
            dst = A_WIDTH + j * LANES + a * HEAD_DIM
            src = A_WIDTH + (j + 3 * a) * HEAD_DIM
            perm[dst:dst + HEAD_DIM] = np.arange(src, src + HEAD_DIM)
    return perm


def _gdn_params(a_log, dt_bias):
    rows = np.array([4 * (h // 2) + 2 + h % 2 for h in range(A_HEADS)])
    prm = jnp.zeros((2, GATE_ROWS, LANES), F32)
    prm = prm.at[:, rows, 0].set(a_log.astype(F32))
    prm = prm.at[:, rows, 1].set(dt_bias.astype(F32))
    return prm.at[:, rows, 2].set(1.0)


def kernel(x, mem, w_in, conv_w, gdn_a_log, gdn_dt_bias, gdn_norm_w, q_norm_w, k_norm_w, hgrn_lb_logits,
           hgrn_norm_w, w_out, ln1_g, ln1_b, xq, xk, xv, xo, ln2_g, ln2_b, ffn_wg, ffn_wu, ffn_wd,
           moe_router, moe_wg, moe_wu, moe_wd, ln3_g, ln3_b):
    b, s, d = x.shape
    m = b * s
    cos_t, sin_t = rope_tables(s)
    lb_p = jax.nn.softmax(hgrn_lb_logits.astype(F32), axis=0)
    lb_c = jnp.cumsum(lb_p, axis=0)
    lower_bounds = lb_c - lb_c[0]
    in_perm = _w_in_perm()
    out_perm = _w_out_perm()
    mem_b = mem.astype(BF16).reshape(b * mem.shape[1], d)
    tile2 = lambda w: jnp.tile(w.astype(F32), 2)[None, :]
    row = lambda v: v.astype(F32)[None, :]

    x2 = x.reshape(m, d)
    xb2 = x2.astype(BF16)
    for l in range(DEPTH):
        w_in_p = jnp.take(jnp.pad(w_in[l], ((0, 0), (0, 1))), in_perm, axis=1).astype(BF16)
        wide2 = matmul(xb2, w_in_p[:, :D_WIDE], BF16, min(1024, m), 768)
        gates2 = matmul(xb2, w_in_p[:, D_WIDE:], F32, min(1024, m), D_IN_P - D_WIDE)
        wide3 = wide2.reshape(b, s, D_WIDE)
        gates3 = gates2.reshape(b, s, D_IN_P - D_WIDE)

        oa_f, oa_b = gdn_scan(wide3, gates3, conv_w[l], _gdn_params(gdn_a_log[l], gdn_dt_bias[l]))
        ob = gqa_attention(wide3, cos_t, sin_t, tile2(q_norm_w[l]), tile2(k_norm_w[l]), s)
        lb = lower_bounds[l][None, :]
        oc_f = hgrn_direction(wide3, gates3, 0, lb)
        oc_b = hgrn_direction(wide3, gates3, 1, lb)

        w_out_p = jnp.take(w_out[l], out_perm, axis=0).astype(BF16)
        x2, xb2 = mixer_output(oa_f.reshape(m, A_WIDTH), oa_b.reshape(m, A_WIDTH), wide2,
                               ob.reshape(m, B_WIDTH), oc_f.reshape(m, C_WIDTH), oc_b.reshape(m, C_WIDTH),
                               x2, tile2(gdn_norm_w[l]), tile2(hgrn_norm_w[l]), w_out_p,
                               row(ln1_g[l]), row(ln1_b[l]))

        k3 = matmul(mem_b, xk[l].astype(BF16), BF16, 256, 512).reshape(b, -1, d)
        v3 = matmul(mem_b, xv[l].astype(BF16), BF16, 256, 512).reshape(b, -1, d)
        x3, xb3 = cross_attention(xb2.reshape(b, s, d), x2.reshape(b, s, d), k3, v3,
                                  xq[l].astype(BF16), xo[l].astype(BF16), row(ln2_g[l]), row(ln2_b[l]))
        x2, xb2 = x3.reshape(m, d), xb3.reshape(m, d)

        if l % 2 == 0:
            x2, xb2 = dense_ffn(xb2, x2, ffn_wg[l // 2].astype(BF16), ffn_wu[l // 2].astype(BF16),
                                ffn_wd[l // 2].astype(BF16), row(ln3_g[l]), row(ln3_b[l]))
        else:
            x2 = moe_layer(xb2, x2, moe_router[l // 2], moe_wg[l // 2].astype(BF16),
                           moe_wu[l // 2].astype(BF16), moe_wd[l // 2].astype(BF16),
                           row(ln3_g[l]), row(ln3_b[l]))
            xb2 = x2.astype(BF16)
    return x2.reshape(b, s, d)
```

```python
import math

import numpy as np
import jax
import jax.numpy as jnp
from jax import lax
from jax.experimental import pallas as pl
from jax.experimental.pallas import tpu as pltpu

F32 = jnp.float32
BF16 = jnp.bfloat16

D_MODEL = 1024
DEPTH = 2
HEAD_DIM = 64
A_HEADS = 6
A_WIDTH = A_HEADS * HEAD_DIM
B_Q_HEADS = 6
B_KV_HEADS = 2
B_WIDTH = B_Q_HEADS * HEAD_DIM
C_HEADS = 4
C_WIDTH = C_HEADS * HEAD_DIM
D_MIX = A_WIDTH + B_WIDTH + C_WIDTH
CONV_K = 5
CHUNK = 64
GRID_W = 64
ROPE_AXIS_DIM = HEAD_DIM // 2
ROPE_THETA = 10000.0
X_HEADS = 4
X_HEAD_DIM = D_MODEL // X_HEADS
D_FF = 2816
N_EXPERTS = 8
TOP_K = 2
D_FF_EXPERT = 3584
DN_ALPHA = (2 * DEPTH) ** 0.25
LN_EPS = 1e-5
RMS_EPS = 1e-6

LANES = 128
SUBLANES = 8
HALO = 16
GROUP = 256
CHUNKS_PER_GROUP = GROUP // CHUNK
VMEM_LIMIT = 56 * 1024 * 1024

COL_A_QKV = 0
COL_A_Z = 1152
COL_B_Q = 1536
COL_B_K = 1920
COL_C_G = 2048
COL_B_V = 2304
COL_C_I = 2560
COL_C_Q = 2816
D_WIDE = 3072
COL_C_F = D_WIDE
COL_GATES0 = D_WIDE + 512
COL_GATES1 = D_WIDE + 640
D_IN_P = 3840


def _cparams(sem):
    return pltpu.CompilerParams(dimension_semantics=sem, vmem_limit_bytes=VMEM_LIMIT)


def _sigmoid(x):
    return 1.0 / (1.0 + jnp.exp(-x))


def _softplus(x):
    return jnp.maximum(x, 0.0) + jnp.log1p(jnp.exp(-jnp.abs(x)))


def _dot(a, b):
    return jnp.dot(a.astype(BF16), b.astype(BF16), preferred_element_type=F32)


def _dot_nt(a, b):
    return lax.dot_general(a.astype(BF16), b.astype(BF16), (((1,), (1,)), ((), ())),
                           preferred_element_type=F32)


def _dot_sel(m01, x):
    m = jnp.where(m01, 1.0, 0.0).astype(BF16)
    hi = x.astype(BF16)
    r1 = x - hi.astype(F32)
    mid = r1.astype(BF16)
    lo = (r1 - mid.astype(F32)).astype(BF16)
    out = jnp.dot(m, hi, preferred_element_type=F32)
    out = out + jnp.dot(m, mid, preferred_element_type=F32)
    return out + jnp.dot(m, lo, preferred_element_type=F32)


def _dot_sel_right(x, m01):
    m = jnp.where(m01, 1.0, 0.0).astype(BF16)
    hi = x.astype(BF16)
    r1 = x - hi.astype(F32)
    mid = r1.astype(BF16)
    lo = (r1 - mid.astype(F32)).astype(BF16)
    out = jnp.dot(hi, m, preferred_element_type=F32)
    out = out + jnp.dot(mid, m, preferred_element_type=F32)
    return out + jnp.dot(lo, m, preferred_element_type=F32)


def _layer_norm_rows(y, g, b):
    mu = jnp.mean(y, axis=-1, keepdims=True)
    yc = y - mu
    var = jnp.mean(yc * yc, axis=-1, keepdims=True)
    return yc * lax.rsqrt(var + LN_EPS) * g + b


def _iota2(shape, dim):
    return lax.broadcasted_iota(jnp.int32, shape, dim)


def _head_sum(x, first_head):
    s0 = jnp.sum(jnp.where(first_head, x, 0.0), axis=-1, keepdims=True)
    s1 = jnp.sum(jnp.where(first_head, 0.0, x), axis=-1, keepdims=True)
    return jnp.where(first_head, s0, s1)


def _block_rows(t, block, row_in_block):
    parts = [jnp.broadcast_to(t[b * block + row_in_block:b * block + row_in_block + 1, :], (block, t.shape[1]))
             for b in range(GROUP // block)]
    return jnp.concatenate(parts, axis=0)


def _mm_kernel(x_ref, w_ref, o_ref):
    o_ref[...] = jnp.dot(x_ref[...], w_ref[...], preferred_element_type=F32).astype(o_ref.dtype)


def matmul(x, w, out_dtype, tm, tn):
    m, k = x.shape
    n = w.shape[1]
    return pl.pallas_call(
        _mm_kernel,
        out_shape=jax.ShapeDtypeStruct((m, n), out_dtype),
        grid=(n // tn, m // tm),
        in_specs=[pl.BlockSpec((tm, k), lambda j, i: (i, 0)),
                  pl.BlockSpec((k, tn), lambda j, i: (0, j))],
        out_specs=pl.BlockSpec((tm, tn), lambda j, i: (i, j)),
        compiler_params=_cparams(("parallel", "arbitrary")),
        name="matmul",
    )(x, w)


def _scan_masks(reverse):
    ri = _iota2((GROUP, GROUP), 0)
    ci = _iota2((GROUP, GROUP), 1)
    same_chunk = (ri >> 6) == (ci >> 6)
    if reverse:
        return ri, ci, same_chunk, same_chunk & (ri <= ci), same_chunk & (ri < ci)
    return ri, ci, same_chunk, same_chunk & (ri >= ci), same_chunk & (ri > ci)


GATE_ROWS = 16


class _GdnProblem:
    def __init__(self, **kw):
        self.__dict__.update(kw)


def _gdn_group(probs, same_chunk, h0):
    nh0 = jnp.logical_not(h0)
    sr = _iota2((CHUNK, GROUP), 0)
    sc = _iota2((CHUNK, GROUP), 1)
    eye_side = jnp.where(sr == (sc & (CHUNK - 1)), 1.0, 0.0)
    tl = _iota2((LANES, GROUP), 1)
    br = _iota2((LANES, LANES), 0)
    bc = _iota2((LANES, LANES), 1)
    bd128 = (br >> 6) == (bc >> 6)

    def block_diag(side):
        return jnp.where(same_chunk, jnp.concatenate([side] * CHUNKS_PER_GROUP, axis=0), 0.0).astype(BF16)

    for pr in probs:
        pr.q = pr.q * lax.rsqrt(_head_sum(pr.q * pr.q, h0) + RMS_EPS) * (HEAD_DIM ** -0.5)
        pr.k = pr.k * lax.rsqrt(_head_sum(pr.k * pr.k, h0) + RMS_EPS)
        pr.gedge = _block_rows(pr.gc, CHUNK, 0 if pr.reverse else CHUNK - 1)
        pr.exp_g = jnp.exp(pr.gc)
        pr.kb = pr.k * pr.beta_t
        pr.rhs = jnp.concatenate([pr.v * pr.beta_t, pr.kb * pr.exp_g], axis=1).astype(BF16)
        pr.k16 = pr.k.astype(BF16)
        pr.decay, pr.x, pr.pw, pr.bd = [], [], [], []
        for h in range(2):
            decay = jnp.exp(jnp.where(pr.incl, pr.g_col[h] - pr.g_row[h], -1e30))
            kk = _dot_nt(jnp.where(h0 if h == 0 else nh0, pr.kb, 0.0), pr.k16)
            low = jnp.where(pr.strict, kk * decay, 0.0)
            a = -(low[0:CHUNK] + low[CHUNK:2 * CHUNK] + low[2 * CHUNK:3 * CHUNK] + low[3 * CHUNK:4 * CHUNK])
            pr.decay.append(decay)
            pr.x.append(eye_side + a)
            pr.pw.append(a)
            pr.bd.append(block_diag(a))

    for _ in range(5):
        for pr in probs:
            for h in range(2):
                pr.pw[h] = jnp.dot(pr.pw[h].astype(BF16), pr.bd[h], preferred_element_type=F32)
        for pr in probs:
            for h in range(2):
                pr.bd[h] = block_diag(pr.pw[h])
        for pr in probs:
            for h in range(2):
                pr.x[h] = pr.x[h] + jnp.dot(pr.x[h].astype(BF16), pr.bd[h], preferred_element_type=F32)

    for pr in probs:
        sols = [jnp.dot(block_diag(pr.x[h]), pr.rhs, preferred_element_type=F32) for h in range(2)]
        u = jnp.where(h0, sols[0][:, :LANES], sols[1][:, :LANES])
        w = jnp.where(h0, sols[0][:, LANES:], sols[1][:, LANES:])
        wu = jnp.concatenate([w, u], axis=1)
        wu16 = wu.astype(BF16)
        aw = []
        for h in range(2):
            attn = jnp.where(pr.incl, _dot_nt(jnp.where(h0 if h == 0 else nh0, pr.q, 0.0), pr.k16) * pr.decay[h], 0.0)
            aw.append(jnp.dot(attn.astype(BF16), wu16, preferred_element_type=F32))
        pr.qp = (pr.q * pr.exp_g - jnp.where(h0, aw[0][:, :LANES], aw[1][:, :LANES])).astype(BF16)
        pr.op = jnp.where(h0, aw[0][:, LANES:], aw[1][:, LANES:])
        kd_t = (pr.k * jnp.exp(pr.gedge - pr.gc)).T
        pr.kmat, pr.nmat = [], []
        for c in range(CHUNKS_PER_GROUP):
            km = jnp.dot(jnp.where((tl >> 6) == c, kd_t, 0.0), wu, preferred_element_type=F32)
            pr.kmat.append(jnp.where(bd128, km[:, :LANES], 0.0).astype(BF16))
            pr.nmat.append(jnp.where(bd128, km[:, LANES:], 0.0))
        pr.out = [None] * CHUNKS_PER_GROUP

    for t in range(CHUNKS_PER_GROUP):
        for pr in probs:
            c = CHUNKS_PER_GROUP - 1 - t if pr.reverse else t
            rs = slice(c * CHUNK, (c + 1) * CHUNK)
            s16 = pr.state.astype(BF16)
            pr.out[c] = jnp.dot(pr.qp[rs], s16, preferred_element_type=F32) + pr.op[rs]
            cd = jnp.exp(pr.gedge[c * CHUNK:c * CHUNK + 1, :])
            pr.state = pr.state * cd - jnp.dot(pr.kmat[c], s16, preferred_element_type=F32) + pr.nmat[c]
    return [jnp.concatenate(pr.out, axis=0) for pr in probs]


def _gdn_kernel(xf_ref, xfp_ref, xfn_ref, xb_ref, xbp_ref, xbn_ref, cw_ref, gf_ref, gb_ref, prm_ref,
                of_ref, ob_ref, s_scr):
    n = pl.program_id(1)
    n_groups = pl.num_programs(1)

    @pl.when(n == 0)
    def _():
        s_scr[...] = jnp.zeros_like(s_scr)

    lane = _iota2((GROUP, LANES), 1)
    h0 = lane < HEAD_DIM
    ri = _iota2((GROUP, GROUP), 0)
    ci = _iota2((GROUP, GROUP), 1)
    same_chunk = (ri >> 6) == (ci >> 6)
    cw = cw_ref[...]
    base = SUBLANES - CONV_K // 2
    pairs = A_HEADS // 2

    probs = []
    for d, (gi, x_ref, xp_ref, xn_ref, g_ref) in enumerate(((n, xf_ref, xfp_ref, xfn_ref, gf_ref),
                                                            (n_groups - 1 - n, xb_ref, xbp_ref, xbn_ref, gb_ref))):
        reverse = d == 1
        lower, upper = same_chunk & (ri >= ci), same_chunk & (ri <= ci)
        incl = upper if reverse else lower
        strict = same_chunk & ((ri < ci) if reverse else (ri > ci))

        prev = jnp.where(gi > 0, xp_ref[0].astype(F32)[HALO - SUBLANES:], 0.0)
        nxt = jnp.where(gi < n_groups - 1, xn_ref[0].astype(F32)[:SUBLANES], 0.0)
        ext = jnp.concatenate([prev, x_ref[0].astype(F32), nxt], axis=0)
        acc = None
        for t in range(CONV_K):
            shift = (CONV_K // 2 - t) % ext.shape[0]
            tap = (pltpu.roll(ext, shift, 0) if shift else ext)[SUBLANES:SUBLANES + GROUP, :] * cw[t:t + 1, :]
            acc = tap if acc is None else acc + tap
        qkv = acc * _sigmoid(acc)

        gt_t = g_ref[0].T[0:GATE_ROWS, :]
        prm = prm_ref[d]
        e_rows = jnp.where(prm[:, 2:3] > 0.5,
                           -jnp.exp(prm[:, 0:1]) * _softplus(gt_t + prm[:, 1:2]), _sigmoid(gt_t))
        g_rows = _dot_sel_right(e_rows, lower if reverse else upper)
        slab = jnp.concatenate([e_rows, g_rows, jnp.zeros((LANES - 2 * GATE_ROWS, GROUP), F32)], axis=0)
        cols = slab.T
        for p in range(pairs):
            sl = lambda part: slice(part * A_WIDTH + p * LANES, part * A_WIDTH + (p + 1) * LANES)
            col = lambda r: cols[:, r:r + 1]
            probs.append(_GdnProblem(
                reverse=reverse, incl=incl, strict=strict, slot=d * pairs + p,
                q=qkv[:, sl(0)], k=qkv[:, sl(1)], v=qkv[:, sl(2)],
                beta_t=jnp.where(h0, col(4 * p), col(4 * p + 1)),
                gc=jnp.where(h0, col(GATE_ROWS + 4 * p + 2), col(GATE_ROWS + 4 * p + 3)),
                g_col=(col(GATE_ROWS + 4 * p + 2), col(GATE_ROWS + 4 * p + 3)),
                g_row=(g_rows[4 * p + 2:4 * p + 3, :], g_rows[4 * p + 3:4 * p + 4, :]),
                state=s_scr[d * pairs + p]))

    outs = _gdn_group(probs, same_chunk, h0)
    for pr, out in zip(probs, outs):
        o_ref = ob_ref if pr.reverse else of_ref
        p = pr.slot % pairs
        o_ref[0, :, p * LANES:(p + 1) * LANES] = out.astype(BF16)
        s_scr[pr.slot] = pr.state


def gdn_scan(wide3, gates3, conv_w, prm):
    b, seq_len, _ = wide3.shape
    n_groups = seq_len // GROUP
    halo_per_group = GROUP // HALO
    n_halo = seq_len // HALO
    width = 3 * A_WIDTH
    fwd = lambda n: n
    bwd = lambda n: n_groups - 1 - n

    def x_specs(gidx):
        return [pl.BlockSpec((1, GROUP, width), lambda i, n: (i, gidx(n), 0)),
                pl.BlockSpec((1, HALO, width),
                             lambda i, n: (i, jnp.maximum(gidx(n) * halo_per_group - 1, 0), 0)),
                pl.BlockSpec((1, HALO, width),
                             lambda i, n: (i, jnp.minimum((gidx(n) + 1) * halo_per_group, n_halo - 1), 0))]

    out_sds = jax.ShapeDtypeStruct((b, seq_len, A_WIDTH), BF16)
    return pl.pallas_call(
        _gdn_kernel,
        out_shape=(out_sds, out_sds),
        grid=(b, n_groups),
        in_specs=x_specs(fwd) + x_specs(bwd) + [
            pl.BlockSpec((CONV_K, width), lambda i, n: (0, 0)),
            pl.BlockSpec((1, GROUP, LANES), lambda i, n: (i, fwd(n), (COL_GATES0 - D_WIDE) // LANES)),
            pl.BlockSpec((1, GROUP, LANES), lambda i, n: (i, bwd(n), (COL_GATES1 - D_WIDE) // LANES)),
            pl.BlockSpec((2, GATE_ROWS, LANES), lambda i, n: (0, 0, 0))],
        out_specs=(pl.BlockSpec((1, GROUP, A_WIDTH), lambda i, n: (i, fwd(n), 0)),
                   pl.BlockSpec((1, GROUP, A_WIDTH), lambda i, n: (i, bwd(n), 0))),
        scratch_shapes=[pltpu.VMEM((A_HEADS, LANES, LANES), F32)],
        compiler_params=_cparams(("parallel", "arbitrary")),
        name="gdn_scan",
    )(wide3, wide3, wide3, wide3, wide3, wide3, conv_w, gates3, gates3, prm)


EXP_CLAMP = 60.0


class _HgrnProblem:
    def __init__(self, **kw):
        self.__dict__.update(kw)


def _hgrn_group(probs):
    lane = _iota2((GROUP, LANES), 1)
    h0 = lane < HEAD_DIM
    nh0 = jnp.logical_not(h0)
    tl = _iota2((LANES, GROUP), 1)
    br = _iota2((LANES, LANES), 0)
    bc = _iota2((LANES, LANES), 1)
    bd128 = (br >> 6) == (bc >> 6)

    for pr in probs:
        ri, ci, same_chunk, incl, _ = pr.masks
        f = pr.lb + (1.0 - pr.lb) * _sigmoid(pr.f_raw)
        kk = 1.0 - f
        q = pr.q_raw * _sigmoid(pr.q_raw) * (HEAD_DIM ** -0.5)
        cum = _dot_sel(incl, jnp.log(f))
        pr.cedge = _block_rows(cum, CHUNK, 0 if pr.reverse else CHUNK - 1)
        pr.qd = q * jnp.exp(cum)
        pr.kd = kk * jnp.exp(pr.cedge - cum)
        if pr.reverse:
            m32 = same_chunk & ((ri & 63) < 32) & ((ci & 63) >= 32)
            m16 = ((ri >> 5) == (ci >> 5)) & ((ri & 31) < 16) & ((ci & 31) >= 16)
            mdg = ((ri >> 4) == (ci >> 4)) & (ri <= ci)
            ref32, ref16, refdg = _block_rows(cum, 64, 32), _block_rows(cum, 32, 16), _block_rows(cum, 16, 8)
        else:
            m32 = same_chunk & ((ri & 63) >= 32) & ((ci & 63) < 32)
            m16 = ((ri >> 5) == (ci >> 5)) & ((ri & 31) >= 16) & ((ci & 31) < 16)
            mdg = ((ri >> 4) == (ci >> 4)) & (ri >= ci)
            ref32, ref16, refdg = _block_rows(cum, 64, 31), _block_rows(cum, 32, 15), _block_rows(cum, 16, 7)
        pr.level_masks = (m32, m16, mdg)
        pr.qs = (q * jnp.exp(jnp.minimum(cum - ref32, 0.0)),
                 q * jnp.exp(jnp.minimum(cum - ref16, 0.0)),
                 q * jnp.exp(jnp.clip(cum - refdg, -EXP_CLAMP, EXP_CLAMP)))
        pr.ks = ((kk * jnp.exp(jnp.minimum(ref32 - cum, 0.0))).astype(BF16),
                 (kk * jnp.exp(jnp.minimum(ref16 - cum, 0.0))).astype(BF16),
                 (kk * jnp.exp(jnp.clip(refdg - cum, -EXP_CLAMP, EXP_CLAMP))).astype(BF16))

    for pr in probs:
        v16 = pr.v.astype(BF16)
        m32, m16, mdg = pr.level_masks
        intra = []
        for h in range(2):
            mh = h0 if h == 0 else nh0
            a = jnp.where(m32, _dot_nt(jnp.where(mh, pr.qs[0], 0.0), pr.ks[0]),
                          jnp.where(m16, _dot_nt(jnp.where(mh, pr.qs[1], 0.0), pr.ks[1]),
                                    jnp.where(mdg, _dot_nt(jnp.where(mh, pr.qs[2], 0.0), pr.ks[2]), 0.0)))
            intra.append(jnp.dot(a.astype(BF16), v16, preferred_element_type=F32))
        pr.intra = jnp.where(h0, intra[0], intra[1])
        v_t = pr.v.T
        pr.upd = [jnp.where(bd128, jnp.dot(jnp.where((tl >> 6) == c, v_t, 0.0), pr.kd,
                                           preferred_element_type=F32), 0.0)
                  for c in range(CHUNKS_PER_GROUP)]
        pr.out = [None] * CHUNKS_PER_GROUP

    for t in range(CHUNKS_PER_GROUP):
        for pr in probs:
            c = CHUNKS_PER_GROUP - 1 - t if pr.reverse else t
            rs = slice(c * CHUNK, (c + 1) * CHUNK)
            pr.out[c] = _dot_nt(pr.qd[rs], pr.state_t) + pr.intra[rs]
            cd = jnp.exp(pr.cedge[c * CHUNK:c * CHUNK + 1, :])
            pr.state_t = pr.state_t * cd + pr.upd[c]
    return [jnp.concatenate(pr.out, axis=0) for pr in probs]


def _hgrn_kernel(ff_ref, fb_ref, if_ref, ib_ref, qf_ref, qb_ref, lb_ref, of_ref, ob_ref, s_scr):
    n = pl.program_id(1)

    @pl.when(n == 0)
    def _():
        s_scr[...] = jnp.zeros_like(s_scr)

    pairs = C_HEADS // 2
    probs = []
    for d, (f_ref, i_ref, q_ref) in enumerate(((ff_ref, if_ref, qf_ref), (fb_ref, ib_ref, qb_ref))):
        masks = _scan_masks(d == 1)
        for p in range(pairs):
            sl = slice(p * LANES, (p + 1) * LANES)
            probs.append(_HgrnProblem(reverse=d == 1, masks=masks, slot=d * pairs + p, lb=lb_ref[:, sl],
                                      f_raw=f_ref[0, :, sl], v=i_ref[0, :, sl].astype(F32),
                                      q_raw=q_ref[0, :, sl].astype(F32), state_t=s_scr[d * pairs + p]))
    outs = _hgrn_group(probs)
    for pr, out in zip(probs, outs):
        o_ref = ob_ref if pr.reverse else of_ref
        p = pr.slot % pairs
        o_ref[0, :, p * LANES:(p + 1) * LANES] = out.astype(BF16)
        s_scr[pr.slot] = pr.state_t


def hgrn_scan(wide3, gates3, lb):
    b, seq_len, _ = wide3.shape
    n_groups = seq_len // GROUP
    fwd = lambda n: n
    bwd = lambda n: n_groups - 1 - n
    spec = lambda gidx, blk: pl.BlockSpec((1, GROUP, C_WIDTH), lambda i, n: (i, gidx(n), blk))
    f_blk = (COL_C_F - D_WIDE) // C_WIDTH
    out_sds = jax.ShapeDtypeStruct((b, seq_len, C_WIDTH), BF16)
    return pl.pallas_call(
        _hgrn_kernel,
        out_shape=(out_sds, out_sds),
        grid=(b, n_groups),
        in_specs=[spec(fwd, f_blk), spec(bwd, f_blk + 1),
                  spec(fwd, COL_C_I // C_WIDTH), spec(bwd, COL_C_I // C_WIDTH),
                  spec(fwd, COL_C_Q // C_WIDTH), spec(bwd, COL_C_Q // C_WIDTH),
                  pl.BlockSpec((1, C_WIDTH), lambda i, n: (0, 0))],
        out_specs=(spec(fwd, 0), spec(bwd, 0)),
        scratch_shapes=[pltpu.VMEM((C_HEADS, LANES, LANES), F32)],
        compiler_params=_cparams(("parallel", "arbitrary")),
        name="hgrn_scan",
    )(gates3, gates3, wide3, wide3, wide3, wide3, lb)


ATT_TQ = 512


def _norm_rope(x, w, cos, sin_signed, first_half16):
    lane = _iota2(x.shape, 1)
    h0 = lane < HEAD_DIM
    ms = _head_sum(x * x, h0) * (1.0 / HEAD_DIM)
    xn = x * lax.rsqrt(ms + RMS_EPS) * w
    partner = jnp.where(first_half16, pltpu.roll(xn, LANES - 16, 1), pltpu.roll(xn, 16, 1))
    return xn * cos + partner * sin_signed


def _attn_kernel(q_ref, k_ref, v_ref, cosq_ref, sinq_ref, cosk_ref, sink_ref, qw_ref, kw_ref, o_ref,
                 k_scr, v_scr):
    qi = pl.program_id(1)

    @pl.when(qi == 0)
    def _():
        kx = k_ref[0].astype(F32)
        lane_k = _iota2(kx.shape, 1)
        k_scr[...] = _norm_rope(kx, kw_ref[...], cosk_ref[...], sink_ref[...], (lane_k & 31) < 16).astype(BF16)
        v_scr[:, :LANES] = v_ref[0]
        v_scr[:, LANES:] = jnp.ones((v_scr.shape[0], LANES), BF16)

    lane = _iota2((ATT_TQ, LANES), 1)
    h0 = lane < HEAD_DIM
    fh = (lane & 31) < 16
    cos = cosq_ref[...]
    sin = sinq_ref[...]
    kmat = k_scr[...]
    vmat = v_scr[...]
    q_scale = (HEAD_DIM ** -0.5) * math.log2(math.e)
    for j in range(B_Q_HEADS // 2):
        qn = _norm_rope(q_ref[0, :, j * LANES:(j + 1) * LANES].astype(F32), qw_ref[...], cos, sin, fh) * q_scale
        outs = []
        for half in range(2):
            mh = h0 if half == 0 else jnp.logical_not(h0)
            s = _dot_nt(jnp.where(mh, qn, 0.0), kmat)
            m = jnp.max(s, axis=-1, keepdims=True)
            e = jnp.exp2((s - m).astype(BF16))
            pvl = jnp.dot(e, vmat, preferred_element_type=F32)
            outs.append(pvl[:, :LANES] / pvl[:, LANES:LANES + 1])
        o_ref[0, :, j * LANES:(j + 1) * LANES] = jnp.where(h0, outs[0], outs[1]).astype(BF16)


def gqa_attention(proj, cos_t, sin_t, q_w, k_w, seq_len):
    b = proj.shape[0]
    return pl.pallas_call(
        _attn_kernel,
        out_shape=jax.ShapeDtypeStruct((b, seq_len, B_WIDTH), BF16),
        grid=(b, seq_len // ATT_TQ),
        in_specs=[pl.BlockSpec((1, ATT_TQ, B_WIDTH), lambda i, t: (i, t, COL_B_Q // B_WIDTH)),
                  pl.BlockSpec((1, seq_len, LANES), lambda i, t: (i, 0, COL_B_K // LANES)),
                  pl.BlockSpec((1, seq_len, LANES), lambda i, t: (i, 0, COL_B_V // LANES)),
                  pl.BlockSpec((ATT_TQ, LANES), lambda i, t: (t, 0)),
                  pl.BlockSpec((ATT_TQ, LANES), lambda i, t: (t, 0)),
                  pl.BlockSpec((seq_len, LANES), lambda i, t: (0, 0)),
                  pl.BlockSpec((seq_len, LANES), lambda i, t: (0, 0)),
                  pl.BlockSpec((1, LANES), lambda i, t: (0, 0)),
                  pl.BlockSpec((1, LANES), lambda i, t: (0, 0))],
        out_specs=pl.BlockSpec((1, ATT_TQ, B_WIDTH), lambda i, t: (i, t, 0)),
        scratch_shapes=[pltpu.VMEM((seq_len, LANES), BF16), pltpu.VMEM((seq_len, 2 * LANES), BF16)],
        compiler_params=_cparams(("parallel", "arbitrary")),
        name="gqa_attention",
    )(proj, proj, proj, cos_t, sin_t, cos_t, sin_t, q_w, k_w)


def rope_tables(seq_len):
    rows = seq_len // GRID_W
    row = jnp.repeat(jnp.arange(rows, dtype=F32), GRID_W)
    col = jnp.tile(jnp.arange(GRID_W, dtype=F32), rows)
    inv_freq = ROPE_THETA ** (-jnp.arange(0, ROPE_AXIS_DIM, 2, dtype=F32) / ROPE_AXIS_DIM)
    ang_r = row[:, None] * inv_freq
    ang_c = col[:, None] * inv_freq
    cos64 = jnp.concatenate([jnp.cos(ang_r), jnp.cos(ang_r), jnp.cos(ang_c), jnp.cos(ang_c)], axis=1)
    sin64 = jnp.concatenate([-jnp.sin(ang_r), jnp.sin(ang_r), -jnp.sin(ang_c), jnp.sin(ang_c)], axis=1)
    return jnp.tile(cos64, (1, 2)), jnp.tile(sin64, (1, 2))


MIX_TM = 256


def _mixout_kernel(oaf_ref, oab_ref, z_ref, ob_ref, ocf_ref, ocb_ref, g_ref, x_ref, wa_ref, wc_ref,
                   wo_ref, lg_ref, lb_ref, xo_ref, xb_ref):
    lane = _iota2((MIX_TM, LANES), 1)
    h0 = lane < HEAD_DIM
    parts = []
    for j in range(A_WIDTH // LANES):
        sl = slice(j * LANES, (j + 1) * LANES)
        o = oaf_ref[:, sl].astype(F32) + oab_ref[:, sl].astype(F32)
        ms = _head_sum(o * o, h0) * (1.0 / HEAD_DIM)
        z = z_ref[:, sl].astype(F32)
        parts.append((o * lax.rsqrt(ms + RMS_EPS) * wa_ref[...] * (z * _sigmoid(z))).astype(BF16))
    parts.append(ob_ref[...])
    for j in range(C_WIDTH // LANES):
        sl = slice(j * LANES, (j + 1) * LANES)
        o = ocf_ref[:, sl].astype(F32) + ocb_ref[:, sl].astype(F32)
        ms = _head_sum(o * o, h0) * (1.0 / HEAD_DIM)
        parts.append((o * lax.rsqrt(ms + RMS_EPS) * wc_ref[...]
                      * _sigmoid(g_ref[:, sl].astype(F32))).astype(BF16))
    mixed = jnp.concatenate(parts, axis=1)
    h = jnp.dot(mixed, wo_ref[...], preferred_element_type=F32)
    y = _layer_norm_rows(DN_ALPHA * x_ref[...] + h, lg_ref[...], lb_ref[...])
    xo_ref[...] = y
    xb_ref[...] = y.astype(BF16)


def mixer_output(oa_f, oa_b, proj2, ob, oc_f, oc_b, x2, gdn_w, hgrn_w, w_out, ln_g, ln_b):
    m = x2.shape[0]
    row = lambda w: pl.BlockSpec((MIX_TM, w), lambda i: (i, 0))
    const = lambda r, c: pl.BlockSpec((r, c), lambda i: (0, 0))
    return pl.pallas_call(
        _mixout_kernel,
        out_shape=(jax.ShapeDtypeStruct((m, D_MODEL), F32), jax.ShapeDtypeStruct((m, D_MODEL), BF16)),
        grid=(m // MIX_TM,),
        in_specs=[row(A_WIDTH), row(A_WIDTH),
                  pl.BlockSpec((MIX_TM, A_WIDTH), lambda i: (i, COL_A_Z // A_WIDTH)),
                  row(B_WIDTH), row(C_WIDTH), row(C_WIDTH),
                  pl.BlockSpec((MIX_TM, C_WIDTH), lambda i: (i, COL_C_G // C_WIDTH)),
                  row(D_MODEL), const(1, LANES), const(1, LANES),
                  const(D_MIX, D_MODEL), const(1, D_MODEL), const(1, D_MODEL)],
        out_specs=(row(D_MODEL), row(D_MODEL)),
        compiler_params=_cparams(("parallel",)),
        name="mixer_output",
    )(oa_f, oa_b, proj2, ob, oc_f, oc_b, proj2, x2, gdn_w, hgrn_w, w_out, ln_g, ln_b)


XATT_TM = 512


def _xattn_kernel(xb_ref, x_ref, k_ref, v_ref, wq_ref, wo_ref, lg_ref, lb_ref, xo_ref, xob_ref):
    q = jnp.dot(xb_ref[0], wq_ref[...], preferred_element_type=F32) * (X_HEAD_DIM ** -0.5)
    outs = []
    for h in range(X_HEADS):
        sl = slice(h * X_HEAD_DIM, (h + 1) * X_HEAD_DIM)
        s = _dot_nt(q[:, sl], k_ref[0, :, sl])
        m = jnp.max(s, axis=-1, keepdims=True)
        e = jnp.exp(s - m)
        l = jnp.sum(e, axis=-1, keepdims=True)
        outs.append((jnp.dot(e.astype(BF16), v_ref[0, :, sl], preferred_element_type=F32) / l).astype(BF16))
    o = jnp.concatenate(outs, axis=1)
    c = jnp.dot(o, wo_ref[...], preferred_element_type=F32)
    y = _layer_norm_rows(DN_ALPHA * x_ref[0] + c, lg_ref[...], lb_ref[...])
    xo_ref[0] = y
    xob_ref[0] = y.astype(BF16)


def cross_attention(xb3, x3, k3, v3, wq, wo, ln_g, ln_b):
    b, s, _ = x3.shape
    mem = k3.shape[1]
    row = pl.BlockSpec((1, XATT_TM, D_MODEL), lambda i, t: (i, t, 0))
    const = lambda r, c: pl.BlockSpec((r, c), lambda i, t: (0, 0))
    kv = pl.BlockSpec((1, mem, D_MODEL), lambda i, t: (i, 0, 0))
    return pl.pallas_call(
        _xattn_kernel,
        out_shape=(jax.ShapeDtypeStruct((b, s, D_MODEL), F32), jax.ShapeDtypeStruct((b, s, D_MODEL), BF16)),
        grid=(b, s // XATT_TM),
        in_specs=[row, row, kv, kv, const(D_MODEL, D_MODEL), const(D_MODEL, D_MODEL),
                  const(1, D_MODEL), const(1, D_MODEL)],
        out_specs=(row, row),
        compiler_params=_cparams(("parallel", "parallel")),
        name="cross_attention",
    )(xb3, x3, k3, v3, wq, wo, ln_g, ln_b)


FFN_TM = 512
FFN_TF = 1408


def _ffn_kernel(xb_ref, x_ref, wg_ref, wu_ref, wd_ref, lg_ref, lb_ref, xo_ref, xob_ref):
    xb = xb_ref[...]
    acc = None
    for j in range(D_FF // FFN_TF):
        sl = slice(j * FFN_TF, (j + 1) * FFN_TF)
        g = jnp.dot(xb, wg_ref[:, sl], preferred_element_type=F32)
        u = jnp.dot(xb, wu_ref[:, sl], preferred_element_type=F32)
        h = (g * _sigmoid(g) * u).astype(BF16)
        part = jnp.dot(h, wd_ref[sl, :], preferred_element_type=F32)
        acc = part if acc is None else acc + part
    y = _layer_norm_rows(DN_ALPHA * x_ref[...] + acc, lg_ref[...], lb_ref[...])
    xo_ref[...] = y
    xob_ref[...] = y.astype(BF16)


def dense_ffn(xb2, x2, wg, wu, wd, ln_g, ln_b):
    m = x2.shape[0]
    row = pl.BlockSpec((FFN_TM, D_MODEL), lambda i: (i, 0))
    const = pl.BlockSpec((1, D_MODEL), lambda i: (0, 0))
    resident = lambda r, c: pl.BlockSpec((r, c), lambda i: (0, 0), pipeline_mode=pl.Buffered(1))
    return pl.pallas_call(
        _ffn_kernel,
        out_shape=(jax.ShapeDtypeStruct((m, D_MODEL), F32), jax.ShapeDtypeStruct((m, D_MODEL), BF16)),
        grid=(m // FFN_TM,),
        in_specs=[row, row, resident(D_MODEL, D_FF), resident(D_MODEL, D_FF), resident(D_FF, D_MODEL),
                  const, const],
        out_specs=(row, row),
        compiler_params=_cparams(("parallel",)),
        name="dense_ffn",
    )(xb2, x2, wg, wu, wd, ln_g, ln_b)


MOE_TM = 512
MOE_WIN = 512
MOE_TF = 1792
ROUTER_TM = 512


def _router_kernel(xb_ref, wr_ref, e_ref, w_ref):
    logits = jnp.dot(xb_ref[...], wr_ref[...], preferred_element_type=F32)
    lane = _iota2(logits.shape, 1)
    neg = jnp.float32(-jnp.inf)
    lane_f = lane.astype(F32)
    lg = jnp.where(lane < N_EXPERTS, logits, neg)
    m1 = jnp.max(lg, axis=-1, keepdims=True)
    i1 = jnp.min(jnp.where(lg == m1, lane_f, float(LANES)), axis=-1, keepdims=True)
    lg2 = jnp.where(lane_f == i1, neg, lg)
    m2 = jnp.max(lg2, axis=-1, keepdims=True)
    i2 = jnp.min(jnp.where(lg2 == m2, lane_f, float(LANES)), axis=-1, keepdims=True)
    t = jnp.exp(m2 - m1)
    w1 = 1.0 / (1.0 + t)
    w2 = t / (1.0 + t)
    e_ref[...] = jnp.where(lane == 0, i1, jnp.where(lane == 1, i2, 0.0)).astype(jnp.int32)
    w_ref[...] = jnp.where(lane == 0, w1, jnp.where(lane == 1, w2, 0.0))


def moe_router(xb2, w_router_p):
    m = xb2.shape[0]
    return pl.pallas_call(
        _router_kernel,
        out_shape=(jax.ShapeDtypeStruct((m, LANES), jnp.int32), jax.ShapeDtypeStruct((m, LANES), F32)),
        grid=(m // ROUTER_TM,),
        in_specs=[pl.BlockSpec((ROUTER_TM, D_MODEL), lambda i: (i, 0)),
                  pl.BlockSpec((D_MODEL, LANES), lambda i: (0, 0))],
        out_specs=(pl.BlockSpec((ROUTER_TM, LANES), lambda i: (i, 0)),
                   pl.BlockSpec((ROUTER_TM, LANES), lambda i: (i, 0))),
        compiler_params=_cparams(("parallel",)),
        name="moe_router",
    )(xb2, w_router_p)


def _moe_gather_kernel(rb_ref, win_ref, flag_ref, dest_ref, x_ref, o_ref):
    p = pl.program_id(0)
    flag = flag_ref[p]

    @pl.when((flag & 2) != 0)
    def _():
        o_ref[...] = jnp.zeros_like(o_ref)

    @pl.when((flag & 1) != 0)
    def _():
        d = dest_ref[0]
        rows = _iota2((MOE_TM, MOE_WIN), 0) + rb_ref[p] * MOE_TM
        onehot = jnp.where((d[0:1, :] == rows) | (d[1:2, :] == rows), 1.0, 0.0).astype(BF16)
        o_ref[...] += jnp.dot(onehot, x_ref[...], preferred_element_type=F32).astype(BF16)


def moe_gather(pair_rb, pair_win, pair_flag, dest_rows, xb2, n_rows):
    n_pairs = pair_rb.shape[0]
    grid_spec = pltpu.PrefetchScalarGridSpec(
        num_scalar_prefetch=3,
        grid=(n_pairs,),
        in_specs=[pl.BlockSpec((1, TOP_K, MOE_WIN), lambda p, rb, win, fl: (win[p], 0, 0)),
                  pl.BlockSpec((MOE_WIN, D_MODEL), lambda p, rb, win, fl: (win[p], 0))],
        out_specs=pl.BlockSpec((MOE_TM, D_MODEL), lambda p, rb, win, fl: (rb[p], 0)),
    )
    return pl.pallas_call(
        _moe_gather_kernel,
        out_shape=jax.ShapeDtypeStruct((n_rows, D_MODEL), BF16),
        grid_spec=grid_spec,
        compiler_params=_cparams(("arbitrary",)),
        name="moe_gather",
    )(pair_rb, pair_win, pair_flag, dest_rows, xb2)


def _moe_ffn_kernel(be_ref, bv_ref, x_ref, wg_ref, wu_ref, wd_ref, o_ref, acc_ref):
    i = pl.program_id(0)
    j = pl.program_id(1)
    valid = bv_ref[i] != 0

    @pl.when(j == 0)
    def _():
        acc_ref[...] = jnp.zeros_like(acc_ref)

    @pl.when(valid)
    def _():
        xb = x_ref[...]
        g = jnp.dot(xb, wg_ref[0], preferred_element_type=F32)
        u = jnp.dot(xb, wu_ref[0], preferred_element_type=F32)
        h = (g * _sigmoid(g) * u).astype(BF16)
        acc_ref[...] += jnp.dot(h, wd_ref[0], preferred_element_type=F32)

    @pl.when(j == pl.num_programs(1) - 1)
    def _():
        o_ref[...] = acc_ref[...].astype(BF16)


def moe_expert_ffn(block_e, block_valid, xrows, wg, wu, wd):
    n_rows = xrows.shape[0]
    grid_spec = pltpu.PrefetchScalarGridSpec(
        num_scalar_prefetch=2,
        grid=(n_rows // MOE_TM, D_FF_EXPERT // MOE_TF),
        in_specs=[pl.BlockSpec((MOE_TM, D_MODEL), lambda i, j, be, bv: (i * bv[i], 0)),
                  pl.BlockSpec((1, D_MODEL, MOE_TF), lambda i, j, be, bv: (be[i], 0, j)),
                  pl.BlockSpec((1, D_MODEL, MOE_TF), lambda i, j, be, bv: (be[i], 0, j)),
                  pl.BlockSpec((1, MOE_TF, D_MODEL), lambda i, j, be, bv: (be[i], j, 0))],
        out_specs=pl.BlockSpec((MOE_TM, D_MODEL), lambda i, j, be, bv: (i, 0)),
        scratch_shapes=[pltpu.VMEM((MOE_TM, D_MODEL), F32)],
    )
    return pl.pallas_call(
        _moe_ffn_kernel,
        out_shape=jax.ShapeDtypeStruct((n_rows, D_MODEL), BF16),
        grid_spec=grid_spec,
        compiler_params=_cparams(("arbitrary", "arbitrary")),
        name="moe_expert_ffn",
    )(block_e, block_valid, xrows, wg, wu, wd)


def _moe_combine_kernel(rb_ref, win_ref, flag_ref, dest_ref, w_ref, y_ref, x_ref, lg_ref, lb_ref, xo_ref, acc_ref):
    p = pl.program_id(0)
    flag = flag_ref[p]

    @pl.when((flag & 2) != 0)
    def _():
        acc_ref[...] = jnp.zeros_like(acc_ref)

    @pl.when((flag & 1) != 0)
    def _():
        d = dest_ref[...]
        wt = w_ref[...]
        cols = _iota2((MOE_WIN, MOE_TM), 1) + rb_ref[p] * MOE_TM
        sel = (jnp.where(d[:, 0:1] == cols, wt[:, 0:1], 0.0)
               + jnp.where(d[:, 1:2] == cols, wt[:, 1:2], 0.0)).astype(BF16)
        acc_ref[...] += jnp.dot(sel, y_ref[...], preferred_element_type=F32)

    @pl.when((flag & 4) != 0)
    def _():
        xo_ref[...] = _layer_norm_rows(DN_ALPHA * x_ref[...] + acc_ref[...], lg_ref[...], lb_ref[...])


def moe_combine(pair_rb, pair_win, pair_flag, dest_cols, w_cols, yrows, x2, ln_g, ln_b):
    n_pairs = pair_rb.shape[0]
    m = x2.shape[0]
    tok = lambda p, rb, win, fl: (win[p], 0)
    grid_spec = pltpu.PrefetchScalarGridSpec(
        num_scalar_prefetch=3,
        grid=(n_pairs,),
        in_specs=[pl.BlockSpec((MOE_WIN, TOP_K), tok),
                  pl.BlockSpec((MOE_WIN, TOP_K), tok),
                  pl.BlockSpec((MOE_TM, D_MODEL), lambda p, rb, win, fl: (rb[p], 0)),
                  pl.BlockSpec((MOE_WIN, D_MODEL), tok),
                  pl.BlockSpec((1, D_MODEL), lambda p, rb, win, fl: (0, 0)),
                  pl.BlockSpec((1, D_MODEL), lambda p, rb, win, fl: (0, 0))],
        out_specs=pl.BlockSpec((MOE_WIN, D_MODEL), tok),
        scratch_shapes=[pltpu.VMEM((MOE_WIN, D_MODEL), F32)],
    )
    return pl.pallas_call(
        _moe_combine_kernel,
        out_shape=jax.ShapeDtypeStruct((m, D_MODEL), F32),
        grid_spec=grid_spec,
        compiler_params=_cparams(("arbitrary",)),
        name="moe_combine",
    )(pair_rb, pair_win, pair_flag, dest_cols, w_cols, yrows, x2, ln_g, ln_b)


def _pair_tables(n_blk, blk_lo, n_pairs):
    cell_end = jnp.cumsum(n_blk)
    cell_start = cell_end - n_blk
    pidx = jnp.arange(n_pairs, dtype=jnp.int32)
    valid = pidx < cell_end[-1]
    cell = jnp.sum((cell_end[None, :] <= pidx[:, None]).astype(jnp.int32), axis=1)
    cell = jnp.minimum(cell, n_blk.shape[0] - 1)
    rb = blk_lo[cell] + pidx - cell_start[cell]
    return cell, rb, valid


def moe_layer(xb2, x2, w_router, wg, wu, wd, ln_g, ln_b):
    n = x2.shape[0]
    n_win = n // MOE_WIN
    n_rows = n * TOP_K + N_EXPERTS * MOE_TM
    n_blocks = n_rows // MOE_TM
    n_pairs = n_blocks + N_EXPERTS * n_win

    wr = jnp.pad(w_router, ((0, 0), (0, LANES - N_EXPERTS))).astype(BF16)
    top_e_t, top_w_t = moe_router(xb2, wr)
    top_e = top_e_t[:, :TOP_K]
    top_w = top_w_t[:, :TOP_K]

    experts = jnp.arange(N_EXPERTS, dtype=jnp.int32)
    tok_onehot = ((top_e[:, 0:1] == experts[None, :]) | (top_e[:, 1:2] == experts[None, :])).astype(jnp.int32)
    csum = jnp.cumsum(tok_onehot, axis=0)
    counts = csum[-1]
    rank = csum - tok_onehot
    padded = (counts + MOE_TM - 1) // MOE_TM * MOE_TM
    pad_end = jnp.cumsum(padded)
    pad_start = pad_end - padded
    dest = pad_start[top_e] + jnp.take_along_axis(rank, top_e, axis=1)

    blk_row0 = jnp.arange(n_blocks, dtype=jnp.int32) * MOE_TM
    block_e = jnp.minimum(jnp.sum((pad_end[None, :] <= blk_row0[:, None]).astype(jnp.int32), axis=1),
                          N_EXPERTS - 1).astype(jnp.int32)
    block_valid = (blk_row0 < pad_end[-1]).astype(jnp.int32)

    cell_cnt = tok_onehot.reshape(n_win, MOE_WIN, N_EXPERTS).sum(axis=1)
    cell_lo = pad_start[None, :] + jnp.cumsum(cell_cnt, axis=0) - cell_cnt
    blk_lo = cell_lo // MOE_TM
    n_blk = jnp.where(cell_cnt > 0, (cell_lo + cell_cnt - 1) // MOE_TM - blk_lo + 1, 0)

    cell, g_rb, g_valid = _pair_tables(n_blk.T.reshape(-1), blk_lo.T.reshape(-1), n_pairs)
    g_win = cell % n_win
    last_rb = jnp.max(jnp.where(g_valid, g_rb, 0))
    g_rb = jnp.where(g_valid, g_rb, last_rb).astype(jnp.int32)
    g_win = jnp.where(g_valid, g_win, 0).astype(jnp.int32)
    prev_rb = jnp.concatenate([jnp.full((1,), -1, jnp.int32), g_rb[:-1]])
    g_flag = g_valid.astype(jnp.int32) + 2 * (g_valid & (g_rb != prev_rb)).astype(jnp.int32)
    dest_rows = dest.reshape(n_win, MOE_WIN, TOP_K).transpose(0, 2, 1)
    xrows = moe_gather(g_rb, g_win, g_flag, dest_rows, xb2, n_rows)

    yrows = moe_expert_ffn(block_e, block_valid, xrows, wg, wu, wd)

    cell, c_rb, c_valid = _pair_tables(n_blk.reshape(-1), blk_lo.reshape(-1), n_pairs)
    c_win = jnp.where(c_valid, cell // N_EXPERTS, n_win - 1).astype(jnp.int32)
    c_rb = jnp.where(c_valid, c_rb, 0).astype(jnp.int32)
    prev_win = jnp.concatenate([jnp.full((1,), -1, jnp.int32), c_win[:-1]])
    next_win = jnp.concatenate([c_win[1:], jnp.full((1,), -1, jnp.int32)])
    next_valid = jnp.concatenate([c_valid[1:], jnp.zeros((1,), bool)])
    c_first = c_valid & (c_win != prev_win)
    c_last = c_valid & ((c_win != next_win) | jnp.logical_not(next_valid))
    c_flag = c_valid.astype(jnp.int32) + 2 * c_first.astype(jnp.int32) + 4 * c_last.astype(jnp.int32)
    return moe_combine(c_rb, c_win, c_flag, dest, top_w, yrows, x2, ln_g, ln_b)


def _w_in_perm():
    zero = 3480
    perm = np.full((D_IN_P,), zero, np.int64)
    o_qkv, o_z, o_beta, o_decay, o_bq, o_bk, o_bv, o_cf, o_ci, o_cq, o_cg = (
        0, 1152, 1536, 1548, 1560, 1944, 2072, 2200, 2712, 2968, 3224)
    perm[COL_A_QKV:COL_A_QKV + 1152] = np.arange(o_qkv, o_qkv + 1152)
    perm[COL_A_Z:COL_A_Z + 384] = np.arange(o_z, o_z + 384)
    for j in range(3):
        for a in range(2):
            dst = COL_B_Q + j * LANES + a * HEAD_DIM
            src = o_bq + (j + 3 * a) * HEAD_DIM
            perm[dst:dst + HEAD_DIM] = np.arange(src, src + HEAD_DIM)
    perm[COL_B_K:COL_B_K + 128] = np.arange(o_bk, o_bk + 128)
    perm[COL_C_G:COL_C_G + 256] = np.arange(o_cg, o_cg + 256)
    perm[COL_B_V:COL_B_V + 128] = np.arange(o_bv, o_bv + 128)
    perm[COL_C_F:COL_C_F + 512] = np.arange(o_cf, o_cf + 512)
    perm[COL_C_I:COL_C_I + 256] = np.arange(o_ci, o_ci + 256)
    perm[COL_C_Q:COL_C_Q + 256] = np.arange(o_cq, o_cq + 256)
    for d, col in enumerate((COL_GATES0, COL_GATES1)):
        for p in range(A_HEADS // 2):
            dst = col + 4 * p
            perm[dst + 0] = o_beta + d * A_HEADS + 2 * p
            perm[dst + 1] = o_beta + d * A_HEADS + 2 * p + 1
            perm[dst + 2] = o_decay + d * A_HEADS + 2 * p
            perm[dst + 3] = o_decay + d * A_HEADS + 2 * p + 1
    return perm


def _w_out_perm():
    perm = np.arange(D_MIX)
    for j in range(3):
        for a in range(2):
            dst = A_WIDTH + j * LANES + a * HEAD_DIM
            src = A_WIDTH + (j + 3 * a) * HEAD_DIM
            perm[dst:dst + HEAD_DIM] = np.arange(src, src + HEAD_DIM)
    return perm


def _gdn_params(a_log, dt_bias):
    rows = np.array([4 * (h // 2) + 2 + h % 2 for h in range(A_HEADS)])
    prm = jnp.zeros((2, GATE_ROWS, LANES), F32)
    prm = prm.at[:, rows, 0].set(a_log.astype(F32))
    prm = prm.at[:, rows, 1].set(dt_bias.astype(F32))
    return prm.at[:, rows, 2].set(1.0)


def kernel(x, mem, w_in, conv_w, gdn_a_log, gdn_dt_bias, gdn_norm_w, q_norm_w, k_norm_w, hgrn_lb_logits,
           hgrn_norm_w, w_out, ln1_g, ln1_b, xq, xk, xv, xo, ln2_g, ln2_b, ffn_wg, ffn_wu, ffn_wd,
           moe_router, moe_wg, moe_wu, moe_wd, ln3_g, ln3_b):
    b, s, d = x.shape
    m = b * s
    cos_t, sin_t = rope_tables(s)
    lb_p = jax.nn.softmax(hgrn_lb_logits.astype(F32), axis=0)
    lb_c = jnp.cumsum(lb_p, axis=0)
    lower_bounds = lb_c - lb_c[0]
    in_perm = _w_in_perm()
    out_perm = _w_out_perm()
    mem_b = mem.astype(BF16).reshape(b * mem.shape[1], d)
    tile2 = lambda w: jnp.tile(w.astype(F32), 2)[None, :]
    row = lambda v: v.astype(F32)[None, :]

    x2 = x.reshape(m, d)
    xb2 = x2.astype(BF16)
    for l in range(DEPTH):
        w_in_p = jnp.take(jnp.pad(w_in[l], ((0, 0), (0, 1))), in_perm, axis=1).astype(BF16)
        wide2 = matmul(xb2, w_in_p[:, :D_WIDE], BF16, min(1024, m), 768)
        gates2 = matmul(xb2, w_in_p[:, D_WIDE:], F32, min(1024, m), D_IN_P - D_WIDE)
        wide3 = wide2.reshape(b, s, D_WIDE)
        gates3 = gates2.reshape(b, s, D_IN_P - D_WIDE)

        oa_f, oa_b = gdn_scan(wide3, gates3, conv_w[l], _gdn_params(gdn_a_log[l], gdn_dt_bias[l]))
        ob = gqa_attention(wide3, cos_t, sin_t, tile2(q_norm_w[l]), tile2(k_norm_w[l]), s)
        lb = lower_bounds[l][None, :]
        oc_f, oc_b = hgrn_scan(wide3, gates3, lb)

        w_out_p = jnp.take(w_out[l], out_perm, axis=0).astype(BF16)
        x2, xb2 = mixer_output(oa_f.reshape(m, A_WIDTH), oa_b.reshape(m, A_WIDTH), wide2,
                               ob.reshape(m, B_WIDTH), oc_f.reshape(m, C_WIDTH), oc_b.reshape(m, C_WIDTH),
                               x2, tile2(gdn_norm_w[l]), tile2(hgrn_norm_w[l]), w_out_p,
                               row(ln1_g[l]), row(ln1_b[l]))

        k3 = matmul(mem_b, xk[l].astype(BF16), BF16, 256, 512).reshape(b, -1, d)
        v3 = matmul(mem_b, xv[l].astype(BF16), BF16, 256, 512).reshape(b, -1, d)
        x3, xb3 = cross_attention(xb2.reshape(b, s, d), x2.reshape(b, s, d), k3, v3,
                                  xq[l].astype(BF16), xo[l].astype(BF16), row(ln2_g[l]), row(ln2_b[l]))
        x2, xb2 = x3.reshape(m, d), xb3.reshape(m, d)

        if l % 2 == 0:
            x2, xb2 = dense_ffn(xb2, x2, ffn_wg[l // 2].astype(BF16), ffn_wu[l // 2].astype(BF16),
                                ffn_wd[l // 2].astype(BF16), row(ln3_g[l]), row(ln3_b[l]))
        else:
            x2 = moe_layer(xb2, x2, moe_router[l // 2], moe_wg[l // 2].astype(BF16),
                           moe_wu[l // 2].astype(BF16), moe_wd[l // 2].astype(BF16),
                           row(ln3_g[l]), row(ln3_b[l]))
            xb2 = x2.astype(BF16)
    return x2.reshape(b, s, d)
```

```python
import math

import numpy as np
import jax
import jax.numpy as jnp
from jax import lax
from jax.experimental import pallas as pl
from jax.experimental.pallas import tpu as pltpu
from jax.experimental.pallas import tpu_sc as plsc

F32 = jnp.float32
BF16 = jnp.bfloat16

D_MODEL = 1024
DEPTH = 2
HEAD_DIM = 64
A_HEADS = 6
A_WIDTH = A_HEADS * HEAD_DIM
B_Q_HEADS = 6
B_KV_HEADS = 2
B_WIDTH = B_Q_HEADS * HEAD_DIM
C_HEADS = 4
C_WIDTH = C_HEADS * HEAD_DIM
D_MIX = A_WIDTH + B_WIDTH + C_WIDTH
CONV_K = 5
CHUNK = 64
GRID_W = 64
ROPE_AXIS_DIM = HEAD_DIM // 2
ROPE_THETA = 10000.0
X_HEADS = 4
X_HEAD_DIM = D_MODEL // X_HEADS
D_FF = 2816
N_EXPERTS = 8
TOP_K = 2
D_FF_EXPERT = 3584
DN_ALPHA = (2 * DEPTH) ** 0.25
LN_EPS = 1e-5
RMS_EPS = 1e-6

LANES = 128
SUBLANES = 8
HALO = 16
GROUP = 256
CHUNKS_PER_GROUP = GROUP // CHUNK
VMEM_LIMIT = 56 * 1024 * 1024

COL_A_QKV = 0
COL_A_Z = 1152
COL_B_Q = 1536
COL_B_K = 1920
COL_C_G = 2048
COL_B_V = 2304
COL_C_I = 2560
COL_C_Q = 2816
D_WIDE = 3072
COL_C_F = D_WIDE
COL_GATES0 = D_WIDE + 512
COL_GATES1 = D_WIDE + 640
D_IN_P = 3840


def _cparams(sem):
    return pltpu.CompilerParams(dimension_semantics=sem, vmem_limit_bytes=VMEM_LIMIT)


def _sigmoid(x):
    return 1.0 / (1.0 + jnp.exp(-x))


def _softplus(x):
    return jnp.maximum(x, 0.0) + jnp.log1p(jnp.exp(-jnp.abs(x)))


def _dot(a, b):
    return jnp.dot(a.astype(BF16), b.astype(BF16), preferred_element_type=F32)


def _dot_nt(a, b):
    return lax.dot_general(a.astype(BF16), b.astype(BF16), (((1,), (1,)), ((), ())),
                           preferred_element_type=F32)


def _dot_sel(m01, x):
    m = jnp.where(m01, 1.0, 0.0).astype(BF16)
    hi = x.astype(BF16)
    r1 = x - hi.astype(F32)
    mid = r1.astype(BF16)
    lo = (r1 - mid.astype(F32)).astype(BF16)
    out = jnp.dot(m, hi, preferred_element_type=F32)
    out = out + jnp.dot(m, mid, preferred_element_type=F32)
    return out + jnp.dot(m, lo, preferred_element_type=F32)


def _dot_sel_right(x, m01):
    m = jnp.where(m01, 1.0, 0.0).astype(BF16)
    hi = x.astype(BF16)
    r1 = x - hi.astype(F32)
    mid = r1.astype(BF16)
    lo = (r1 - mid.astype(F32)).astype(BF16)
    out = jnp.dot(hi, m, preferred_element_type=F32)
    out = out + jnp.dot(mid, m, preferred_element_type=F32)
    return out + jnp.dot(lo, m, preferred_element_type=F32)


def _layer_norm_rows(y, g, b):
    mu = jnp.mean(y, axis=-1, keepdims=True)
    yc = y - mu
    var = jnp.mean(yc * yc, axis=-1, keepdims=True)
    return yc * lax.rsqrt(var + LN_EPS) * g + b


def _iota2(shape, dim):
    return lax.broadcasted_iota(jnp.int32, shape, dim)


def _head_sum(x, first_head):
    s0 = jnp.sum(jnp.where(first_head, x, 0.0), axis=-1, keepdims=True)
    s1 = jnp.sum(jnp.where(first_head, 0.0, x), axis=-1, keepdims=True)
    return jnp.where(first_head, s0, s1)


def _block_rows(t, block, row_in_block):
    parts = [jnp.broadcast_to(t[b * block + row_in_block:b * block + row_in_block + 1, :], (block, t.shape[1]))
             for b in range(GROUP // block)]
    return jnp.concatenate(parts, axis=0)


def _mm_kernel(x_ref, w_ref, o_ref):
    o_ref[...] = jnp.dot(x_ref[...], w_ref[...], preferred_element_type=F32).astype(o_ref.dtype)


def matmul(x, w, out_dtype, tm, tn):
    m, k = x.shape
    n = w.shape[1]
    return pl.pallas_call(
        _mm_kernel,
        out_shape=jax.ShapeDtypeStruct((m, n), out_dtype),
        grid=(n // tn, m // tm),
        in_specs=[pl.BlockSpec((tm, k), lambda j, i: (i, 0)),
                  pl.BlockSpec((k, tn), lambda j, i: (0, j))],
        out_specs=pl.BlockSpec((tm, tn), lambda j, i: (i, j)),
        compiler_params=_cparams(("parallel", "arbitrary")),
        name="matmul",
    )(x, w)


def _scan_masks(reverse):
    ri = _iota2((GROUP, GROUP), 0)
    ci = _iota2((GROUP, GROUP), 1)
    same_chunk = (ri >> 6) == (ci >> 6)
    if reverse:
        return ri, ci, same_chunk, same_chunk & (ri <= ci), same_chunk & (ri < ci)
    return ri, ci, same_chunk, same_chunk & (ri >= ci), same_chunk & (ri > ci)


GATE_ROWS = 16


class _GdnProblem:
    def __init__(self, **kw):
        self.__dict__.update(kw)


def _gdn_group(probs, same_chunk, h0):
    nh0 = jnp.logical_not(h0)
    sr = _iota2((CHUNK, GROUP), 0)
    sc = _iota2((CHUNK, GROUP), 1)
    eye_side = jnp.where(sr == (sc & (CHUNK - 1)), 1.0, 0.0)
    tl = _iota2((LANES, GROUP), 1)
    br = _iota2((LANES, LANES), 0)
    bc = _iota2((LANES, LANES), 1)
    bd128 = (br >> 6) == (bc >> 6)

    def block_diag(side):
        return jnp.where(same_chunk, jnp.concatenate([side] * CHUNKS_PER_GROUP, axis=0), 0.0).astype(BF16)

    for pr in probs:
        pr.q = pr.q * lax.rsqrt(_head_sum(pr.q * pr.q, h0) + RMS_EPS) * (HEAD_DIM ** -0.5)
        pr.k = pr.k * lax.rsqrt(_head_sum(pr.k * pr.k, h0) + RMS_EPS)
        pr.gedge = _block_rows(pr.gc, CHUNK, 0 if pr.reverse else CHUNK - 1)
        pr.exp_g = jnp.exp(pr.gc)
        pr.kb = pr.k * pr.beta_t
        pr.rhs = jnp.concatenate([pr.v * pr.beta_t, pr.kb * pr.exp_g], axis=1).astype(BF16)
        pr.k16 = pr.k.astype(BF16)
        pr.decay, pr.x, pr.pw, pr.bd = [], [], [], []
        for h in range(2):
            decay = jnp.exp(jnp.where(pr.incl, pr.g_col[h] - pr.g_row[h], -1e30))
            kk = _dot_nt(jnp.where(h0 if h == 0 else nh0, pr.kb, 0.0), pr.k16)
            low = jnp.where(pr.strict, kk * decay, 0.0)
            a = -(low[0:CHUNK] + low[CHUNK:2 * CHUNK] + low[2 * CHUNK:3 * CHUNK] + low[3 * CHUNK:4 * CHUNK])
            pr.decay.append(decay)
            pr.x.append(eye_side + a)
            pr.pw.append(a)
            pr.bd.append(block_diag(a))

    for _ in range(5):
        for pr in probs:
            for h in range(2):
                pr.pw[h] = jnp.dot(pr.pw[h].astype(BF16), pr.bd[h], preferred_element_type=F32)
        for pr in probs:
            for h in range(2):
                pr.bd[h] = block_diag(pr.pw[h])
        for pr in probs:
            for h in range(2):
                pr.x[h] = pr.x[h] + jnp.dot(pr.x[h].astype(BF16), pr.bd[h], preferred_element_type=F32)

    for pr in probs:
        sols = [jnp.dot(block_diag(pr.x[h]), pr.rhs, preferred_element_type=F32) for h in range(2)]
        u = jnp.where(h0, sols[0][:, :LANES], sols[1][:, :LANES])
        w = jnp.where(h0, sols[0][:, LANES:], sols[1][:, LANES:])
        wu = jnp.concatenate([w, u], axis=1)
        wu16 = wu.astype(BF16)
        aw = []
        for h in range(2):
            attn = jnp.where(pr.incl, _dot_nt(jnp.where(h0 if h == 0 else nh0, pr.q, 0.0), pr.k16) * pr.decay[h], 0.0)
            aw.append(jnp.dot(attn.astype(BF16), wu16, preferred_element_type=F32))
        pr.qp = (pr.q * pr.exp_g - jnp.where(h0, aw[0][:, :LANES], aw[1][:, :LANES])).astype(BF16)
        pr.op = jnp.where(h0, aw[0][:, LANES:], aw[1][:, LANES:])
        kd_t = (pr.k * jnp.exp(pr.gedge - pr.gc)).T
        pr.kmat, pr.nmat = [], []
        for c in range(CHUNKS_PER_GROUP):
            km = jnp.dot(jnp.where((tl >> 6) == c, kd_t, 0.0), wu, preferred_element_type=F32)
            pr.kmat.append(jnp.where(bd128, km[:, :LANES], 0.0).astype(BF16))
            pr.nmat.append(jnp.where(bd128, km[:, LANES:], 0.0))
        pr.out = [None] * CHUNKS_PER_GROUP

    for t in range(CHUNKS_PER_GROUP):
        for pr in probs:
            c = CHUNKS_PER_GROUP - 1 - t if pr.reverse else t
            rs = slice(c * CHUNK, (c + 1) * CHUNK)
            s16 = pr.state.astype(BF16)
            pr.out[c] = jnp.dot(pr.qp[rs], s16, preferred_element_type=F32) + pr.op[rs]
            cd = jnp.exp(pr.gedge[c * CHUNK:c * CHUNK + 1, :])
            pr.state = pr.state * cd - jnp.dot(pr.kmat[c], s16, preferred_element_type=F32) + pr.nmat[c]
    return [jnp.concatenate(pr.out, axis=0) for pr in probs]


def _gdn_kernel(xf_ref, xfp_ref, xfn_ref, xb_ref, xbp_ref, xbn_ref, cw_ref, gf_ref, gb_ref, prm_ref,
                of_ref, ob_ref, s_scr):
    n = pl.program_id(1)
    n_groups = pl.num_programs(1)

    @pl.when(n == 0)
    def _():
        s_scr[...] = jnp.zeros_like(s_scr)

    lane = _iota2((GROUP, LANES), 1)
    h0 = lane < HEAD_DIM
    ri = _iota2((GROUP, GROUP), 0)
    ci = _iota2((GROUP, GROUP), 1)
    same_chunk = (ri >> 6) == (ci >> 6)
    cw = cw_ref[...]
    base = SUBLANES - CONV_K // 2
    pairs = A_HEADS // 2

    probs = []
    for d, (gi, x_ref, xp_ref, xn_ref, g_ref) in enumerate(((n, xf_ref, xfp_ref, xfn_ref, gf_ref),
                                                            (n_groups - 1 - n, xb_ref, xbp_ref, xbn_ref, gb_ref))):
        reverse = d == 1
        lower, upper = same_chunk & (ri >= ci), same_chunk & (ri <= ci)
        incl = upper if reverse else lower
        strict = same_chunk & ((ri < ci) if reverse else (ri > ci))

        prev = jnp.where(gi > 0, xp_ref[0].astype(F32)[HALO - SUBLANES:], 0.0)
        nxt = jnp.where(gi < n_groups - 1, xn_ref[0].astype(F32)[:SUBLANES], 0.0)
        ext = jnp.concatenate([prev, x_ref[0].astype(F32), nxt], axis=0)
        acc = None
        for t in range(CONV_K):
            shift = (CONV_K // 2 - t) % ext.shape[0]
            tap = (pltpu.roll(ext, shift, 0) if shift else ext)[SUBLANES:SUBLANES + GROUP, :] * cw[t:t + 1, :]
            acc = tap if acc is None else acc + tap
        qkv = acc * _sigmoid(acc)

        gt_t = g_ref[0].T[0:GATE_ROWS, :]
        prm = prm_ref[d]
        e_rows = jnp.where(prm[:, 2:3] > 0.5,
                           -jnp.exp(prm[:, 0:1]) * _softplus(gt_t + prm[:, 1:2]), _sigmoid(gt_t))
        g_rows = _dot_sel_right(e_rows, lower if reverse else upper)
        slab = jnp.concatenate([e_rows, g_rows, jnp.zeros((LANES - 2 * GATE_ROWS, GROUP), F32)], axis=0)
        cols = slab.T
        for p in range(pairs):
            sl = lambda part: slice(part * A_WIDTH + p * LANES, part * A_WIDTH + (p + 1) * LANES)
            col = lambda r: cols[:, r:r + 1]
            probs.append(_GdnProblem(
                reverse=reverse, incl=incl, strict=strict, slot=d * pairs + p,
                q=qkv[:, sl(0)], k=qkv[:, sl(1)], v=qkv[:, sl(2)],
                beta_t=jnp.where(h0, col(4 * p), col(4 * p + 1)),
                gc=jnp.where(h0, col(GATE_ROWS + 4 * p + 2), col(GATE_ROWS + 4 * p + 3)),
                g_col=(col(GATE_ROWS + 4 * p + 2), col(GATE_ROWS + 4 * p + 3)),
                g_row=(g_rows[4 * p + 2:4 * p + 3, :], g_rows[4 * p + 3:4 * p + 4, :]),
                state=s_scr[d * pairs + p]))

    outs = _gdn_group(probs, same_chunk, h0)
    for pr, out in zip(probs, outs):
        o_ref = ob_ref if pr.reverse else of_ref
        p = pr.slot % pairs
        o_ref[0, :, p * LANES:(p + 1) * LANES] = out.astype(BF16)
        s_scr[pr.slot] = pr.state


def gdn_scan(wide3, gates3, conv_w, prm):
    b, seq_len, _ = wide3.shape
    n_groups = seq_len // GROUP
    halo_per_group = GROUP // HALO
    n_halo = seq_len // HALO
    width = 3 * A_WIDTH
    fwd = lambda n: n
    bwd = lambda n: n_groups - 1 - n

    def x_specs(gidx):
        return [pl.BlockSpec((1, GROUP, width), lambda i, n: (i, gidx(n), 0)),
                pl.BlockSpec((1, HALO, width),
                             lambda i, n: (i, jnp.maximum(gidx(n) * halo_per_group - 1, 0), 0)),
                pl.BlockSpec((1, HALO, width),
                             lambda i, n: (i, jnp.minimum((gidx(n) + 1) * halo_per_group, n_halo - 1), 0))]

    out_sds = jax.ShapeDtypeStruct((b, seq_len, A_WIDTH), BF16)
    return pl.pallas_call(
        _gdn_kernel,
        out_shape=(out_sds, out_sds),
        grid=(b, n_groups),
        in_specs=x_specs(fwd) + x_specs(bwd) + [
            pl.BlockSpec((CONV_K, width), lambda i, n: (0, 0)),
            pl.BlockSpec((1, GROUP, LANES), lambda i, n: (i, fwd(n), (COL_GATES0 - D_WIDE) // LANES)),
            pl.BlockSpec((1, GROUP, LANES), lambda i, n: (i, bwd(n), (COL_GATES1 - D_WIDE) // LANES)),
            pl.BlockSpec((2, GATE_ROWS, LANES), lambda i, n: (0, 0, 0))],
        out_specs=(pl.BlockSpec((1, GROUP, A_WIDTH), lambda i, n: (i, fwd(n), 0)),
                   pl.BlockSpec((1, GROUP, A_WIDTH), lambda i, n: (i, bwd(n), 0))),
        scratch_shapes=[pltpu.VMEM((A_HEADS, LANES, LANES), F32)],
        compiler_params=_cparams(("parallel", "arbitrary")),
        name="gdn_scan",
    )(wide3, wide3, wide3, wide3, wide3, wide3, conv_w, gates3, gates3, prm)


EXP_CLAMP = 60.0


class _HgrnProblem:
    def __init__(self, **kw):
        self.__dict__.update(kw)


def _hgrn_group(probs):
    lane = _iota2((GROUP, LANES), 1)
    h0 = lane < HEAD_DIM
    nh0 = jnp.logical_not(h0)
    tl = _iota2((LANES, GROUP), 1)
    br = _iota2((LANES, LANES), 0)
    bc = _iota2((LANES, LANES), 1)
    bd128 = (br >> 6) == (bc >> 6)

    for pr in probs:
        ri, ci, same_chunk, incl, _ = pr.masks
        f = pr.lb + (1.0 - pr.lb) * _sigmoid(pr.f_raw)
        kk = 1.0 - f
        q = pr.q_raw * _sigmoid(pr.q_raw) * (HEAD_DIM ** -0.5)
        cum = _dot_sel(incl, jnp.log(f))
        pr.cedge = _block_rows(cum, CHUNK, 0 if pr.reverse else CHUNK - 1)
        pr.qd = q * jnp.exp(cum)
        pr.kd = kk * jnp.exp(pr.cedge - cum)
        if pr.reverse:
            m32 = same_chunk & ((ri & 63) < 32) & ((ci & 63) >= 32)
            m16 = ((ri >> 5) == (ci >> 5)) & ((ri & 31) < 16) & ((ci & 31) >= 16)
            mdg = ((ri >> 4) == (ci >> 4)) & (ri <= ci)
            ref32, ref16, refdg = _block_rows(cum, 64, 32), _block_rows(cum, 32, 16), _block_rows(cum, 16, 8)
        else:
            m32 = same_chunk & ((ri & 63) >= 32) & ((ci & 63) < 32)
            m16 = ((ri >> 5) == (ci >> 5)) & ((ri & 31) >= 16) & ((ci & 31) < 16)
            mdg = ((ri >> 4) == (ci >> 4)) & (ri >= ci)
            ref32, ref16, refdg = _block_rows(cum, 64, 31), _block_rows(cum, 32, 15), _block_rows(cum, 16, 7)
        pr.level_masks = (m32, m16, mdg)
        pr.qs = (q * jnp.exp(jnp.minimum(cum - ref32, 0.0)),
                 q * jnp.exp(jnp.minimum(cum - ref16, 0.0)),
                 q * jnp.exp(jnp.clip(cum - refdg, -EXP_CLAMP, EXP_CLAMP)))
        pr.ks = ((kk * jnp.exp(jnp.minimum(ref32 - cum, 0.0))).astype(BF16),
                 (kk * jnp.exp(jnp.minimum(ref16 - cum, 0.0))).astype(BF16),
                 (kk * jnp.exp(jnp.clip(refdg - cum, -EXP_CLAMP, EXP_CLAMP))).astype(BF16))

    for pr in probs:
        v16 = pr.v.astype(BF16)
        m32, m16, mdg = pr.level_masks
        intra = []
        for h in range(2):
            mh = h0 if h == 0 else nh0
            a = jnp.where(m32, _dot_nt(jnp.where(mh, pr.qs[0], 0.0), pr.ks[0]),
                          jnp.where(m16, _dot_nt(jnp.where(mh, pr.qs[1], 0.0), pr.ks[1]),
                                    jnp.where(mdg, _dot_nt(jnp.where(mh, pr.qs[2], 0.0), pr.ks[2]), 0.0)))
            intra.append(jnp.dot(a.astype(BF16), v16, preferred_element_type=F32))
        pr.intra = jnp.where(h0, intra[0], intra[1])
        v_t = pr.v.T
        pr.upd = [jnp.where(bd128, jnp.dot(jnp.where((tl >> 6) == c, v_t, 0.0), pr.kd,
                                           preferred_element_type=F32), 0.0)
                  for c in range(CHUNKS_PER_GROUP)]
        pr.out = [None] * CHUNKS_PER_GROUP

    for t in range(CHUNKS_PER_GROUP):
        for pr in probs:
            c = CHUNKS_PER_GROUP - 1 - t if pr.reverse else t
            rs = slice(c * CHUNK, (c + 1) * CHUNK)
            pr.out[c] = _dot_nt(pr.qd[rs], pr.state_t) + pr.intra[rs]
            cd = jnp.exp(pr.cedge[c * CHUNK:c * CHUNK + 1, :])
            pr.state_t = pr.state_t * cd + pr.upd[c]
    return [jnp.concatenate(pr.out, axis=0) for pr in probs]


def _hgrn_kernel(ff_ref, fb_ref, if_ref, ib_ref, qf_ref, qb_ref, lb_ref, of_ref, ob_ref, s_scr):
    n = pl.program_id(1)

    @pl.when(n == 0)
    def _():
        s_scr[...] = jnp.zeros_like(s_scr)

    pairs = C_HEADS // 2
    probs = []
    for d, (f_ref, i_ref, q_ref) in enumerate(((ff_ref, if_ref, qf_ref), (fb_ref, ib_ref, qb_ref))):
        masks = _scan_masks(d == 1)
        for p in range(pairs):
            sl = slice(p * LANES, (p + 1) * LANES)
            probs.append(_HgrnProblem(reverse=d == 1, masks=masks, slot=d * pairs + p, lb=lb_ref[:, sl],
                                      f_raw=f_ref[0, :, sl], v=i_ref[0, :, sl].astype(F32),
                                      q_raw=q_ref[0, :, sl].astype(F32), state_t=s_scr[d * pairs + p]))
    outs = _hgrn_group(probs)
    for pr, out in zip(probs, outs):
        o_ref = ob_ref if pr.reverse else of_ref
        p = pr.slot % pairs
        o_ref[0, :, p * LANES:(p + 1) * LANES] = out.astype(BF16)
        s_scr[pr.slot] = pr.state_t


def hgrn_scan(wide3, gates3, lb):
    b, seq_len, _ = wide3.shape
    n_groups = seq_len // GROUP
    fwd = lambda n: n
    bwd = lambda n: n_groups - 1 - n
    spec = lambda gidx, blk: pl.BlockSpec((1, GROUP, C_WIDTH), lambda i, n: (i, gidx(n), blk))
    f_blk = (COL_C_F - D_WIDE) // C_WIDTH
    out_sds = jax.ShapeDtypeStruct((b, seq_len, C_WIDTH), BF16)
    return pl.pallas_call(
        _hgrn_kernel,
        out_shape=(out_sds, out_sds),
        grid=(b, n_groups),
        in_specs=[spec(fwd, f_blk), spec(bwd, f_blk + 1),
                  spec(fwd, COL_C_I // C_WIDTH), spec(bwd, COL_C_I // C_WIDTH),
                  spec(fwd, COL_C_Q // C_WIDTH), spec(bwd, COL_C_Q // C_WIDTH),
                  pl.BlockSpec((1, C_WIDTH), lambda i, n: (0, 0))],
        out_specs=(spec(fwd, 0), spec(bwd, 0)),
        scratch_shapes=[pltpu.VMEM((C_HEADS, LANES, LANES), F32)],
        compiler_params=_cparams(("parallel", "arbitrary")),
        name="hgrn_scan",
    )(gates3, gates3, wide3, wide3, wide3, wide3, lb)


ATT_TQ = 512


def _norm_rope(x, w, cos, sin_signed, first_half16):
    lane = _iota2(x.shape, 1)
    h0 = lane < HEAD_DIM
    ms = _head_sum(x * x, h0) * (1.0 / HEAD_DIM)
    xn = x * lax.rsqrt(ms + RMS_EPS) * w
    partner = jnp.where(first_half16, pltpu.roll(xn, LANES - 16, 1), pltpu.roll(xn, 16, 1))
    return xn * cos + partner * sin_signed


def _attn_kernel(q_ref, k_ref, v_ref, cosq_ref, sinq_ref, cosk_ref, sink_ref, qw_ref, kw_ref, o_ref,
                 k_scr, v_scr):
    qi = pl.program_id(1)

    @pl.when(qi == 0)
    def _():
        kx = k_ref[0].astype(F32)
        lane_k = _iota2(kx.shape, 1)
        k_scr[...] = _norm_rope(kx, kw_ref[...], cosk_ref[...], sink_ref[...], (lane_k & 31) < 16).astype(BF16)
        v_scr[:, :LANES] = v_ref[0]
        v_scr[:, LANES:] = jnp.ones((v_scr.shape[0], LANES), BF16)

    lane = _iota2((ATT_TQ, LANES), 1)
    h0 = lane < HEAD_DIM
    fh = (lane & 31) < 16
    cos = cosq_ref[...]
    sin = sinq_ref[...]
    kmat = k_scr[...]
    vmat = v_scr[...]
    q_scale = (HEAD_DIM ** -0.5) * math.log2(math.e)
    for j in range(B_Q_HEADS // 2):
        qn = _norm_rope(q_ref[0, :, j * LANES:(j + 1) * LANES].astype(F32), qw_ref[...], cos, sin, fh) * q_scale
        outs = []
        for half in range(2):
            mh = h0 if half == 0 else jnp.logical_not(h0)
            s = _dot_nt(jnp.where(mh, qn, 0.0), kmat)
            m = jnp.max(s, axis=-1, keepdims=True)
            e = jnp.exp2((s - m).astype(BF16))
            pvl = jnp.dot(e, vmat, preferred_element_type=F32)
            outs.append(pvl[:, :LANES] / pvl[:, LANES:LANES + 1])
        o_ref[0, :, j * LANES:(j + 1) * LANES] = jnp.where(h0, outs[0], outs[1]).astype(BF16)


def gqa_attention(proj, cos_t, sin_t, q_w, k_w, seq_len):
    b = proj.shape[0]
    return pl.pallas_call(
        _attn_kernel,
        out_shape=jax.ShapeDtypeStruct((b, seq_len, B_WIDTH), BF16),
        grid=(b, seq_len // ATT_TQ),
        in_specs=[pl.BlockSpec((1, ATT_TQ, B_WIDTH), lambda i, t: (i, t, COL_B_Q // B_WIDTH)),
                  pl.BlockSpec((1, seq_len, LANES), lambda i, t: (i, 0, COL_B_K // LANES)),
                  pl.BlockSpec((1, seq_len, LANES), lambda i, t: (i, 0, COL_B_V // LANES)),
                  pl.BlockSpec((ATT_TQ, LANES), lambda i, t: (t, 0)),
                  pl.BlockSpec((ATT_TQ, LANES), lambda i, t: (t, 0)),
                  pl.BlockSpec((seq_len, LANES), lambda i, t: (0, 0)),
                  pl.BlockSpec((seq_len, LANES), lambda i, t: (0, 0)),
                  pl.BlockSpec((1, LANES), lambda i, t: (0, 0)),
                  pl.BlockSpec((1, LANES), lambda i, t: (0, 0))],
        out_specs=pl.BlockSpec((1, ATT_TQ, B_WIDTH), lambda i, t: (i, t, 0)),
        scratch_shapes=[pltpu.VMEM((seq_len, LANES), BF16), pltpu.VMEM((seq_len, 2 * LANES), BF16)],
        compiler_params=_cparams(("parallel", "arbitrary")),
        name="gqa_attention",
    )(proj, proj, proj, cos_t, sin_t, cos_t, sin_t, q_w, k_w)


def rope_tables(seq_len):
    rows = seq_len // GRID_W
    row = jnp.repeat(jnp.arange(rows, dtype=F32), GRID_W)
    col = jnp.tile(jnp.arange(GRID_W, dtype=F32), rows)
    inv_freq = ROPE_THETA ** (-jnp.arange(0, ROPE_AXIS_DIM, 2, dtype=F32) / ROPE_AXIS_DIM)
    ang_r = row[:, None] * inv_freq
    ang_c = col[:, None] * inv_freq
    cos64 = jnp.concatenate([jnp.cos(ang_r), jnp.cos(ang_r), jnp.cos(ang_c), jnp.cos(ang_c)], axis=1)
    sin64 = jnp.concatenate([-jnp.sin(ang_r), jnp.sin(ang_r), -jnp.sin(ang_c), jnp.sin(ang_c)], axis=1)
    return jnp.tile(cos64, (1, 2)), jnp.tile(sin64, (1, 2))


MIX_TM = 256


def _mixout_kernel(oaf_ref, oab_ref, z_ref, ob_ref, ocf_ref, ocb_ref, g_ref, x_ref, wa_ref, wc_ref,
                   wo_ref, lg_ref, lb_ref, xo_ref, xb_ref):
    lane = _iota2((MIX_TM, LANES), 1)
    h0 = lane < HEAD_DIM
    parts = []
    for j in range(A_WIDTH // LANES):
        sl = slice(j * LANES, (j + 1) * LANES)
        o = oaf_ref[:, sl].astype(F32) + oab_ref[:, sl].astype(F32)
        ms = _head_sum(o * o, h0) * (1.0 / HEAD_DIM)
        z = z_ref[:, sl].astype(F32)
        parts.append((o * lax.rsqrt(ms + RMS_EPS) * wa_ref[...] * (z * _sigmoid(z))).astype(BF16))
    parts.append(ob_ref[...])
    for j in range(C_WIDTH // LANES):
        sl = slice(j * LANES, (j + 1) * LANES)
        o = ocf_ref[:, sl].astype(F32) + ocb_ref[:, sl].astype(F32)
        ms = _head_sum(o * o, h0) * (1.0 / HEAD_DIM)
        parts.append((o * lax.rsqrt(ms + RMS_EPS) * wc_ref[...]
                      * _sigmoid(g_ref[:, sl].astype(F32))).astype(BF16))
    mixed = jnp.concatenate(parts, axis=1)
    h = jnp.dot(mixed, wo_ref[...], preferred_element_type=F32)
    y = _layer_norm_rows(DN_ALPHA * x_ref[...] + h, lg_ref[...], lb_ref[...])
    xo_ref[...] = y
    xb_ref[...] = y.astype(BF16)


def mixer_output(oa_f, oa_b, proj2, ob, oc_f, oc_b, x2, gdn_w, hgrn_w, w_out, ln_g, ln_b):
    m = x2.shape[0]
    row = lambda w: pl.BlockSpec((MIX_TM, w), lambda i: (i, 0))
    const = lambda r, c: pl.BlockSpec((r, c), lambda i: (0, 0))
    return pl.pallas_call(
        _mixout_kernel,
        out_shape=(jax.ShapeDtypeStruct((m, D_MODEL), F32), jax.ShapeDtypeStruct((m, D_MODEL), BF16)),
        grid=(m // MIX_TM,),
        in_specs=[row(A_WIDTH), row(A_WIDTH),
                  pl.BlockSpec((MIX_TM, A_WIDTH), lambda i: (i, COL_A_Z // A_WIDTH)),
                  row(B_WIDTH), row(C_WIDTH), row(C_WIDTH),
                  pl.BlockSpec((MIX_TM, C_WIDTH), lambda i: (i, COL_C_G // C_WIDTH)),
                  row(D_MODEL), const(1, LANES), const(1, LANES),
                  const(D_MIX, D_MODEL), const(1, D_MODEL), const(1, D_MODEL)],
        out_specs=(row(D_MODEL), row(D_MODEL)),
        compiler_params=_cparams(("parallel",)),
        name="mixer_output",
    )(oa_f, oa_b, proj2, ob, oc_f, oc_b, proj2, x2, gdn_w, hgrn_w, w_out, ln_g, ln_b)


XATT_TM = 512


def _xattn_kernel(xb_ref, x_ref, k_ref, v_ref, wq_ref, wo_ref, lg_ref, lb_ref, xo_ref, xob_ref):
    q = jnp.dot(xb_ref[0], wq_ref[...], preferred_element_type=F32) * (X_HEAD_DIM ** -0.5)
    outs = []
    for h in range(X_HEADS):
        sl = slice(h * X_HEAD_DIM, (h + 1) * X_HEAD_DIM)
        s = _dot_nt(q[:, sl], k_ref[0, :, sl])
        m = jnp.max(s, axis=-1, keepdims=True)
        e = jnp.exp(s - m)
        l = jnp.sum(e, axis=-1, keepdims=True)
        outs.append((jnp.dot(e.astype(BF16), v_ref[0, :, sl], preferred_element_type=F32) / l).astype(BF16))
    o = jnp.concatenate(outs, axis=1)
    c = jnp.dot(o, wo_ref[...], preferred_element_type=F32)
    y = _layer_norm_rows(DN_ALPHA * x_ref[0] + c, lg_ref[...], lb_ref[...])
    xo_ref[0] = y
    xob_ref[0] = y.astype(BF16)


def cross_attention(xb3, x3, k3, v3, wq, wo, ln_g, ln_b):
    b, s, _ = x3.shape
    mem = k3.shape[1]
    row = pl.BlockSpec((1, XATT_TM, D_MODEL), lambda i, t: (i, t, 0))
    const = lambda r, c: pl.BlockSpec((r, c), lambda i, t: (0, 0))
    kv = pl.BlockSpec((1, mem, D_MODEL), lambda i, t: (i, 0, 0))
    return pl.pallas_call(
        _xattn_kernel,
        out_shape=(jax.ShapeDtypeStruct((b, s, D_MODEL), F32), jax.ShapeDtypeStruct((b, s, D_MODEL), BF16)),
        grid=(b, s // XATT_TM),
        in_specs=[row, row, kv, kv, const(D_MODEL, D_MODEL), const(D_MODEL, D_MODEL),
                  const(1, D_MODEL), const(1, D_MODEL)],
        out_specs=(row, row),
        compiler_params=_cparams(("parallel", "parallel")),
        name="cross_attention",
    )(xb3, x3, k3, v3, wq, wo, ln_g, ln_b)


FFN_TM = 512
FFN_TF = 1408


def _ffn_kernel(xb_ref, x_ref, wg_ref, wu_ref, wd_ref, lg_ref, lb_ref, xo_ref, xob_ref):
    xb = xb_ref[...]
    acc = None
    for j in range(D_FF // FFN_TF):
        sl = slice(j * FFN_TF, (j + 1) * FFN_TF)
        g = jnp.dot(xb, wg_ref[:, sl], preferred_element_type=F32)
        u = jnp.dot(xb, wu_ref[:, sl], preferred_element_type=F32)
        h = (g * _sigmoid(g) * u).astype(BF16)
        part = jnp.dot(h, wd_ref[sl, :], preferred_element_type=F32)
        acc = part if acc is None else acc + part
    y = _layer_norm_rows(DN_ALPHA * x_ref[...] + acc, lg_ref[...], lb_ref[...])
    xo_ref[...] = y
    xob_ref[...] = y.astype(BF16)


def dense_ffn(xb2, x2, wg, wu, wd, ln_g, ln_b):
    m = x2.shape[0]
    row = pl.BlockSpec((FFN_TM, D_MODEL), lambda i: (i, 0))
    const = pl.BlockSpec((1, D_MODEL), lambda i: (0, 0))
    resident = lambda r, c: pl.BlockSpec((r, c), lambda i: (0, 0), pipeline_mode=pl.Buffered(1))
    return pl.pallas_call(
        _ffn_kernel,
        out_shape=(jax.ShapeDtypeStruct((m, D_MODEL), F32), jax.ShapeDtypeStruct((m, D_MODEL), BF16)),
        grid=(m // FFN_TM,),
        in_specs=[row, row, resident(D_MODEL, D_FF), resident(D_MODEL, D_FF), resident(D_FF, D_MODEL),
                  const, const],
        out_specs=(row, row),
        compiler_params=_cparams(("parallel",)),
        name="dense_ffn",
    )(xb2, x2, wg, wu, wd, ln_g, ln_b)


MOE_TM = 512
MOE_WIN = 512
MOE_TF = 1792
ROUTER_TM = 512


def _router_kernel(xb_ref, wr_ref, e_ref, w_ref):
    logits = jnp.dot(xb_ref[...], wr_ref[...], preferred_element_type=F32)
    lane = _iota2(logits.shape, 1)
    neg = jnp.float32(-jnp.inf)
    lane_f = lane.astype(F32)
    lg = jnp.where(lane < N_EXPERTS, logits, neg)
    m1 = jnp.max(lg, axis=-1, keepdims=True)
    i1 = jnp.min(jnp.where(lg == m1, lane_f, float(LANES)), axis=-1, keepdims=True)
    lg2 = jnp.where(lane_f == i1, neg, lg)
    m2 = jnp.max(lg2, axis=-1, keepdims=True)
    i2 = jnp.min(jnp.where(lg2 == m2, lane_f, float(LANES)), axis=-1, keepdims=True)
    t = jnp.exp(m2 - m1)
    w1 = 1.0 / (1.0 + t)
    w2 = t / (1.0 + t)
    e_ref[...] = jnp.where(lane == 0, i1, jnp.where(lane == 1, i2, 0.0)).astype(jnp.int32)
    w_ref[...] = jnp.where(lane == 0, w1, jnp.where(lane == 1, w2, 0.0))


def moe_router(xb2, w_router_p):
    m = xb2.shape[0]
    return pl.pallas_call(
        _router_kernel,
        out_shape=(jax.ShapeDtypeStruct((m, LANES), jnp.int32), jax.ShapeDtypeStruct((m, LANES), F32)),
        grid=(m // ROUTER_TM,),
        in_specs=[pl.BlockSpec((ROUTER_TM, D_MODEL), lambda i: (i, 0)),
                  pl.BlockSpec((D_MODEL, LANES), lambda i: (0, 0))],
        out_specs=(pl.BlockSpec((ROUTER_TM, LANES), lambda i: (i, 0)),
                   pl.BlockSpec((ROUTER_TM, LANES), lambda i: (i, 0))),
        compiler_params=_cparams(("parallel",)),
        name="moe_router",
    )(xb2, w_router_p)


def _moe_gather_kernel(rb_ref, win_ref, flag_ref, dest_ref, x_ref, o_ref):
    p = pl.program_id(0)
    flag = flag_ref[p]

    @pl.when((flag & 2) != 0)
    def _():
        o_ref[...] = jnp.zeros_like(o_ref)

    @pl.when((flag & 1) != 0)
    def _():
        d = dest_ref[0]
        rows = _iota2((MOE_TM, MOE_WIN), 0) + rb_ref[p] * MOE_TM
        onehot = jnp.where((d[0:1, :] == rows) | (d[1:2, :] == rows), 1.0, 0.0).astype(BF16)
        o_ref[...] += jnp.dot(onehot, x_ref[...], preferred_element_type=F32).astype(BF16)


def moe_gather(pair_rb, pair_win, pair_flag, dest_rows, xb2, n_rows):
    n_pairs = pair_rb.shape[0]
    grid_spec = pltpu.PrefetchScalarGridSpec(
        num_scalar_prefetch=3,
        grid=(n_pairs,),
        in_specs=[pl.BlockSpec((1, TOP_K, MOE_WIN), lambda p, rb, win, fl: (win[p], 0, 0)),
                  pl.BlockSpec((MOE_WIN, D_MODEL), lambda p, rb, win, fl: (win[p], 0))],
        out_specs=pl.BlockSpec((MOE_TM, D_MODEL), lambda p, rb, win, fl: (rb[p], 0)),
    )
    return pl.pallas_call(
        _moe_gather_kernel,
        out_shape=jax.ShapeDtypeStruct((n_rows, D_MODEL), BF16),
        grid_spec=grid_spec,
        compiler_params=_cparams(("arbitrary",)),
        name="moe_gather",
    )(pair_rb, pair_win, pair_flag, dest_rows, xb2)


def _moe_ffn_kernel(be_ref, bv_ref, x_ref, wg_ref, wu_ref, wd_ref, o_ref, acc_ref):
    i = pl.program_id(0)
    j = pl.program_id(1)
    valid = bv_ref[i] != 0

    @pl.when(j == 0)
    def _():
        acc_ref[...] = jnp.zeros_like(acc_ref)

    @pl.when(valid)
    def _():
        xb = x_ref[...]
        g = jnp.dot(xb, wg_ref[0], preferred_element_type=F32)
        u = jnp.dot(xb, wu_ref[0], preferred_element_type=F32)
        h = (g * _sigmoid(g) * u).astype(BF16)
        acc_ref[...] += jnp.dot(h, wd_ref[0], preferred_element_type=F32)

    @pl.when(j == pl.num_programs(1) - 1)
    def _():
        y = acc_ref[...]
        lo = lax.bitcast_convert_type(y[:, :PACKED_WIDTH].astype(BF16).astype(F32), jnp.uint32) >> 16
        hi = lax.bitcast_convert_type(y[:, PACKED_WIDTH:].astype(BF16).astype(F32), jnp.uint32)
        o_ref[...] = hi | lo


def moe_expert_ffn(block_e, block_valid, xrows, wg, wu, wd):
    n_rows = xrows.shape[0]
    grid_spec = pltpu.PrefetchScalarGridSpec(
        num_scalar_prefetch=2,
        grid=(n_rows // MOE_TM, D_FF_EXPERT // MOE_TF),
        in_specs=[pl.BlockSpec((MOE_TM, D_MODEL), lambda i, j, be, bv: (i * bv[i], 0)),
                  pl.BlockSpec((1, D_MODEL, MOE_TF), lambda i, j, be, bv: (be[i], 0, j)),
                  pl.BlockSpec((1, D_MODEL, MOE_TF), lambda i, j, be, bv: (be[i], 0, j)),
                  pl.BlockSpec((1, MOE_TF, D_MODEL), lambda i, j, be, bv: (be[i], j, 0))],
        out_specs=pl.BlockSpec((MOE_TM, PACKED_WIDTH), lambda i, j, be, bv: (i, 0)),
        scratch_shapes=[pltpu.VMEM((MOE_TM, D_MODEL), F32)],
    )
    return pl.pallas_call(
        _moe_ffn_kernel,
        out_shape=jax.ShapeDtypeStruct((n_rows, PACKED_WIDTH), jnp.uint32),
        grid_spec=grid_spec,
        compiler_params=_cparams(("arbitrary", "arbitrary")),
        name="moe_expert_ffn",
    )(block_e, block_valid, xrows, wg, wu, wd)


PACKED_WIDTH = D_MODEL // 2
SC_ROWS = 128
SC_SPLIT = 4


def sc_gather_rows(data, idx):
    rows, width = data.shape
    idx = (idx[:, None] * SC_SPLIT + jnp.arange(SC_SPLIT, dtype=jnp.int32)[None, :]).reshape(-1)
    out = _sc_gather(data.reshape(rows * SC_SPLIT, width // SC_SPLIT), idx)
    return out.reshape(-1, width)


def _sc_gather(data, idx):
    n = idx.shape[0]
    d = data.shape[1]
    sc = plsc.get_sparse_core_info()
    mesh = plsc.VectorSubcoreMesh(core_axis_name="core", subcore_axis_name="subcore")
    steps = n // (SC_ROWS * sc.num_cores)

    @pl.kernel(out_type=jax.ShapeDtypeStruct((n, d), data.dtype), mesh=mesh, scratch_types=[])
    def gather_kernel(x_hbm, i_hbm, o_hbm):
        def body(i_vmem, o_vmem):
            pltpu.sync_copy(x_hbm.at[i_vmem.at[0]], o_vmem)

        pltpu.emit_pipeline(
            body,
            grid=(sc.num_cores, steps),
            in_specs=[pl.BlockSpec((1, SC_ROWS), index_map=lambda c, i: (0, c * steps + i))],
            out_specs=[pl.BlockSpec((SC_ROWS, d), index_map=lambda c, i: (c * steps + i, 0))],
            core_axis_name=("core", "subcore"),
            dimension_semantics=(pltpu.PARALLEL, pltpu.PARALLEL),
        )(i_hbm, o_hbm)

    return gather_kernel(data, idx.reshape(1, n))


def _unpack_bf16_pairs(w):
    lo = lax.bitcast_convert_type(w << 16, F32)
    hi = lax.bitcast_convert_type(w & jnp.uint32(0xFFFF0000), F32)
    return jnp.concatenate([lo, hi], axis=1)


def _moe_finish_kernel(y0_ref, y1_ref, w_ref, x_ref, lg_ref, lb_ref, xo_ref):
    wt = w_ref[...]
    y = wt[:, 0:1] * _unpack_bf16_pairs(y0_ref[...]) + wt[:, 1:2] * _unpack_bf16_pairs(y1_ref[...])
    xo_ref[...] = _layer_norm_rows(DN_ALPHA * x_ref[...] + y, lg_ref[...], lb_ref[...])


def moe_finish(ytok, top_w, x2, ln_g, ln_b):
    m = x2.shape[0]
    n_win = m // MOE_WIN
    packed = lambda off: pl.BlockSpec((MOE_WIN, PACKED_WIDTH), lambda i: (i + off, 0))
    row = lambda off: pl.BlockSpec((MOE_WIN, D_MODEL), lambda i: (i + off, 0))
    const = pl.BlockSpec((1, D_MODEL), lambda i: (0, 0))
    return pl.pallas_call(
        _moe_finish_kernel,
        out_shape=jax.ShapeDtypeStruct((m, D_MODEL), F32),
        grid=(n_win,),
        in_specs=[packed(0), packed(n_win), pl.BlockSpec((MOE_WIN, TOP_K), lambda i: (i, 0)), row(0), const, const],
        out_specs=row(0),
        compiler_params=_cparams(("parallel",)),
        name="moe_finish",
    )(ytok, ytok, top_w, x2, ln_g, ln_b)


def _pair_tables(n_blk, blk_lo, n_pairs):
    cell_end = jnp.cumsum(n_blk)
    cell_start = cell_end - n_blk
    pidx = jnp.arange(n_pairs, dtype=jnp.int32)
    valid = pidx < cell_end[-1]
    cell = jnp.sum((cell_end[None, :] <= pidx[:, None]).astype(jnp.int32), axis=1)
    cell = jnp.minimum(cell, n_blk.shape[0] - 1)
    rb = blk_lo[cell] + pidx - cell_start[cell]
    return cell, rb, valid


def moe_layer(xb2, x2, w_router, wg, wu, wd, ln_g, ln_b):
    n = x2.shape[0]
    n_win = n // MOE_WIN
    n_rows = n * TOP_K + N_EXPERTS * MOE_TM
    n_blocks = n_rows // MOE_TM
    n_pairs = n_blocks + N_EXPERTS * n_win

    wr = jnp.pad(w_router, ((0, 0), (0, LANES - N_EXPERTS))).astype(BF16)
    top_e_t, top_w_t = moe_router(xb2, wr)
    top_e = top_e_t[:, :TOP_K]
    top_w = top_w_t[:, :TOP_K]

    experts = jnp.arange(N_EXPERTS, dtype=jnp.int32)
    tok_onehot = ((top_e[:, 0:1] == experts[None, :]) | (top_e[:, 1:2] == experts[None, :])).astype(jnp.int32)
    csum = jnp.cumsum(tok_onehot, axis=0)
    counts = csum[-1]
    rank = csum - tok_onehot
    padded = (counts + MOE_TM - 1) // MOE_TM * MOE_TM
    pad_end = jnp.cumsum(padded)
    pad_start = pad_end - padded
    dest = pad_start[top_e] + jnp.take_along_axis(rank, top_e, axis=1)

    blk_row0 = jnp.arange(n_blocks, dtype=jnp.int32) * MOE_TM
    block_e = jnp.minimum(jnp.sum((pad_end[None, :] <= blk_row0[:, None]).astype(jnp.int32), axis=1),
                          N_EXPERTS - 1).astype(jnp.int32)
    block_valid = (blk_row0 < pad_end[-1]).astype(jnp.int32)

    cell_cnt = tok_onehot.reshape(n_win, MOE_WIN, N_EXPERTS).sum(axis=1)
    cell_lo = pad_start[None, :] + jnp.cumsum(cell_cnt, axis=0) - cell_cnt
    blk_lo = cell_lo // MOE_TM
    n_blk = jnp.where(cell_cnt > 0, (cell_lo + cell_cnt - 1) // MOE_TM - blk_lo + 1, 0)

    cell, g_rb, g_valid = _pair_tables(n_blk.T.reshape(-1), blk_lo.T.reshape(-1), n_pairs)
    g_win = cell % n_win
    n_used = pad_end[-1] // MOE_TM
    spare_rb = n_used + jnp.arange(n_pairs, dtype=jnp.int32) - jnp.sum(n_blk)
    g_rb = jnp.where(g_valid, g_rb, jnp.minimum(spare_rb, n_blocks - 1)).astype(jnp.int32)
    g_win = jnp.where(g_valid, g_win, 0).astype(jnp.int32)
    prev_rb = jnp.concatenate([jnp.full((1,), -1, jnp.int32), g_rb[:-1]])
    g_flag = g_valid.astype(jnp.int32) + 2 * (g_rb != prev_rb).astype(jnp.int32)
    dest_rows = dest.reshape(n_win, MOE_WIN, TOP_K).transpose(0, 2, 1)
    xrows = moe_gather(g_rb, g_win, g_flag, dest_rows, xb2, n_rows)

    yrows = moe_expert_ffn(block_e, block_valid, xrows, wg, wu, wd)

    ytok = sc_gather_rows(yrows, dest.T.reshape(-1))
    return moe_finish(ytok, top_w, x2, ln_g, ln_b)


def _w_in_perm():
    zero = 3480
    perm = np.full((D_IN_P,), zero, np.int64)
    o_qkv, o_z, o_beta, o_decay, o_bq, o_bk, o_bv, o_cf, o_ci, o_cq, o_cg = (
        0, 1152, 1536, 1548, 1560, 1944, 2072, 2200, 2712, 2968, 3224)
    perm[COL_A_QKV:COL_A_QKV + 1152] = np.arange(o_qkv, o_qkv + 1152)
    perm[COL_A_Z:COL_A_Z + 384] = np.arange(o_z, o_z + 384)
    for j in range(3):
        for a in range(2):
            dst = COL_B_Q + j * LANES + a * HEAD_DIM
            src = o_bq + (j + 3 * a) * HEAD_DIM
            perm[dst:dst + HEAD_DIM] = np.arange(src, src + HEAD_DIM)
    perm[COL_B_K:COL_B_K + 128] = np.arange(o_bk, o_bk + 128)
    perm[COL_C_G:COL_C_G + 256] = np.arange(o_cg, o_cg + 256)
    perm[COL_B_V:COL_B_V + 128] = np.arange(o_bv, o_bv + 128)
    perm[COL_C_F:COL_C_F + 512] = np.arange(o_cf, o_cf + 512)
    perm[COL_C_I:COL_C_I + 256] = np.arange(o_ci, o_ci + 256)
    perm[COL_C_Q:COL_C_Q + 256] = np.arange(o_cq, o_cq + 256)
    for d, col in enumerate((COL_GATES0, COL_GATES1)):
        for p in range(A_HEADS // 2):
            dst = col + 4 * p
            perm[dst + 0] = o_beta + d * A_HEADS + 2 * p
            perm[dst + 1] = o_beta + d * A_HEADS + 2 * p + 1
            perm[dst + 2] = o_decay + d * A_HEADS + 2 * p
            perm[dst + 3] = o_decay + d * A_HEADS + 2 * p + 1
    return perm


def _w_out_perm():
    perm = np.arange(D_MIX)
    for j in range(3):
        for a in range(2):
            dst = A_WIDTH + j * LANES + a * HEAD_DIM
            src = A_WIDTH + (j + 3 * a) * HEAD_DIM
            perm[dst:dst + HEAD_DIM] = np.arange(src, src + HEAD_DIM)
    return perm


def _gdn_params(a_log, dt_bias):
    rows = np.array([4 * (h // 2) + 2 + h % 2 for h in range(A_HEADS)])
    prm = jnp.zeros((2, GATE_ROWS, LANES), F32)
    prm = prm.at[:, rows, 0].set(a_log.astype(F32))
    prm = prm.at[:, rows, 1].set(dt_bias.astype(F32))
    return prm.at[:, rows, 2].set(1.0)


def kernel(x, mem, w_in, conv_w, gdn_a_log, gdn_dt_bias, gdn_norm_w, q_norm_w, k_norm_w, hgrn_lb_logits,
           hgrn_norm_w, w_out, ln1_g, ln1_b, xq, xk, xv, xo, ln2_g, ln2_b, ffn_wg, ffn_wu, ffn_wd,
           moe_router, moe_wg, moe_wu, moe_wd, ln3_g, ln3_b):
    b, s, d = x.shape
    m = b * s
    cos_t, sin_t = rope_tables(s)
    lb_p = jax.nn.softmax(hgrn_lb_logits.astype(F32), axis=0)
    lb_c = jnp.cumsum(lb_p, axis=0)
    lower_bounds = lb_c - lb_c[0]
    in_perm = _w_in_perm()
    out_perm = _w_out_perm()
    mem_b = mem.astype(BF16).reshape(b * mem.shape[1], d)
    tile2 = lambda w: jnp.tile(w.astype(F32), 2)[None, :]
    row = lambda v: v.astype(F32)[None, :]

    x2 = x.reshape(m, d)
    xb2 = x2.astype(BF16)
    for l in range(DEPTH):
        w_in_p = jnp.take(jnp.pad(w_in[l], ((0, 0), (0, 1))), in_perm, axis=1).astype(BF16)
        wide2 = matmul(xb2, w_in_p[:, :D_WIDE], BF16, min(1024, m), 768)
        gates2 = matmul(xb2, w_in_p[:, D_WIDE:], F32, min(1024, m), D_IN_P - D_WIDE)
        wide3 = wide2.reshape(b, s, D_WIDE)
        gates3 = gates2.reshape(b, s, D_IN_P - D_WIDE)

        oa_f, oa_b = gdn_scan(wide3, gates3, conv_w[l], _gdn_params(gdn_a_log[l], gdn_dt_bias[l]))
        ob = gqa_attention(wide3, cos_t, sin_t, tile2(q_norm_w[l]), tile2(k_norm_w[l]), s)
        lb = lower_bounds[l][None, :]
        oc_f, oc_b = hgrn_scan(wide3, gates3, lb)

        w_out_p = jnp.take(w_out[l], out_perm, axis=0).astype(BF16)
        x2, xb2 = mixer_output(oa_f.reshape(m, A_WIDTH), oa_b.reshape(m, A_WIDTH), wide2,
                               ob.reshape(m, B_WIDTH), oc_f.reshape(m, C_WIDTH), oc_b.reshape(m, C_WIDTH),
                               x2, tile2(gdn_norm_w[l]), tile2(hgrn_norm_w[l]), w_out_p,
                               row(ln1_g[l]), row(ln1_b[l]))

        k3 = matmul(mem_b, xk[l].astype(BF16), BF16, 256, 512).reshape(b, -1, d)
        v3 = matmul(mem_b, xv[l].astype(BF16), BF16, 256, 512).reshape(b, -1, d)
        x3, xb3 = cross_attention(xb2.reshape(b, s, d), x2.reshape(b, s, d), k3, v3,
                                  xq[l].astype(BF16), xo[l].astype(BF16), row(ln2_g[l]), row(ln2_b[l]))
        x2, xb2 = x3.reshape(m, d), xb3.reshape(m, d)

        if l % 2 == 0:
            x2, xb2 = dense_ffn(xb2, x2, ffn_wg[l // 2].astype(BF16), ffn_wu[l // 2].astype(BF16),
                                ffn_wd[l // 2].astype(BF16), row(ln3_g[l]), row(ln3_b[l]))
        else:
            x2 = moe_layer(xb2, x2, moe_router[l // 2], moe_wg[l // 2].astype(BF16),
                           moe_wu[l // 2].astype(BF16), moe_wd[l // 2].astype(BF16),
                           row(ln3_g[l]), row(ln3_b[l]))
            xb2 = x2.astype(BF16)
    return x2.reshape(b, s, d)
```

```python
import math

import numpy as np
import jax
import jax.numpy as jnp
from jax import lax
from jax.experimental import pallas as pl
from jax.experimental.pallas import tpu as pltpu
from jax.experimental.pallas import tpu_sc as plsc

F32 = jnp.float32
BF16 = jnp.bfloat16

D_MODEL = 1024
DEPTH = 2
HEAD_DIM = 64
A_HEADS = 6
A_WIDTH = A_HEADS * HEAD_DIM
B_Q_HEADS = 6
B_KV_HEADS = 2
B_WIDTH = B_Q_HEADS * HEAD_DIM
C_HEADS = 4
C_WIDTH = C_HEADS * HEAD_DIM
D_MIX = A_WIDTH + B_WIDTH + C_WIDTH
CONV_K = 5
CHUNK = 64
GRID_W = 64
ROPE_AXIS_DIM = HEAD_DIM // 2
ROPE_THETA = 10000.0
X_HEADS = 4
X_HEAD_DIM = D_MODEL // X_HEADS
D_FF = 2816
N_EXPERTS = 8
TOP_K = 2
D_FF_EXPERT = 3584
DN_ALPHA = (2 * DEPTH) ** 0.25
LN_EPS = 1e-5
RMS_EPS = 1e-6

LANES = 128
SUBLANES = 8
HALO = 16
GROUP = 256
CHUNKS_PER_GROUP = GROUP // CHUNK
VMEM_LIMIT = 56 * 1024 * 1024

COL_A_QKV = 0
COL_A_Z = 1152
COL_B_Q = 1536
COL_B_K = 1920
COL_C_G = 2048
COL_B_V = 2304
COL_C_I = 2560
COL_C_Q = 2816
D_WIDE = 3072
COL_C_F = D_WIDE
COL_GATES0 = D_WIDE + 512
COL_GATES1 = D_WIDE + 640
D_IN_P = 3840


def _cparams(sem):
    return pltpu.CompilerParams(dimension_semantics=sem, vmem_limit_bytes=VMEM_LIMIT)


def _sigmoid(x):
    return 1.0 / (1.0 + jnp.exp(-x))


def _softplus(x):
    return jnp.maximum(x, 0.0) + jnp.log1p(jnp.exp(-jnp.abs(x)))


def _dot(a, b):
    return jnp.dot(a.astype(BF16), b.astype(BF16), preferred_element_type=F32)


def _dot_nt(a, b):
    return lax.dot_general(a.astype(BF16), b.astype(BF16), (((1,), (1,)), ((), ())),
                           preferred_element_type=F32)


def _dot_sel(m01, x):
    m = jnp.where(m01, 1.0, 0.0).astype(BF16)
    hi = x.astype(BF16)
    r1 = x - hi.astype(F32)
    mid = r1.astype(BF16)
    lo = (r1 - mid.astype(F32)).astype(BF16)
    out = jnp.dot(m, hi, preferred_element_type=F32)
    out = out + jnp.dot(m, mid, preferred_element_type=F32)
    return out + jnp.dot(m, lo, preferred_element_type=F32)


def _dot_sel_right(x, m01):
    m = jnp.where(m01, 1.0, 0.0).astype(BF16)
    hi = x.astype(BF16)
    r1 = x - hi.astype(F32)
    mid = r1.astype(BF16)
    lo = (r1 - mid.astype(F32)).astype(BF16)
    out = jnp.dot(hi, m, preferred_element_type=F32)
    out = out + jnp.dot(mid, m, preferred_element_type=F32)
    return out + jnp.dot(lo, m, preferred_element_type=F32)


def _layer_norm_rows(y, g, b):
    mu = jnp.mean(y, axis=-1, keepdims=True)
    yc = y - mu
    var = jnp.mean(yc * yc, axis=-1, keepdims=True)
    return yc * lax.rsqrt(var + LN_EPS) * g + b


def _iota2(shape, dim):
    return lax.broadcasted_iota(jnp.int32, shape, dim)


def _head_sum(x, first_head):
    s0 = jnp.sum(jnp.where(first_head, x, 0.0), axis=-1, keepdims=True)
    s1 = jnp.sum(jnp.where(first_head, 0.0, x), axis=-1, keepdims=True)
    return jnp.where(first_head, s0, s1)


def _block_rows(t, block, row_in_block):
    parts = [jnp.broadcast_to(t[b * block + row_in_block:b * block + row_in_block + 1, :], (block, t.shape[1]))
             for b in range(GROUP // block)]
    return jnp.concatenate(parts, axis=0)


def _mm_kernel(x_ref, w_ref, o_ref):
    o_ref[...] = jnp.dot(x_ref[...], w_ref[...], preferred_element_type=F32).astype(o_ref.dtype)


def matmul(x, w, out_dtype, tm, tn):
    m, k = x.shape
    n = w.shape[1]
    return pl.pallas_call(
        _mm_kernel,
        out_shape=jax.ShapeDtypeStruct((m, n), out_dtype),
        grid=(n // tn, m // tm),
        in_specs=[pl.BlockSpec((tm, k), lambda j, i: (i, 0)),
                  pl.BlockSpec((k, tn), lambda j, i: (0, j))],
        out_specs=pl.BlockSpec((tm, tn), lambda j, i: (i, j)),
        compiler_params=_cparams(("parallel", "arbitrary")),
        name="matmul",
    )(x, w)


def _scan_masks(reverse):
    ri = _iota2((GROUP, GROUP), 0)
    ci = _iota2((GROUP, GROUP), 1)
    same_chunk = (ri >> 6) == (ci >> 6)
    if reverse:
        return ri, ci, same_chunk, same_chunk & (ri <= ci), same_chunk & (ri < ci)
    return ri, ci, same_chunk, same_chunk & (ri >= ci), same_chunk & (ri > ci)


GATE_ROWS = 16


class _GdnProblem:
    def __init__(self, **kw):
        self.__dict__.update(kw)


def _gdn_group(probs, same_chunk, h0):
    nh0 = jnp.logical_not(h0)
    sr = _iota2((CHUNK, GROUP), 0)
    sc = _iota2((CHUNK, GROUP), 1)
    eye_side = jnp.where(sr == (sc & (CHUNK - 1)), 1.0, 0.0)
    tl = _iota2((LANES, GROUP), 1)
    br = _iota2((LANES, LANES), 0)
    bc = _iota2((LANES, LANES), 1)
    bd128 = (br >> 6) == (bc >> 6)

    def block_diag(side):
        return jnp.where(same_chunk, jnp.concatenate([side] * CHUNKS_PER_GROUP, axis=0), 0.0).astype(BF16)

    for pr in probs:
        pr.q = pr.q * lax.rsqrt(_head_sum(pr.q * pr.q, h0) + RMS_EPS) * (HEAD_DIM ** -0.5)
        pr.k = pr.k * lax.rsqrt(_head_sum(pr.k * pr.k, h0) + RMS_EPS)
        pr.gedge = _block_rows(pr.gc, CHUNK, 0 if pr.reverse else CHUNK - 1)
        pr.exp_g = jnp.exp(pr.gc)
        pr.kb = pr.k * pr.beta_t
        pr.rhs = jnp.concatenate([pr.v * pr.beta_t, pr.kb * pr.exp_g], axis=1).astype(BF16)
        pr.k16 = pr.k.astype(BF16)
        pr.decay, pr.x, pr.pw, pr.bd = [], [], [], []
        for h in range(2):
            decay = jnp.exp(jnp.where(pr.incl, pr.g_col[h] - pr.g_row[h], -1e30))
            kk = _dot_nt(jnp.where(h0 if h == 0 else nh0, pr.kb, 0.0), pr.k16)
            low = jnp.where(pr.strict, kk * decay, 0.0)
            a = -(low[0:CHUNK] + low[CHUNK:2 * CHUNK] + low[2 * CHUNK:3 * CHUNK] + low[3 * CHUNK:4 * CHUNK])
            pr.decay.append(decay)
            pr.x.append(eye_side + a)
            pr.pw.append(a)
            pr.bd.append(block_diag(a))

    for _ in range(5):
        for pr in probs:
            for h in range(2):
                pr.pw[h] = jnp.dot(pr.pw[h].astype(BF16), pr.bd[h], preferred_element_type=F32)
        for pr in probs:
            for h in range(2):
                pr.bd[h] = block_diag(pr.pw[h])
        for pr in probs:
            for h in range(2):
                pr.x[h] = pr.x[h] + jnp.dot(pr.x[h].astype(BF16), pr.bd[h], preferred_element_type=F32)

    for pr in probs:
        sols = [jnp.dot(block_diag(pr.x[h]), pr.rhs, preferred_element_type=F32) for h in range(2)]
        u = jnp.where(h0, sols[0][:, :LANES], sols[1][:, :LANES])
        w = jnp.where(h0, sols[0][:, LANES:], sols[1][:, LANES:])
        wu = jnp.concatenate([w, u], axis=1)
        wu16 = wu.astype(BF16)
        aw = []
        for h in range(2):
            attn = jnp.where(pr.incl, _dot_nt(jnp.where(h0 if h == 0 else nh0, pr.q, 0.0), pr.k16) * pr.decay[h], 0.0)
            aw.append(jnp.dot(attn.astype(BF16), wu16, preferred_element_type=F32))
        pr.qp = (pr.q * pr.exp_g - jnp.where(h0, aw[0][:, :LANES], aw[1][:, :LANES])).astype(BF16)
        pr.op = jnp.where(h0, aw[0][:, LANES:], aw[1][:, LANES:])
        kd_t = (pr.k * jnp.exp(pr.gedge - pr.gc)).T
        pr.kmat, pr.nmat = [], []
        for c in range(CHUNKS_PER_GROUP):
            km = jnp.dot(jnp.where((tl >> 6) == c, kd_t, 0.0), wu, preferred_element_type=F32)
            pr.kmat.append(jnp.where(bd128, km[:, :LANES], 0.0).astype(BF16))
            pr.nmat.append(jnp.where(bd128, km[:, LANES:], 0.0))
        pr.out = [None] * CHUNKS_PER_GROUP

    for t in range(CHUNKS_PER_GROUP):
        for pr in probs:
            c = CHUNKS_PER_GROUP - 1 - t if pr.reverse else t
            rs = slice(c * CHUNK, (c + 1) * CHUNK)
            s16 = pr.state.astype(BF16)
            pr.out[c] = jnp.dot(pr.qp[rs], s16, preferred_element_type=F32) + pr.op[rs]
            cd = jnp.exp(pr.gedge[c * CHUNK:c * CHUNK + 1, :])
            pr.state = pr.state * cd - jnp.dot(pr.kmat[c], s16, preferred_element_type=F32) + pr.nmat[c]
    return [jnp.concatenate(pr.out, axis=0) for pr in probs]


def _gdn_kernel(xf_ref, xfp_ref, xfn_ref, xb_ref, xbp_ref, xbn_ref, cw_ref, gf_ref, gb_ref, prm_ref,
                of_ref, ob_ref, s_scr):
    n = pl.program_id(1)
    n_groups = pl.num_programs(1)

    @pl.when(n == 0)
    def _():
        s_scr[...] = jnp.zeros_like(s_scr)

    lane = _iota2((GROUP, LANES), 1)
    h0 = lane < HEAD_DIM
    ri = _iota2((GROUP, GROUP), 0)
    ci = _iota2((GROUP, GROUP), 1)
    same_chunk = (ri >> 6) == (ci >> 6)
    cw = cw_ref[...]
    base = SUBLANES - CONV_K // 2
    pairs = A_HEADS // 2

    probs = []
    for d, (gi, x_ref, xp_ref, xn_ref, g_ref) in enumerate(((n, xf_ref, xfp_ref, xfn_ref, gf_ref),
                                                            (n_groups - 1 - n, xb_ref, xbp_ref, xbn_ref, gb_ref))):
        reverse = d == 1
        lower, upper = same_chunk & (ri >= ci), same_chunk & (ri <= ci)
        incl = upper if reverse else lower
        strict = same_chunk & ((ri < ci) if reverse else (ri > ci))

        prev = jnp.where(gi > 0, xp_ref[0].astype(F32)[HALO - SUBLANES:], 0.0)
        nxt = jnp.where(gi < n_groups - 1, xn_ref[0].astype(F32)[:SUBLANES], 0.0)
        ext = jnp.concatenate([prev, x_ref[0].astype(F32), nxt], axis=0)
        acc = None
        for t in range(CONV_K):
            shift = (CONV_K // 2 - t) % ext.shape[0]
            tap = (pltpu.roll(ext, shift, 0) if shift else ext)[SUBLANES:SUBLANES + GROUP, :] * cw[t:t + 1, :]
            acc = tap if acc is None else acc + tap
        qkv = acc * _sigmoid(acc)

        gt_t = g_ref[0].T[0:GATE_ROWS, :]
        prm = prm_ref[d]
        e_rows = jnp.where(prm[:, 2:3] > 0.5,
                           -jnp.exp(prm[:, 0:1]) * _softplus(gt_t + prm[:, 1:2]), _sigmoid(gt_t))
        g_rows = _dot_sel_right(e_rows, lower if reverse else upper)
        slab = jnp.concatenate([e_rows, g_rows, jnp.zeros((LANES - 2 * GATE_ROWS, GROUP), F32)], axis=0)
        cols = slab.T
        for p in range(pairs):
            sl = lambda part: slice(part * A_WIDTH + p * LANES, part * A_WIDTH + (p + 1) * LANES)
            col = lambda r: cols[:, r:r + 1]
            probs.append(_GdnProblem(
                reverse=reverse, incl=incl, strict=strict, slot=d * pairs + p,
                q=qkv[:, sl(0)], k=qkv[:, sl(1)], v=qkv[:, sl(2)],
                beta_t=jnp.where(h0, col(4 * p), col(4 * p + 1)),
                gc=jnp.where(h0, col(GATE_ROWS + 4 * p + 2), col(GATE_ROWS + 4 * p + 3)),
                g_col=(col(GATE_ROWS + 4 * p + 2), col(GATE_ROWS + 4 * p + 3)),
                g_row=(g_rows[4 * p + 2:4 * p + 3, :], g_rows[4 * p + 3:4 * p + 4, :]),
                state=s_scr[d * pairs + p]))

    outs = _gdn_group(probs, same_chunk, h0)
    for pr, out in zip(probs, outs):
        o_ref = ob_ref if pr.reverse else of_ref
        p = pr.slot % pairs
        o_ref[0, :, p * LANES:(p + 1) * LANES] = out.astype(BF16)
        s_scr[pr.slot] = pr.state


def gdn_scan(wide3, gates3, conv_w, prm):
    b, seq_len, _ = wide3.shape
    n_groups = seq_len // GROUP
    halo_per_group = GROUP // HALO
    n_halo = seq_len // HALO
    width = 3 * A_WIDTH
    fwd = lambda n: n
    bwd = lambda n: n_groups - 1 - n

    def x_specs(gidx):
        return [pl.BlockSpec((1, GROUP, width), lambda i, n: (i, gidx(n), 0)),
                pl.BlockSpec((1, HALO, width),
                             lambda i, n: (i, jnp.maximum(gidx(n) * halo_per_group - 1, 0), 0)),
                pl.BlockSpec((1, HALO, width),
                             lambda i, n: (i, jnp.minimum((gidx(n) + 1) * halo_per_group, n_halo - 1), 0))]

    out_sds = jax.ShapeDtypeStruct((b, seq_len, A_WIDTH), BF16)
    return pl.pallas_call(
        _gdn_kernel,
        out_shape=(out_sds, out_sds),
        grid=(b, n_groups),
        in_specs=x_specs(fwd) + x_specs(bwd) + [
            pl.BlockSpec((CONV_K, width), lambda i, n: (0, 0)),
            pl.BlockSpec((1, GROUP, LANES), lambda i, n: (i, fwd(n), (COL_GATES0 - D_WIDE) // LANES)),
            pl.BlockSpec((1, GROUP, LANES), lambda i, n: (i, bwd(n), (COL_GATES1 - D_WIDE) // LANES)),
            pl.BlockSpec((2, GATE_ROWS, LANES), lambda i, n: (0, 0, 0))],
        out_specs=(pl.BlockSpec((1, GROUP, A_WIDTH), lambda i, n: (i, fwd(n), 0)),
                   pl.BlockSpec((1, GROUP, A_WIDTH), lambda i, n: (i, bwd(n), 0))),
        scratch_shapes=[pltpu.VMEM((A_HEADS, LANES, LANES), F32)],
        compiler_params=_cparams(("parallel", "arbitrary")),
        name="gdn_scan",
    )(wide3, wide3, wide3, wide3, wide3, wide3, conv_w, gates3, gates3, prm)


EXP_CLAMP = 60.0


class _HgrnProblem:
    def __init__(self, **kw):
        self.__dict__.update(kw)


def _hgrn_group(probs):
    lane = _iota2((GROUP, LANES), 1)
    h0 = lane < HEAD_DIM
    nh0 = jnp.logical_not(h0)
    tl = _iota2((LANES, GROUP), 1)
    br = _iota2((LANES, LANES), 0)
    bc = _iota2((LANES, LANES), 1)
    bd128 = (br >> 6) == (bc >> 6)

    for pr in probs:
        ri, ci, same_chunk, incl, _ = pr.masks
        f = pr.lb + (1.0 - pr.lb) * _sigmoid(pr.f_raw)
        kk = 1.0 - f
        q = pr.q_raw * _sigmoid(pr.q_raw) * (HEAD_DIM ** -0.5)
        cum = _dot_sel(incl, jnp.log(f))
        pr.cedge = _block_rows(cum, CHUNK, 0 if pr.reverse else CHUNK - 1)
        pr.qd = q * jnp.exp(cum)
        pr.kd = kk * jnp.exp(pr.cedge - cum)
        if pr.reverse:
            m32 = same_chunk & ((ri & 63) < 32) & ((ci & 63) >= 32)
            m16 = ((ri >> 5) == (ci >> 5)) & ((ri & 31) < 16) & ((ci & 31) >= 16)
            mdg = ((ri >> 4) == (ci >> 4)) & (ri <= ci)
            ref32, ref16, refdg = _block_rows(cum, 64, 32), _block_rows(cum, 32, 16), _block_rows(cum, 16, 8)
        else:
            m32 = same_chunk & ((ri & 63) >= 32) & ((ci & 63) < 32)
            m16 = ((ri >> 5) == (ci >> 5)) & ((ri & 31) >= 16) & ((ci & 31) < 16)
            mdg = ((ri >> 4) == (ci >> 4)) & (ri >= ci)
            ref32, ref16, refdg = _block_rows(cum, 64, 31), _block_rows(cum, 32, 15), _block_rows(cum, 16, 7)
        pr.level_masks = (m32, m16, mdg)
        pr.qs = (q * jnp.exp(jnp.minimum(cum - ref32, 0.0)),
                 q * jnp.exp(jnp.minimum(cum - ref16, 0.0)),
                 q * jnp.exp(jnp.clip(cum - refdg, -EXP_CLAMP, EXP_CLAMP)))
        pr.ks = ((kk * jnp.exp(jnp.minimum(ref32 - cum, 0.0))).astype(BF16),
                 (kk * jnp.exp(jnp.minimum(ref16 - cum, 0.0))).astype(BF16),
                 (kk * jnp.exp(jnp.clip(refdg - cum, -EXP_CLAMP, EXP_CLAMP))).astype(BF16))

    for pr in probs:
        v16 = pr.v.astype(BF16)
        m32, m16, mdg = pr.level_masks
        intra = []
        for h in range(2):
            mh = h0 if h == 0 else nh0
            a = jnp.where(m32, _dot_nt(jnp.where(mh, pr.qs[0], 0.0), pr.ks[0]),
                          jnp.where(m16, _dot_nt(jnp.where(mh, pr.qs[1], 0.0), pr.ks[1]),
                                    jnp.where(mdg, _dot_nt(jnp.where(mh, pr.qs[2], 0.0), pr.ks[2]), 0.0)))
            intra.append(jnp.dot(a.astype(BF16), v16, preferred_element_type=F32))
        pr.intra = jnp.where(h0, intra[0], intra[1])
        v_t = pr.v.T
        pr.upd = [jnp.where(bd128, jnp.dot(jnp.where((tl >> 6) == c, v_t, 0.0), pr.kd,
                                           preferred_element_type=F32), 0.0)
                  for c in range(CHUNKS_PER_GROUP)]
        pr.out = [None] * CHUNKS_PER_GROUP

    for t in range(CHUNKS_PER_GROUP):
        for pr in probs:
            c = CHUNKS_PER_GROUP - 1 - t if pr.reverse else t
            rs = slice(c * CHUNK, (c + 1) * CHUNK)
            pr.out[c] = _dot_nt(pr.qd[rs], pr.state_t) + pr.intra[rs]
            cd = jnp.exp(pr.cedge[c * CHUNK:c * CHUNK + 1, :])
            pr.state_t = pr.state_t * cd + pr.upd[c]
    return [jnp.concatenate(pr.out, axis=0) for pr in probs]


def _hgrn_kernel(ff_ref, fb_ref, if_ref, ib_ref, qf_ref, qb_ref, lb_ref, of_ref, ob_ref, s_scr):
    n = pl.program_id(1)

    @pl.when(n == 0)
    def _():
        s_scr[...] = jnp.zeros_like(s_scr)

    pairs = C_HEADS // 2
    probs = []
    for d, (f_ref, i_ref, q_ref) in enumerate(((ff_ref, if_ref, qf_ref), (fb_ref, ib_ref, qb_ref))):
        masks = _scan_masks(d == 1)
        for p in range(pairs):
            sl = slice(p * LANES, (p + 1) * LANES)
            probs.append(_HgrnProblem(reverse=d == 1, masks=masks, slot=d * pairs + p, lb=lb_ref[:, sl],
                                      f_raw=f_ref[0, :, sl], v=i_ref[0, :, sl].astype(F32),
                                      q_raw=q_ref[0, :, sl].astype(F32), state_t=s_scr[d * pairs + p]))
    outs = _hgrn_group(probs)
    for pr, out in zip(probs, outs):
        o_ref = ob_ref if pr.reverse else of_ref
        p = pr.slot % pairs
        o_ref[0, :, p * LANES:(p + 1) * LANES] = out.astype(BF16)
        s_scr[pr.slot] = pr.state_t


def hgrn_scan(wide3, gates3, lb):
    b, seq_len, _ = wide3.shape
    n_groups = seq_len // GROUP
    fwd = lambda n: n
    bwd = lambda n: n_groups - 1 - n
    spec = lambda gidx, blk: pl.BlockSpec((1, GROUP, C_WIDTH), lambda i, n: (i, gidx(n), blk))
    f_blk = (COL_C_F - D_WIDE) // C_WIDTH
    out_sds = jax.ShapeDtypeStruct((b, seq_len, C_WIDTH), BF16)
    return pl.pallas_call(
        _hgrn_kernel,
        out_shape=(out_sds, out_sds),
        grid=(b, n_groups),
        in_specs=[spec(fwd, f_blk), spec(bwd, f_blk + 1),
                  spec(fwd, COL_C_I // C_WIDTH), spec(bwd, COL_C_I // C_WIDTH),
                  spec(fwd, COL_C_Q // C_WIDTH), spec(bwd, COL_C_Q // C_WIDTH),
                  pl.BlockSpec((1, C_WIDTH), lambda i, n: (0, 0))],
        out_specs=(spec(fwd, 0), spec(bwd, 0)),
        scratch_shapes=[pltpu.VMEM((C_HEADS, LANES, LANES), F32)],
        compiler_params=_cparams(("parallel", "arbitrary")),
        name="hgrn_scan",
    )(gates3, gates3, wide3, wide3, wide3, wide3, lb)


ATT_TQ = 512


def _norm_rope(x, w, cos, sin_signed, first_half16):
    lane = _iota2(x.shape, 1)
    h0 = lane < HEAD_DIM
    ms = _head_sum(x * x, h0) * (1.0 / HEAD_DIM)
    xn = x * lax.rsqrt(ms + RMS_EPS) * w
    partner = jnp.where(first_half16, pltpu.roll(xn, LANES - 16, 1), pltpu.roll(xn, 16, 1))
    return xn * cos + partner * sin_signed


def _attn_kernel(q_ref, k_ref, v_ref, cosq_ref, sinq_ref, cosk_ref, sink_ref, qw_ref, kw_ref, o_ref,
                 k_scr, v_scr):
    qi = pl.program_id(1)

    @pl.when(qi == 0)
    def _():
        kx = k_ref[0].astype(F32)
        lane_k = _iota2(kx.shape, 1)
        k_scr[...] = _norm_rope(kx, kw_ref[...], cosk_ref[...], sink_ref[...], (lane_k & 31) < 16).astype(BF16)
        v_scr[:, :LANES] = v_ref[0]
        v_scr[:, LANES:] = jnp.ones((v_scr.shape[0], LANES), BF16)

    lane = _iota2((ATT_TQ, LANES), 1)
    h0 = lane < HEAD_DIM
    fh = (lane & 31) < 16
    cos = cosq_ref[...]
    sin = sinq_ref[...]
    kmat = k_scr[...]
    vmat = v_scr[...]
    q_scale = (HEAD_DIM ** -0.5) * math.log2(math.e)
    for j in range(B_Q_HEADS // 2):
        qn = _norm_rope(q_ref[0, :, j * LANES:(j + 1) * LANES].astype(F32), qw_ref[...], cos, sin, fh) * q_scale
        outs = []
        for half in range(2):
            mh = h0 if half == 0 else jnp.logical_not(h0)
            s = _dot_nt(jnp.where(mh, qn, 0.0), kmat)
            m = jnp.max(s, axis=-1, keepdims=True)
            e = jnp.exp2((s - m).astype(BF16))
            pvl = jnp.dot(e, vmat, preferred_element_type=F32)
            outs.append(pvl[:, :LANES] / pvl[:, LANES:LANES + 1])
        o_ref[0, :, j * LANES:(j + 1) * LANES] = jnp.where(h0, outs[0], outs[1]).astype(BF16)


def gqa_attention(proj, cos_t, sin_t, q_w, k_w, seq_len):
    b = proj.shape[0]
    return pl.pallas_call(
        _attn_kernel,
        out_shape=jax.ShapeDtypeStruct((b, seq_len, B_WIDTH), BF16),
        grid=(b, seq_len // ATT_TQ),
        in_specs=[pl.BlockSpec((1, ATT_TQ, B_WIDTH), lambda i, t: (i, t, COL_B_Q // B_WIDTH)),
                  pl.BlockSpec((1, seq_len, LANES), lambda i, t: (i, 0, COL_B_K // LANES)),
                  pl.BlockSpec((1, seq_len, LANES), lambda i, t: (i, 0, COL_B_V // LANES)),
                  pl.BlockSpec((ATT_TQ, LANES), lambda i, t: (t, 0)),
                  pl.BlockSpec((ATT_TQ, LANES), lambda i, t: (t, 0)),
                  pl.BlockSpec((seq_len, LANES), lambda i, t: (0, 0)),
                  pl.BlockSpec((seq_len, LANES), lambda i, t: (0, 0)),
                  pl.BlockSpec((1, LANES), lambda i, t: (0, 0)),
                  pl.BlockSpec((1, LANES), lambda i, t: (0, 0))],
        out_specs=pl.BlockSpec((1, ATT_TQ, B_WIDTH), lambda i, t: (i, t, 0)),
        scratch_shapes=[pltpu.VMEM((seq_len, LANES), BF16), pltpu.VMEM((seq_len, 2 * LANES), BF16)],
        compiler_params=_cparams(("parallel", "arbitrary")),
        name="gqa_attention",
    )(proj, proj, proj, cos_t, sin_t, cos_t, sin_t, q_w, k_w)


def rope_tables(seq_len):
    rows = seq_len // GRID_W
    row = jnp.repeat(jnp.arange(rows, dtype=F32), GRID_W)
    col = jnp.tile(jnp.arange(GRID_W, dtype=F32), rows)
    inv_freq = ROPE_THETA ** (-jnp.arange(0, ROPE_AXIS_DIM, 2, dtype=F32) / ROPE_AXIS_DIM)
    ang_r = row[:, None] * inv_freq
    ang_c = col[:, None] * inv_freq
    cos64 = jnp.concatenate([jnp.cos(ang_r), jnp.cos(ang_r), jnp.cos(ang_c), jnp.cos(ang_c)], axis=1)
    sin64 = jnp.concatenate([-jnp.sin(ang_r), jnp.sin(ang_r), -jnp.sin(ang_c), jnp.sin(ang_c)], axis=1)
    return jnp.tile(cos64, (1, 2)), jnp.tile(sin64, (1, 2))


MIX_TM = 256


def _mixout_kernel(oaf_ref, oab_ref, z_ref, ob_ref, ocf_ref, ocb_ref, g_ref, x_ref, wa_ref, wc_ref,
                   wo_ref, lg_ref, lb_ref, xo_ref, xb_ref):
    lane = _iota2((MIX_TM, LANES), 1)
    h0 = lane < HEAD_DIM
    parts = []
    for j in range(A_WIDTH // LANES):
        sl = slice(j * LANES, (j + 1) * LANES)
        o = oaf_ref[:, sl].astype(F32) + oab_ref[:, sl].astype(F32)
        ms = _head_sum(o * o, h0) * (1.0 / HEAD_DIM)
        z = z_ref[:, sl].astype(F32)
        parts.append((o * lax.rsqrt(ms + RMS_EPS) * wa_ref[...] * (z * _sigmoid(z))).astype(BF16))
    parts.append(ob_ref[...])
    for j in range(C_WIDTH // LANES):
        sl = slice(j * LANES, (j + 1) * LANES)
        o = ocf_ref[:, sl].astype(F32) + ocb_ref[:, sl].astype(F32)
        ms = _head_sum(o * o, h0) * (1.0 / HEAD_DIM)
        parts.append((o * lax.rsqrt(ms + RMS_EPS) * wc_ref[...]
                      * _sigmoid(g_ref[:, sl].astype(F32))).astype(BF16))
    mixed = jnp.concatenate(parts, axis=1)
    h = jnp.dot(mixed, wo_ref[...], preferred_element_type=F32)
    y = _layer_norm_rows(DN_ALPHA * x_ref[...] + h, lg_ref[...], lb_ref[...])
    xo_ref[...] = y
    xb_ref[...] = y.astype(BF16)


def mixer_output(oa_f, oa_b, proj2, ob, oc_f, oc_b, x2, gdn_w, hgrn_w, w_out, ln_g, ln_b):
    m = x2.shape[0]
    row = lambda w: pl.BlockSpec((MIX_TM, w), lambda i: (i, 0))
    const = lambda r, c: pl.BlockSpec((r, c), lambda i: (0, 0))
    return pl.pallas_call(
        _mixout_kernel,
        out_shape=(jax.ShapeDtypeStruct((m, D_MODEL), F32), jax.ShapeDtypeStruct((m, D_MODEL), BF16)),
        grid=(m // MIX_TM,),
        in_specs=[row(A_WIDTH), row(A_WIDTH),
                  pl.BlockSpec((MIX_TM, A_WIDTH), lambda i: (i, COL_A_Z // A_WIDTH)),
                  row(B_WIDTH), row(C_WIDTH), row(C_WIDTH),
                  pl.BlockSpec((MIX_TM, C_WIDTH), lambda i: (i, COL_C_G // C_WIDTH)),
                  row(D_MODEL), const(1, LANES), const(1, LANES),
                  const(D_MIX, D_MODEL), const(1, D_MODEL), const(1, D_MODEL)],
        out_specs=(row(D_MODEL), row(D_MODEL)),
        compiler_params=_cparams(("parallel",)),
        name="mixer_output",
    )(oa_f, oa_b, proj2, ob, oc_f, oc_b, proj2, x2, gdn_w, hgrn_w, w_out, ln_g, ln_b)


XATT_TM = 512


def _xattn_kernel(xb_ref, x_ref, k_ref, v_ref, wq_ref, wo_ref, lg_ref, lb_ref, xo_ref, xob_ref):
    q = jnp.dot(xb_ref[0], wq_ref[...], preferred_element_type=F32) * (X_HEAD_DIM ** -0.5)
    outs = []
    for h in range(X_HEADS):
        sl = slice(h * X_HEAD_DIM, (h + 1) * X_HEAD_DIM)
        s = _dot_nt(q[:, sl], k_ref[0, :, sl])
        m = jnp.max(s, axis=-1, keepdims=True)
        e = jnp.exp(s - m)
        l = jnp.sum(e, axis=-1, keepdims=True)
        outs.append((jnp.dot(e.astype(BF16), v_ref[0, :, sl], preferred_element_type=F32) / l).astype(BF16))
    o = jnp.concatenate(outs, axis=1)
    c = jnp.dot(o, wo_ref[...], preferred_element_type=F32)
    y = _layer_norm_rows(DN_ALPHA * x_ref[0] + c, lg_ref[...], lb_ref[...])
    xo_ref[0] = y
    xob_ref[0] = y.astype(BF16)


def cross_attention(xb3, x3, k3, v3, wq, wo, ln_g, ln_b):
    b, s, _ = x3.shape
    mem = k3.shape[1]
    row = pl.BlockSpec((1, XATT_TM, D_MODEL), lambda i, t: (i, t, 0))
    const = lambda r, c: pl.BlockSpec((r, c), lambda i, t: (0, 0))
    kv = pl.BlockSpec((1, mem, D_MODEL), lambda i, t: (i, 0, 0))
    return pl.pallas_call(
        _xattn_kernel,
        out_shape=(jax.ShapeDtypeStruct((b, s, D_MODEL), F32), jax.ShapeDtypeStruct((b, s, D_MODEL), BF16)),
        grid=(b, s // XATT_TM),
        in_specs=[row, row, kv, kv, const(D_MODEL, D_MODEL), const(D_MODEL, D_MODEL),
                  const(1, D_MODEL), const(1, D_MODEL)],
        out_specs=(row, row),
        compiler_params=_cparams(("parallel", "parallel")),
        name="cross_attention",
    )(xb3, x3, k3, v3, wq, wo, ln_g, ln_b)


FFN_TM = 512
FFN_TF = 1408


def _ffn_kernel(xb_ref, x_ref, wg_ref, wu_ref, wd_ref, lg_ref, lb_ref, xo_ref, xob_ref):
    xb = xb_ref[...]
    acc = None
    for j in range(D_FF // FFN_TF):
        sl = slice(j * FFN_TF, (j + 1) * FFN_TF)
        g = jnp.dot(xb, wg_ref[:, sl], preferred_element_type=F32)
        u = jnp.dot(xb, wu_ref[:, sl], preferred_element_type=F32)
        h = (g * _sigmoid(g) * u).astype(BF16)
        part = jnp.dot(h, wd_ref[sl, :], preferred_element_type=F32)
        acc = part if acc is None else acc + part
    y = _layer_norm_rows(DN_ALPHA * x_ref[...] + acc, lg_ref[...], lb_ref[...])
    xo_ref[...] = y
    xob_ref[...] = y.astype(BF16)


def dense_ffn(xb2, x2, wg, wu, wd, ln_g, ln_b):
    m = x2.shape[0]
    row = pl.BlockSpec((FFN_TM, D_MODEL), lambda i: (i, 0))
    const = pl.BlockSpec((1, D_MODEL), lambda i: (0, 0))
    resident = lambda r, c: pl.BlockSpec((r, c), lambda i: (0, 0), pipeline_mode=pl.Buffered(1))
    return pl.pallas_call(
        _ffn_kernel,
        out_shape=(jax.ShapeDtypeStruct((m, D_MODEL), F32), jax.ShapeDtypeStruct((m, D_MODEL), BF16)),
        grid=(m // FFN_TM,),
        in_specs=[row, row, resident(D_MODEL, D_FF), resident(D_MODEL, D_FF), resident(D_FF, D_MODEL),
                  const, const],
        out_specs=(row, row),
        compiler_params=_cparams(("parallel",)),
        name="dense_ffn",
    )(xb2, x2, wg, wu, wd, ln_g, ln_b)


MOE_TM = 512
MOE_WIN = 512
MOE_TF = 1792
ROUTER_TM = 512


def _router_kernel(xb_ref, wr_ref, e_ref, w_ref, xp_ref):
    logits = jnp.dot(xb_ref[...], wr_ref[...], preferred_element_type=F32)
    lane = _iota2(logits.shape, 1)
    neg = jnp.float32(-jnp.inf)
    lane_f = lane.astype(F32)
    lg = jnp.where(lane < N_EXPERTS, logits, neg)
    m1 = jnp.max(lg, axis=-1, keepdims=True)
    i1 = jnp.min(jnp.where(lg == m1, lane_f, float(LANES)), axis=-1, keepdims=True)
    lg2 = jnp.where(lane_f == i1, neg, lg)
    m2 = jnp.max(lg2, axis=-1, keepdims=True)
    i2 = jnp.min(jnp.where(lg2 == m2, lane_f, float(LANES)), axis=-1, keepdims=True)
    t = jnp.exp(m2 - m1)
    w1 = 1.0 / (1.0 + t)
    w2 = t / (1.0 + t)
    e_ref[...] = jnp.where(lane == 0, i1, jnp.where(lane == 1, i2, 0.0)).astype(jnp.int32)
    w_ref[...] = jnp.where(lane == 0, w1, jnp.where(lane == 1, w2, 0.0))
    _store_packed(xp_ref, xb_ref[...].astype(F32))


def moe_router(xb2, w_router_p):
    m = xb2.shape[0]
    lane_out = pl.BlockSpec((ROUTER_TM, LANES), lambda i: (i, 0))
    return pl.pallas_call(
        _router_kernel,
        out_shape=(jax.ShapeDtypeStruct((m, LANES), jnp.int32), jax.ShapeDtypeStruct((m, LANES), F32),
                   jax.ShapeDtypeStruct((SC_SPLIT, m, LANES), jnp.uint32)),
        grid=(m // ROUTER_TM,),
        in_specs=[pl.BlockSpec((ROUTER_TM, D_MODEL), lambda i: (i, 0)),
                  pl.BlockSpec((D_MODEL, LANES), lambda i: (0, 0))],
        out_specs=(lane_out, lane_out, pl.BlockSpec((SC_SPLIT, ROUTER_TM, LANES), lambda i: (0, i, 0))),
        compiler_params=_cparams(("parallel",)),
        name="moe_router",
    )(xb2, w_router_p)


def _moe_ffn_kernel(be_ref, bv_ref, x_ref, wg_ref, wu_ref, wd_ref, o_ref, acc_ref):
    i = pl.program_id(0)
    j = pl.program_id(1)
    valid = bv_ref[i] != 0

    @pl.when(j == 0)
    def _():
        acc_ref[...] = jnp.zeros_like(acc_ref)

    @pl.when(valid)
    def _():
        xb = _load_packed(x_ref).astype(BF16)
        g = jnp.dot(xb, wg_ref[0], preferred_element_type=F32)
        u = jnp.dot(xb, wu_ref[0], preferred_element_type=F32)
        h = (g * _sigmoid(g) * u).astype(BF16)
        acc_ref[...] += jnp.dot(h, wd_ref[0], preferred_element_type=F32)

    @pl.when(j == pl.num_programs(1) - 1)
    def _():
        _store_packed(o_ref, acc_ref[...])


def moe_expert_ffn(block_e, block_valid, xrows, wg, wu, wd):
    n_rows = xrows.shape[1]
    grid_spec = pltpu.PrefetchScalarGridSpec(
        num_scalar_prefetch=2,
        grid=(n_rows // MOE_TM, D_FF_EXPERT // MOE_TF),
        in_specs=[pl.BlockSpec((SC_SPLIT, MOE_TM, LANES), lambda i, j, be, bv: (0, i * bv[i], 0)),
                  pl.BlockSpec((1, D_MODEL, MOE_TF), lambda i, j, be, bv: (be[i], 0, j)),
                  pl.BlockSpec((1, D_MODEL, MOE_TF), lambda i, j, be, bv: (be[i], 0, j)),
                  pl.BlockSpec((1, MOE_TF, D_MODEL), lambda i, j, be, bv: (be[i], j, 0))],
        out_specs=pl.BlockSpec((SC_SPLIT, MOE_TM, LANES), lambda i, j, be, bv: (0, i, 0)),
        scratch_shapes=[pltpu.VMEM((MOE_TM, D_MODEL), F32)],
    )
    return pl.pallas_call(
        _moe_ffn_kernel,
        out_shape=jax.ShapeDtypeStruct((SC_SPLIT, n_rows, LANES), jnp.uint32),
        grid_spec=grid_spec,
        compiler_params=_cparams(("arbitrary", "arbitrary")),
        name="moe_expert_ffn",
    )(block_e, block_valid, xrows, wg, wu, wd)


PACKED_WIDTH = D_MODEL // 2
SC_ROWS = 128
SC_SPLIT = 4


def _store_packed(o_ref, y):
    lo = lax.bitcast_convert_type(y[:, :PACKED_WIDTH].astype(BF16).astype(F32), jnp.uint32) >> 16
    hi = lax.bitcast_convert_type(y[:, PACKED_WIDTH:].astype(BF16).astype(F32), jnp.uint32)
    word = hi | lo
    for c in range(SC_SPLIT):
        o_ref[c] = word[:, c * LANES:(c + 1) * LANES]


def _load_packed(ref):
    w = jnp.concatenate([ref[c] for c in range(SC_SPLIT)], axis=1)
    lo = lax.bitcast_convert_type(w << 16, F32)
    hi = lax.bitcast_convert_type(w & jnp.uint32(0xFFFF0000), F32)
    return jnp.concatenate([lo, hi], axis=1)


def sc_gather_rows(data, idx):
    pieces, rows, width = data.shape
    idx = (jnp.arange(pieces, dtype=jnp.int32)[:, None] * rows + idx[None, :]).reshape(-1)
    return _sc_gather(data.reshape(pieces * rows, width), idx).reshape(pieces, -1, width)


def _sc_gather(data, idx):
    n = idx.shape[0]
    d = data.shape[1]
    sc = plsc.get_sparse_core_info()
    mesh = plsc.VectorSubcoreMesh(core_axis_name="core", subcore_axis_name="subcore")
    steps = n // (SC_ROWS * sc.num_cores)

    @pl.kernel(out_type=jax.ShapeDtypeStruct((n, d), data.dtype), mesh=mesh, scratch_types=[])
    def gather_kernel(x_hbm, i_hbm, o_hbm):
        def body(i_vmem, o_vmem):
            pltpu.sync_copy(x_hbm.at[i_vmem.at[0]], o_vmem)

        pltpu.emit_pipeline(
            body,
            grid=(sc.num_cores, steps),
            in_specs=[pl.BlockSpec((1, SC_ROWS), index_map=lambda c, i: (0, c * steps + i))],
            out_specs=[pl.BlockSpec((SC_ROWS, d), index_map=lambda c, i: (c * steps + i, 0))],
            core_axis_name=("core", "subcore"),
            dimension_semantics=(pltpu.PARALLEL, pltpu.PARALLEL),
        )(i_hbm, o_hbm)

    return gather_kernel(data, idx.reshape(1, n))


def sc_scatter_rows(data, idx0, idx1, n_out):
    pieces, n, width = data.shape
    off = jnp.arange(pieces, dtype=jnp.int32)[:, None] * n_out
    out = _sc_scatter(data.reshape(pieces * n, width), (off + idx0[None, :]).reshape(-1),
                      (off + idx1[None, :]).reshape(-1), pieces * n_out)
    return out.reshape(pieces, n_out, width)


def _sc_scatter(data, idx0, idx1, n_out):
    n, d = data.shape
    sc = plsc.get_sparse_core_info()
    mesh = plsc.VectorSubcoreMesh(core_axis_name="core", subcore_axis_name="subcore")
    steps = n // (SC_ROWS * sc.num_cores)
    idx_spec = pl.BlockSpec((1, SC_ROWS), index_map=lambda c, i: (0, c * steps + i))

    @pl.kernel(out_type=jax.ShapeDtypeStruct((n_out, d), data.dtype), mesh=mesh, scratch_types=[])
    def scatter_kernel(x_hbm, i0_hbm, i1_hbm, o_hbm):
        def body(x_vmem, i0_vmem, i1_vmem):
            pltpu.sync_copy(x_vmem, o_hbm.at[i0_vmem.at[0]])
            pltpu.sync_copy(x_vmem, o_hbm.at[i1_vmem.at[0]])

        pltpu.emit_pipeline(
            body,
            grid=(sc.num_cores, steps),
            in_specs=[pl.BlockSpec((SC_ROWS, d), index_map=lambda c, i: (c * steps + i, 0)), idx_spec, idx_spec],
            out_specs=[],
            core_axis_name=("core", "subcore"),
            dimension_semantics=(pltpu.PARALLEL, pltpu.PARALLEL),
        )(x_hbm, i0_hbm, i1_hbm)

    return scatter_kernel(data, idx0.reshape(1, n), idx1.reshape(1, n))


def _moe_finish_kernel(y0_ref, y1_ref, w_ref, x_ref, lg_ref, lb_ref, xo_ref):
    wt = w_ref[...]
    y = wt[:, 0:1] * _load_packed(y0_ref) + wt[:, 1:2] * _load_packed(y1_ref)
    xo_ref[...] = _layer_norm_rows(DN_ALPHA * x_ref[...] + y, lg_ref[...], lb_ref[...])


def moe_finish(ytok, top_w, x2, ln_g, ln_b):
    m = x2.shape[0]
    n_win = m // MOE_WIN
    packed = lambda off: pl.BlockSpec((SC_SPLIT, MOE_WIN, LANES), lambda i: (0, i + off, 0))
    row = pl.BlockSpec((MOE_WIN, D_MODEL), lambda i: (i, 0))
    const = pl.BlockSpec((1, D_MODEL), lambda i: (0, 0))
    return pl.pallas_call(
        _moe_finish_kernel,
        out_shape=jax.ShapeDtypeStruct((m, D_MODEL), F32),
        grid=(n_win,),
        in_specs=[packed(0), packed(n_win), pl.BlockSpec((MOE_WIN, TOP_K), lambda i: (i, 0)), row, const, const],
        out_specs=row,
        compiler_params=_cparams(("parallel",)),
        name="moe_finish",
    )(ytok, ytok, top_w, x2, ln_g, ln_b)


def moe_layer(xb2, x2, w_router, wg, wu, wd, ln_g, ln_b):
    n = x2.shape[0]
    n_rows = n * TOP_K + N_EXPERTS * MOE_TM
    n_blocks = n_rows // MOE_TM

    wr = jnp.pad(w_router, ((0, 0), (0, LANES - N_EXPERTS))).astype(BF16)
    top_e_t, top_w_t, x_packed = moe_router(xb2, wr)
    top_e = top_e_t[:, :TOP_K]
    top_w = top_w_t[:, :TOP_K]

    experts = jnp.arange(N_EXPERTS, dtype=jnp.int32)
    tok_onehot = ((top_e[:, 0:1] == experts[None, :]) | (top_e[:, 1:2] == experts[None, :])).astype(jnp.int32)
    csum = jnp.cumsum(tok_onehot, axis=0)
    counts = csum[-1]
    rank = csum - tok_onehot
    padded = (counts + MOE_TM - 1) // MOE_TM * MOE_TM
    pad_end = jnp.cumsum(padded)
    pad_start = pad_end - padded
    dest = pad_start[top_e] + jnp.take_along_axis(rank, top_e, axis=1)

    blk_row0 = jnp.arange(n_blocks, dtype=jnp.int32) * MOE_TM
    block_e = jnp.minimum(jnp.sum((pad_end[None, :] <= blk_row0[:, None]).astype(jnp.int32), axis=1),
                          N_EXPERTS - 1).astype(jnp.int32)
    block_valid = (blk_row0 < pad_end[-1]).astype(jnp.int32)

    xrows = sc_scatter_rows(x_packed, dest[:, 0], dest[:, 1], n_rows)
    yrows = moe_expert_ffn(block_e, block_valid, xrows, wg, wu, wd)

    ytok = sc_gather_rows(yrows, dest.T.reshape(-1))
    return moe_finish(ytok, top_w, x2, ln_g, ln_b)


def _w_in_perm():
    zero = 3480
    perm = np.full((D_IN_P,), zero, np.int64)
    o_qkv, o_z, o_beta, o_decay, o_bq, o_bk, o_bv, o_cf, o_ci, o_cq, o_cg = (
        0, 1152, 1536, 1548, 1560, 1944, 2072, 2200, 2712, 2968, 3224)
    perm[COL_A_QKV:COL_A_QKV + 1152] = np.arange(o_qkv, o_qkv + 1152)
    perm[COL_A_Z:COL_A_Z + 384] = np.arange(o_z, o_z + 384)
    for j in range(3):
        for a in range(2):
            dst = COL_B_Q + j * LANES + a * HEAD_DIM
            src = o_bq + (j + 3 * a) * HEAD_DIM
            perm[dst:dst + HEAD_DIM] = np.arange(src, src + HEAD_DIM)
    perm[COL_B_K:COL_B_K + 128] = np.arange(o_bk, o_bk + 128)
    perm[COL_C_G:COL_C_G + 256] = np.arange(o_cg, o_cg + 256)
    perm[COL_B_V:COL_B_V + 128] = np.arange(o_bv, o_bv + 128)
    perm[COL_C_F:COL_C_F + 512] = np.arange(o_cf, o_cf + 512)
    perm[COL_C_I:COL_C_I + 256] = np.arange(o_ci, o_ci + 256)
    perm[COL_C_Q:COL_C_Q + 256] = np.arange(o_cq, o_cq + 256)
    for d, col in enumerate((COL_GATES0, COL_GATES1)):
        for p in range(A_HEADS // 2):
            dst = col + 4 * p
            perm[dst + 0] = o_beta + d * A_HEADS + 2 * p
            perm[dst + 1] = o_beta + d * A_HEADS + 2 * p + 1
            perm[dst + 2] = o_decay + d * A_HEADS + 2 * p
            perm[dst + 3] = o_decay + d * A_HEADS + 2 * p + 1
    return perm


def _w_out_perm():
    perm = np.arange(D_MIX)
    for j in range(3):
        for a in range(2):
            dst = A_WIDTH + j * LANES + a * HEAD_DIM
            src = A_WIDTH + (j + 3 * a) * HEAD_DIM
            perm[dst:dst + HEAD_DIM] = np.arange(src, src + HEAD_DIM)
    return perm


def _gdn_params(a_log, dt_bias):
    rows = np.array([4 * (h // 2) + 2 + h % 2 for h in range(A_HEADS)])
    prm = jnp.zeros((2, GATE_ROWS, LANES), F32)
    prm = prm.at[:, rows, 0].set(a_log.astype(F32))
    prm = prm.at[:, rows, 1].set(dt_bias.astype(F32))
    return prm.at[:, rows, 2].set(1.0)


def kernel(x, mem, w_in, conv_w, gdn_a_log, gdn_dt_bias, gdn_norm_w, q_norm_w, k_norm_w, hgrn_lb_logits,
           hgrn_norm_w, w_out, ln1_g, ln1_b, xq, xk, xv, xo, ln2_g, ln2_b, ffn_wg, ffn_wu, ffn_wd,
           moe_router, moe_wg, moe_wu, moe_wd, ln3_g, ln3_b):
    b, s, d = x.shape
    m = b * s
    cos_t, sin_t = rope_tables(s)
    lb_p = jax.nn.softmax(hgrn_lb_logits.astype(F32), axis=0)
    lb_c = jnp.cumsum(lb_p, axis=0)
    lower_bounds = lb_c - lb_c[0]
    in_perm = _w_in_perm()
    out_perm = _w_out_perm()
    mem_b = mem.astype(BF16).reshape(b * mem.shape[1], d)
    tile2 = lambda w: jnp.tile(w.astype(F32), 2)[None, :]
    row = lambda v: v.astype(F32)[None, :]

    x2 = x.reshape(m, d)
    xb2 = x2.astype(BF16)
    for l in range(DEPTH):
        w_in_p = jnp.take(jnp.pad(w_in[l], ((0, 0), (0, 1))), in_perm, axis=1).astype(BF16)
        wide2 = matmul(xb2, w_in_p[:, :D_WIDE], BF16, min(1024, m), 768)
        gates2 = matmul(xb2, w_in_p[:, D_WIDE:], F32, min(1024, m), D_IN_P - D_WIDE)
        wide3 = wide2.reshape(b, s, D_WIDE)
        gates3 = gates2.reshape(b, s, D_IN_P - D_WIDE)

        oa_f, oa_b = gdn_scan(wide3, gates3, conv_w[l], _gdn_params(gdn_a_log[l], gdn_dt_bias[l]))
        ob = gqa_attention(wide3, cos_t, sin_t, tile2(q_norm_w[l]), tile2(k_norm_w[l]), s)
        lb = lower_bounds[l][None, :]
        oc_f, oc_b = hgrn_scan(wide3, gates3, lb)

        w_out_p = jnp.take(w_out[l], out_perm, axis=0).astype(BF16)
        x2, xb2 = mixer_output(oa_f.reshape(m, A_WIDTH), oa_b.reshape(m, A_WIDTH), wide2,
                               ob.reshape(m, B_WIDTH), oc_f.reshape(m, C_WIDTH), oc_b.reshape(m, C_WIDTH),
                               x2, tile2(gdn_norm_w[l]), tile2(hgrn_norm_w[l]), w_out_p,
                               row(ln1_g[l]), row(ln1_b[l]))

        k3 = matmul(mem_b, xk[l].astype(BF16), BF16, 256, 512).reshape(b, -1, d)
        v3 = matmul(mem_b, xv[l].astype(BF16), BF16, 256, 512).reshape(b, -1, d)
        x3, xb3 = cross_attention(xb2.reshape(b, s, d), x2.reshape(b, s, d), k3, v3,
                                  xq[l].astype(BF16), xo[l].astype(BF16), row(ln2_g[l]), row(ln2_b[l]))
        x2, xb2 = x3.reshape(m, d), xb3.reshape(m, d)

        if l % 2 == 0:
            x2, xb2 = dense_ffn(xb2, x2, ffn_wg[l // 2].astype(BF16), ffn_wu[l // 2].astype(BF16),
                                ffn_wd[l // 2].astype(BF16), row(ln3_g[l]), row(ln3_b[l]))
        else:
            x2 = moe_layer(xb2, x2, moe_router[l // 2], moe_wg[l // 2].astype(BF16),
                           moe_wu[l // 2].astype(BF16), moe_wd[l // 2].astype(BF16),
                           row(ln3_g[l]), row(ln3_b[l]))
            xb2 = x2.astype(BF16)
    return x2.reshape(b, s, d)
```

```python
import math

import numpy as np
import jax
import jax.numpy as jnp
from jax import lax
from jax.experimental import pallas as pl
from jax.experimental.pallas import tpu as pltpu
from jax.experimental.pallas import tpu_sc as plsc

F32 = jnp.float32
BF16 = jnp.bfloat16

D_MODEL = 1024
DEPTH = 2
HEAD_DIM = 64
A_HEADS = 6
A_WIDTH = A_HEADS * HEAD_DIM
B_Q_HEADS = 6
B_KV_HEADS = 2
B_WIDTH = B_Q_HEADS * HEAD_DIM
C_HEADS = 4
C_WIDTH = C_HEADS * HEAD_DIM
D_MIX = A_WIDTH + B_WIDTH + C_WIDTH
CONV_K = 5
CHUNK = 64
GRID_W = 64
ROPE_AXIS_DIM = HEAD_DIM // 2
ROPE_THETA = 10000.0
X_HEADS = 4
X_HEAD_DIM = D_MODEL // X_HEADS
D_FF = 2816
N_EXPERTS = 8
TOP_K = 2
D_FF_EXPERT = 3584
DN_ALPHA = (2 * DEPTH) ** 0.25
LN_EPS = 1e-5
RMS_EPS = 1e-6

LANES = 128
SUBLANES = 8
HALO = 16
GROUP = 256
CHUNKS_PER_GROUP = GROUP // CHUNK
VMEM_LIMIT = 56 * 1024 * 1024

COL_A_QKV = 0
COL_A_Z = 1152
COL_B_Q = 1536
COL_B_K = 1920
COL_C_G = 2048
COL_B_V = 2304
COL_C_I = 2560
COL_C_Q = 2816
D_WIDE = 3072
COL_C_F = D_WIDE
COL_GATES0 = D_WIDE + 512
COL_GATES1 = D_WIDE + 640
D_IN_P = 3840


def _cparams(sem):
    return pltpu.CompilerParams(dimension_semantics=sem, vmem_limit_bytes=VMEM_LIMIT)


def _sigmoid(x):
    return 1.0 / (1.0 + jnp.exp(-x))


def _softplus(x):
    return jnp.maximum(x, 0.0) + jnp.log1p(jnp.exp(-jnp.abs(x)))


def _dot(a, b):
    return jnp.dot(a.astype(BF16), b.astype(BF16), preferred_element_type=F32)


def _dot_nt(a, b):
    return lax.dot_general(a.astype(BF16), b.astype(BF16), (((1,), (1,)), ((), ())),
                           preferred_element_type=F32)


def _dot_sel(m01, x):
    m = jnp.where(m01, 1.0, 0.0).astype(BF16)
    hi = x.astype(BF16)
    r1 = x - hi.astype(F32)
    mid = r1.astype(BF16)
    lo = (r1 - mid.astype(F32)).astype(BF16)
    out = jnp.dot(m, hi, preferred_element_type=F32)
    out = out + jnp.dot(m, mid, preferred_element_type=F32)
    return out + jnp.dot(m, lo, preferred_element_type=F32)


def _dot_sel_right(x, m01):
    m = jnp.where(m01, 1.0, 0.0).astype(BF16)
    hi = x.astype(BF16)
    r1 = x - hi.astype(F32)
    mid = r1.astype(BF16)
    lo = (r1 - mid.astype(F32)).astype(BF16)
    out = jnp.dot(hi, m, preferred_element_type=F32)
    out = out + jnp.dot(mid, m, preferred_element_type=F32)
    return out + jnp.dot(lo, m, preferred_element_type=F32)


def _layer_norm_rows(y, g, b):
    mu = jnp.mean(y, axis=-1, keepdims=True)
    yc = y - mu
    var = jnp.mean(yc * yc, axis=-1, keepdims=True)
    return yc * lax.rsqrt(var + LN_EPS) * g + b


def _iota2(shape, dim):
    return lax.broadcasted_iota(jnp.int32, shape, dim)


def _head_sum(x, first_head):
    s0 = jnp.sum(jnp.where(first_head, x, 0.0), axis=-1, keepdims=True)
    s1 = jnp.sum(jnp.where(first_head, 0.0, x), axis=-1, keepdims=True)
    return jnp.where(first_head, s0, s1)


def _block_rows(t, block, row_in_block):
    parts = [jnp.broadcast_to(t[b * block + row_in_block:b * block + row_in_block + 1, :], (block, t.shape[1]))
             for b in range(GROUP // block)]
    return jnp.concatenate(parts, axis=0)


def _mm_kernel(x_ref, w_ref, o_ref):
    o_ref[...] = jnp.dot(x_ref[...], w_ref[...], preferred_element_type=F32).astype(o_ref.dtype)


def matmul(x, w, out_dtype, tm, tn):
    m, k = x.shape
    n = w.shape[1]
    return pl.pallas_call(
        _mm_kernel,
        out_shape=jax.ShapeDtypeStruct((m, n), out_dtype),
        grid=(n // tn, m // tm),
        in_specs=[pl.BlockSpec((tm, k), lambda j, i: (i, 0)),
                  pl.BlockSpec((k, tn), lambda j, i: (0, j))],
        out_specs=pl.BlockSpec((tm, tn), lambda j, i: (i, j)),
        compiler_params=_cparams(("parallel", "arbitrary")),
        name="matmul",
    )(x, w)


def _scan_masks(reverse):
    ri = _iota2((GROUP, GROUP), 0)
    ci = _iota2((GROUP, GROUP), 1)
    same_chunk = (ri >> 6) == (ci >> 6)
    if reverse:
        return ri, ci, same_chunk, same_chunk & (ri <= ci), same_chunk & (ri < ci)
    return ri, ci, same_chunk, same_chunk & (ri >= ci), same_chunk & (ri > ci)


GATE_ROWS = 16


class _GdnProblem:
    def __init__(self, **kw):
        self.__dict__.update(kw)


def _gdn_group(probs, same_chunk, h0):
    nh0 = jnp.logical_not(h0)
    sr = _iota2((CHUNK, GROUP), 0)
    sc = _iota2((CHUNK, GROUP), 1)
    eye_side = jnp.where(sr == (sc & (CHUNK - 1)), 1.0, 0.0)
    tl = _iota2((LANES, GROUP), 1)
    br = _iota2((LANES, LANES), 0)
    bc = _iota2((LANES, LANES), 1)
    bd128 = (br >> 6) == (bc >> 6)

    def block_diag(side):
        return jnp.where(same_chunk, jnp.concatenate([side] * CHUNKS_PER_GROUP, axis=0), 0.0).astype(BF16)

    for pr in probs:
        pr.gedge = _block_rows(pr.gc, CHUNK, 0 if pr.reverse else CHUNK - 1)
        pr.exp_g = jnp.exp(pr.gc)
        pr.kb = pr.k * pr.beta_t
        pr.rhs = jnp.concatenate([pr.v * pr.beta_t, pr.kb * pr.exp_g], axis=1).astype(BF16)
        pr.k16 = pr.k.astype(BF16)
        pr.decay, pr.x, pr.pw, pr.bd = [], [], [], []
        for h in range(2):
            decay = jnp.exp(jnp.where(pr.incl, pr.g_col[h] - pr.g_row[h], -1e30))
            kk = _dot_nt(jnp.where(h0 if h == 0 else nh0, pr.kb, 0.0), pr.k16)
            low = jnp.where(pr.strict, kk * decay, 0.0)
            a = -(low[0:CHUNK] + low[CHUNK:2 * CHUNK] + low[2 * CHUNK:3 * CHUNK] + low[3 * CHUNK:4 * CHUNK])
            pr.decay.append(decay)
            pr.x.append(eye_side + a)
            pr.pw.append(a)
            pr.bd.append(block_diag(a))

    for _ in range(5):
        for pr in probs:
            for h in range(2):
                pr.pw[h] = jnp.dot(pr.pw[h].astype(BF16), pr.bd[h], preferred_element_type=F32)
        for pr in probs:
            for h in range(2):
                pr.bd[h] = block_diag(pr.pw[h])
        for pr in probs:
            for h in range(2):
                pr.x[h] = pr.x[h] + jnp.dot(pr.x[h].astype(BF16), pr.bd[h], preferred_element_type=F32)

    for pr in probs:
        sols = [jnp.dot(block_diag(pr.x[h]), pr.rhs, preferred_element_type=F32) for h in range(2)]
        u = jnp.where(h0, sols[0][:, :LANES], sols[1][:, :LANES])
        w = jnp.where(h0, sols[0][:, LANES:], sols[1][:, LANES:])
        wu = jnp.concatenate([w, u], axis=1)
        wu16 = wu.astype(BF16)
        aw = []
        for h in range(2):
            attn = jnp.where(pr.incl, _dot_nt(jnp.where(h0 if h == 0 else nh0, pr.q, 0.0), pr.k16) * pr.decay[h], 0.0)
            aw.append(jnp.dot(attn.astype(BF16), wu16, preferred_element_type=F32))
        pr.qp = (pr.q * pr.exp_g - jnp.where(h0, aw[0][:, :LANES], aw[1][:, :LANES])).astype(BF16)
        pr.op = jnp.where(h0, aw[0][:, LANES:], aw[1][:, LANES:])
        kd_t = (pr.k * jnp.exp(pr.gedge - pr.gc)).T
        pr.kmat, pr.nmat = [], []
        for c in range(CHUNKS_PER_GROUP):
            km = jnp.dot(jnp.where((tl >> 6) == c, kd_t, 0.0), wu, preferred_element_type=F32)
            pr.kmat.append(jnp.where(bd128, km[:, :LANES], 0.0).astype(BF16))
            pr.nmat.append(jnp.where(bd128, km[:, LANES:], 0.0))
        pr.out = [None] * CHUNKS_PER_GROUP

    for t in range(CHUNKS_PER_GROUP):
        for pr in probs:
            c = CHUNKS_PER_GROUP - 1 - t if pr.reverse else t
            rs = slice(c * CHUNK, (c + 1) * CHUNK)
            s16 = pr.state.astype(BF16)
            pr.out[c] = jnp.dot(pr.qp[rs], s16, preferred_element_type=F32) + pr.op[rs]
            cd = jnp.exp(pr.gedge[c * CHUNK:c * CHUNK + 1, :])
            pr.state = pr.state * cd - jnp.dot(pr.kmat[c], s16, preferred_element_type=F32) + pr.nmat[c]
    return [jnp.concatenate(pr.out, axis=0) for pr in probs]


def _gdn_kernel(xf_ref, xfp_ref, xfn_ref, xb_ref, xbp_ref, xbn_ref, cw_ref, gf_ref, gb_ref, prm_ref,
                of_ref, ob_ref, s_scr, c_scr):
    n = pl.program_id(1)
    n_groups = pl.num_programs(1)

    @pl.when(n == 0)
    def _():
        s_scr[...] = jnp.zeros_like(s_scr)

    lane = _iota2((GROUP, LANES), 1)
    h0 = lane < HEAD_DIM
    ri = _iota2((GROUP, GROUP), 0)
    ci = _iota2((GROUP, GROUP), 1)
    same_chunk = (ri >> 6) == (ci >> 6)
    cw = cw_ref[...]
    base = SUBLANES - CONV_K // 2
    pairs = A_HEADS // 2

    dirs = ((n, xf_ref, xfp_ref, xfn_ref, gf_ref), (n_groups - 1 - n, xb_ref, xbp_ref, xbn_ref, gb_ref))

    @pl.when(2 * n < n_groups)
    def _():
        for gi, x_ref, xp_ref, xn_ref, _ in dirs:
            prev = jnp.where(gi > 0, xp_ref[0].astype(F32)[HALO - SUBLANES:], 0.0)
            nxt = jnp.where(gi < n_groups - 1, xn_ref[0].astype(F32)[:SUBLANES], 0.0)
            ext = jnp.concatenate([prev, x_ref[0].astype(F32), nxt], axis=0)
            acc = None
            for t in range(CONV_K):
                shift = (CONV_K // 2 - t) % ext.shape[0]
                tap = (pltpu.roll(ext, shift, 0) if shift else ext)[SUBLANES:SUBLANES + GROUP, :] * cw[t:t + 1, :]
                acc = tap if acc is None else acc + tap
            act = acc * _sigmoid(acc)
            rows = pl.ds(pl.multiple_of(gi * GROUP, GROUP), GROUP)
            for p in range(pairs):
                q_sl = slice(p * LANES, (p + 1) * LANES)
                k_sl = slice(A_WIDTH + p * LANES, A_WIDTH + (p + 1) * LANES)
                q = act[:, q_sl]
                k = act[:, k_sl]
                c_scr[rows, q_sl] = q * lax.rsqrt(_head_sum(q * q, h0) + RMS_EPS) * (HEAD_DIM ** -0.5)
                c_scr[rows, k_sl] = k * lax.rsqrt(_head_sum(k * k, h0) + RMS_EPS)
            c_scr[rows, 2 * A_WIDTH:] = act[:, 2 * A_WIDTH:]

    probs = []
    for d, (gi, _, _, _, g_ref) in enumerate(dirs):
        reverse = d == 1
        lower, upper = same_chunk & (ri >= ci), same_chunk & (ri <= ci)
        incl = upper if reverse else lower
        strict = same_chunk & ((ri < ci) if reverse else (ri > ci))
        qkv = c_scr[pl.ds(pl.multiple_of(gi * GROUP, GROUP), GROUP), :]

        gt_t = g_ref[0].T[0:GATE_ROWS, :]
        prm = prm_ref[d]
        e_rows = jnp.where(prm[:, 2:3] > 0.5,
                           -jnp.exp(prm[:, 0:1]) * _softplus(gt_t + prm[:, 1:2]), _sigmoid(gt_t))
        g_rows = _dot_sel_right(e_rows, lower if reverse else upper)
        slab = jnp.concatenate([e_rows, g_rows, jnp.zeros((LANES - 2 * GATE_ROWS, GROUP), F32)], axis=0)
        cols = slab.T
        for p in range(pairs):
            sl = lambda part: slice(part * A_WIDTH + p * LANES, part * A_WIDTH + (p + 1) * LANES)
            col = lambda r: cols[:, r:r + 1]
            probs.append(_GdnProblem(
                reverse=reverse, incl=incl, strict=strict, slot=d * pairs + p,
                q=qkv[:, sl(0)], k=qkv[:, sl(1)], v=qkv[:, sl(2)],
                beta_t=jnp.where(h0, col(4 * p), col(4 * p + 1)),
                gc=jnp.where(h0, col(GATE_ROWS + 4 * p + 2), col(GATE_ROWS + 4 * p + 3)),
                g_col=(col(GATE_ROWS + 4 * p + 2), col(GATE_ROWS + 4 * p + 3)),
                g_row=(g_rows[4 * p + 2:4 * p + 3, :], g_rows[4 * p + 3:4 * p + 4, :]),
                state=s_scr[d * pairs + p]))

    outs = _gdn_group(probs, same_chunk, h0)
    for pr, out in zip(probs, outs):
        o_ref = ob_ref if pr.reverse else of_ref
        p = pr.slot % pairs
        o_ref[0, :, p * LANES:(p + 1) * LANES] = out.astype(BF16)
        s_scr[pr.slot] = pr.state


def gdn_scan(wide3, gates3, conv_w, prm):
    b, seq_len, _ = wide3.shape
    n_groups = seq_len // GROUP
    halo_per_group = GROUP // HALO
    n_halo = seq_len // HALO
    width = 3 * A_WIDTH
    fwd = lambda n: n
    bwd = lambda n: n_groups - 1 - n

    def x_specs(gidx):
        return [pl.BlockSpec((1, GROUP, width), lambda i, n: (i, gidx(n), 0)),
                pl.BlockSpec((1, HALO, width),
                             lambda i, n: (i, jnp.maximum(gidx(n) * halo_per_group - 1, 0), 0)),
                pl.BlockSpec((1, HALO, width),
                             lambda i, n: (i, jnp.minimum((gidx(n) + 1) * halo_per_group, n_halo - 1), 0))]

    out_sds = jax.ShapeDtypeStruct((b, seq_len, A_WIDTH), BF16)
    return pl.pallas_call(
        _gdn_kernel,
        out_shape=(out_sds, out_sds),
        grid=(b, n_groups),
        in_specs=x_specs(fwd) + x_specs(bwd) + [
            pl.BlockSpec((CONV_K, width), lambda i, n: (0, 0)),
            pl.BlockSpec((1, GROUP, LANES), lambda i, n: (i, fwd(n), (COL_GATES0 - D_WIDE) // LANES)),
            pl.BlockSpec((1, GROUP, LANES), lambda i, n: (i, bwd(n), (COL_GATES1 - D_WIDE) // LANES)),
            pl.BlockSpec((2, GATE_ROWS, LANES), lambda i, n: (0, 0, 0))],
        out_specs=(pl.BlockSpec((1, GROUP, A_WIDTH), lambda i, n: (i, fwd(n), 0)),
                   pl.BlockSpec((1, GROUP, A_WIDTH), lambda i, n: (i, bwd(n), 0))),
        scratch_shapes=[pltpu.VMEM((A_HEADS, LANES, LANES), F32), pltpu.VMEM((seq_len, width), F32)],
        compiler_params=_cparams(("parallel", "arbitrary")),
        name="gdn_scan",
    )(wide3, wide3, wide3, wide3, wide3, wide3, conv_w, gates3, gates3, prm)


EXP_CLAMP = 60.0


class _HgrnProblem:
    def __init__(self, **kw):
        self.__dict__.update(kw)


def _hgrn_group(probs):
    lane = _iota2((GROUP, LANES), 1)
    h0 = lane < HEAD_DIM
    nh0 = jnp.logical_not(h0)
    tl = _iota2((LANES, GROUP), 1)
    br = _iota2((LANES, LANES), 0)
    bc = _iota2((LANES, LANES), 1)
    bd128 = (br >> 6) == (bc >> 6)

    for pr in probs:
        ri, ci, same_chunk, incl, _ = pr.masks
        f = pr.lb + (1.0 - pr.lb) * _sigmoid(pr.f_raw)
        kk = 1.0 - f
        q = pr.q_raw * _sigmoid(pr.q_raw) * (HEAD_DIM ** -0.5)
        cum = _dot_sel(incl, jnp.log(f))
        pr.cedge = _block_rows(cum, CHUNK, 0 if pr.reverse else CHUNK - 1)
        pr.qd = q * jnp.exp(cum)
        pr.kd = kk * jnp.exp(pr.cedge - cum)
        if pr.reverse:
            m32 = same_chunk & ((ri & 63) < 32) & ((ci & 63) >= 32)
            m16 = ((ri >> 5) == (ci >> 5)) & ((ri & 31) < 16) & ((ci & 31) >= 16)
            mdg = ((ri >> 4) == (ci >> 4)) & (ri <= ci)
            ref32, ref16, refdg = _block_rows(cum, 64, 32), _block_rows(cum, 32, 16), _block_rows(cum, 16, 8)
        else:
            m32 = same_chunk & ((ri & 63) >= 32) & ((ci & 63) < 32)
            m16 = ((ri >> 5) == (ci >> 5)) & ((ri & 31) >= 16) & ((ci & 31) < 16)
            mdg = ((ri >> 4) == (ci >> 4)) & (ri >= ci)
            ref32, ref16, refdg = _block_rows(cum, 64, 31), _block_rows(cum, 32, 15), _block_rows(cum, 16, 7)
        pr.level_masks = (m32, m16, mdg)
        pr.qs = (q * jnp.exp(jnp.minimum(cum - ref32, 0.0)),
                 q * jnp.exp(jnp.minimum(cum - ref16, 0.0)),
                 q * jnp.exp(jnp.clip(cum - refdg, -EXP_CLAMP, EXP_CLAMP)))
        pr.ks = ((kk * jnp.exp(jnp.minimum(ref32 - cum, 0.0))).astype(BF16),
                 (kk * jnp.exp(jnp.minimum(ref16 - cum, 0.0))).astype(BF16),
                 (kk * jnp.exp(jnp.clip(refdg - cum, -EXP_CLAMP, EXP_CLAMP))).astype(BF16))

    for pr in probs:
        v16 = pr.v.astype(BF16)
        m32, m16, mdg = pr.level_masks
        intra = []
        for h in range(2):
            mh = h0 if h == 0 else nh0
            a = jnp.where(m32, _dot_nt(jnp.where(mh, pr.qs[0], 0.0), pr.ks[0]),
                          jnp.where(m16, _dot_nt(jnp.where(mh, pr.qs[1], 0.0), pr.ks[1]),
                                    jnp.where(mdg, _dot_nt(jnp.where(mh, pr.qs[2], 0.0), pr.ks[2]), 0.0)))
            intra.append(jnp.dot(a.astype(BF16), v16, preferred_element_type=F32))
        pr.intra = jnp.where(h0, intra[0], intra[1])
        v_t = pr.v.T
        pr.upd = [jnp.where(bd128, jnp.dot(jnp.where((tl >> 6) == c, v_t, 0.0), pr.kd,
                                           preferred_element_type=F32), 0.0)
                  for c in range(CHUNKS_PER_GROUP)]
        pr.out = [None] * CHUNKS_PER_GROUP

    for t in range(CHUNKS_PER_GROUP):
        for pr in probs:
            c = CHUNKS_PER_GROUP - 1 - t if pr.reverse else t
            rs = slice(c * CHUNK, (c + 1) * CHUNK)
            pr.out[c] = _dot_nt(pr.qd[rs], pr.state_t) + pr.intra[rs]
            cd = jnp.exp(pr.cedge[c * CHUNK:c * CHUNK + 1, :])
            pr.state_t = pr.state_t * cd + pr.upd[c]
    return [jnp.concatenate(pr.out, axis=0) for pr in probs]


def _hgrn_kernel(ff_ref, fb_ref, if_ref, ib_ref, qf_ref, qb_ref, lb_ref, of_ref, ob_ref, s_scr):
    n = pl.program_id(1)

    @pl.when(n == 0)
    def _():
        s_scr[...] = jnp.zeros_like(s_scr)

    pairs = C_HEADS // 2
    probs = []
    for d, (f_ref, i_ref, q_ref) in enumerate(((ff_ref, if_ref, qf_ref), (fb_ref, ib_ref, qb_ref))):
        masks = _scan_masks(d == 1)
        for p in range(pairs):
            sl = slice(p * LANES, (p + 1) * LANES)
            probs.append(_HgrnProblem(reverse=d == 1, masks=masks, slot=d * pairs + p, lb=lb_ref[:, sl],
                                      f_raw=f_ref[0, :, sl], v=i_ref[0, :, sl].astype(F32),
                                      q_raw=q_ref[0, :, sl].astype(F32), state_t=s_scr[d * pairs + p]))
    outs = _hgrn_group(probs)
    for pr, out in zip(probs, outs):
        o_ref = ob_ref if pr.reverse else of_ref
        p = pr.slot % pairs
        o_ref[0, :, p * LANES:(p + 1) * LANES] = out.astype(BF16)
        s_scr[pr.slot] = pr.state_t


def hgrn_scan(wide3, gates3, lb):
    b, seq_len, _ = wide3.shape
    n_groups = seq_len // GROUP
    fwd = lambda n: n
    bwd = lambda n: n_groups - 1 - n
    spec = lambda gidx, blk: pl.BlockSpec((1, GROUP, C_WIDTH), lambda i, n: (i, gidx(n), blk))
    f_blk = (COL_C_F - D_WIDE) // C_WIDTH
    out_sds = jax.ShapeDtypeStruct((b, seq_len, C_WIDTH), BF16)
    return pl.pallas_call(
        _hgrn_kernel,
        out_shape=(out_sds, out_sds),
        grid=(b, n_groups),
        in_specs=[spec(fwd, f_blk), spec(bwd, f_blk + 1),
                  spec(fwd, COL_C_I // C_WIDTH), spec(bwd, COL_C_I // C_WIDTH),
                  spec(fwd, COL_C_Q // C_WIDTH), spec(bwd, COL_C_Q // C_WIDTH),
                  pl.BlockSpec((1, C_WIDTH), lambda i, n: (0, 0))],
        out_specs=(spec(fwd, 0), spec(bwd, 0)),
        scratch_shapes=[pltpu.VMEM((C_HEADS, LANES, LANES), F32)],
        compiler_params=_cparams(("parallel", "arbitrary")),
        name="hgrn_scan",
    )(gates3, gates3, wide3, wide3, wide3, wide3, lb)


ATT_TQ = 512


def _norm_rope(x, w, cos, sin_signed, first_half16):
    lane = _iota2(x.shape, 1)
    h0 = lane < HEAD_DIM
    ms = _head_sum(x * x, h0) * (1.0 / HEAD_DIM)
    xn = x * lax.rsqrt(ms + RMS_EPS) * w
    partner = jnp.where(first_half16, pltpu.roll(xn, LANES - 16, 1), pltpu.roll(xn, 16, 1))
    return xn * cos + partner * sin_signed


def _attn_kernel(q_ref, k_ref, v_ref, cosq_ref, sinq_ref, cosk_ref, sink_ref, qw_ref, kw_ref, o_ref,
                 k_scr, v_scr):
    qi = pl.program_id(1)

    @pl.when(qi == 0)
    def _():
        kx = k_ref[0].astype(F32)
        lane_k = _iota2(kx.shape, 1)
        k_scr[...] = _norm_rope(kx, kw_ref[...], cosk_ref[...], sink_ref[...], (lane_k & 31) < 16).astype(BF16)
        v_scr[:, :LANES] = v_ref[0]
        v_scr[:, LANES:] = jnp.ones((v_scr.shape[0], LANES), BF16)

    lane = _iota2((ATT_TQ, LANES), 1)
    h0 = lane < HEAD_DIM
    fh = (lane & 31) < 16
    cos = cosq_ref[...]
    sin = sinq_ref[...]
    kmat = k_scr[...]
    vmat = v_scr[...]
    q_scale = (HEAD_DIM ** -0.5) * math.log2(math.e)
    for j in range(B_Q_HEADS // 2):
        qn = _norm_rope(q_ref[0, :, j * LANES:(j + 1) * LANES].astype(F32), qw_ref[...], cos, sin, fh) * q_scale
        outs = []
        for half in range(2):
            mh = h0 if half == 0 else jnp.logical_not(h0)
            s = _dot_nt(jnp.where(mh, qn, 0.0), kmat)
            m = jnp.max(s, axis=-1, keepdims=True)
            e = jnp.exp2((s - m).astype(BF16))
            pvl = jnp.dot(e, vmat, preferred_element_type=F32)
            outs.append(pvl[:, :LANES] / pvl[:, LANES:LANES + 1])
        o_ref[0, :, j * LANES:(j + 1) * LANES] = jnp.where(h0, outs[0], outs[1]).astype(BF16)


def gqa_attention(proj, cos_t, sin_t, q_w, k_w, seq_len):
    b = proj.shape[0]
    return pl.pallas_call(
        _attn_kernel,
        out_shape=jax.ShapeDtypeStruct((b, seq_len, B_WIDTH), BF16),
        grid=(b, seq_len // ATT_TQ),
        in_specs=[pl.BlockSpec((1, ATT_TQ, B_WIDTH), lambda i, t: (i, t, COL_B_Q // B_WIDTH)),
                  pl.BlockSpec((1, seq_len, LANES), lambda i, t: (i, 0, COL_B_K // LANES)),
                  pl.BlockSpec((1, seq_len, LANES), lambda i, t: (i, 0, COL_B_V // LANES)),
                  pl.BlockSpec((ATT_TQ, LANES), lambda i, t: (t, 0)),
                  pl.BlockSpec((ATT_TQ, LANES), lambda i, t: (t, 0)),
                  pl.BlockSpec((seq_len, LANES), lambda i, t: (0, 0)),
                  pl.BlockSpec((seq_len, LANES), lambda i, t: (0, 0)),
                  pl.BlockSpec((1, LANES), lambda i, t: (0, 0)),
                  pl.BlockSpec((1, LANES), lambda i, t: (0, 0))],
        out_specs=pl.BlockSpec((1, ATT_TQ, B_WIDTH), lambda i, t: (i, t, 0)),
        scratch_shapes=[pltpu.VMEM((seq_len, LANES), BF16), pltpu.VMEM((seq_len, 2 * LANES), BF16)],
        compiler_params=_cparams(("parallel", "arbitrary")),
        name="gqa_attention",
    )(proj, proj, proj, cos_t, sin_t, cos_t, sin_t, q_w, k_w)


def rope_tables(seq_len):
    rows = seq_len // GRID_W
    row = np.repeat(np.arange(rows, dtype=np.float64), GRID_W)
    col = np.tile(np.arange(GRID_W, dtype=np.float64), rows)
    inv_freq = ROPE_THETA ** (-np.arange(0, ROPE_AXIS_DIM, 2, dtype=np.float64) / ROPE_AXIS_DIM)
    ang_r = row[:, None] * inv_freq
    ang_c = col[:, None] * inv_freq
    cos64 = np.concatenate([np.cos(ang_r), np.cos(ang_r), np.cos(ang_c), np.cos(ang_c)], axis=1)
    sin64 = np.concatenate([-np.sin(ang_r), np.sin(ang_r), -np.sin(ang_c), np.sin(ang_c)], axis=1)
    return (jnp.asarray(np.tile(cos64, (1, 2)), dtype=F32), jnp.asarray(np.tile(sin64, (1, 2)), dtype=F32))


MIX_TM = 256


def _mixout_kernel(oaf_ref, oab_ref, z_ref, ob_ref, ocf_ref, ocb_ref, g_ref, x_ref, wa_ref, wc_ref,
                   wo_ref, lg_ref, lb_ref, xo_ref, xb_ref):
    lane = _iota2((MIX_TM, LANES), 1)
    h0 = lane < HEAD_DIM
    parts = []
    for j in range(A_WIDTH // LANES):
        sl = slice(j * LANES, (j + 1) * LANES)
        o = oaf_ref[:, sl].astype(F32) + oab_ref[:, sl].astype(F32)
        ms = _head_sum(o * o, h0) * (1.0 / HEAD_DIM)
        z = z_ref[:, sl].astype(F32)
        parts.append((o * lax.rsqrt(ms + RMS_EPS) * wa_ref[...] * (z * _sigmoid(z))).astype(BF16))
    parts.append(ob_ref[...])
    for j in range(C_WIDTH // LANES):
        sl = slice(j * LANES, (j + 1) * LANES)
        o = ocf_ref[:, sl].astype(F32) + ocb_ref[:, sl].astype(F32)
        ms = _head_sum(o * o, h0) * (1.0 / HEAD_DIM)
        parts.append((o * lax.rsqrt(ms + RMS_EPS) * wc_ref[...]
                      * _sigmoid(g_ref[:, sl].astype(F32))).astype(BF16))
    mixed = jnp.concatenate(parts, axis=1)
    h = jnp.dot(mixed, wo_ref[...], preferred_element_type=F32)
    y = _layer_norm_rows(DN_ALPHA * x_ref[...] + h, lg_ref[...], lb_ref[...])
    xo_ref[...] = y
    xb_ref[...] = y.astype(BF16)


def mixer_output(oa_f, oa_b, proj2, ob, oc_f, oc_b, x2, gdn_w, hgrn_w, w_out, ln_g, ln_b):
    m = x2.shape[0]
    row = lambda w: pl.BlockSpec((MIX_TM, w), lambda i: (i, 0))
    const = lambda r, c: pl.BlockSpec((r, c), lambda i: (0, 0))
    return pl.pallas_call(
        _mixout_kernel,
        out_shape=(jax.ShapeDtypeStruct((m, D_MODEL), F32), jax.ShapeDtypeStruct((m, D_MODEL), BF16)),
        grid=(m // MIX_TM,),
        in_specs=[row(A_WIDTH), row(A_WIDTH),
                  pl.BlockSpec((MIX_TM, A_WIDTH), lambda i: (i, COL_A_Z // A_WIDTH)),
                  row(B_WIDTH), row(C_WIDTH), row(C_WIDTH),
                  pl.BlockSpec((MIX_TM, C_WIDTH), lambda i: (i, COL_C_G // C_WIDTH)),
                  row(D_MODEL), const(1, LANES), const(1, LANES),
                  const(D_MIX, D_MODEL), const(1, D_MODEL), const(1, D_MODEL)],
        out_specs=(row(D_MODEL), row(D_MODEL)),
        compiler_params=_cparams(("parallel",)),
        name="mixer_output",
    )(oa_f, oa_b, proj2, ob, oc_f, oc_b, proj2, x2, gdn_w, hgrn_w, w_out, ln_g, ln_b)


XATT_TM = 512


def _xattn_kernel(xb_ref, x_ref, k_ref, v_ref, wq_ref, wo_ref, lg_ref, lb_ref, xo_ref, xob_ref):
    q = jnp.dot(xb_ref[0], wq_ref[...], preferred_element_type=F32) * (X_HEAD_DIM ** -0.5)
    outs = []
    for h in range(X_HEADS):
        sl = slice(h * X_HEAD_DIM, (h + 1) * X_HEAD_DIM)
        s = _dot_nt(q[:, sl], k_ref[0, :, sl])
        m = jnp.max(s, axis=-1, keepdims=True)
        e = jnp.exp(s - m)
        l = jnp.sum(e, axis=-1, keepdims=True)
        outs.append((jnp.dot(e.astype(BF16), v_ref[0, :, sl], preferred_element_type=F32) / l).astype(BF16))
    o = jnp.concatenate(outs, axis=1)
    c = jnp.dot(o, wo_ref[...], preferred_element_type=F32)
    y = _layer_norm_rows(DN_ALPHA * x_ref[0] + c, lg_ref[...], lb_ref[...])
    xo_ref[0] = y
    xob_ref[0] = y.astype(BF16)


def cross_attention(xb3, x3, k3, v3, wq, wo, ln_g, ln_b):
    b, s, _ = x3.shape
    mem = k3.shape[1]
    row = pl.BlockSpec((1, XATT_TM, D_MODEL), lambda i, t: (i, t, 0))
    const = lambda r, c: pl.BlockSpec((r, c), lambda i, t: (0, 0))
    kv = pl.BlockSpec((1, mem, D_MODEL), lambda i, t: (i, 0, 0))
    return pl.pallas_call(
        _xattn_kernel,
        out_shape=(jax.ShapeDtypeStruct((b, s, D_MODEL), F32), jax.ShapeDtypeStruct((b, s, D_MODEL), BF16)),
        grid=(b, s // XATT_TM),
        in_specs=[row, row, kv, kv, const(D_MODEL, D_MODEL), const(D_MODEL, D_MODEL),
                  const(1, D_MODEL), const(1, D_MODEL)],
        out_specs=(row, row),
        compiler_params=_cparams(("parallel", "parallel")),
        name="cross_attention",
    )(xb3, x3, k3, v3, wq, wo, ln_g, ln_b)


FFN_TM = 512
FFN_TF = 1408


def _ffn_kernel(xb_ref, x_ref, wg_ref, wu_ref, wd_ref, lg_ref, lb_ref, xo_ref, xob_ref):
    xb = xb_ref[...]
    acc = None
    for j in range(D_FF // FFN_TF):
        sl = slice(j * FFN_TF, (j + 1) * FFN_TF)
        g = jnp.dot(xb, wg_ref[:, sl], preferred_element_type=F32)
        u = jnp.dot(xb, wu_ref[:, sl], preferred_element_type=F32)
        h = (g * _sigmoid(g) * u).astype(BF16)
        part = jnp.dot(h, wd_ref[sl, :], preferred_element_type=F32)
        acc = part if acc is None else acc + part
    y = _layer_norm_rows(DN_ALPHA * x_ref[...] + acc, lg_ref[...], lb_ref[...])
    xo_ref[...] = y
    xob_ref[...] = y.astype(BF16)


def dense_ffn(xb2, x2, wg, wu, wd, ln_g, ln_b):
    m = x2.shape[0]
    row = pl.BlockSpec((FFN_TM, D_MODEL), lambda i: (i, 0))
    const = pl.BlockSpec((1, D_MODEL), lambda i: (0, 0))
    resident = lambda r, c: pl.BlockSpec((r, c), lambda i: (0, 0), pipeline_mode=pl.Buffered(1))
    return pl.pallas_call(
        _ffn_kernel,
        out_shape=(jax.ShapeDtypeStruct((m, D_MODEL), F32), jax.ShapeDtypeStruct((m, D_MODEL), BF16)),
        grid=(m // FFN_TM,),
        in_specs=[row, row, resident(D_MODEL, D_FF), resident(D_MODEL, D_FF), resident(D_FF, D_MODEL),
                  const, const],
        out_specs=(row, row),
        compiler_params=_cparams(("parallel",)),
        name="dense_ffn",
    )(xb2, x2, wg, wu, wd, ln_g, ln_b)


MOE_TM = 512
MOE_WIN = 512
MOE_TF = 1792
ROUTER_TM = 512


def _router_kernel(xb_ref, wr_ref, e_ref, w_ref, xp_ref):
    logits = jnp.dot(xb_ref[...], wr_ref[...], preferred_element_type=F32)
    lane = _iota2(logits.shape, 1)
    neg = jnp.float32(-jnp.inf)
    lane_f = lane.astype(F32)
    lg = jnp.where(lane < N_EXPERTS, logits, neg)
    m1 = jnp.max(lg, axis=-1, keepdims=True)
    i1 = jnp.min(jnp.where(lg == m1, lane_f, float(LANES)), axis=-1, keepdims=True)
    lg2 = jnp.where(lane_f == i1, neg, lg)
    m2 = jnp.max(lg2, axis=-1, keepdims=True)
    i2 = jnp.min(jnp.where(lg2 == m2, lane_f, float(LANES)), axis=-1, keepdims=True)
    t = jnp.exp(m2 - m1)
    w1 = 1.0 / (1.0 + t)
    w2 = t / (1.0 + t)
    e_ref[...] = jnp.where(lane == 0, i1, jnp.where(lane == 1, i2, 0.0)).astype(jnp.int32)
    w_ref[...] = jnp.where(lane == 0, w1, jnp.where(lane == 1, w2, 0.0))
    _store_packed(xp_ref, xb_ref[...].astype(F32))


def moe_router(xb2, w_router_p):
    m = xb2.shape[0]
    lane_out = pl.BlockSpec((ROUTER_TM, LANES), lambda i: (i, 0))
    return pl.pallas_call(
        _router_kernel,
        out_shape=(jax.ShapeDtypeStruct((m, LANES), jnp.int32), jax.ShapeDtypeStruct((m, LANES), F32),
                   jax.ShapeDtypeStruct((SC_SPLIT, m, LANES), jnp.uint32)),
        grid=(m // ROUTER_TM,),
        in_specs=[pl.BlockSpec((ROUTER_TM, D_MODEL), lambda i: (i, 0)),
                  pl.BlockSpec((D_MODEL, LANES), lambda i: (0, 0))],
        out_specs=(lane_out, lane_out, pl.BlockSpec((SC_SPLIT, ROUTER_TM, LANES), lambda i: (0, i, 0))),
        compiler_params=_cparams(("parallel",)),
        name="moe_router",
    )(xb2, w_router_p)


def _moe_ffn_kernel(be_ref, bv_ref, x_ref, wg_ref, wu_ref, wd_ref, o_ref, acc_ref):
    i = pl.program_id(0)
    j = pl.program_id(1)
    valid = bv_ref[i] != 0

    @pl.when(j == 0)
    def _():
        acc_ref[...] = jnp.zeros_like(acc_ref)

    @pl.when(valid)
    def _():
        xb = _load_packed(x_ref).astype(BF16)
        g = jnp.dot(xb, wg_ref[0], preferred_element_type=F32)
        u = jnp.dot(xb, wu_ref[0], preferred_element_type=F32)
        h = (g * _sigmoid(g) * u).astype(BF16)
        acc_ref[...] += jnp.dot(h, wd_ref[0], preferred_element_type=F32)

    @pl.when(j == pl.num_programs(1) - 1)
    def _():
        _store_packed(o_ref, acc_ref[...])


def moe_expert_ffn(block_e, block_valid, xrows, wg, wu, wd):
    n_rows = xrows.shape[1]
    grid_spec = pltpu.PrefetchScalarGridSpec(
        num_scalar_prefetch=2,
        grid=(n_rows // MOE_TM, D_FF_EXPERT // MOE_TF),
        in_specs=[pl.BlockSpec((SC_SPLIT, MOE_TM, LANES), lambda i, j, be, bv: (0, i * bv[i], 0)),
                  pl.BlockSpec((1, D_MODEL, MOE_TF), lambda i, j, be, bv: (be[i], 0, j)),
                  pl.BlockSpec((1, D_MODEL, MOE_TF), lambda i, j, be, bv: (be[i], 0, j)),
                  pl.BlockSpec((1, MOE_TF, D_MODEL), lambda i, j, be, bv: (be[i], j, 0))],
        out_specs=pl.BlockSpec((SC_SPLIT, MOE_TM, LANES), lambda i, j, be, bv: (0, i, 0)),
        scratch_shapes=[pltpu.VMEM((MOE_TM, D_MODEL), F32)],
    )
    return pl.pallas_call(
        _moe_ffn_kernel,
        out_shape=jax.ShapeDtypeStruct((SC_SPLIT, n_rows, LANES), jnp.uint32),
        grid_spec=grid_spec,
        compiler_params=_cparams(("arbitrary", "arbitrary")),
        name="moe_expert_ffn",
    )(block_e, block_valid, xrows, wg, wu, wd)


PACKED_WIDTH = D_MODEL // 2
SC_ROWS = 128
SC_SPLIT = 4


def _store_packed(o_ref, y):
    lo = lax.bitcast_convert_type(y[:, :PACKED_WIDTH].astype(BF16).astype(F32), jnp.uint32) >> 16
    hi = lax.bitcast_convert_type(y[:, PACKED_WIDTH:].astype(BF16).astype(F32), jnp.uint32)
    word = hi | lo
    for c in range(SC_SPLIT):
        o_ref[c] = word[:, c * LANES:(c + 1) * LANES]


def _load_packed(ref):
    w = jnp.concatenate([ref[c] for c in range(SC_SPLIT)], axis=1)
    lo = lax.bitcast_convert_type(w << 16, F32)
    hi = lax.bitcast_convert_type(w & jnp.uint32(0xFFFF0000), F32)
    return jnp.concatenate([lo, hi], axis=1)


def sc_gather_rows(data, idx):
    pieces, rows, width = data.shape
    idx = (jnp.arange(pieces, dtype=jnp.int32)[:, None] * rows + idx[None, :]).reshape(-1)
    return _sc_gather(data.reshape(pieces * rows, width), idx).reshape(pieces, -1, width)


def _sc_gather(data, idx):
    n = idx.shape[0]
    d = data.shape[1]
    sc = plsc.get_sparse_core_info()
    mesh = plsc.VectorSubcoreMesh(core_axis_name="core", subcore_axis_name="subcore")
    steps = n // (SC_ROWS * sc.num_cores)

    @pl.kernel(out_type=jax.ShapeDtypeStruct((n, d), data.dtype), mesh=mesh, scratch_types=[])
    def gather_kernel(x_hbm, i_hbm, o_hbm):
        def body(i_vmem, o_vmem):
            pltpu.sync_copy(x_hbm.at[i_vmem.at[0]], o_vmem)

        pltpu.emit_pipeline(
            body,
            grid=(sc.num_cores, steps),
            in_specs=[pl.BlockSpec((1, SC_ROWS), index_map=lambda c, i: (0, c * steps + i))],
            out_specs=[pl.BlockSpec((SC_ROWS, d), index_map=lambda c, i: (c * steps + i, 0))],
            core_axis_name=("core", "subcore"),
            dimension_semantics=(pltpu.PARALLEL, pltpu.PARALLEL),
        )(i_hbm, o_hbm)

    return gather_kernel(data, idx.reshape(1, n))


def sc_scatter_rows(data, idx0, idx1, n_out):
    pieces, n, width = data.shape
    off = jnp.arange(pieces, dtype=jnp.int32)[:, None] * n_out
    out = _sc_scatter(data.reshape(pieces * n, width), (off + idx0[None, :]).reshape(-1),
                      (off + idx1[None, :]).reshape(-1), pieces * n_out)
    return out.reshape(pieces, n_out, width)


def _sc_scatter(data, idx0, idx1, n_out):
    n, d = data.shape
    sc = plsc.get_sparse_core_info()
    mesh = plsc.VectorSubcoreMesh(core_axis_name="core", subcore_axis_name="subcore")
    steps = n // (SC_ROWS * sc.num_cores)
    idx_spec = pl.BlockSpec((1, SC_ROWS), index_map=lambda c, i: (0, c * steps + i))

    @pl.kernel(out_type=jax.ShapeDtypeStruct((n_out, d), data.dtype), mesh=mesh, scratch_types=[])
    def scatter_kernel(x_hbm, i0_hbm, i1_hbm, o_hbm):
        def body(x_vmem, i0_vmem, i1_vmem):
            pltpu.sync_copy(x_vmem, o_hbm.at[i0_vmem.at[0]])
            pltpu.sync_copy(x_vmem, o_hbm.at[i1_vmem.at[0]])

        pltpu.emit_pipeline(
            body,
            grid=(sc.num_cores, steps),
            in_specs=[pl.BlockSpec((SC_ROWS, d), index_map=lambda c, i: (c * steps + i, 0)), idx_spec, idx_spec],
            out_specs=[],
            core_axis_name=("core", "subcore"),
            dimension_semantics=(pltpu.PARALLEL, pltpu.PARALLEL),
        )(x_hbm, i0_hbm, i1_hbm)

    return scatter_kernel(data, idx0.reshape(1, n), idx1.reshape(1, n))


def _moe_finish_kernel(y0_ref, y1_ref, w_ref, x_ref, lg_ref, lb_ref, xo_ref):
    wt = w_ref[...]
    y = wt[:, 0:1] * _load_packed(y0_ref) + wt[:, 1:2] * _load_packed(y1_ref)
    xo_ref[...] = _layer_norm_rows(DN_ALPHA * x_ref[...] + y, lg_ref[...], lb_ref[...])


def moe_finish(ytok, top_w, x2, ln_g, ln_b):
    m = x2.shape[0]
    n_win = m // MOE_WIN
    packed = lambda off: pl.BlockSpec((SC_SPLIT, MOE_WIN, LANES), lambda i: (0, i + off, 0))
    row = pl.BlockSpec((MOE_WIN, D_MODEL), lambda i: (i, 0))
    const = pl.BlockSpec((1, D_MODEL), lambda i: (0, 0))
    return pl.pallas_call(
        _moe_finish_kernel,
        out_shape=jax.ShapeDtypeStruct((m, D_MODEL), F32),
        grid=(n_win,),
        in_specs=[packed(0), packed(n_win), pl.BlockSpec((MOE_WIN, TOP_K), lambda i: (i, 0)), row, const, const],
        out_specs=row,
        compiler_params=_cparams(("parallel",)),
        name="moe_finish",
    )(ytok, ytok, top_w, x2, ln_g, ln_b)


def moe_layer(xb2, x2, w_router, wg, wu, wd, ln_g, ln_b):
    n = x2.shape[0]
    n_rows = n * TOP_K + N_EXPERTS * MOE_TM
    n_blocks = n_rows // MOE_TM

    wr = jnp.pad(w_router, ((0, 0), (0, LANES - N_EXPERTS))).astype(BF16)
    top_e_t, top_w_t, x_packed = moe_router(xb2, wr)
    top_e = top_e_t[:, :TOP_K]
    top_w = top_w_t[:, :TOP_K]

    experts = jnp.arange(N_EXPERTS, dtype=jnp.int32)
    tok_onehot = ((top_e[:, 0:1] == experts[None, :]) | (top_e[:, 1:2] == experts[None, :])).astype(jnp.int32)
    csum = jnp.cumsum(tok_onehot, axis=0)
    counts = csum[-1]
    rank = csum - tok_onehot
    padded = (counts + MOE_TM - 1) // MOE_TM * MOE_TM
    pad_end = jnp.cumsum(padded)
    pad_start = pad_end - padded
    dest = pad_start[top_e] + jnp.take_along_axis(rank, top_e, axis=1)

    blk_row0 = jnp.arange(n_blocks, dtype=jnp.int32) * MOE_TM
    block_e = jnp.minimum(jnp.sum((pad_end[None, :] <= blk_row0[:, None]).astype(jnp.int32), axis=1),
                          N_EXPERTS - 1).astype(jnp.int32)
    block_valid = (blk_row0 < pad_end[-1]).astype(jnp.int32)

    xrows = sc_scatter_rows(x_packed, dest[:, 0], dest[:, 1], n_rows)
    yrows = moe_expert_ffn(block_e, block_valid, xrows, wg, wu, wd)

    ytok = sc_gather_rows(yrows, dest.T.reshape(-1))
    return moe_finish(ytok, top_w, x2, ln_g, ln_b)


def _w_in_perm():
    zero = 3480
    perm = np.full((D_IN_P,), zero, np.int64)
    o_qkv, o_z, o_beta, o_decay, o_bq, o_bk, o_bv, o_cf, o_ci, o_cq, o_cg = (
        0, 1152, 1536, 1548, 1560, 1944, 2072, 2200, 2712, 2968, 3224)
    perm[COL_A_QKV:COL_A_QKV + 1152] = np.arange(o_qkv, o_qkv + 1152)
    perm[COL_A_Z:COL_A_Z + 384] = np.arange(o_z, o_z + 384)
    for j in range(3):
        for a in range(2):
            dst = COL_B_Q + j * LANES + a * HEAD_DIM
            src = o_bq + (j + 3 * a) * HEAD_DIM
            perm[dst:dst + HEAD_DIM] = np.arange(src, src + HEAD_DIM)
    perm[COL_B_K:COL_B_K + 128] = np.arange(o_bk, o_bk + 128)
    perm[COL_C_G:COL_C_G + 256] = np.arange(o_cg, o_cg + 256)
    perm[COL_B_V:COL_B_V + 128] = np.arange(o_bv, o_bv + 128)
    perm[COL_C_F:COL_C_F + 512] = np.arange(o_cf, o_cf + 512)
    perm[COL_C_I:COL_C_I + 256] = np.arange(o_ci, o_ci + 256)
    perm[COL_C_Q:COL_C_Q + 256] = np.arange(o_cq, o_cq + 256)
    for d, col in enumerate((COL_GATES0, COL_GATES1)):
        for p in range(A_HEADS // 2):
            dst = col + 4 * p
            perm[dst + 0] = o_beta + d * A_HEADS + 2 * p
            perm[dst + 1] = o_beta + d * A_HEADS + 2 * p + 1
            perm[dst + 2] = o_decay + d * A_HEADS + 2 * p
            perm[dst + 3] = o_decay + d * A_HEADS + 2 * p + 1
    return perm


def _w_out_perm():
    perm = np.arange(D_MIX)
    for j in range(3):
        for a in range(2):
            dst = A_WIDTH + j * LANES + a * HEAD_DIM
            src = A_WIDTH + (j + 3 * a) * HEAD_DIM
            perm[dst:dst + HEAD_DIM] = np.arange(src, src + HEAD_DIM)
    return perm


def _gdn_params(a_log, dt_bias):
    rows = np.array([4 * (h // 2) + 2 + h % 2 for h in range(A_HEADS)])
    prm = jnp.zeros((2, GATE_ROWS, LANES), F32)
    prm = prm.at[:, rows, 0].set(a_log.astype(F32))
    prm = prm.at[:, rows, 1].set(dt_bias.astype(F32))
    return prm.at[:, rows, 2].set(1.0)


def kernel(x, mem, w_in, conv_w, gdn_a_log, gdn_dt_bias, gdn_norm_w, q_norm_w, k_norm_w, hgrn_lb_logits,
           hgrn_norm_w, w_out, ln1_g, ln1_b, xq, xk, xv, xo, ln2_g, ln2_b, ffn_wg, ffn_wu, ffn_wd,
           moe_router, moe_wg, moe_wu, moe_wd, ln3_g, ln3_b):
    b, s, d = x.shape
    m = b * s
    cos_t, sin_t = rope_tables(s)
    lb_p = jax.nn.softmax(hgrn_lb_logits.astype(F32), axis=0)
    lb_c = jnp.cumsum(lb_p, axis=0)
    lower_bounds = lb_c - lb_c[0]
    in_perm = _w_in_perm()
    out_perm = _w_out_perm()
    mem_b = mem.astype(BF16).reshape(b * mem.shape[1], d)
    tile2 = lambda w: jnp.tile(w.astype(F32), 2)[None, :]
    row = lambda v: v.astype(F32)[None, :]

    x2 = x.reshape(m, d)
    xb2 = x2.astype(BF16)
    for l in range(DEPTH):
        w_in_p = jnp.take(jnp.pad(w_in[l].astype(BF16), ((0, 0), (0, 1))), in_perm, axis=1)
        wide2 = matmul(xb2, w_in_p[:, :D_WIDE], BF16, min(1024, m), 768)
        gates2 = matmul(xb2, w_in_p[:, D_WIDE:], F32, min(1024, m), D_IN_P - D_WIDE)
        wide3 = wide2.reshape(b, s, D_WIDE)
        gates3 = gates2.reshape(b, s, D_IN_P - D_WIDE)

        oa_f, oa_b = gdn_scan(wide3, gates3, conv_w[l], _gdn_params(gdn_a_log[l], gdn_dt_bias[l]))
        ob = gqa_attention(wide3, cos_t, sin_t, tile2(q_norm_w[l]), tile2(k_norm_w[l]), s)
        lb = lower_bounds[l][None, :]
        oc_f, oc_b = hgrn_scan(wide3, gates3, lb)

        w_out_p = jnp.take(w_out[l], out_perm, axis=0).astype(BF16)
        x2, xb2 = mixer_output(oa_f.reshape(m, A_WIDTH), oa_b.reshape(m, A_WIDTH), wide2,
                               ob.reshape(m, B_WIDTH), oc_f.reshape(m, C_WIDTH), oc_b.reshape(m, C_WIDTH),
                               x2, tile2(gdn_norm_w[l]), tile2(hgrn_norm_w[l]), w_out_p,
                               row(ln1_g[l]), row(ln1_b[l]))

        k3 = matmul(mem_b, xk[l].astype(BF16), BF16, 256, 512).reshape(b, -1, d)
        v3 = matmul(mem_b, xv[l].astype(BF16), BF16, 256, 512).reshape(b, -1, d)
        x3, xb3 = cross_attention(xb2.reshape(b, s, d), x2.reshape(b, s, d), k3, v3,
                                  xq[l].astype(BF16), xo[l].astype(BF16), row(ln2_g[l]), row(ln2_b[l]))
        x2, xb2 = x3.reshape(m, d), xb3.reshape(m, d)

        if l % 2 == 0:
            x2, xb2 = dense_ffn(xb2, x2, ffn_wg[l // 2].astype(BF16), ffn_wu[l // 2].astype(BF16),
                                ffn_wd[l // 2].astype(BF16), row(ln3_g[l]), row(ln3_b[l]))
        else:
            x2 = moe_layer(xb2, x2, moe_router[l // 2], moe_wg[l // 2].astype(BF16),
                           moe_wu[l // 2].astype(BF16), moe_wd[l // 2].astype(BF16),
                           row(ln3_g[l]), row(ln3_b[l]))
            xb2 = x2.astype(BF16)
    return x2.reshape(b, s, d)
```

```python
import math

import numpy as np
import jax
import jax.numpy as jnp
from jax import lax
from jax.experimental import pallas as pl
from jax.experimental.pallas import tpu as pltpu
from jax.experimental.pallas import tpu_sc as plsc

F32 = jnp.float32
BF16 = jnp.bfloat16

D_MODEL = 1024
DEPTH = 2
HEAD_DIM = 64
A_HEADS = 6
A_WIDTH = A_HEADS * HEAD_DIM
B_Q_HEADS = 6
B_KV_HEADS = 2
B_WIDTH = B_Q_HEADS * HEAD_DIM
C_HEADS = 4
C_WIDTH = C_HEADS * HEAD_DIM
D_MIX = A_WIDTH + B_WIDTH + C_WIDTH
CONV_K = 5
CHUNK = 64
GRID_W = 64
ROPE_AXIS_DIM = HEAD_DIM // 2
ROPE_THETA = 10000.0
X_HEADS = 4
X_HEAD_DIM = D_MODEL // X_HEADS
D_FF = 2816
N_EXPERTS = 8
TOP_K = 2
D_FF_EXPERT = 3584
DN_ALPHA = (2 * DEPTH) ** 0.25
LN_EPS = 1e-5
RMS_EPS = 1e-6

LANES = 128
SUBLANES = 8
HALO = 16
GROUP = 256
CHUNKS_PER_GROUP = GROUP // CHUNK
VMEM_LIMIT = 56 * 1024 * 1024

COL_A_QKV = 0
COL_A_Z = 1152
COL_B_Q = 1536
COL_B_K = 1920
COL_C_G = 2048
COL_B_V = 2304
COL_C_I = 2560
COL_C_Q = 2816
D_WIDE = 3072
COL_C_F = D_WIDE
COL_GATES0 = D_WIDE + 512
COL_GATES1 = D_WIDE + 640
D_IN_P = 3840


def _cparams(sem):
    return pltpu.CompilerParams(dimension_semantics=sem, vmem_limit_bytes=VMEM_LIMIT)


def _sigmoid(x):
    return 1.0 / (1.0 + jnp.exp(-x))


def _softplus(x):
    return jnp.maximum(x, 0.0) + jnp.log1p(jnp.exp(-jnp.abs(x)))


def _dot(a, b):
    return jnp.dot(a.astype(BF16), b.astype(BF16), preferred_element_type=F32)


def _dot_nt(a, b):
    return lax.dot_general(a.astype(BF16), b.astype(BF16), (((1,), (1,)), ((), ())),
                           preferred_element_type=F32)


def _dot_sel(m01, x):
    m = jnp.where(m01, 1.0, 0.0).astype(BF16)
    hi = x.astype(BF16)
    r1 = x - hi.astype(F32)
    mid = r1.astype(BF16)
    lo = (r1 - mid.astype(F32)).astype(BF16)
    out = jnp.dot(m, hi, preferred_element_type=F32)
    out = out + jnp.dot(m, mid, preferred_element_type=F32)
    return out + jnp.dot(m, lo, preferred_element_type=F32)


def _dot_sel_right(x, m01):
    m = jnp.where(m01, 1.0, 0.0).astype(BF16)
    hi = x.astype(BF16)
    r1 = x - hi.astype(F32)
    mid = r1.astype(BF16)
    lo = (r1 - mid.astype(F32)).astype(BF16)
    out = jnp.dot(hi, m, preferred_element_type=F32)
    out = out + jnp.dot(mid, m, preferred_element_type=F32)
    return out + jnp.dot(lo, m, preferred_element_type=F32)


def _layer_norm_rows(y, g, b):
    mu = jnp.mean(y, axis=-1, keepdims=True)
    yc = y - mu
    var = jnp.mean(yc * yc, axis=-1, keepdims=True)
    return yc * lax.rsqrt(var + LN_EPS) * g + b


def _iota2(shape, dim):
    return lax.broadcasted_iota(jnp.int32, shape, dim)


def _head_sum(x, first_head):
    s0 = jnp.sum(jnp.where(first_head, x, 0.0), axis=-1, keepdims=True)
    s1 = jnp.sum(jnp.where(first_head, 0.0, x), axis=-1, keepdims=True)
    return jnp.where(first_head, s0, s1)


def _block_rows(t, block, row_in_block):
    parts = [jnp.broadcast_to(t[b * block + row_in_block:b * block + row_in_block + 1, :], (block, t.shape[1]))
             for b in range(GROUP // block)]
    return jnp.concatenate(parts, axis=0)


def _mm_kernel(x_ref, w_ref, o_ref):
    o_ref[...] = jnp.dot(x_ref[...], w_ref[...], preferred_element_type=F32).astype(o_ref.dtype)


def matmul(x, w, out_dtype, tm, tn):
    m, k = x.shape
    n = w.shape[1]
    return pl.pallas_call(
        _mm_kernel,
        out_shape=jax.ShapeDtypeStruct((m, n), out_dtype),
        grid=(n // tn, m // tm),
        in_specs=[pl.BlockSpec((tm, k), lambda j, i: (i, 0)),
                  pl.BlockSpec((k, tn), lambda j, i: (0, j))],
        out_specs=pl.BlockSpec((tm, tn), lambda j, i: (i, j)),
        compiler_params=_cparams(("parallel", "arbitrary")),
        name="matmul",
    )(x, w)


def _scan_masks(reverse):
    ri = _iota2((GROUP, GROUP), 0)
    ci = _iota2((GROUP, GROUP), 1)
    same_chunk = (ri >> 6) == (ci >> 6)
    if reverse:
        return ri, ci, same_chunk, same_chunk & (ri <= ci), same_chunk & (ri < ci)
    return ri, ci, same_chunk, same_chunk & (ri >= ci), same_chunk & (ri > ci)


GATE_ROWS = 16


class _GdnProblem:
    def __init__(self, **kw):
        self.__dict__.update(kw)


def _gdn_group(probs, same_chunk, h0):
    nh0 = jnp.logical_not(h0)
    sr = _iota2((CHUNK, GROUP), 0)
    sc = _iota2((CHUNK, GROUP), 1)
    eye_side = jnp.where(sr == (sc & (CHUNK - 1)), 1.0, 0.0)
    tl = _iota2((LANES, GROUP), 1)
    br = _iota2((LANES, LANES), 0)
    bc = _iota2((LANES, LANES), 1)
    bd128 = (br >> 6) == (bc >> 6)

    def block_diag(side):
        return jnp.where(same_chunk, jnp.concatenate([side] * CHUNKS_PER_GROUP, axis=0), 0.0).astype(BF16)

    for pr in probs:
        pr.gedge = _block_rows(pr.gc, CHUNK, 0 if pr.reverse else CHUNK - 1)
        pr.exp_g = jnp.exp(pr.gc)
        pr.kb = pr.k * pr.beta_t
        pr.rhs = jnp.concatenate([pr.v * pr.beta_t, pr.kb * pr.exp_g], axis=1).astype(BF16)
        pr.k16 = pr.k.astype(BF16)
        pr.decay, pr.x, pr.pw, pr.bd = [], [], [], []
        for h in range(2):
            decay = jnp.exp(jnp.where(pr.incl, pr.g_col[h] - pr.g_row[h], -1e30))
            kk = _dot_nt(jnp.where(h0 if h == 0 else nh0, pr.kb, 0.0), pr.k16)
            low = jnp.where(pr.strict, kk * decay, 0.0)
            a = -(low[0:CHUNK] + low[CHUNK:2 * CHUNK] + low[2 * CHUNK:3 * CHUNK] + low[3 * CHUNK:4 * CHUNK])
            pr.decay.append(decay)
            pr.x.append(eye_side + a)
            pr.pw.append(a)
            pr.bd.append(block_diag(a))

    for _ in range(5):
        for pr in probs:
            for h in range(2):
                pr.pw[h] = jnp.dot(pr.pw[h].astype(BF16), pr.bd[h], preferred_element_type=F32)
        for pr in probs:
            for h in range(2):
                pr.bd[h] = block_diag(pr.pw[h])
        for pr in probs:
            for h in range(2):
                pr.x[h] = pr.x[h] + jnp.dot(pr.x[h].astype(BF16), pr.bd[h], preferred_element_type=F32)

    for pr in probs:
        sols = [jnp.dot(block_diag(pr.x[h]), pr.rhs, preferred_element_type=F32) for h in range(2)]
        u = jnp.where(h0, sols[0][:, :LANES], sols[1][:, :LANES])
        w = jnp.where(h0, sols[0][:, LANES:], sols[1][:, LANES:])
        wu = jnp.concatenate([w, u], axis=1)
        wu16 = wu.astype(BF16)
        aw = []
        for h in range(2):
            attn = jnp.where(pr.incl, _dot_nt(jnp.where(h0 if h == 0 else nh0, pr.q, 0.0), pr.k16) * pr.decay[h], 0.0)
            aw.append(jnp.dot(attn.astype(BF16), wu16, preferred_element_type=F32))
        pr.qp = (pr.q * pr.exp_g - jnp.where(h0, aw[0][:, :LANES], aw[1][:, :LANES])).astype(BF16)
        pr.op = jnp.where(h0, aw[0][:, LANES:], aw[1][:, LANES:])
        kd_t = (pr.k * jnp.exp(pr.gedge - pr.gc)).T
        pr.kmat, pr.nmat = [], []
        for c in range(CHUNKS_PER_GROUP):
            km = jnp.dot(jnp.where((tl >> 6) == c, kd_t, 0.0), wu, preferred_element_type=F32)
            pr.kmat.append(jnp.where(bd128, km[:, :LANES], 0.0).astype(BF16))
            pr.nmat.append(jnp.where(bd128, km[:, LANES:], 0.0))
        pr.out = [None] * CHUNKS_PER_GROUP

    for t in range(CHUNKS_PER_GROUP):
        for pr in probs:
            c = CHUNKS_PER_GROUP - 1 - t if pr.reverse else t
            rs = slice(c * CHUNK, (c + 1) * CHUNK)
            s16 = pr.state.astype(BF16)
            pr.out[c] = jnp.dot(pr.qp[rs], s16, preferred_element_type=F32) + pr.op[rs]
            cd = jnp.exp(pr.gedge[c * CHUNK:c * CHUNK + 1, :])
            pr.state = pr.state * cd - jnp.dot(pr.kmat[c], s16, preferred_element_type=F32) + pr.nmat[c]
    return [jnp.concatenate(pr.out, axis=0) for pr in probs]


def _gdn_kernel(xf_ref, xfp_ref, xfn_ref, xb_ref, xbp_ref, xbn_ref, cw_ref, gf_ref, gb_ref, prm_ref,
                of_ref, ob_ref, s_scr, c_scr):
    n = pl.program_id(1)
    n_groups = pl.num_programs(1)

    @pl.when(n == 0)
    def _():
        s_scr[...] = jnp.zeros_like(s_scr)

    lane = _iota2((GROUP, LANES), 1)
    h0 = lane < HEAD_DIM
    ri = _iota2((GROUP, GROUP), 0)
    ci = _iota2((GROUP, GROUP), 1)
    same_chunk = (ri >> 6) == (ci >> 6)
    cw = cw_ref[...]
    base = SUBLANES - CONV_K // 2
    pairs = A_HEADS // 2

    dirs = ((n, xf_ref, xfp_ref, xfn_ref, gf_ref), (n_groups - 1 - n, xb_ref, xbp_ref, xbn_ref, gb_ref))

    @pl.when(2 * n < n_groups)
    def _():
        for gi, x_ref, xp_ref, xn_ref, _ in dirs:
            prev = jnp.where(gi > 0, xp_ref[0].astype(F32)[HALO - SUBLANES:], 0.0)
            nxt = jnp.where(gi < n_groups - 1, xn_ref[0].astype(F32)[:SUBLANES], 0.0)
            ext = jnp.concatenate([prev, x_ref[0].astype(F32), nxt], axis=0)
            acc = None
            for t in range(CONV_K):
                shift = (CONV_K // 2 - t) % ext.shape[0]
                tap = (pltpu.roll(ext, shift, 0) if shift else ext)[SUBLANES:SUBLANES + GROUP, :] * cw[t:t + 1, :]
                acc = tap if acc is None else acc + tap
            act = acc * _sigmoid(acc)
            rows = pl.ds(pl.multiple_of(gi * GROUP, GROUP), GROUP)
            for p in range(pairs):
                q_sl = slice(p * LANES, (p + 1) * LANES)
                k_sl = slice(A_WIDTH + p * LANES, A_WIDTH + (p + 1) * LANES)
                q = act[:, q_sl]
                k = act[:, k_sl]
                c_scr[rows, q_sl] = q * lax.rsqrt(_head_sum(q * q, h0) + RMS_EPS) * (HEAD_DIM ** -0.5)
                c_scr[rows, k_sl] = k * lax.rsqrt(_head_sum(k * k, h0) + RMS_EPS)
            c_scr[rows, 2 * A_WIDTH:] = act[:, 2 * A_WIDTH:]

    probs = []
    for d, (gi, _, _, _, g_ref) in enumerate(dirs):
        reverse = d == 1
        lower, upper = same_chunk & (ri >= ci), same_chunk & (ri <= ci)
        incl = upper if reverse else lower
        strict = same_chunk & ((ri < ci) if reverse else (ri > ci))
        qkv = c_scr[pl.ds(pl.multiple_of(gi * GROUP, GROUP), GROUP), :]

        gt_t = g_ref[0].T[0:GATE_ROWS, :]
        prm = prm_ref[d]
        e_rows = jnp.where(prm[:, 2:3] > 0.5,
                           -jnp.exp(prm[:, 0:1]) * _softplus(gt_t + prm[:, 1:2]), _sigmoid(gt_t))
        g_rows = _dot_sel_right(e_rows, lower if reverse else upper)
        slab = jnp.concatenate([e_rows, g_rows, jnp.zeros((LANES - 2 * GATE_ROWS, GROUP), F32)], axis=0)
        cols = slab.T
        for p in range(pairs):
            sl = lambda part: slice(part * A_WIDTH + p * LANES, part * A_WIDTH + (p + 1) * LANES)
            col = lambda r: cols[:, r:r + 1]
            probs.append(_GdnProblem(
                reverse=reverse, incl=incl, strict=strict, slot=d * pairs + p,
                q=qkv[:, sl(0)], k=qkv[:, sl(1)], v=qkv[:, sl(2)],
                beta_t=jnp.where(h0, col(4 * p), col(4 * p + 1)),
                gc=jnp.where(h0, col(GATE_ROWS + 4 * p + 2), col(GATE_ROWS + 4 * p + 3)),
                g_col=(col(GATE_ROWS + 4 * p + 2), col(GATE_ROWS + 4 * p + 3)),
                g_row=(g_rows[4 * p + 2:4 * p + 3, :], g_rows[4 * p + 3:4 * p + 4, :]),
                state=s_scr[d * pairs + p]))

    outs = _gdn_group(probs, same_chunk, h0)
    for pr, out in zip(probs, outs):
        o_ref = ob_ref if pr.reverse else of_ref
        p = pr.slot % pairs
        o_ref[0, :, p * LANES:(p + 1) * LANES] = out.astype(BF16)
        s_scr[pr.slot] = pr.state


def gdn_scan(wide3, gates3, conv_w, prm):
    b, seq_len, _ = wide3.shape
    n_groups = seq_len // GROUP
    halo_per_group = GROUP // HALO
    n_halo = seq_len // HALO
    width = 3 * A_WIDTH
    fwd = lambda n: n
    bwd = lambda n: n_groups - 1 - n

    def x_specs(gidx):
        return [pl.BlockSpec((1, GROUP, width), lambda i, n: (i, gidx(n), 0)),
                pl.BlockSpec((1, HALO, width),
                             lambda i, n: (i, jnp.maximum(gidx(n) * halo_per_group - 1, 0), 0)),
                pl.BlockSpec((1, HALO, width),
                             lambda i, n: (i, jnp.minimum((gidx(n) + 1) * halo_per_group, n_halo - 1), 0))]

    out_sds = jax.ShapeDtypeStruct((b, seq_len, A_WIDTH), BF16)
    return pl.pallas_call(
        _gdn_kernel,
        out_shape=(out_sds, out_sds),
        grid=(b, n_groups),
        in_specs=x_specs(fwd) + x_specs(bwd) + [
            pl.BlockSpec((CONV_K, width), lambda i, n: (0, 0)),
            pl.BlockSpec((1, GROUP, LANES), lambda i, n: (i, fwd(n), (COL_GATES0 - D_WIDE) // LANES)),
            pl.BlockSpec((1, GROUP, LANES), lambda i, n: (i, bwd(n), (COL_GATES1 - D_WIDE) // LANES)),
            pl.BlockSpec((2, GATE_ROWS, LANES), lambda i, n: (0, 0, 0))],
        out_specs=(pl.BlockSpec((1, GROUP, A_WIDTH), lambda i, n: (i, fwd(n), 0)),
                   pl.BlockSpec((1, GROUP, A_WIDTH), lambda i, n: (i, bwd(n), 0))),
        scratch_shapes=[pltpu.VMEM((A_HEADS, LANES, LANES), F32), pltpu.VMEM((seq_len, width), F32)],
        compiler_params=_cparams(("parallel", "arbitrary")),
        name="gdn_scan",
    )(wide3, wide3, wide3, wide3, wide3, wide3, conv_w, gates3, gates3, prm)


EXP_CLAMP = 60.0


class _HgrnProblem:
    def __init__(self, **kw):
        self.__dict__.update(kw)


def _hgrn_group(probs):
    lane = _iota2((GROUP, LANES), 1)
    h0 = lane < HEAD_DIM
    nh0 = jnp.logical_not(h0)
    tl = _iota2((LANES, GROUP), 1)
    br = _iota2((LANES, LANES), 0)
    bc = _iota2((LANES, LANES), 1)
    bd128 = (br >> 6) == (bc >> 6)

    for pr in probs:
        ri, ci, same_chunk, incl, _ = pr.masks
        f = pr.lb + (1.0 - pr.lb) * _sigmoid(pr.f_raw)
        kk = 1.0 - f
        q = pr.q_raw * _sigmoid(pr.q_raw) * (HEAD_DIM ** -0.5)
        cum = _dot_sel(incl, jnp.log(f))
        pr.cedge = _block_rows(cum, CHUNK, 0 if pr.reverse else CHUNK - 1)
        pr.qd = q * jnp.exp(cum)
        pr.kd = kk * jnp.exp(pr.cedge - cum)
        if pr.reverse:
            m32 = same_chunk & ((ri & 63) < 32) & ((ci & 63) >= 32)
            m16 = ((ri >> 5) == (ci >> 5)) & ((ri & 31) < 16) & ((ci & 31) >= 16)
            mdg = ((ri >> 4) == (ci >> 4)) & (ri <= ci)
            ref32, ref16, refdg = _block_rows(cum, 64, 32), _block_rows(cum, 32, 16), _block_rows(cum, 16, 8)
        else:
            m32 = same_chunk & ((ri & 63) >= 32) & ((ci & 63) < 32)
            m16 = ((ri >> 5) == (ci >> 5)) & ((ri & 31) >= 16) & ((ci & 31) < 16)
            mdg = ((ri >> 4) == (ci >> 4)) & (ri >= ci)
            ref32, ref16, refdg = _block_rows(cum, 64, 31), _block_rows(cum, 32, 15), _block_rows(cum, 16, 7)
        pr.level_masks = (m32, m16, mdg)
        pr.qs = (q * jnp.exp(jnp.minimum(cum - ref32, 0.0)),
                 q * jnp.exp(jnp.minimum(cum - ref16, 0.0)),
                 q * jnp.exp(jnp.clip(cum - refdg, -EXP_CLAMP, EXP_CLAMP)))
        pr.ks = ((kk * jnp.exp(jnp.minimum(ref32 - cum, 0.0))).astype(BF16),
                 (kk * jnp.exp(jnp.minimum(ref16 - cum, 0.0))).astype(BF16),
                 (kk * jnp.exp(jnp.clip(refdg - cum, -EXP_CLAMP, EXP_CLAMP))).astype(BF16))

    for pr in probs:
        v16 = pr.v.astype(BF16)
        m32, m16, mdg = pr.level_masks
        intra = []
        for h in range(2):
            mh = h0 if h == 0 else nh0
            a = jnp.where(m32, _dot_nt(jnp.where(mh, pr.qs[0], 0.0), pr.ks[0]),
                          jnp.where(m16, _dot_nt(jnp.where(mh, pr.qs[1], 0.0), pr.ks[1]),
                                    jnp.where(mdg, _dot_nt(jnp.where(mh, pr.qs[2], 0.0), pr.ks[2]), 0.0)))
            intra.append(jnp.dot(a.astype(BF16), v16, preferred_element_type=F32))
        pr.intra = jnp.where(h0, intra[0], intra[1])
        v_t = pr.v.T
        pr.upd = [jnp.where(bd128, jnp.dot(jnp.where((tl >> 6) == c, v_t, 0.0), pr.kd,
                                           preferred_element_type=F32), 0.0)
                  for c in range(CHUNKS_PER_GROUP)]
        pr.out = [None] * CHUNKS_PER_GROUP

    for t in range(CHUNKS_PER_GROUP):
        for pr in probs:
            c = CHUNKS_PER_GROUP - 1 - t if pr.reverse else t
            rs = slice(c * CHUNK, (c + 1) * CHUNK)
            pr.out[c] = _dot_nt(pr.qd[rs], pr.state_t) + pr.intra[rs]
            cd = jnp.exp(pr.cedge[c * CHUNK:c * CHUNK + 1, :])
            pr.state_t = pr.state_t * cd + pr.upd[c]
    return [jnp.concatenate(pr.out, axis=0) for pr in probs]


def _hgrn_kernel(ff_ref, fb_ref, if_ref, ib_ref, qf_ref, qb_ref, lb_ref, of_ref, ob_ref, s_scr):
    n = pl.program_id(1)

    @pl.when(n == 0)
    def _():
        s_scr[...] = jnp.zeros_like(s_scr)

    pairs = C_HEADS // 2
    probs = []
    for d, (f_ref, i_ref, q_ref) in enumerate(((ff_ref, if_ref, qf_ref), (fb_ref, ib_ref, qb_ref))):
        masks = _scan_masks(d == 1)
        for p in range(pairs):
            sl = slice(p * LANES, (p + 1) * LANES)
            probs.append(_HgrnProblem(reverse=d == 1, masks=masks, slot=d * pairs + p, lb=lb_ref[:, sl],
                                      f_raw=f_ref[0, :, sl], v=i_ref[0, :, sl].astype(F32),
                                      q_raw=q_ref[0, :, sl].astype(F32), state_t=s_scr[d * pairs + p]))
    outs = _hgrn_group(probs)
    for pr, out in zip(probs, outs):
        o_ref = ob_ref if pr.reverse else of_ref
        p = pr.slot % pairs
        o_ref[0, :, p * LANES:(p + 1) * LANES] = out.astype(BF16)
        s_scr[pr.slot] = pr.state_t


def hgrn_scan(wide3, gates3, lb):
    b, seq_len, _ = wide3.shape
    n_groups = seq_len // GROUP
    fwd = lambda n: n
    bwd = lambda n: n_groups - 1 - n
    spec = lambda gidx, blk: pl.BlockSpec((1, GROUP, C_WIDTH), lambda i, n: (i, gidx(n), blk))
    f_blk = (COL_C_F - D_WIDE) // C_WIDTH
    out_sds = jax.ShapeDtypeStruct((b, seq_len, C_WIDTH), BF16)
    return pl.pallas_call(
        _hgrn_kernel,
        out_shape=(out_sds, out_sds),
        grid=(b, n_groups),
        in_specs=[spec(fwd, f_blk), spec(bwd, f_blk + 1),
                  spec(fwd, COL_C_I // C_WIDTH), spec(bwd, COL_C_I // C_WIDTH),
                  spec(fwd, COL_C_Q // C_WIDTH), spec(bwd, COL_C_Q // C_WIDTH),
                  pl.BlockSpec((1, C_WIDTH), lambda i, n: (0, 0))],
        out_specs=(spec(fwd, 0), spec(bwd, 0)),
        scratch_shapes=[pltpu.VMEM((C_HEADS, LANES, LANES), F32)],
        compiler_params=_cparams(("parallel", "arbitrary")),
        name="hgrn_scan",
    )(gates3, gates3, wide3, wide3, wide3, wide3, lb)


ATT_TQ = 512


def _norm_rope(x, w, cos, sin_signed, first_half16):
    lane = _iota2(x.shape, 1)
    h0 = lane < HEAD_DIM
    ms = _head_sum(x * x, h0) * (1.0 / HEAD_DIM)
    xn = x * lax.rsqrt(ms + RMS_EPS) * w
    partner = jnp.where(first_half16, pltpu.roll(xn, LANES - 16, 1), pltpu.roll(xn, 16, 1))
    return xn * cos + partner * sin_signed


def _attn_kernel(q_ref, k_ref, v_ref, cosq_ref, sinq_ref, cosk_ref, sink_ref, qw_ref, kw_ref, o_ref,
                 k_scr, v_scr):
    qi = pl.program_id(1)

    @pl.when(qi == 0)
    def _():
        kx = k_ref[0].astype(F32)
        lane_k = _iota2(kx.shape, 1)
        k_scr[...] = _norm_rope(kx, kw_ref[...], cosk_ref[...], sink_ref[...], (lane_k & 31) < 16).astype(BF16)
        v_scr[:, :LANES] = v_ref[0]
        v_scr[:, LANES:] = jnp.ones((v_scr.shape[0], LANES), BF16)

    lane = _iota2((ATT_TQ, LANES), 1)
    h0 = lane < HEAD_DIM
    fh = (lane & 31) < 16
    cos = cosq_ref[...]
    sin = sinq_ref[...]
    kmat = k_scr[...]
    vmat = v_scr[...]
    q_scale = (HEAD_DIM ** -0.5) * math.log2(math.e)
    for j in range(B_Q_HEADS // 2):
        qn = _norm_rope(q_ref[0, :, j * LANES:(j + 1) * LANES].astype(F32), qw_ref[...], cos, sin, fh) * q_scale
        outs = []
        for half in range(2):
            mh = h0 if half == 0 else jnp.logical_not(h0)
            s = _dot_nt(jnp.where(mh, qn, 0.0), kmat)
            m = jnp.max(s, axis=-1, keepdims=True)
            e = jnp.exp2((s - m).astype(BF16))
            pvl = jnp.dot(e, vmat, preferred_element_type=F32)
            outs.append(pvl[:, :LANES] / pvl[:, LANES:LANES + 1])
        o_ref[0, :, j * LANES:(j + 1) * LANES] = jnp.where(h0, outs[0], outs[1]).astype(BF16)


def gqa_attention(proj, cos_t, sin_t, q_w, k_w, seq_len):
    b = proj.shape[0]
    return pl.pallas_call(
        _attn_kernel,
        out_shape=jax.ShapeDtypeStruct((b, seq_len, B_WIDTH), BF16),
        grid=(b, seq_len // ATT_TQ),
        in_specs=[pl.BlockSpec((1, ATT_TQ, B_WIDTH), lambda i, t: (i, t, COL_B_Q // B_WIDTH)),
                  pl.BlockSpec((1, seq_len, LANES), lambda i, t: (i, 0, COL_B_K // LANES)),
                  pl.BlockSpec((1, seq_len, LANES), lambda i, t: (i, 0, COL_B_V // LANES)),
                  pl.BlockSpec((ATT_TQ, LANES), lambda i, t: (t, 0)),
                  pl.BlockSpec((ATT_TQ, LANES), lambda i, t: (t, 0)),
                  pl.BlockSpec((seq_len, LANES), lambda i, t: (0, 0)),
                  pl.BlockSpec((seq_len, LANES), lambda i, t: (0, 0)),
                  pl.BlockSpec((1, LANES), lambda i, t: (0, 0)),
                  pl.BlockSpec((1, LANES), lambda i, t: (0, 0))],
        out_specs=pl.BlockSpec((1, ATT_TQ, B_WIDTH), lambda i, t: (i, t, 0)),
        scratch_shapes=[pltpu.VMEM((seq_len, LANES), BF16), pltpu.VMEM((seq_len, 2 * LANES), BF16)],
        compiler_params=_cparams(("parallel", "arbitrary")),
        name="gqa_attention",
    )(proj, proj, proj, cos_t, sin_t, cos_t, sin_t, q_w, k_w)


def rope_tables(seq_len):
    rows = seq_len // GRID_W
    row = np.repeat(np.arange(rows, dtype=np.float64), GRID_W)
    col = np.tile(np.arange(GRID_W, dtype=np.float64), rows)
    inv_freq = ROPE_THETA ** (-np.arange(0, ROPE_AXIS_DIM, 2, dtype=np.float64) / ROPE_AXIS_DIM)
    ang_r = row[:, None] * inv_freq
    ang_c = col[:, None] * inv_freq
    cos64 = np.concatenate([np.cos(ang_r), np.cos(ang_r), np.cos(ang_c), np.cos(ang_c)], axis=1)
    sin64 = np.concatenate([-np.sin(ang_r), np.sin(ang_r), -np.sin(ang_c), np.sin(ang_c)], axis=1)
    return (jnp.asarray(np.tile(cos64, (1, 2)), dtype=F32), jnp.asarray(np.tile(sin64, (1, 2)), dtype=F32))


MIX_TM = 512


def _mixout_kernel(oaf_ref, oab_ref, z_ref, ob_ref, ocf_ref, ocb_ref, g_ref, x_ref, wa_ref, wc_ref,
                   wo_ref, lg_ref, lb_ref, xo_ref, xb_ref):
    lane = _iota2((MIX_TM, LANES), 1)
    h0 = lane < HEAD_DIM
    parts = []
    for j in range(A_WIDTH // LANES):
        sl = slice(j * LANES, (j + 1) * LANES)
        o = oaf_ref[:, sl].astype(F32) + oab_ref[:, sl].astype(F32)
        ms = _head_sum(o * o, h0) * (1.0 / HEAD_DIM)
        z = z_ref[:, sl].astype(F32)
        parts.append((o * lax.rsqrt(ms + RMS_EPS) * wa_ref[...] * (z * _sigmoid(z))).astype(BF16))
    parts.append(ob_ref[...])
    for j in range(C_WIDTH // LANES):
        sl = slice(j * LANES, (j + 1) * LANES)
        o = ocf_ref[:, sl].astype(F32) + ocb_ref[:, sl].astype(F32)
        ms = _head_sum(o * o, h0) * (1.0 / HEAD_DIM)
        parts.append((o * lax.rsqrt(ms + RMS_EPS) * wc_ref[...]
                      * _sigmoid(g_ref[:, sl].astype(F32))).astype(BF16))
    mixed = jnp.concatenate(parts, axis=1)
    h = jnp.dot(mixed, wo_ref[...], preferred_element_type=F32)
    y = _layer_norm_rows(DN_ALPHA * x_ref[...] + h, lg_ref[...], lb_ref[...])
    xo_ref[...] = y
    xb_ref[...] = y.astype(BF16)


def mixer_output(oa_f, oa_b, proj2, ob, oc_f, oc_b, x2, gdn_w, hgrn_w, w_out, ln_g, ln_b):
    m = x2.shape[0]
    row = lambda w: pl.BlockSpec((MIX_TM, w), lambda i: (i, 0))
    const = lambda r, c: pl.BlockSpec((r, c), lambda i: (0, 0))
    return pl.pallas_call(
        _mixout_kernel,
        out_shape=(jax.ShapeDtypeStruct((m, D_MODEL), F32), jax.ShapeDtypeStruct((m, D_MODEL), BF16)),
        grid=(m // MIX_TM,),
        in_specs=[row(A_WIDTH), row(A_WIDTH),
                  pl.BlockSpec((MIX_TM, A_WIDTH), lambda i: (i, COL_A_Z // A_WIDTH)),
                  row(B_WIDTH), row(C_WIDTH), row(C_WIDTH),
                  pl.BlockSpec((MIX_TM, C_WIDTH), lambda i: (i, COL_C_G // C_WIDTH)),
                  row(D_MODEL), const(1, LANES), const(1, LANES),
                  const(D_MIX, D_MODEL), const(1, D_MODEL), const(1, D_MODEL)],
        out_specs=(row(D_MODEL), row(D_MODEL)),
        compiler_params=_cparams(("parallel",)),
        name="mixer_output",
    )(oa_f, oa_b, proj2, ob, oc_f, oc_b, proj2, x2, gdn_w, hgrn_w, w_out, ln_g, ln_b)


XATT_TM = 512


def _xattn_kernel(xb_ref, x_ref, k_ref, v_ref, wq_ref, wo_ref, lg_ref, lb_ref, xo_ref, xob_ref):
    q = jnp.dot(xb_ref[0], wq_ref[...], preferred_element_type=F32) * (X_HEAD_DIM ** -0.5)
    outs = []
    for h in range(X_HEADS):
        sl = slice(h * X_HEAD_DIM, (h + 1) * X_HEAD_DIM)
        s = _dot_nt(q[:, sl], k_ref[0, :, sl])
        m = jnp.max(s, axis=-1, keepdims=True)
        e = jnp.exp(s - m)
        l = jnp.sum(e, axis=-1, keepdims=True)
        outs.append((jnp.dot(e.astype(BF16), v_ref[0, :, sl], preferred_element_type=F32) / l).astype(BF16))
    o = jnp.concatenate(outs, axis=1)
    c = jnp.dot(o, wo_ref[...], preferred_element_type=F32)
    y = _layer_norm_rows(DN_ALPHA * x_ref[0] + c, lg_ref[...], lb_ref[...])
    xo_ref[0] = y
    xob_ref[0] = y.astype(BF16)


def cross_attention(xb3, x3, k3, v3, wq, wo, ln_g, ln_b):
    b, s, _ = x3.shape
    mem = k3.shape[1]
    row = pl.BlockSpec((1, XATT_TM, D_MODEL), lambda i, t: (i, t, 0))
    const = lambda r, c: pl.BlockSpec((r, c), lambda i, t: (0, 0))
    kv = pl.BlockSpec((1, mem, D_MODEL), lambda i, t: (i, 0, 0))
    return pl.pallas_call(
        _xattn_kernel,
        out_shape=(jax.ShapeDtypeStruct((b, s, D_MODEL), F32), jax.ShapeDtypeStruct((b, s, D_MODEL), BF16)),
        grid=(b, s // XATT_TM),
        in_specs=[row, row, kv, kv, const(D_MODEL, D_MODEL), const(D_MODEL, D_MODEL),
                  const(1, D_MODEL), const(1, D_MODEL)],
        out_specs=(row, row),
        compiler_params=_cparams(("parallel", "parallel")),
        name="cross_attention",
    )(xb3, x3, k3, v3, wq, wo, ln_g, ln_b)


FFN_TM = 512
FFN_TF = 1408


def _ffn_kernel(xb_ref, x_ref, wg_ref, wu_ref, wd_ref, lg_ref, lb_ref, xo_ref, xob_ref):
    xb = xb_ref[...]
    acc = None
    for j in range(D_FF // FFN_TF):
        sl = slice(j * FFN_TF, (j + 1) * FFN_TF)
        g = jnp.dot(xb, wg_ref[:, sl], preferred_element_type=F32)
        u = jnp.dot(xb, wu_ref[:, sl], preferred_element_type=F32)
        h = (g * _sigmoid(g) * u).astype(BF16)
        part = jnp.dot(h, wd_ref[sl, :], preferred_element_type=F32)
        acc = part if acc is None else acc + part
    y = _layer_norm_rows(DN_ALPHA * x_ref[...] + acc, lg_ref[...], lb_ref[...])
    xo_ref[...] = y
    xob_ref[...] = y.astype(BF16)


def dense_ffn(xb2, x2, wg, wu, wd, ln_g, ln_b):
    m = x2.shape[0]
    row = pl.BlockSpec((FFN_TM, D_MODEL), lambda i: (i, 0))
    const = pl.BlockSpec((1, D_MODEL), lambda i: (0, 0))
    resident = lambda r, c: pl.BlockSpec((r, c), lambda i: (0, 0), pipeline_mode=pl.Buffered(1))
    return pl.pallas_call(
        _ffn_kernel,
        out_shape=(jax.ShapeDtypeStruct((m, D_MODEL), F32), jax.ShapeDtypeStruct((m, D_MODEL), BF16)),
        grid=(m // FFN_TM,),
        in_specs=[row, row, resident(D_MODEL, D_FF), resident(D_MODEL, D_FF), resident(D_FF, D_MODEL),
                  const, const],
        out_specs=(row, row),
        compiler_params=_cparams(("parallel",)),
        name="dense_ffn",
    )(xb2, x2, wg, wu, wd, ln_g, ln_b)


MOE_TM = 512
MOE_WIN = 512
MOE_TF = 1792
ROUTER_TM = 512


def _router_kernel(xb_ref, wr_ref, e_ref, w_ref, xp_ref):
    logits = jnp.dot(xb_ref[...], wr_ref[...], preferred_element_type=F32)
    lane = _iota2(logits.shape, 1)
    neg = jnp.float32(-jnp.inf)
    lane_f = lane.astype(F32)
    lg = jnp.where(lane < N_EXPERTS, logits, neg)
    m1 = jnp.max(lg, axis=-1, keepdims=True)
    i1 = jnp.min(jnp.where(lg == m1, lane_f, float(LANES)), axis=-1, keepdims=True)
    lg2 = jnp.where(lane_f == i1, neg, lg)
    m2 = jnp.max(lg2, axis=-1, keepdims=True)
    i2 = jnp.min(jnp.where(lg2 == m2, lane_f, float(LANES)), axis=-1, keepdims=True)
    t = jnp.exp(m2 - m1)
    w1 = 1.0 / (1.0 + t)
    w2 = t / (1.0 + t)
    e_ref[...] = jnp.where(lane == 0, i1, jnp.where(lane == 1, i2, 0.0)).astype(jnp.int32)
    w_ref[...] = jnp.where(lane == 0, w1, jnp.where(lane == 1, w2, 0.0))
    _store_packed(xp_ref, xb_ref[...].astype(F32))


def moe_router(xb2, w_router_p):
    m = xb2.shape[0]
    lane_out = pl.BlockSpec((ROUTER_TM, LANES), lambda i: (i, 0))
    return pl.pallas_call(
        _router_kernel,
        out_shape=(jax.ShapeDtypeStruct((m, LANES), jnp.int32), jax.ShapeDtypeStruct((m, LANES), F32),
                   jax.ShapeDtypeStruct((SC_SPLIT, m, LANES), jnp.uint32)),
        grid=(m // ROUTER_TM,),
        in_specs=[pl.BlockSpec((ROUTER_TM, D_MODEL), lambda i: (i, 0)),
                  pl.BlockSpec((D_MODEL, LANES), lambda i: (0, 0))],
        out_specs=(lane_out, lane_out, pl.BlockSpec((SC_SPLIT, ROUTER_TM, LANES), lambda i: (0, i, 0))),
        compiler_params=_cparams(("parallel",)),
        name="moe_router",
    )(xb2, w_router_p)


def _moe_ffn_kernel(be_ref, bv_ref, x_ref, wg_ref, wu_ref, wd_ref, o_ref, acc_ref):
    i = pl.program_id(0)
    j = pl.program_id(1)
    valid = bv_ref[i] != 0

    @pl.when(j == 0)
    def _():
        acc_ref[...] = jnp.zeros_like(acc_ref)

    @pl.when(valid)
    def _():
        xb = _load_packed(x_ref).astype(BF16)
        g = jnp.dot(xb, wg_ref[0], preferred_element_type=F32)
        u = jnp.dot(xb, wu_ref[0], preferred_element_type=F32)
        h = (g * _sigmoid(g) * u).astype(BF16)
        acc_ref[...] += jnp.dot(h, wd_ref[0], preferred_element_type=F32)

    @pl.when(j == pl.num_programs(1) - 1)
    def _():
        _store_packed(o_ref, acc_ref[...])


def moe_expert_ffn(block_e, block_valid, xrows, wg, wu, wd):
    n_rows = xrows.shape[1]
    grid_spec = pltpu.PrefetchScalarGridSpec(
        num_scalar_prefetch=2,
        grid=(n_rows // MOE_TM, D_FF_EXPERT // MOE_TF),
        in_specs=[pl.BlockSpec((SC_SPLIT, MOE_TM, LANES), lambda i, j, be, bv: (0, i * bv[i], 0)),
                  pl.BlockSpec((1, D_MODEL, MOE_TF), lambda i, j, be, bv: (be[i], 0, j)),
                  pl.BlockSpec((1, D_MODEL, MOE_TF), lambda i, j, be, bv: (be[i], 0, j)),
                  pl.BlockSpec((1, MOE_TF, D_MODEL), lambda i, j, be, bv: (be[i], j, 0))],
        out_specs=pl.BlockSpec((SC_SPLIT, MOE_TM, LANES), lambda i, j, be, bv: (0, i, 0)),
        scratch_shapes=[pltpu.VMEM((MOE_TM, D_MODEL), F32)],
    )
    return pl.pallas_call(
        _moe_ffn_kernel,
        out_shape=jax.ShapeDtypeStruct((SC_SPLIT, n_rows, LANES), jnp.uint32),
        grid_spec=grid_spec,
        compiler_params=_cparams(("arbitrary", "arbitrary")),
        name="moe_expert_ffn",
    )(block_e, block_valid, xrows, wg, wu, wd)


PACKED_WIDTH = D_MODEL // 2
SC_ROWS = 128
SC_SPLIT = 4


def _store_packed(o_ref, y):
    lo = lax.bitcast_convert_type(y[:, :PACKED_WIDTH].astype(BF16).astype(F32), jnp.uint32) >> 16
    hi = lax.bitcast_convert_type(y[:, PACKED_WIDTH:].astype(BF16).astype(F32), jnp.uint32)
    word = hi | lo
    for c in range(SC_SPLIT):
        o_ref[c] = word[:, c * LANES:(c + 1) * LANES]


def _load_packed(ref):
    w = jnp.concatenate([ref[c] for c in range(SC_SPLIT)], axis=1)
    lo = lax.bitcast_convert_type(w << 16, F32)
    hi = lax.bitcast_convert_type(w & jnp.uint32(0xFFFF0000), F32)
    return jnp.concatenate([lo, hi], axis=1)


def sc_gather_rows(data, idx):
    pieces, rows, width = data.shape
    idx = (jnp.arange(pieces, dtype=jnp.int32)[:, None] * rows + idx[None, :]).reshape(-1)
    return _sc_gather(data.reshape(pieces * rows, width), idx).reshape(pieces, -1, width)


def _sc_gather(data, idx):
    n = idx.shape[0]
    d = data.shape[1]
    sc = plsc.get_sparse_core_info()
    mesh = plsc.VectorSubcoreMesh(core_axis_name="core", subcore_axis_name="subcore")
    steps = n // (SC_ROWS * sc.num_cores)

    @pl.kernel(out_type=jax.ShapeDtypeStruct((n, d), data.dtype), mesh=mesh, scratch_types=[])
    def gather_kernel(x_hbm, i_hbm, o_hbm):
        def body(i_vmem, o_vmem):
            pltpu.sync_copy(x_hbm.at[i_vmem.at[0]], o_vmem)

        pltpu.emit_pipeline(
            body,
            grid=(sc.num_cores, steps),
            in_specs=[pl.BlockSpec((1, SC_ROWS), index_map=lambda c, i: (0, c * steps + i))],
            out_specs=[pl.BlockSpec((SC_ROWS, d), index_map=lambda c, i: (c * steps + i, 0))],
            core_axis_name=("core", "subcore"),
            dimension_semantics=(pltpu.PARALLEL, pltpu.PARALLEL),
        )(i_hbm, o_hbm)

    return gather_kernel(data, idx.reshape(1, n))


def sc_scatter_rows(data, idx0, idx1, n_out):
    pieces, n, width = data.shape
    off = jnp.arange(pieces, dtype=jnp.int32)[:, None] * n_out
    out = _sc_scatter(data.reshape(pieces * n, width), (off + idx0[None, :]).reshape(-1),
                      (off + idx1[None, :]).reshape(-1), pieces * n_out)
    return out.reshape(pieces, n_out, width)


def _sc_scatter(data, idx0, idx1, n_out):
    n, d = data.shape
    sc = plsc.get_sparse_core_info()
    mesh = plsc.VectorSubcoreMesh(core_axis_name="core", subcore_axis_name="subcore")
    steps = n // (SC_ROWS * sc.num_cores)
    idx_spec = pl.BlockSpec((1, SC_ROWS), index_map=lambda c, i: (0, c * steps + i))

    @pl.kernel(out_type=jax.ShapeDtypeStruct((n_out, d), data.dtype), mesh=mesh, scratch_types=[])
    def scatter_kernel(x_hbm, i0_hbm, i1_hbm, o_hbm):
        def body(x_vmem, i0_vmem, i1_vmem):
            pltpu.sync_copy(x_vmem, o_hbm.at[i0_vmem.at[0]])
            pltpu.sync_copy(x_vmem, o_hbm.at[i1_vmem.at[0]])

        pltpu.emit_pipeline(
            body,
            grid=(sc.num_cores, steps),
            in_specs=[pl.BlockSpec((SC_ROWS, d), index_map=lambda c, i: (c * steps + i, 0)), idx_spec, idx_spec],
            out_specs=[],
            core_axis_name=("core", "subcore"),
            dimension_semantics=(pltpu.PARALLEL, pltpu.PARALLEL),
        )(x_hbm, i0_hbm, i1_hbm)

    return scatter_kernel(data, idx0.reshape(1, n), idx1.reshape(1, n))


def _moe_finish_kernel(y0_ref, y1_ref, w_ref, x_ref, lg_ref, lb_ref, xo_ref):
    wt = w_ref[...]
    y = wt[:, 0:1] * _load_packed(y0_ref) + wt[:, 1:2] * _load_packed(y1_ref)
    xo_ref[...] = _layer_norm_rows(DN_ALPHA * x_ref[...] + y, lg_ref[...], lb_ref[...])


def moe_finish(ytok, top_w, x2, ln_g, ln_b):
    m = x2.shape[0]
    n_win = m // MOE_WIN
    packed = lambda off: pl.BlockSpec((SC_SPLIT, MOE_WIN, LANES), lambda i: (0, i + off, 0))
    row = pl.BlockSpec((MOE_WIN, D_MODEL), lambda i: (i, 0))
    const = pl.BlockSpec((1, D_MODEL), lambda i: (0, 0))
    return pl.pallas_call(
        _moe_finish_kernel,
        out_shape=jax.ShapeDtypeStruct((m, D_MODEL), F32),
        grid=(n_win,),
        in_specs=[packed(0), packed(n_win), pl.BlockSpec((MOE_WIN, TOP_K), lambda i: (i, 0)), row, const, const],
        out_specs=row,
        compiler_params=_cparams(("parallel",)),
        name="moe_finish",
    )(ytok, ytok, top_w, x2, ln_g, ln_b)


def moe_layer(xb2, x2, w_router, wg, wu, wd, ln_g, ln_b):
    n = x2.shape[0]
    n_rows = n * TOP_K + N_EXPERTS * MOE_TM
    n_blocks = n_rows // MOE_TM

    wr = jnp.pad(w_router, ((0, 0), (0, LANES - N_EXPERTS))).astype(BF16)
    top_e_t, top_w_t, x_packed = moe_router(xb2, wr)
    top_e = top_e_t[:, :TOP_K]
    top_w = top_w_t[:, :TOP_K]

    experts = jnp.arange(N_EXPERTS, dtype=jnp.int32)
    tok_onehot = ((top_e[:, 0:1] == experts[None, :]) | (top_e[:, 1:2] == experts[None, :])).astype(jnp.int32)
    csum = jnp.cumsum(tok_onehot, axis=0)
    counts = csum[-1]
    rank = csum - tok_onehot
    padded = (counts + MOE_TM - 1) // MOE_TM * MOE_TM
    pad_end = jnp.cumsum(padded)
    pad_start = pad_end - padded
    dest = pad_start[top_e] + jnp.take_along_axis(rank, top_e, axis=1)

    blk_row0 = jnp.arange(n_blocks, dtype=jnp.int32) * MOE_TM
    block_e = jnp.minimum(jnp.sum((pad_end[None, :] <= blk_row0[:, None]).astype(jnp.int32), axis=1),
                          N_EXPERTS - 1).astype(jnp.int32)
    block_valid = (blk_row0 < pad_end[-1]).astype(jnp.int32)

    xrows = sc_scatter_rows(x_packed, dest[:, 0], dest[:, 1], n_rows)
    yrows = moe_expert_ffn(block_e, block_valid, xrows, wg, wu, wd)

    ytok = sc_gather_rows(yrows, dest.T.reshape(-1))
    return moe_finish(ytok, top_w, x2, ln_g, ln_b)


def _w_in_perm():
    zero = 3480
    perm = np.full((D_IN_P,), zero, np.int64)
    o_qkv, o_z, o_beta, o_decay, o_bq, o_bk, o_bv, o_cf, o_ci, o_cq, o_cg = (
        0, 1152, 1536, 1548, 1560, 1944, 2072, 2200, 2712, 2968, 3224)
    perm[COL_A_QKV:COL_A_QKV + 1152] = np.arange(o_qkv, o_qkv + 1152)
    perm[COL_A_Z:COL_A_Z + 384] = np.arange(o_z, o_z + 384)
    for j in range(3):
        for a in range(2):
            dst = COL_B_Q + j * LANES + a * HEAD_DIM
            src = o_bq + (j + 3 * a) * HEAD_DIM
            perm[dst:dst + HEAD_DIM] = np.arange(src, src + HEAD_DIM)
    perm[COL_B_K:COL_B_K + 128] = np.arange(o_bk, o_bk + 128)
    perm[COL_C_G:COL_C_G + 256] = np.arange(o_cg, o_cg + 256)
    perm[COL_B_V:COL_B_V + 128] = np.arange(o_bv, o_bv + 128)
    perm[COL_C_F:COL_C_F + 512] = np.arange(o_cf, o_cf + 512)
    perm[COL_C_I:COL_C_I + 256] = np.arange(o_ci, o_ci + 256)
    perm[COL_C_Q:COL_C_Q + 256] = np.arange(o_cq, o_cq + 256)
    for d, col in enumerate((COL_GATES0, COL_GATES1)):
        for p in range(A_HEADS // 2):
            dst = col + 4 * p
            perm[dst + 0] = o_beta + d * A_HEADS + 2 * p
            perm[dst + 1] = o_beta + d * A_HEADS + 2 * p + 1
            perm[dst + 2] = o_decay + d * A_HEADS + 2 * p
            perm[dst + 3] = o_decay + d * A_HEADS + 2 * p + 1
    return perm


def _take_static(w, perm, axis, zero_index=None):
    pieces, start = [], 0
    for i in range(1, len(perm) + 1):
        is_zero = perm[start] == zero_index
        if i < len(perm) and ((perm[i] == zero_index) if is_zero
                              else (perm[i] == perm[i - 1] + 1 and perm[i] != zero_index)):
            continue
        if is_zero:
            shape = list(w.shape)
            shape[axis] = i - start
            pieces.append(jnp.zeros(shape, w.dtype))
        else:
            pieces.append(lax.slice_in_dim(w, int(perm[start]), int(perm[i - 1]) + 1, axis=axis))
        start = i
    return jnp.concatenate(pieces, axis=axis)


def _w_out_perm():
    perm = np.arange(D_MIX)
    for j in range(3):
        for a in range(2):
            dst = A_WIDTH + j * LANES + a * HEAD_DIM
            src = A_WIDTH + (j + 3 * a) * HEAD_DIM
            perm[dst:dst + HEAD_DIM] = np.arange(src, src + HEAD_DIM)
    return perm


def _gdn_params(a_log, dt_bias):
    rows = np.array([4 * (h // 2) + 2 + h % 2 for h in range(A_HEADS)])
    prm = jnp.zeros((2, GATE_ROWS, LANES), F32)
    prm = prm.at[:, rows, 0].set(a_log.astype(F32))
    prm = prm.at[:, rows, 1].set(dt_bias.astype(F32))
    return prm.at[:, rows, 2].set(1.0)


def kernel(x, mem, w_in, conv_w, gdn_a_log, gdn_dt_bias, gdn_norm_w, q_norm_w, k_norm_w, hgrn_lb_logits,
           hgrn_norm_w, w_out, ln1_g, ln1_b, xq, xk, xv, xo, ln2_g, ln2_b, ffn_wg, ffn_wu, ffn_wd,
           moe_router, moe_wg, moe_wu, moe_wd, ln3_g, ln3_b):
    b, s, d = x.shape
    m = b * s
    cos_t, sin_t = rope_tables(s)
    lb_p = jax.nn.softmax(hgrn_lb_logits.astype(F32), axis=0)
    lb_c = jnp.cumsum(lb_p, axis=0)
    lower_bounds = lb_c - lb_c[0]
    in_perm = _w_in_perm()
    out_perm = _w_out_perm()
    mem_b = mem.astype(BF16).reshape(b * mem.shape[1], d)
    tile2 = lambda w: jnp.tile(w.astype(F32), 2)[None, :]
    row = lambda v: v.astype(F32)[None, :]

    x2 = x.reshape(m, d)
    xb2 = x2.astype(BF16)
    for l in range(DEPTH):
        w_in_p = _take_static(w_in[l].astype(BF16), in_perm, 1, zero_index=w_in.shape[2])
        wide2 = matmul(xb2, w_in_p[:, :D_WIDE], BF16, min(1024, m), 768)
        gates2 = matmul(xb2, w_in_p[:, D_WIDE:], F32, min(1024, m), D_IN_P - D_WIDE)
        wide3 = wide2.reshape(b, s, D_WIDE)
        gates3 = gates2.reshape(b, s, D_IN_P - D_WIDE)

        oa_f, oa_b = gdn_scan(wide3, gates3, conv_w[l], _gdn_params(gdn_a_log[l], gdn_dt_bias[l]))
        ob = gqa_attention(wide3, cos_t, sin_t, tile2(q_norm_w[l]), tile2(k_norm_w[l]), s)
        lb = lower_bounds[l][None, :]
        oc_f, oc_b = hgrn_scan(wide3, gates3, lb)

        w_out_p = _take_static(w_out[l].astype(BF16), out_perm, 0)
        x2, xb2 = mixer_output(oa_f.reshape(m, A_WIDTH), oa_b.reshape(m, A_WIDTH), wide2,
                               ob.reshape(m, B_WIDTH), oc_f.reshape(m, C_WIDTH), oc_b.reshape(m, C_WIDTH),
                               x2, tile2(gdn_norm_w[l]), tile2(hgrn_norm_w[l]), w_out_p,
                               row(ln1_g[l]), row(ln1_b[l]))

        k3 = matmul(mem_b, xk[l].astype(BF16), BF16, 256, 512).reshape(b, -1, d)
        v3 = matmul(mem_b, xv[l].astype(BF16), BF16, 256, 512).reshape(b, -1, d)
        x3, xb3 = cross_attention(xb2.reshape(b, s, d), x2.reshape(b, s, d), k3, v3,
                                  xq[l].astype(BF16), xo[l].astype(BF16), row(ln2_g[l]), row(ln2_b[l]))
        x2, xb2 = x3.reshape(m, d), xb3.reshape(m, d)

        if l % 2 == 0:
            x2, xb2 = dense_ffn(xb2, x2, ffn_wg[l // 2].astype(BF16), ffn_wu[l // 2].astype(BF16),
                                ffn_wd[l // 2].astype(BF16), row(ln3_g[l]), row(ln3_b[l]))
        else:
            x2 = moe_layer(xb2, x2, moe_router[l // 2], moe_wg[l // 2].astype(BF16),
                           moe_wu[l // 2].astype(BF16), moe_wd[l // 2].astype(BF16),
                           row(ln3_g[l]), row(ln3_b[l]))
            xb2 = x2.astype(BF16)
    return x2.reshape(b, s, d)
```

```python
import math

import numpy as np
import jax
import jax.numpy as jnp
from jax import lax
from jax.experimental import pallas as pl
from jax.experimental.pallas import tpu as pltpu
from jax.experimental.pallas import tpu_sc as plsc

F32 = jnp.float32
BF16 = jnp.bfloat16

D_MODEL = 1024
DEPTH = 2
HEAD_DIM = 64
A_HEADS = 6
A_WIDTH = A_HEADS * HEAD_DIM
B_Q_HEADS = 6
B_WIDTH = B_Q_HEADS * HEAD_DIM
C_HEADS = 4
C_WIDTH = C_HEADS * HEAD_DIM
D_MIX = A_WIDTH + B_WIDTH + C_WIDTH
CONV_K = 5
CHUNK = 64
GRID_W = 64
ROPE_AXIS_DIM = HEAD_DIM // 2
ROPE_THETA = 10000.0
X_HEADS = 4
X_HEAD_DIM = D_MODEL // X_HEADS
D_FF = 2816
N_EXPERTS = 8
TOP_K = 2
D_FF_EXPERT = 3584
DN_ALPHA = (2 * DEPTH) ** 0.25
LN_EPS = 1e-5
RMS_EPS = 1e-6

LANES = 128
SUBLANES = 8
HALO = 16
GROUP = 256
CHUNKS_PER_GROUP = GROUP // CHUNK
VMEM_LIMIT = 56 * 1024 * 1024

COL_A_QKV = 0
COL_A_Z = 1152
COL_B_Q = 1536
COL_B_K = 1920
COL_C_G = 2048
COL_B_V = 2304
COL_C_I = 2560
COL_C_Q = 2816
D_WIDE = 3072
COL_C_F = D_WIDE
COL_GATES0 = D_WIDE + 512
COL_GATES1 = D_WIDE + 640
D_IN_P = 3840


def _cparams(sem):
    return pltpu.CompilerParams(dimension_semantics=sem, vmem_limit_bytes=VMEM_LIMIT)


def _sigmoid(x):
    return 1.0 / (1.0 + jnp.exp(-x))


def _softplus(x):
    return jnp.maximum(x, 0.0) + jnp.log1p(jnp.exp(-jnp.abs(x)))


def _dot_nt(a, b):
    return lax.dot_general(a.astype(BF16), b.astype(BF16), (((1,), (1,)), ((), ())),
                           preferred_element_type=F32)


def _dot_sel(m01, x):
    m = jnp.where(m01, 1.0, 0.0).astype(BF16)
    hi = x.astype(BF16)
    r1 = x - hi.astype(F32)
    mid = r1.astype(BF16)
    lo = (r1 - mid.astype(F32)).astype(BF16)
    out = jnp.dot(m, hi, preferred_element_type=F32)
    out = out + jnp.dot(m, mid, preferred_element_type=F32)
    return out + jnp.dot(m, lo, preferred_element_type=F32)


def _dot_sel_right(x, m01):
    m = jnp.where(m01, 1.0, 0.0).astype(BF16)
    hi = x.astype(BF16)
    r1 = x - hi.astype(F32)
    mid = r1.astype(BF16)
    lo = (r1 - mid.astype(F32)).astype(BF16)
    out = jnp.dot(hi, m, preferred_element_type=F32)
    out = out + jnp.dot(mid, m, preferred_element_type=F32)
    return out + jnp.dot(lo, m, preferred_element_type=F32)


def _layer_norm_rows(y, g, b):
    mu = jnp.mean(y, axis=-1, keepdims=True)
    yc = y - mu
    var = jnp.mean(yc * yc, axis=-1, keepdims=True)
    return yc * lax.rsqrt(var + LN_EPS) * g + b


def _iota2(shape, dim):
    return lax.broadcasted_iota(jnp.int32, shape, dim)


def _head_sum(x, first_head):
    s0 = jnp.sum(jnp.where(first_head, x, 0.0), axis=-1, keepdims=True)
    s1 = jnp.sum(jnp.where(first_head, 0.0, x), axis=-1, keepdims=True)
    return jnp.where(first_head, s0, s1)


def _block_rows(t, block, row_in_block):
    parts = [jnp.broadcast_to(t[b * block + row_in_block:b * block + row_in_block + 1, :], (block, t.shape[1]))
             for b in range(GROUP // block)]
    return jnp.concatenate(parts, axis=0)


def _mm_kernel(x_ref, w_ref, o_ref):
    o_ref[...] = jnp.dot(x_ref[...], w_ref[...], preferred_element_type=F32).astype(o_ref.dtype)


def matmul(x, w, out_dtype, tm, tn):
    m, k = x.shape
    n = w.shape[1]
    return pl.pallas_call(
        _mm_kernel,
        out_shape=jax.ShapeDtypeStruct((m, n), out_dtype),
        grid=(n // tn, m // tm),
        in_specs=[pl.BlockSpec((tm, k), lambda j, i: (i, 0)),
                  pl.BlockSpec((k, tn), lambda j, i: (0, j))],
        out_specs=pl.BlockSpec((tm, tn), lambda j, i: (i, j)),
        compiler_params=_cparams(("parallel", "arbitrary")),
        name="matmul",
    )(x, w)


CHUNK_SHIFT = CHUNK.bit_length() - 1
HEAD_SHIFT = HEAD_DIM.bit_length() - 1


def _scan_masks(reverse):
    ri = _iota2((GROUP, GROUP), 0)
    ci = _iota2((GROUP, GROUP), 1)
    same_chunk = (ri >> CHUNK_SHIFT) == (ci >> CHUNK_SHIFT)
    if reverse:
        return ri, ci, same_chunk, same_chunk & (ri <= ci), same_chunk & (ri < ci)
    return ri, ci, same_chunk, same_chunk & (ri >= ci), same_chunk & (ri > ci)


GATE_ROWS = 16


class _GdnProblem:
    def __init__(self, **kw):
        self.__dict__.update(kw)


def _gdn_group(probs, same_chunk, h0):
    nh0 = jnp.logical_not(h0)
    sr = _iota2((CHUNK, GROUP), 0)
    sc = _iota2((CHUNK, GROUP), 1)
    eye_side = jnp.where(sr == (sc & (CHUNK - 1)), 1.0, 0.0)
    tl = _iota2((LANES, GROUP), 1)
    br = _iota2((LANES, LANES), 0)
    bc = _iota2((LANES, LANES), 1)
    bd128 = (br >> HEAD_SHIFT) == (bc >> HEAD_SHIFT)

    def block_diag(side):
        return jnp.where(same_chunk, jnp.concatenate([side] * CHUNKS_PER_GROUP, axis=0), 0.0).astype(BF16)

    for pr in probs:
        pr.gedge = _block_rows(pr.gc, CHUNK, 0 if pr.reverse else CHUNK - 1)
        pr.exp_g = jnp.exp(pr.gc)
        pr.kb = pr.k * pr.beta_t
        pr.rhs = jnp.concatenate([pr.v * pr.beta_t, pr.kb * pr.exp_g], axis=1).astype(BF16)
        pr.k16 = pr.k.astype(BF16)
        pr.decay, pr.x, pr.pw, pr.bd = [], [], [], []
        for h in range(2):
            decay = jnp.exp(jnp.where(pr.incl, pr.g_col[h] - pr.g_row[h], -1e30))
            kk = _dot_nt(jnp.where(h0 if h == 0 else nh0, pr.kb, 0.0), pr.k16)
            low = jnp.where(pr.strict, kk * decay, 0.0)
            a = -(low[0:CHUNK] + low[CHUNK:2 * CHUNK] + low[2 * CHUNK:3 * CHUNK] + low[3 * CHUNK:4 * CHUNK])
            pr.decay.append(decay)
            pr.x.append(eye_side + a)
            pr.pw.append(a)
            pr.bd.append(block_diag(a))

    for _ in range(5):
        for pr in probs:
            for h in range(2):
                pr.pw[h] = jnp.dot(pr.pw[h].astype(BF16), pr.bd[h], preferred_element_type=F32)
        for pr in probs:
            for h in range(2):
                pr.bd[h] = block_diag(pr.pw[h])
        for pr in probs:
            for h in range(2):
                pr.x[h] = pr.x[h] + jnp.dot(pr.x[h].astype(BF16), pr.bd[h], preferred_element_type=F32)

    for pr in probs:
        sols = [jnp.dot(block_diag(pr.x[h]), pr.rhs, preferred_element_type=F32) for h in range(2)]
        u = jnp.where(h0, sols[0][:, :LANES], sols[1][:, :LANES])
        w = jnp.where(h0, sols[0][:, LANES:], sols[1][:, LANES:])
        wu = jnp.concatenate([w, u], axis=1)
        wu16 = wu.astype(BF16)
        aw = []
        for h in range(2):
            attn = jnp.where(pr.incl, _dot_nt(jnp.where(h0 if h == 0 else nh0, pr.q, 0.0), pr.k16) * pr.decay[h], 0.0)
            aw.append(jnp.dot(attn.astype(BF16), wu16, preferred_element_type=F32))
        pr.qp = (pr.q * pr.exp_g - jnp.where(h0, aw[0][:, :LANES], aw[1][:, :LANES])).astype(BF16)
        pr.op = jnp.where(h0, aw[0][:, LANES:], aw[1][:, LANES:])
        kd_t = (pr.k * jnp.exp(pr.gedge - pr.gc)).T
        pr.kmat, pr.nmat = [], []
        for c in range(CHUNKS_PER_GROUP):
            km = jnp.dot(jnp.where((tl >> CHUNK_SHIFT) == c, kd_t, 0.0), wu, preferred_element_type=F32)
            pr.kmat.append(jnp.where(bd128, km[:, :LANES], 0.0).astype(BF16))
            pr.nmat.append(jnp.where(bd128, km[:, LANES:], 0.0))
        pr.out = [None] * CHUNKS_PER_GROUP

    for t in range(CHUNKS_PER_GROUP):
        for pr in probs:
            c = CHUNKS_PER_GROUP - 1 - t if pr.reverse else t
            rs = slice(c * CHUNK, (c + 1) * CHUNK)
            s16 = pr.state.astype(BF16)
            pr.out[c] = jnp.dot(pr.qp[rs], s16, preferred_element_type=F32) + pr.op[rs]
            cd = jnp.exp(pr.gedge[c * CHUNK:c * CHUNK + 1, :])
            pr.state = pr.state * cd - jnp.dot(pr.kmat[c], s16, preferred_element_type=F32) + pr.nmat[c]
    return [jnp.concatenate(pr.out, axis=0) for pr in probs]


def _gdn_kernel(xf_ref, xfp_ref, xfn_ref, xb_ref, xbp_ref, xbn_ref, cw_ref, gf_ref, gb_ref, prm_ref,
                of_ref, ob_ref, s_scr, c_scr):
    n = pl.program_id(1)
    n_groups = pl.num_programs(1)

    @pl.when(n == 0)
    def _():
        s_scr[...] = jnp.zeros_like(s_scr)

    lane = _iota2((GROUP, LANES), 1)
    h0 = lane < HEAD_DIM
    ri = _iota2((GROUP, GROUP), 0)
    ci = _iota2((GROUP, GROUP), 1)
    same_chunk = (ri >> CHUNK_SHIFT) == (ci >> CHUNK_SHIFT)
    cw = cw_ref[...]
    base = SUBLANES - CONV_K // 2
    pairs = A_HEADS // 2

    dirs = ((n, xf_ref, xfp_ref, xfn_ref, gf_ref), (n_groups - 1 - n, xb_ref, xbp_ref, xbn_ref, gb_ref))

    @pl.when(2 * n < n_groups)
    def _():
        for gi, x_ref, xp_ref, xn_ref, _ in dirs:
            prev = jnp.where(gi > 0, xp_ref[0].astype(F32)[HALO - SUBLANES:], 0.0)
            nxt = jnp.where(gi < n_groups - 1, xn_ref[0].astype(F32)[:SUBLANES], 0.0)
            ext = jnp.concatenate([prev, x_ref[0].astype(F32), nxt], axis=0)
            acc = None
            for t in range(CONV_K):
                shift = (CONV_K // 2 - t) % ext.shape[0]
                tap = (pltpu.roll(ext, shift, 0) if shift else ext)[SUBLANES:SUBLANES + GROUP, :] * cw[t:t + 1, :]
                acc = tap if acc is None else acc + tap
            act = acc * _sigmoid(acc)
            rows = pl.ds(pl.multiple_of(gi * GROUP, GROUP), GROUP)
            for p in range(pairs):
                q_sl = slice(p * LANES, (p + 1) * LANES)
                k_sl = slice(A_WIDTH + p * LANES, A_WIDTH + (p + 1) * LANES)
                q = act[:, q_sl]
                k = act[:, k_sl]
                c_scr[rows, q_sl] = q * lax.rsqrt(_head_sum(q * q, h0) + RMS_EPS) * (HEAD_DIM ** -0.5)
                c_scr[rows, k_sl] = k * lax.rsqrt(_head_sum(k * k, h0) + RMS_EPS)
            c_scr[rows, 2 * A_WIDTH:] = act[:, 2 * A_WIDTH:]

    probs = []
    for d, (gi, _, _, _, g_ref) in enumerate(dirs):
        reverse = d == 1
        lower, upper = same_chunk & (ri >= ci), same_chunk & (ri <= ci)
        incl = upper if reverse else lower
        strict = same_chunk & ((ri < ci) if reverse else (ri > ci))
        qkv = c_scr[pl.ds(pl.multiple_of(gi * GROUP, GROUP), GROUP), :]

        gt_t = g_ref[0].T[0:GATE_ROWS, :]
        prm = prm_ref[d]
        e_rows = jnp.where(prm[:, 2:3] > 0.5,
                           -jnp.exp(prm[:, 0:1]) * _softplus(gt_t + prm[:, 1:2]), _sigmoid(gt_t))
        g_rows = _dot_sel_right(e_rows, lower if reverse else upper)
        slab = jnp.concatenate([e_rows, g_rows, jnp.zeros((LANES - 2 * GATE_ROWS, GROUP), F32)], axis=0)
        cols = slab.T
        for p in range(pairs):
            sl = lambda part: slice(part * A_WIDTH + p * LANES, part * A_WIDTH + (p + 1) * LANES)
            col = lambda r: cols[:, r:r + 1]
            probs.append(_GdnProblem(
                reverse=reverse, incl=incl, strict=strict, slot=d * pairs + p,
                q=qkv[:, sl(0)], k=qkv[:, sl(1)], v=qkv[:, sl(2)],
                beta_t=jnp.where(h0, col(4 * p), col(4 * p + 1)),
                gc=jnp.where(h0, col(GATE_ROWS + 4 * p + 2), col(GATE_ROWS + 4 * p + 3)),
                g_col=(col(GATE_ROWS + 4 * p + 2), col(GATE_ROWS + 4 * p + 3)),
                g_row=(g_rows[4 * p + 2:4 * p + 3, :], g_rows[4 * p + 3:4 * p + 4, :]),
                state=s_scr[d * pairs + p]))

    outs = _gdn_group(probs, same_chunk, h0)
    for pr, out in zip(probs, outs):
        o_ref = ob_ref if pr.reverse else of_ref
        p = pr.slot % pairs
        o_ref[0, :, p * LANES:(p + 1) * LANES] = out.astype(BF16)
        s_scr[pr.slot] = pr.state


def gdn_scan(wide3, gates3, conv_w, prm):
    b, seq_len, _ = wide3.shape
    n_groups = seq_len // GROUP
    halo_per_group = GROUP // HALO
    n_halo = seq_len // HALO
    width = 3 * A_WIDTH
    fwd = lambda n: n
    bwd = lambda n: n_groups - 1 - n

    def x_specs(gidx):
        return [pl.BlockSpec((1, GROUP, width), lambda i, n: (i, gidx(n), 0)),
                pl.BlockSpec((1, HALO, width),
                             lambda i, n: (i, jnp.maximum(gidx(n) * halo_per_group - 1, 0), 0)),
                pl.BlockSpec((1, HALO, width),
                             lambda i, n: (i, jnp.minimum((gidx(n) + 1) * halo_per_group, n_halo - 1), 0))]

    out_sds = jax.ShapeDtypeStruct((b, seq_len, A_WIDTH), BF16)
    return pl.pallas_call(
        _gdn_kernel,
        out_shape=(out_sds, out_sds),
        grid=(b, n_groups),
        in_specs=x_specs(fwd) + x_specs(bwd) + [
            pl.BlockSpec((CONV_K, width), lambda i, n: (0, 0)),
            pl.BlockSpec((1, GROUP, LANES), lambda i, n: (i, fwd(n), (COL_GATES0 - D_WIDE) // LANES)),
            pl.BlockSpec((1, GROUP, LANES), lambda i, n: (i, bwd(n), (COL_GATES1 - D_WIDE) // LANES)),
            pl.BlockSpec((2, GATE_ROWS, LANES), lambda i, n: (0, 0, 0))],
        out_specs=(pl.BlockSpec((1, GROUP, A_WIDTH), lambda i, n: (i, fwd(n), 0)),
                   pl.BlockSpec((1, GROUP, A_WIDTH), lambda i, n: (i, bwd(n), 0))),
        scratch_shapes=[pltpu.VMEM((A_HEADS, LANES, LANES), F32), pltpu.VMEM((seq_len, width), F32)],
        compiler_params=_cparams(("parallel", "arbitrary")),
        name="gdn_scan",
    )(wide3, wide3, wide3, wide3, wide3, wide3, conv_w, gates3, gates3, prm)


EXP_CLAMP = 60.0


class _HgrnProblem:
    def __init__(self, **kw):
        self.__dict__.update(kw)


def _hgrn_group(probs):
    lane = _iota2((GROUP, LANES), 1)
    h0 = lane < HEAD_DIM
    nh0 = jnp.logical_not(h0)
    tl = _iota2((LANES, GROUP), 1)
    br = _iota2((LANES, LANES), 0)
    bc = _iota2((LANES, LANES), 1)
    bd128 = (br >> HEAD_SHIFT) == (bc >> HEAD_SHIFT)

    for pr in probs:
        ri, ci, same_chunk, incl, _ = pr.masks
        f = pr.lb + (1.0 - pr.lb) * _sigmoid(pr.f_raw)
        kk = 1.0 - f
        q = pr.q_raw * _sigmoid(pr.q_raw) * (HEAD_DIM ** -0.5)
        cum = _dot_sel(incl, jnp.log(f))
        pr.cedge = _block_rows(cum, CHUNK, 0 if pr.reverse else CHUNK - 1)
        pr.qd = q * jnp.exp(cum)
        pr.kd = kk * jnp.exp(pr.cedge - cum)
        half, quarter = CHUNK // 2, CHUNK // 4
        pos_in_chunk = lambda t: t & (CHUNK - 1)
        pos_in_half = lambda t: t & (half - 1)
        same_half = (ri >> (CHUNK_SHIFT - 1)) == (ci >> (CHUNK_SHIFT - 1))
        same_quarter = (ri >> (CHUNK_SHIFT - 2)) == (ci >> (CHUNK_SHIFT - 2))
        if pr.reverse:
            m32 = same_chunk & (pos_in_chunk(ri) < half) & (pos_in_chunk(ci) >= half)
            m16 = same_half & (pos_in_half(ri) < quarter) & (pos_in_half(ci) >= quarter)
            mdg = same_quarter & (ri <= ci)
            ref32 = _block_rows(cum, CHUNK, half)
            ref16 = _block_rows(cum, half, quarter)
            refdg = _block_rows(cum, quarter, quarter // 2)
        else:
            m32 = same_chunk & (pos_in_chunk(ri) >= half) & (pos_in_chunk(ci) < half)
            m16 = same_half & (pos_in_half(ri) >= quarter) & (pos_in_half(ci) < quarter)
            mdg = same_quarter & (ri >= ci)
            ref32 = _block_rows(cum, CHUNK, half - 1)
            ref16 = _block_rows(cum, half, quarter - 1)
            refdg = _block_rows(cum, quarter, quarter // 2 - 1)
        pr.level_masks = (m32, m16, mdg)
        pr.qs = (q * jnp.exp(jnp.minimum(cum - ref32, 0.0)),
                 q * jnp.exp(jnp.minimum(cum - ref16, 0.0)),
                 q * jnp.exp(jnp.clip(cum - refdg, -EXP_CLAMP, EXP_CLAMP)))
        pr.ks = ((kk * jnp.exp(jnp.minimum(ref32 - cum, 0.0))).astype(BF16),
                 (kk * jnp.exp(jnp.minimum(ref16 - cum, 0.0))).astype(BF16),
                 (kk * jnp.exp(jnp.clip(refdg - cum, -EXP_CLAMP, EXP_CLAMP))).astype(BF16))

    for pr in probs:
        v16 = pr.v.astype(BF16)
        m32, m16, mdg = pr.level_masks
        intra = []
        for h in range(2):
            mh = h0 if h == 0 else nh0
            a = jnp.where(m32, _dot_nt(jnp.where(mh, pr.qs[0], 0.0), pr.ks[0]),
                          jnp.where(m16, _dot_nt(jnp.where(mh, pr.qs[1], 0.0), pr.ks[1]),
                                    jnp.where(mdg, _dot_nt(jnp.where(mh, pr.qs[2], 0.0), pr.ks[2]), 0.0)))
            intra.append(jnp.dot(a.astype(BF16), v16, preferred_element_type=F32))
        pr.intra = jnp.where(h0, intra[0], intra[1])
        v_t = pr.v.T
        pr.upd = [jnp.where(bd128, jnp.dot(jnp.where((tl >> CHUNK_SHIFT) == c, v_t, 0.0), pr.kd,
                                           preferred_element_type=F32), 0.0)
                  for c in range(CHUNKS_PER_GROUP)]
        pr.out = [None] * CHUNKS_PER_GROUP

    for t in range(CHUNKS_PER_GROUP):
        for pr in probs:
            c = CHUNKS_PER_GROUP - 1 - t if pr.reverse else t
            rs = slice(c * CHUNK, (c + 1) * CHUNK)
            pr.out[c] = _dot_nt(pr.qd[rs], pr.state_t) + pr.intra[rs]
            cd = jnp.exp(pr.cedge[c * CHUNK:c * CHUNK + 1, :])
            pr.state_t = pr.state_t * cd + pr.upd[c]
    return [jnp.concatenate(pr.out, axis=0) for pr in probs]


def _hgrn_kernel(ff_ref, fb_ref, if_ref, ib_ref, qf_ref, qb_ref, lb_ref, of_ref, ob_ref, s_scr):
    n = pl.program_id(1)

    @pl.when(n == 0)
    def _():
        s_scr[...] = jnp.zeros_like(s_scr)

    pairs = C_HEADS // 2
    probs = []
    for d, (f_ref, i_ref, q_ref) in enumerate(((ff_ref, if_ref, qf_ref), (fb_ref, ib_ref, qb_ref))):
        masks = _scan_masks(d == 1)
        for p in range(pairs):
            sl = slice(p * LANES, (p + 1) * LANES)
            probs.append(_HgrnProblem(reverse=d == 1, masks=masks, slot=d * pairs + p, lb=lb_ref[:, sl],
                                      f_raw=f_ref[0, :, sl], v=i_ref[0, :, sl].astype(F32),
                                      q_raw=q_ref[0, :, sl].astype(F32), state_t=s_scr[d * pairs + p]))
    outs = _hgrn_group(probs)
    for pr, out in zip(probs, outs):
        o_ref = ob_ref if pr.reverse else of_ref
        p = pr.slot % pairs
        o_ref[0, :, p * LANES:(p + 1) * LANES] = out.astype(BF16)
        s_scr[pr.slot] = pr.state_t


def hgrn_scan(wide3, gates3, lb):
    b, seq_len, _ = wide3.shape
    n_groups = seq_len // GROUP
    fwd = lambda n: n
    bwd = lambda n: n_groups - 1 - n
    spec = lambda gidx, blk: pl.BlockSpec((1, GROUP, C_WIDTH), lambda i, n: (i, gidx(n), blk))
    f_blk = (COL_C_F - D_WIDE) // C_WIDTH
    out_sds = jax.ShapeDtypeStruct((b, seq_len, C_WIDTH), BF16)
    return pl.pallas_call(
        _hgrn_kernel,
        out_shape=(out_sds, out_sds),
        grid=(b, n_groups),
        in_specs=[spec(fwd, f_blk), spec(bwd, f_blk + 1),
                  spec(fwd, COL_C_I // C_WIDTH), spec(bwd, COL_C_I // C_WIDTH),
                  spec(fwd, COL_C_Q // C_WIDTH), spec(bwd, COL_C_Q // C_WIDTH),
                  pl.BlockSpec((1, C_WIDTH), lambda i, n: (0, 0))],
        out_specs=(spec(fwd, 0), spec(bwd, 0)),
        scratch_shapes=[pltpu.VMEM((C_HEADS, LANES, LANES), F32)],
        compiler_params=_cparams(("parallel", "arbitrary")),
        name="hgrn_scan",
    )(gates3, gates3, wide3, wide3, wide3, wide3, lb)


ATT_TQ = 512


def _norm_rope(x, w, cos, sin_signed, first_half16):
    lane = _iota2(x.shape, 1)
    h0 = lane < HEAD_DIM
    ms = _head_sum(x * x, h0) * (1.0 / HEAD_DIM)
    xn = x * lax.rsqrt(ms + RMS_EPS) * w
    partner = jnp.where(first_half16, pltpu.roll(xn, LANES - 16, 1), pltpu.roll(xn, 16, 1))
    return xn * cos + partner * sin_signed


def _attn_kernel(q_ref, k_ref, v_ref, cosq_ref, sinq_ref, cosk_ref, sink_ref, qw_ref, kw_ref, o_ref,
                 k_scr, v_scr):
    qi = pl.program_id(1)

    @pl.when(qi == 0)
    def _():
        kx = k_ref[0].astype(F32)
        lane_k = _iota2(kx.shape, 1)
        k_scr[...] = _norm_rope(kx, kw_ref[...], cosk_ref[...], sink_ref[...], (lane_k & 31) < 16).astype(BF16)
        v_scr[:, :LANES] = v_ref[0]
        v_scr[:, LANES:] = jnp.ones((v_scr.shape[0], LANES), BF16)

    lane = _iota2((ATT_TQ, LANES), 1)
    h0 = lane < HEAD_DIM
    fh = (lane & 31) < 16
    cos = cosq_ref[...]
    sin = sinq_ref[...]
    kmat = k_scr[...]
    vmat = v_scr[...]
    q_scale = (HEAD_DIM ** -0.5) * math.log2(math.e)
    for j in range(B_Q_HEADS // 2):
        qn = _norm_rope(q_ref[0, :, j * LANES:(j + 1) * LANES].astype(F32), qw_ref[...], cos, sin, fh) * q_scale
        outs = []
        for half in range(2):
            mh = h0 if half == 0 else jnp.logical_not(h0)
            s = _dot_nt(jnp.where(mh, qn, 0.0), kmat)
            m = jnp.max(s, axis=-1, keepdims=True)
            e = jnp.exp2((s - m).astype(BF16))
            pvl = jnp.dot(e, vmat, preferred_element_type=F32)
            outs.append(pvl[:, :LANES] / pvl[:, LANES:LANES + 1])
        o_ref[0, :, j * LANES:(j + 1) * LANES] = jnp.where(h0, outs[0], outs[1]).astype(BF16)


def gqa_attention(proj, cos_t, sin_t, q_w, k_w, seq_len):
    b = proj.shape[0]
    return pl.pallas_call(
        _attn_kernel,
        out_shape=jax.ShapeDtypeStruct((b, seq_len, B_WIDTH), BF16),
        grid=(b, seq_len // ATT_TQ),
        in_specs=[pl.BlockSpec((1, ATT_TQ, B_WIDTH), lambda i, t: (i, t, COL_B_Q // B_WIDTH)),
                  pl.BlockSpec((1, seq_len, LANES), lambda i, t: (i, 0, COL_B_K // LANES)),
                  pl.BlockSpec((1, seq_len, LANES), lambda i, t: (i, 0, COL_B_V // LANES)),
                  pl.BlockSpec((ATT_TQ, LANES), lambda i, t: (t, 0)),
                  pl.BlockSpec((ATT_TQ, LANES), lambda i, t: (t, 0)),
                  pl.BlockSpec((seq_len, LANES), lambda i, t: (0, 0)),
                  pl.BlockSpec((seq_len, LANES), lambda i, t: (0, 0)),
                  pl.BlockSpec((1, LANES), lambda i, t: (0, 0)),
                  pl.BlockSpec((1, LANES), lambda i, t: (0, 0))],
        out_specs=pl.BlockSpec((1, ATT_TQ, B_WIDTH), lambda i, t: (i, t, 0)),
        scratch_shapes=[pltpu.VMEM((seq_len, LANES), BF16), pltpu.VMEM((seq_len, 2 * LANES), BF16)],
        compiler_params=_cparams(("parallel", "arbitrary")),
        name="gqa_attention",
    )(proj, proj, proj, cos_t, sin_t, cos_t, sin_t, q_w, k_w)


def rope_tables(seq_len):
    rows = seq_len // GRID_W
    row = np.repeat(np.arange(rows, dtype=np.float64), GRID_W)
    col = np.tile(np.arange(GRID_W, dtype=np.float64), rows)
    inv_freq = ROPE_THETA ** (-np.arange(0, ROPE_AXIS_DIM, 2, dtype=np.float64) / ROPE_AXIS_DIM)
    ang_r = row[:, None] * inv_freq
    ang_c = col[:, None] * inv_freq
    cos64 = np.concatenate([np.cos(ang_r), np.cos(ang_r), np.cos(ang_c), np.cos(ang_c)], axis=1)
    sin64 = np.concatenate([-np.sin(ang_r), np.sin(ang_r), -np.sin(ang_c), np.sin(ang_c)], axis=1)
    return (jnp.asarray(np.tile(cos64, (1, 2)), dtype=F32), jnp.asarray(np.tile(sin64, (1, 2)), dtype=F32))


MIX_TM = 512


def _mixout_kernel(oaf_ref, oab_ref, z_ref, ob_ref, ocf_ref, ocb_ref, g_ref, x_ref, wa_ref, wc_ref,
                   wo_ref, lg_ref, lb_ref, xo_ref, xb_ref):
    lane = _iota2((MIX_TM, LANES), 1)
    h0 = lane < HEAD_DIM
    parts = []
    for j in range(A_WIDTH // LANES):
        sl = slice(j * LANES, (j + 1) * LANES)
        o = oaf_ref[:, sl].astype(F32) + oab_ref[:, sl].astype(F32)
        ms = _head_sum(o * o, h0) * (1.0 / HEAD_DIM)
        z = z_ref[:, sl].astype(F32)
        parts.append((o * lax.rsqrt(ms + RMS_EPS) * wa_ref[...] * (z * _sigmoid(z))).astype(BF16))
    parts.append(ob_ref[...])
    for j in range(C_WIDTH // LANES):
        sl = slice(j * LANES, (j + 1) * LANES)
        o = ocf_ref[:, sl].astype(F32) + ocb_ref[:, sl].astype(F32)
        ms = _head_sum(o * o, h0) * (1.0 / HEAD_DIM)
        parts.append((o * lax.rsqrt(ms + RMS_EPS) * wc_ref[...]
                      * _sigmoid(g_ref[:, sl].astype(F32))).astype(BF16))
    mixed = jnp.concatenate(parts, axis=1)
    h = jnp.dot(mixed, wo_ref[...], preferred_element_type=F32)
    y = _layer_norm_rows(DN_ALPHA * x_ref[...] + h, lg_ref[...], lb_ref[...])
    xo_ref[...] = y
    xb_ref[...] = y.astype(BF16)


def mixer_output(oa_f, oa_b, proj2, ob, oc_f, oc_b, x2, gdn_w, hgrn_w, w_out, ln_g, ln_b):
    m = x2.shape[0]
    row = lambda w: pl.BlockSpec((MIX_TM, w), lambda i: (i, 0))
    const = lambda r, c: pl.BlockSpec((r, c), lambda i: (0, 0))
    return pl.pallas_call(
        _mixout_kernel,
        out_shape=(jax.ShapeDtypeStruct((m, D_MODEL), F32), jax.ShapeDtypeStruct((m, D_MODEL), BF16)),
        grid=(m // MIX_TM,),
        in_specs=[row(A_WIDTH), row(A_WIDTH),
                  pl.BlockSpec((MIX_TM, A_WIDTH), lambda i: (i, COL_A_Z // A_WIDTH)),
                  row(B_WIDTH), row(C_WIDTH), row(C_WIDTH),
                  pl.BlockSpec((MIX_TM, C_WIDTH), lambda i: (i, COL_C_G // C_WIDTH)),
                  row(D_MODEL), const(1, LANES), const(1, LANES),
                  const(D_MIX, D_MODEL), const(1, D_MODEL), const(1, D_MODEL)],
        out_specs=(row(D_MODEL), row(D_MODEL)),
        compiler_params=_cparams(("parallel",)),
        name="mixer_output",
    )(oa_f, oa_b, proj2, ob, oc_f, oc_b, proj2, x2, gdn_w, hgrn_w, w_out, ln_g, ln_b)


XATT_TM = 512


def _xattn_kernel(xb_ref, x_ref, k_ref, v_ref, wq_ref, wo_ref, lg_ref, lb_ref, xo_ref, xob_ref):
    q = jnp.dot(xb_ref[0], wq_ref[...], preferred_element_type=F32) * (X_HEAD_DIM ** -0.5)
    outs = []
    for h in range(X_HEADS):
        sl = slice(h * X_HEAD_DIM, (h + 1) * X_HEAD_DIM)
        s = _dot_nt(q[:, sl], k_ref[0, :, sl])
        m = jnp.max(s, axis=-1, keepdims=True)
        e = jnp.exp(s - m)
        l = jnp.sum(e, axis=-1, keepdims=True)
        outs.append((jnp.dot(e.astype(BF16), v_ref[0, :, sl], preferred_element_type=F32) / l).astype(BF16))
    o = jnp.concatenate(outs, axis=1)
    c = jnp.dot(o, wo_ref[...], preferred_element_type=F32)
    y = _layer_norm_rows(DN_ALPHA * x_ref[0] + c, lg_ref[...], lb_ref[...])
    xo_ref[0] = y
    xob_ref[0] = y.astype(BF16)


def cross_attention(xb3, x3, k3, v3, wq, wo, ln_g, ln_b):
    b, s, _ = x3.shape
    mem = k3.shape[1]
    row = pl.BlockSpec((1, XATT_TM, D_MODEL), lambda i, t: (i, t, 0))
    const = lambda r, c: pl.BlockSpec((r, c), lambda i, t: (0, 0))
    kv = pl.BlockSpec((1, mem, D_MODEL), lambda i, t: (i, 0, 0))
    return pl.pallas_call(
        _xattn_kernel,
        out_shape=(jax.ShapeDtypeStruct((b, s, D_MODEL), F32), jax.ShapeDtypeStruct((b, s, D_MODEL), BF16)),
        grid=(b, s // XATT_TM),
        in_specs=[row, row, kv, kv, const(D_MODEL, D_MODEL), const(D_MODEL, D_MODEL),
                  const(1, D_MODEL), const(1, D_MODEL)],
        out_specs=(row, row),
        compiler_params=_cparams(("parallel", "parallel")),
        name="cross_attention",
    )(xb3, x3, k3, v3, wq, wo, ln_g, ln_b)


FFN_TM = 512
FFN_TF = 1408


def _ffn_kernel(xb_ref, x_ref, wg_ref, wu_ref, wd_ref, lg_ref, lb_ref, xo_ref, xob_ref):
    xb = xb_ref[...]
    acc = None
    for j in range(D_FF // FFN_TF):
        sl = slice(j * FFN_TF, (j + 1) * FFN_TF)
        g = jnp.dot(xb, wg_ref[:, sl], preferred_element_type=F32)
        u = jnp.dot(xb, wu_ref[:, sl], preferred_element_type=F32)
        h = (g * _sigmoid(g) * u).astype(BF16)
        part = jnp.dot(h, wd_ref[sl, :], preferred_element_type=F32)
        acc = part if acc is None else acc + part
    y = _layer_norm_rows(DN_ALPHA * x_ref[...] + acc, lg_ref[...], lb_ref[...])
    xo_ref[...] = y
    xob_ref[...] = y.astype(BF16)


def dense_ffn(xb2, x2, wg, wu, wd, ln_g, ln_b):
    m = x2.shape[0]
    row = pl.BlockSpec((FFN_TM, D_MODEL), lambda i: (i, 0))
    const = pl.BlockSpec((1, D_MODEL), lambda i: (0, 0))
    resident = lambda r, c: pl.BlockSpec((r, c), lambda i: (0, 0), pipeline_mode=pl.Buffered(1))
    return pl.pallas_call(
        _ffn_kernel,
        out_shape=(jax.ShapeDtypeStruct((m, D_MODEL), F32), jax.ShapeDtypeStruct((m, D_MODEL), BF16)),
        grid=(m // FFN_TM,),
        in_specs=[row, row, resident(D_MODEL, D_FF), resident(D_MODEL, D_FF), resident(D_FF, D_MODEL),
                  const, const],
        out_specs=(row, row),
        compiler_params=_cparams(("parallel",)),
        name="dense_ffn",
    )(xb2, x2, wg, wu, wd, ln_g, ln_b)


MOE_TM = 512
MOE_WIN = 512
MOE_TF = 1792
ROUTER_TM = 512


def _router_kernel(xb_ref, wr_ref, e_ref, w_ref, xp_ref):
    logits = jnp.dot(xb_ref[...], wr_ref[...], preferred_element_type=F32)
    lane = _iota2(logits.shape, 1)
    neg = jnp.float32(-jnp.inf)
    lane_f = lane.astype(F32)
    lg = jnp.where(lane < N_EXPERTS, logits, neg)
    m1 = jnp.max(lg, axis=-1, keepdims=True)
    i1 = jnp.min(jnp.where(lg == m1, lane_f, float(LANES)), axis=-1, keepdims=True)
    lg2 = jnp.where(lane_f == i1, neg, lg)
    m2 = jnp.max(lg2, axis=-1, keepdims=True)
    i2 = jnp.min(jnp.where(lg2 == m2, lane_f, float(LANES)), axis=-1, keepdims=True)
    t = jnp.exp(m2 - m1)
    w1 = 1.0 / (1.0 + t)
    w2 = t / (1.0 + t)
    e_ref[...] = jnp.where(lane == 0, i1, jnp.where(lane == 1, i2, 0.0)).astype(jnp.int32)[:, :N_EXPERTS]
    w_ref[...] = jnp.where(lane == 0, w1, jnp.where(lane == 1, w2, 0.0))[:, :N_EXPERTS]
    _store_packed(xp_ref, xb_ref[...].astype(F32))


def moe_router(xb2, w_router_p):
    m = xb2.shape[0]
    lane_out = pl.BlockSpec((ROUTER_TM, N_EXPERTS), lambda i: (i, 0))
    return pl.pallas_call(
        _router_kernel,
        out_shape=(jax.ShapeDtypeStruct((m, N_EXPERTS), jnp.int32), jax.ShapeDtypeStruct((m, N_EXPERTS), F32),
                   jax.ShapeDtypeStruct((SC_SPLIT, m, LANES), jnp.uint32)),
        grid=(m // ROUTER_TM,),
        in_specs=[pl.BlockSpec((ROUTER_TM, D_MODEL), lambda i: (i, 0)),
                  pl.BlockSpec((D_MODEL, LANES), lambda i: (0, 0))],
        out_specs=(lane_out, lane_out, pl.BlockSpec((SC_SPLIT, ROUTER_TM, LANES), lambda i: (0, i, 0))),
        compiler_params=_cparams(("parallel",)),
        name="moe_router",
    )(xb2, w_router_p)


def _moe_ffn_kernel(be_ref, bv_ref, x_ref, wg_ref, wu_ref, wd_ref, o_ref, acc_ref):
    i = pl.program_id(0)
    j = pl.program_id(1)
    valid = bv_ref[i] != 0

    @pl.when(j == 0)
    def _():
        acc_ref[...] = jnp.zeros_like(acc_ref)

    @pl.when(valid)
    def _():
        xb = _load_packed(x_ref).astype(BF16)
        g = jnp.dot(xb, wg_ref[0], preferred_element_type=F32)
        u = jnp.dot(xb, wu_ref[0], preferred_element_type=F32)
        h = (g * _sigmoid(g) * u).astype(BF16)
        acc_ref[...] += jnp.dot(h, wd_ref[0], preferred_element_type=F32)

    @pl.when(j == pl.num_programs(1) - 1)
    def _():
        _store_packed(o_ref, acc_ref[...])


def moe_expert_ffn(block_e, block_valid, xrows, wg, wu, wd):
    n_rows = xrows.shape[1]
    grid_spec = pltpu.PrefetchScalarGridSpec(
        num_scalar_prefetch=2,
        grid=(n_rows // MOE_TM, D_FF_EXPERT // MOE_TF),
        in_specs=[pl.BlockSpec((SC_SPLIT, MOE_TM, LANES), lambda i, j, be, bv: (0, i * bv[i], 0)),
                  pl.BlockSpec((1, D_MODEL, MOE_TF), lambda i, j, be, bv: (be[i], 0, j)),
                  pl.BlockSpec((1, D_MODEL, MOE_TF), lambda i, j, be, bv: (be[i], 0, j)),
                  pl.BlockSpec((1, MOE_TF, D_MODEL), lambda i, j, be, bv: (be[i], j, 0))],
        out_specs=pl.BlockSpec((SC_SPLIT, MOE_TM, LANES), lambda i, j, be, bv: (0, i, 0)),
        scratch_shapes=[pltpu.VMEM((MOE_TM, D_MODEL), F32)],
    )
    return pl.pallas_call(
        _moe_ffn_kernel,
        out_shape=jax.ShapeDtypeStruct((SC_SPLIT, n_rows, LANES), jnp.uint32),
        grid_spec=grid_spec,
        compiler_params=_cparams(("arbitrary", "arbitrary")),
        name="moe_expert_ffn",
    )(block_e, block_valid, xrows, wg, wu, wd)


PACKED_WIDTH = D_MODEL // 2
SC_ROWS = 128
SC_SPLIT = 4


def _store_packed(o_ref, y):
    lo = lax.bitcast_convert_type(y[:, :PACKED_WIDTH].astype(BF16).astype(F32), jnp.uint32) >> 16
    hi = lax.bitcast_convert_type(y[:, PACKED_WIDTH:].astype(BF16).astype(F32), jnp.uint32)
    word = hi | lo
    for c in range(SC_SPLIT):
        o_ref[c] = word[:, c * LANES:(c + 1) * LANES]


def _load_packed(ref):
    w = jnp.concatenate([ref[c] for c in range(SC_SPLIT)], axis=1)
    lo = lax.bitcast_convert_type(w << 16, F32)
    hi = lax.bitcast_convert_type(w & jnp.uint32(0xFFFF0000), F32)
    return jnp.concatenate([lo, hi], axis=1)


def sc_gather_rows(data, idx):
    pieces, rows, width = data.shape
    idx = (jnp.arange(pieces, dtype=jnp.int32)[:, None] * rows + idx[None, :]).reshape(-1)
    return _sc_gather(data.reshape(pieces * rows, width), idx).reshape(pieces, -1, width)


def _sc_gather(data, idx):
    n = idx.shape[0]
    d = data.shape[1]
    sc = plsc.get_sparse_core_info()
    mesh = plsc.VectorSubcoreMesh(core_axis_name="core", subcore_axis_name="subcore")
    steps = n // (SC_ROWS * sc.num_cores)

    @pl.kernel(out_type=jax.ShapeDtypeStruct((n, d), data.dtype), mesh=mesh, scratch_types=[])
    def gather_kernel(x_hbm, i_hbm, o_hbm):
        def body(i_vmem, o_vmem):
            pltpu.sync_copy(x_hbm.at[i_vmem.at[0]], o_vmem)

        pltpu.emit_pipeline(
            body,
            grid=(sc.num_cores, steps),
            in_specs=[pl.BlockSpec((1, SC_ROWS), index_map=lambda c, i: (0, c * steps + i))],
            out_specs=[pl.BlockSpec((SC_ROWS, d), index_map=lambda c, i: (c * steps + i, 0))],
            core_axis_name=("core", "subcore"),
            dimension_semantics=(pltpu.PARALLEL, pltpu.PARALLEL),
        )(i_hbm, o_hbm)

    return gather_kernel(data, idx.reshape(1, n))


def sc_scatter_rows(data, idx0, idx1, n_out):
    pieces, n, width = data.shape
    off = jnp.arange(pieces, dtype=jnp.int32)[:, None] * n_out
    out = _sc_scatter(data.reshape(pieces * n, width), (off + idx0[None, :]).reshape(-1),
                      (off + idx1[None, :]).reshape(-1), pieces * n_out)
    return out.reshape(pieces, n_out, width)


def _sc_scatter(data, idx0, idx1, n_out):
    n, d = data.shape
    sc = plsc.get_sparse_core_info()
    mesh = plsc.VectorSubcoreMesh(core_axis_name="core", subcore_axis_name="subcore")
    steps = n // (SC_ROWS * sc.num_cores)
    idx_spec = pl.BlockSpec((1, SC_ROWS), index_map=lambda c, i: (0, c * steps + i))

    @pl.kernel(out_type=jax.ShapeDtypeStruct((n_out, d), data.dtype), mesh=mesh, scratch_types=[])
    def scatter_kernel(x_hbm, i0_hbm, i1_hbm, o_hbm):
        def body(x_vmem, i0_vmem, i1_vmem):
            pltpu.sync_copy(x_vmem, o_hbm.at[i0_vmem.at[0]])
            pltpu.sync_copy(x_vmem, o_hbm.at[i1_vmem.at[0]])

        pltpu.emit_pipeline(
            body,
            grid=(sc.num_cores, steps),
            in_specs=[pl.BlockSpec((SC_ROWS, d), index_map=lambda c, i: (c * steps + i, 0)), idx_spec, idx_spec],
            out_specs=[],
            core_axis_name=("core", "subcore"),
            dimension_semantics=(pltpu.PARALLEL, pltpu.PARALLEL),
        )(x_hbm, i0_hbm, i1_hbm)

    return scatter_kernel(data, idx0.reshape(1, n), idx1.reshape(1, n))


def _moe_finish_kernel(y0_ref, y1_ref, w_ref, x_ref, lg_ref, lb_ref, xo_ref):
    wt = w_ref[...]
    y = wt[:, 0:1] * _load_packed(y0_ref) + wt[:, 1:2] * _load_packed(y1_ref)
    xo_ref[...] = _layer_norm_rows(DN_ALPHA * x_ref[...] + y, lg_ref[...], lb_ref[...])


def moe_finish(ytok, top_w, x2, ln_g, ln_b):
    m = x2.shape[0]
    n_win = m // MOE_WIN
    packed = lambda off: pl.BlockSpec((SC_SPLIT, MOE_WIN, LANES), lambda i: (0, i + off, 0))
    row = pl.BlockSpec((MOE_WIN, D_MODEL), lambda i: (i, 0))
    const = pl.BlockSpec((1, D_MODEL), lambda i: (0, 0))
    return pl.pallas_call(
        _moe_finish_kernel,
        out_shape=jax.ShapeDtypeStruct((m, D_MODEL), F32),
        grid=(n_win,),
        in_specs=[packed(0), packed(n_win), pl.BlockSpec((MOE_WIN, TOP_K), lambda i: (i, 0)), row, const, const],
        out_specs=row,
        compiler_params=_cparams(("parallel",)),
        name="moe_finish",
    )(ytok, ytok, top_w, x2, ln_g, ln_b)


def moe_layer(xb2, x2, w_router, wg, wu, wd, ln_g, ln_b):
    n = x2.shape[0]
    n_rows = n * TOP_K + N_EXPERTS * MOE_TM
    n_blocks = n_rows // MOE_TM

    wr = jnp.pad(w_router, ((0, 0), (0, LANES - N_EXPERTS))).astype(BF16)
    top_e_t, top_w_t, x_packed = moe_router(xb2, wr)
    top_e = top_e_t[:, :TOP_K]
    top_w = top_w_t[:, :TOP_K]

    experts = jnp.arange(N_EXPERTS, dtype=jnp.int32)
    tok_onehot = ((top_e[:, 0:1] == experts[None, :]) | (top_e[:, 1:2] == experts[None, :])).astype(jnp.int32)
    csum = jnp.cumsum(tok_onehot, axis=0)
    counts = csum[-1]
    rank = csum - tok_onehot
    padded = (counts + MOE_TM - 1) // MOE_TM * MOE_TM
    pad_end = jnp.cumsum(padded)
    pad_start = pad_end - padded
    dest = pad_start[top_e] + jnp.take_along_axis(rank, top_e, axis=1)

    blk_row0 = jnp.arange(n_blocks, dtype=jnp.int32) * MOE_TM
    block_e = jnp.minimum(jnp.sum((pad_end[None, :] <= blk_row0[:, None]).astype(jnp.int32), axis=1),
                          N_EXPERTS - 1).astype(jnp.int32)
    block_valid = (blk_row0 < pad_end[-1]).astype(jnp.int32)

    xrows = sc_scatter_rows(x_packed, dest[:, 0], dest[:, 1], n_rows)
    yrows = moe_expert_ffn(block_e, block_valid, xrows, wg, wu, wd)

    ytok = sc_gather_rows(yrows, dest.T.reshape(-1))
    return moe_finish(ytok, top_w, x2, ln_g, ln_b)


def _w_in_perm():
    zero = 3480
    perm = np.full((D_IN_P,), zero, np.int64)
    o_qkv, o_z, o_beta, o_decay, o_bq, o_bk, o_bv, o_cf, o_ci, o_cq, o_cg = (
        0, 1152, 1536, 1548, 1560, 1944, 2072, 2200, 2712, 2968, 3224)
    perm[COL_A_QKV:COL_A_QKV + 1152] = np.arange(o_qkv, o_qkv + 1152)
    perm[COL_A_Z:COL_A_Z + 384] = np.arange(o_z, o_z + 384)
    for j in range(3):
        for a in range(2):
            dst = COL_B_Q + j * LANES + a * HEAD_DIM
            src = o_bq + (j + 3 * a) * HEAD_DIM
            perm[dst:dst + HEAD_DIM] = np.arange(src, src + HEAD_DIM)
    perm[COL_B_K:COL_B_K + 128] = np.arange(o_bk, o_bk + 128)
    perm[COL_C_G:COL_C_G + 256] = np.arange(o_cg, o_cg + 256)
    perm[COL_B_V:COL_B_V + 128] = np.arange(o_bv, o_bv + 128)
    perm[COL_C_F:COL_C_F + 512] = np.arange(o_cf, o_cf + 512)
    perm[COL_C_I:COL_C_I + 256] = np.arange(o_ci, o_ci + 256)
    perm[COL_C_Q:COL_C_Q + 256] = np.arange(o_cq, o_cq + 256)
    for d, col in enumerate((COL_GATES0, COL_GATES1)):
        for p in range(A_HEADS // 2):
            dst = col + 4 * p
            perm[dst + 0] = o_beta + d * A_HEADS + 2 * p
            perm[dst + 1] = o_beta + d * A_HEADS + 2 * p + 1
            perm[dst + 2] = o_decay + d * A_HEADS + 2 * p
            perm[dst + 3] = o_decay + d * A_HEADS + 2 * p + 1
    return perm


def _take_static(w, perm, axis, zero_index=None):
    pieces, start = [], 0
    for i in range(1, len(perm) + 1):
        is_zero = perm[start] == zero_index
        if i < len(perm) and ((perm[i] == zero_index) if is_zero
                              else (perm[i] == perm[i - 1] + 1 and perm[i] != zero_index)):
            continue
        if is_zero:
            shape = list(w.shape)
            shape[axis] = i - start
            pieces.append(jnp.zeros(shape, w.dtype))
        else:
            pieces.append(lax.slice_in_dim(w, int(perm[start]), int(perm[i - 1]) + 1, axis=axis))
        start = i
    return jnp.concatenate(pieces, axis=axis)


def _w_out_perm():
    perm = np.arange(D_MIX)
    for j in range(3):
        for a in range(2):
            dst = A_WIDTH + j * LANES + a * HEAD_DIM
            src = A_WIDTH + (j + 3 * a) * HEAD_DIM
            perm[dst:dst + HEAD_DIM] = np.arange(src, src + HEAD_DIM)
    return perm


def _gdn_params(a_log, dt_bias):
    rows = np.array([4 * (h // 2) + 2 + h % 2 for h in range(A_HEADS)])
    prm = jnp.zeros((2, GATE_ROWS, LANES), F32)
    prm = prm.at[:, rows, 0].set(a_log.astype(F32))
    prm = prm.at[:, rows, 1].set(dt_bias.astype(F32))
    return prm.at[:, rows, 2].set(1.0)


def kernel(x, mem, w_in, conv_w, gdn_a_log, gdn_dt_bias, gdn_norm_w, q_norm_w, k_norm_w, hgrn_lb_logits,
           hgrn_norm_w, w_out, ln1_g, ln1_b, xq, xk, xv, xo, ln2_g, ln2_b, ffn_wg, ffn_wu, ffn_wd,
           moe_router, moe_wg, moe_wu, moe_wd, ln3_g, ln3_b):
    b, s, d = x.shape
    m = b * s
    cos_t, sin_t = rope_tables(s)
    lb_p = jax.nn.softmax(hgrn_lb_logits.astype(F32), axis=0)
    lb_c = jnp.cumsum(lb_p, axis=0)
    lower_bounds = lb_c - lb_c[0]
    in_perm = _w_in_perm()
    out_perm = _w_out_perm()
    mem_b = mem.astype(BF16).reshape(b * mem.shape[1], d)
    tile2 = lambda w: jnp.tile(w.astype(F32), 2)[None, :]
    row = lambda v: v.astype(F32)[None, :]

    x2 = x.reshape(m, d)
    xb2 = x2.astype(BF16)
    for l in range(DEPTH):
        w_in_p = _take_static(w_in[l].astype(BF16), in_perm, 1, zero_index=w_in.shape[2])
        wide2 = matmul(xb2, w_in_p[:, :D_WIDE], BF16, min(1024, m), 768)
        gates2 = matmul(xb2, w_in_p[:, D_WIDE:], F32, min(1024, m), D_IN_P - D_WIDE)
        wide3 = wide2.reshape(b, s, D_WIDE)
        gates3 = gates2.reshape(b, s, D_IN_P - D_WIDE)

        oa_f, oa_b = gdn_scan(wide3, gates3, conv_w[l], _gdn_params(gdn_a_log[l], gdn_dt_bias[l]))
        ob = gqa_attention(wide3, cos_t, sin_t, tile2(q_norm_w[l]), tile2(k_norm_w[l]), s)
        lb = lower_bounds[l][None, :]
        oc_f, oc_b = hgrn_scan(wide3, gates3, lb)

        w_out_p = _take_static(w_out[l].astype(BF16), out_perm, 0)
        x2, xb2 = mixer_output(oa_f.reshape(m, A_WIDTH), oa_b.reshape(m, A_WIDTH), wide2,
                               ob.reshape(m, B_WIDTH), oc_f.reshape(m, C_WIDTH), oc_b.reshape(m, C_WIDTH),
                               x2, tile2(gdn_norm_w[l]), tile2(hgrn_norm_w[l]), w_out_p,
                               row(ln1_g[l]), row(ln1_b[l]))

        k3 = matmul(mem_b, xk[l].astype(BF16), BF16, 256, 512).reshape(b, -1, d)
        v3 = matmul(mem_b, xv[l].astype(BF16), BF16, 256, 512).reshape(b, -1, d)
        x3, xb3 = cross_attention(xb2.reshape(b, s, d), x2.reshape(b, s, d), k3, v3,
                                  xq[l].astype(BF16), xo[l].astype(BF16), row(ln2_g[l]), row(ln2_b[l]))
        x2, xb2 = x3.reshape(m, d), xb3.reshape(m, d)

        if l % 2 == 0:
            x2, xb2 = dense_ffn(xb2, x2, ffn_wg[l // 2].astype(BF16), ffn_wu[l // 2].astype(BF16),
                                ffn_wd[l // 2].astype(BF16), row(ln3_g[l]), row(ln3_b[l]))
        else:
            x2 = moe_layer(xb2, x2, moe_router[l // 2], moe_wg[l // 2].astype(BF16),
                           moe_wu[l // 2].astype(BF16), moe_wd[l // 2].astype(BF16),
                           row(ln3_g[l]), row(ln3_b[l]))
            xb2 = x2.astype(BF16)
    return x2.reshape(b, s, d)
```

```python
import math

import numpy as np
import jax
import jax.numpy as jnp
from jax import lax
from jax.experimental import pallas as pl
from jax.experimental.pallas import tpu as pltpu
from jax.experimental.pallas import tpu_sc as plsc

F32 = jnp.float32
BF16 = jnp.bfloat16

D_MODEL = 1024
DEPTH = 2
HEAD_DIM = 64
A_HEADS = 6
A_WIDTH = A_HEADS * HEAD_DIM
B_Q_HEADS = 6
B_WIDTH = B_Q_HEADS * HEAD_DIM
C_HEADS = 4
C_WIDTH = C_HEADS * HEAD_DIM
D_MIX = A_WIDTH + B_WIDTH + C_WIDTH
CONV_K = 5
CHUNK = 64
GRID_W = 64
ROPE_AXIS_DIM = HEAD_DIM // 2
ROPE_THETA = 10000.0
X_HEADS = 4
X_HEAD_DIM = D_MODEL // X_HEADS
D_FF = 2816
N_EXPERTS = 8
TOP_K = 2
D_FF_EXPERT = 3584
DN_ALPHA = (2 * DEPTH) ** 0.25
LN_EPS = 1e-5
RMS_EPS = 1e-6

LANES = 128
SUBLANES = 8
HALO = 16
GROUP = 256
CHUNKS_PER_GROUP = GROUP // CHUNK
VMEM_LIMIT = 56 * 1024 * 1024

COL_A_QKV = 0
COL_A_Z = 1152
COL_B_Q = 1536
COL_B_K = 1920
COL_C_G = 2048
COL_B_V = 2304
COL_C_I = 2560
COL_C_Q = 2816
D_WIDE = 3072
COL_C_F = D_WIDE
COL_GATES0 = D_WIDE + 512
COL_GATES1 = D_WIDE + 640
D_IN_P = 3840


def _cparams(sem):
    return pltpu.CompilerParams(dimension_semantics=sem, vmem_limit_bytes=VMEM_LIMIT)


def _sigmoid(x):
    return 1.0 / (1.0 + jnp.exp(-x))


def _softplus(x):
    return jnp.maximum(x, 0.0) + jnp.log1p(jnp.exp(-jnp.abs(x)))


def _dot_nt(a, b):
    return lax.dot_general(a.astype(BF16), b.astype(BF16), (((1,), (1,)), ((), ())),
                           preferred_element_type=F32)


def _dot_sel(m01, x):
    m = jnp.where(m01, 1.0, 0.0).astype(BF16)
    hi = x.astype(BF16)
    r1 = x - hi.astype(F32)
    mid = r1.astype(BF16)
    lo = (r1 - mid.astype(F32)).astype(BF16)
    out = jnp.dot(m, hi, preferred_element_type=F32)
    out = out + jnp.dot(m, mid, preferred_element_type=F32)
    return out + jnp.dot(m, lo, preferred_element_type=F32)


def _dot_sel_right(x, m01):
    m = jnp.where(m01, 1.0, 0.0).astype(BF16)
    hi = x.astype(BF16)
    r1 = x - hi.astype(F32)
    mid = r1.astype(BF16)
    lo = (r1 - mid.astype(F32)).astype(BF16)
    out = jnp.dot(hi, m, preferred_element_type=F32)
    out = out + jnp.dot(mid, m, preferred_element_type=F32)
    return out + jnp.dot(lo, m, preferred_element_type=F32)


def _layer_norm_rows(y, g, b):
    mu = jnp.mean(y, axis=-1, keepdims=True)
    yc = y - mu
    var = jnp.mean(yc * yc, axis=-1, keepdims=True)
    return yc * lax.rsqrt(var + LN_EPS) * g + b


def _iota2(shape, dim):
    return lax.broadcasted_iota(jnp.int32, shape, dim)


def _head_sum(x, first_head):
    s0 = jnp.sum(jnp.where(first_head, x, 0.0), axis=-1, keepdims=True)
    s1 = jnp.sum(jnp.where(first_head, 0.0, x), axis=-1, keepdims=True)
    return jnp.where(first_head, s0, s1)


def _block_rows(t, block, row_in_block):
    parts = [jnp.broadcast_to(t[b * block + row_in_block:b * block + row_in_block + 1, :], (block, t.shape[1]))
             for b in range(GROUP // block)]
    return jnp.concatenate(parts, axis=0)


def _mm_kernel(x_ref, w_ref, o_ref):
    o_ref[...] = jnp.dot(x_ref[...], w_ref[...], preferred_element_type=F32).astype(o_ref.dtype)


def matmul(x, w, out_dtype, tm, tn):
    m, k = x.shape
    n = w.shape[1]
    return pl.pallas_call(
        _mm_kernel,
        out_shape=jax.ShapeDtypeStruct((m, n), out_dtype),
        grid=(n // tn, m // tm),
        in_specs=[pl.BlockSpec((tm, k), lambda j, i: (i, 0)),
                  pl.BlockSpec((k, tn), lambda j, i: (0, j))],
        out_specs=pl.BlockSpec((tm, tn), lambda j, i: (i, j)),
        compiler_params=_cparams(("parallel", "arbitrary")),
        name="matmul",
    )(x, w)


CHUNK_SHIFT = CHUNK.bit_length() - 1
HEAD_SHIFT = HEAD_DIM.bit_length() - 1


def _scan_masks(reverse):
    ri = _iota2((GROUP, GROUP), 0)
    ci = _iota2((GROUP, GROUP), 1)
    same_chunk = (ri >> CHUNK_SHIFT) == (ci >> CHUNK_SHIFT)
    if reverse:
        return ri, ci, same_chunk, same_chunk & (ri <= ci), same_chunk & (ri < ci)
    return ri, ci, same_chunk, same_chunk & (ri >= ci), same_chunk & (ri > ci)


GATE_ROWS = 16


class _GdnProblem:
    def __init__(self, **kw):
        self.__dict__.update(kw)


def _gdn_group(probs, same_chunk, h0):
    nh0 = jnp.logical_not(h0)
    sr = _iota2((CHUNK, GROUP), 0)
    sc = _iota2((CHUNK, GROUP), 1)
    eye_side = jnp.where(sr == (sc & (CHUNK - 1)), 1.0, 0.0)
    tl = _iota2((LANES, GROUP), 1)
    br = _iota2((LANES, LANES), 0)
    bc = _iota2((LANES, LANES), 1)
    bd128 = (br >> HEAD_SHIFT) == (bc >> HEAD_SHIFT)

    def block_diag(side):
        return jnp.where(same_chunk, jnp.concatenate([side] * CHUNKS_PER_GROUP, axis=0), 0.0).astype(BF16)

    for pr in probs:
        pr.gedge = _block_rows(pr.gc, CHUNK, 0 if pr.reverse else CHUNK - 1)
        pr.exp_g = jnp.exp(pr.gc)
        pr.kb = pr.k * pr.beta_t
        pr.rhs = jnp.concatenate([pr.v * pr.beta_t, pr.kb * pr.exp_g], axis=1).astype(BF16)
        pr.k16 = pr.k.astype(BF16)
        pr.decay, pr.x, pr.pw, pr.bd = [], [], [], []
        for h in range(2):
            decay = jnp.exp(jnp.where(pr.incl, pr.g_col[h] - pr.g_row[h], -1e30))
            kk = _dot_nt(jnp.where(h0 if h == 0 else nh0, pr.kb, 0.0), pr.k16)
            low = jnp.where(pr.strict, kk * decay, 0.0)
            a = -(low[0:CHUNK] + low[CHUNK:2 * CHUNK] + low[2 * CHUNK:3 * CHUNK] + low[3 * CHUNK:4 * CHUNK])
            pr.decay.append(decay)
            pr.x.append(eye_side + a)
            pr.pw.append(a)
            pr.bd.append(block_diag(a))

    for _ in range(5):
        for pr in probs:
            for h in range(2):
                pr.pw[h] = jnp.dot(pr.pw[h].astype(BF16), pr.bd[h], preferred_element_type=F32)
        for pr in probs:
            for h in range(2):
                pr.bd[h] = block_diag(pr.pw[h])
        for pr in probs:
            for h in range(2):
                pr.x[h] = pr.x[h] + jnp.dot(pr.x[h].astype(BF16), pr.bd[h], preferred_element_type=F32)

    for pr in probs:
        sols = [jnp.dot(block_diag(pr.x[h]), pr.rhs, preferred_element_type=F32) for h in range(2)]
        u = jnp.where(h0, sols[0][:, :LANES], sols[1][:, :LANES])
        w = jnp.where(h0, sols[0][:, LANES:], sols[1][:, LANES:])
        wu = jnp.concatenate([w, u], axis=1)
        wu16 = wu.astype(BF16)
        aw = []
        for h in range(2):
            attn = jnp.where(pr.incl, _dot_nt(jnp.where(h0 if h == 0 else nh0, pr.q, 0.0), pr.k16) * pr.decay[h], 0.0)
            aw.append(jnp.dot(attn.astype(BF16), wu16, preferred_element_type=F32))
        pr.qp = (pr.q * pr.exp_g - jnp.where(h0, aw[0][:, :LANES], aw[1][:, :LANES])).astype(BF16)
        pr.op = jnp.where(h0, aw[0][:, LANES:], aw[1][:, LANES:])
        kd_t = (pr.k * jnp.exp(pr.gedge - pr.gc)).T
        pr.kmat, pr.nmat = [], []
        for c in range(CHUNKS_PER_GROUP):
            km = jnp.dot(jnp.where((tl >> CHUNK_SHIFT) == c, kd_t, 0.0), wu, preferred_element_type=F32)
            pr.kmat.append(jnp.where(bd128, km[:, :LANES], 0.0).astype(BF16))
            pr.nmat.append(jnp.where(bd128, km[:, LANES:], 0.0))
        pr.out = [None] * CHUNKS_PER_GROUP

    for t in range(CHUNKS_PER_GROUP):
        for pr in probs:
            c = CHUNKS_PER_GROUP - 1 - t if pr.reverse else t
            rs = slice(c * CHUNK, (c + 1) * CHUNK)
            s16 = pr.state.astype(BF16)
            pr.out[c] = jnp.dot(pr.qp[rs], s16, preferred_element_type=F32) + pr.op[rs]
            cd = jnp.exp(pr.gedge[c * CHUNK:c * CHUNK + 1, :])
            pr.state = pr.state * cd - jnp.dot(pr.kmat[c], s16, preferred_element_type=F32) + pr.nmat[c]
    return [jnp.concatenate(pr.out, axis=0) for pr in probs]


def _gdn_kernel(xf_ref, xfp_ref, xfn_ref, xb_ref, xbp_ref, xbn_ref, cw_ref, gf_ref, gb_ref, prm_ref,
                of_ref, ob_ref, s_scr, c_scr):
    n = pl.program_id(1)
    n_groups = pl.num_programs(1)

    @pl.when(n == 0)
    def _():
        s_scr[...] = jnp.zeros_like(s_scr)

    lane = _iota2((GROUP, LANES), 1)
    h0 = lane < HEAD_DIM
    ri = _iota2((GROUP, GROUP), 0)
    ci = _iota2((GROUP, GROUP), 1)
    same_chunk = (ri >> CHUNK_SHIFT) == (ci >> CHUNK_SHIFT)
    cw = cw_ref[...]
    base = SUBLANES - CONV_K // 2
    pairs = A_HEADS // 2

    dirs = ((n, xf_ref, xfp_ref, xfn_ref, gf_ref), (n_groups - 1 - n, xb_ref, xbp_ref, xbn_ref, gb_ref))

    @pl.when(2 * n < n_groups)
    def _():
        for gi, x_ref, xp_ref, xn_ref, _ in dirs:
            prev = jnp.where(gi > 0, xp_ref[0].astype(F32)[HALO - SUBLANES:], 0.0)
            nxt = jnp.where(gi < n_groups - 1, xn_ref[0].astype(F32)[:SUBLANES], 0.0)
            ext = jnp.concatenate([prev, x_ref[0].astype(F32), nxt], axis=0)
            acc = None
            for t in range(CONV_K):
                shift = (CONV_K // 2 - t) % ext.shape[0]
                tap = (pltpu.roll(ext, shift, 0) if shift else ext)[SUBLANES:SUBLANES + GROUP, :] * cw[t:t + 1, :]
                acc = tap if acc is None else acc + tap
            act = acc * _sigmoid(acc)
            rows = pl.ds(pl.multiple_of(gi * GROUP, GROUP), GROUP)
            for p in range(pairs):
                q_sl = slice(p * LANES, (p + 1) * LANES)
                k_sl = slice(A_WIDTH + p * LANES, A_WIDTH + (p + 1) * LANES)
                q = act[:, q_sl]
                k = act[:, k_sl]
                c_scr[rows, q_sl] = q * lax.rsqrt(_head_sum(q * q, h0) + RMS_EPS) * (HEAD_DIM ** -0.5)
                c_scr[rows, k_sl] = k * lax.rsqrt(_head_sum(k * k, h0) + RMS_EPS)
            c_scr[rows, 2 * A_WIDTH:] = act[:, 2 * A_WIDTH:]

    probs = []
    for d, (gi, _, _, _, g_ref) in enumerate(dirs):
        reverse = d == 1
        lower, upper = same_chunk & (ri >= ci), same_chunk & (ri <= ci)
        incl = upper if reverse else lower
        strict = same_chunk & ((ri < ci) if reverse else (ri > ci))
        qkv = c_scr[pl.ds(pl.multiple_of(gi * GROUP, GROUP), GROUP), :]

        gt_t = g_ref[0].T[0:GATE_ROWS, :]
        prm = prm_ref[d]
        e_rows = jnp.where(prm[:, 2:3] > 0.5,
                           -jnp.exp(prm[:, 0:1]) * _softplus(gt_t + prm[:, 1:2]), _sigmoid(gt_t))
        g_rows = _dot_sel_right(e_rows, lower if reverse else upper)
        slab = jnp.concatenate([e_rows, g_rows, jnp.zeros((LANES - 2 * GATE_ROWS, GROUP), F32)], axis=0)
        cols = slab.T
        for p in range(pairs):
            sl = lambda part: slice(part * A_WIDTH + p * LANES, part * A_WIDTH + (p + 1) * LANES)
            col = lambda r: cols[:, r:r + 1]
            probs.append(_GdnProblem(
                reverse=reverse, incl=incl, strict=strict, slot=d * pairs + p,
                q=qkv[:, sl(0)], k=qkv[:, sl(1)], v=qkv[:, sl(2)],
                beta_t=jnp.where(h0, col(4 * p), col(4 * p + 1)),
                gc=jnp.where(h0, col(GATE_ROWS + 4 * p + 2), col(GATE_ROWS + 4 * p + 3)),
                g_col=(col(GATE_ROWS + 4 * p + 2), col(GATE_ROWS + 4 * p + 3)),
                g_row=(g_rows[4 * p + 2:4 * p + 3, :], g_rows[4 * p + 3:4 * p + 4, :]),
                state=s_scr[d * pairs + p]))

    outs = _gdn_group(probs, same_chunk, h0)
    for pr, out in zip(probs, outs):
        o_ref = ob_ref if pr.reverse else of_ref
        p = pr.slot % pairs
        o_ref[0, :, p * LANES:(p + 1) * LANES] = out.astype(BF16)
        s_scr[pr.slot] = pr.state


def gdn_scan(wide3, gates3, conv_w, prm):
    b, seq_len, _ = wide3.shape
    n_groups = seq_len // GROUP
    halo_per_group = GROUP // HALO
    n_halo = seq_len // HALO
    width = 3 * A_WIDTH
    fwd = lambda n: n
    bwd = lambda n: n_groups - 1 - n

    def x_specs(gidx):
        return [pl.BlockSpec((1, GROUP, width), lambda i, n: (i, gidx(n), 0)),
                pl.BlockSpec((1, HALO, width),
                             lambda i, n: (i, jnp.maximum(gidx(n) * halo_per_group - 1, 0), 0)),
                pl.BlockSpec((1, HALO, width),
                             lambda i, n: (i, jnp.minimum((gidx(n) + 1) * halo_per_group, n_halo - 1), 0))]

    out_sds = jax.ShapeDtypeStruct((b, seq_len, A_WIDTH), BF16)
    return pl.pallas_call(
        _gdn_kernel,
        out_shape=(out_sds, out_sds),
        grid=(b, n_groups),
        in_specs=x_specs(fwd) + x_specs(bwd) + [
            pl.BlockSpec((CONV_K, width), lambda i, n: (0, 0)),
            pl.BlockSpec((1, GROUP, LANES), lambda i, n: (i, fwd(n), (COL_GATES0 - D_WIDE) // LANES)),
            pl.BlockSpec((1, GROUP, LANES), lambda i, n: (i, bwd(n), (COL_GATES1 - D_WIDE) // LANES)),
            pl.BlockSpec((2, GATE_ROWS, LANES), lambda i, n: (0, 0, 0))],
        out_specs=(pl.BlockSpec((1, GROUP, A_WIDTH), lambda i, n: (i, fwd(n), 0)),
                   pl.BlockSpec((1, GROUP, A_WIDTH), lambda i, n: (i, bwd(n), 0))),
        scratch_shapes=[pltpu.VMEM((A_HEADS, LANES, LANES), F32), pltpu.VMEM((seq_len, width), F32)],
        compiler_params=_cparams(("parallel", "arbitrary")),
        name="gdn_scan",
    )(wide3, wide3, wide3, wide3, wide3, wide3, conv_w, gates3, gates3, prm)


EXP_CLAMP = 60.0


class _HgrnProblem:
    def __init__(self, **kw):
        self.__dict__.update(kw)


def _hgrn_group(probs):
    lane = _iota2((GROUP, LANES), 1)
    h0 = lane < HEAD_DIM
    nh0 = jnp.logical_not(h0)
    tl = _iota2((LANES, GROUP), 1)
    br = _iota2((LANES, LANES), 0)
    bc = _iota2((LANES, LANES), 1)
    bd128 = (br >> HEAD_SHIFT) == (bc >> HEAD_SHIFT)

    for pr in probs:
        ri, ci, same_chunk, incl, _ = pr.masks
        f = pr.lb + (1.0 - pr.lb) * _sigmoid(pr.f_raw)
        kk = 1.0 - f
        q = pr.q_raw * _sigmoid(pr.q_raw) * (HEAD_DIM ** -0.5)
        cum = _dot_sel(incl, jnp.log(f))
        pr.cedge = _block_rows(cum, CHUNK, 0 if pr.reverse else CHUNK - 1)
        pr.qd = q * jnp.exp(cum)
        pr.kd = kk * jnp.exp(pr.cedge - cum)
        half, quarter = CHUNK // 2, CHUNK // 4
        pos_in_chunk = lambda t: t & (CHUNK - 1)
        pos_in_half = lambda t: t & (half - 1)
        same_half = (ri >> (CHUNK_SHIFT - 1)) == (ci >> (CHUNK_SHIFT - 1))
        same_quarter = (ri >> (CHUNK_SHIFT - 2)) == (ci >> (CHUNK_SHIFT - 2))
        if pr.reverse:
            m32 = same_chunk & (pos_in_chunk(ri) < half) & (pos_in_chunk(ci) >= half)
            m16 = same_half & (pos_in_half(ri) < quarter) & (pos_in_half(ci) >= quarter)
            mdg = same_quarter & (ri <= ci)
            ref32 = _block_rows(cum, CHUNK, half)
            ref16 = _block_rows(cum, half, quarter)
            refdg = _block_rows(cum, quarter, quarter // 2)
        else:
            m32 = same_chunk & (pos_in_chunk(ri) >= half) & (pos_in_chunk(ci) < half)
            m16 = same_half & (pos_in_half(ri) >= quarter) & (pos_in_half(ci) < quarter)
            mdg = same_quarter & (ri >= ci)
            ref32 = _block_rows(cum, CHUNK, half - 1)
            ref16 = _block_rows(cum, half, quarter - 1)
            refdg = _block_rows(cum, quarter, quarter // 2 - 1)
        pr.level_masks = (m32, m16, mdg)
        pr.qs = (q * jnp.exp(jnp.minimum(cum - ref32, 0.0)),
                 q * jnp.exp(jnp.minimum(cum - ref16, 0.0)),
                 q * jnp.exp(jnp.clip(cum - refdg, -EXP_CLAMP, EXP_CLAMP)))
        pr.ks = ((kk * jnp.exp(jnp.minimum(ref32 - cum, 0.0))).astype(BF16),
                 (kk * jnp.exp(jnp.minimum(ref16 - cum, 0.0))).astype(BF16),
                 (kk * jnp.exp(jnp.clip(refdg - cum, -EXP_CLAMP, EXP_CLAMP))).astype(BF16))

    for pr in probs:
        v16 = pr.v.astype(BF16)
        m32, m16, mdg = pr.level_masks
        intra = []
        for h in range(2):
            mh = h0 if h == 0 else nh0
            a = jnp.where(m32, _dot_nt(jnp.where(mh, pr.qs[0], 0.0), pr.ks[0]),
                          jnp.where(m16, _dot_nt(jnp.where(mh, pr.qs[1], 0.0), pr.ks[1]),
                                    jnp.where(mdg, _dot_nt(jnp.where(mh, pr.qs[2], 0.0), pr.ks[2]), 0.0)))
            intra.append(jnp.dot(a.astype(BF16), v16, preferred_element_type=F32))
        pr.intra = jnp.where(h0, intra[0], intra[1])
        v_t = pr.v.T
        pr.upd = [jnp.where(bd128, jnp.dot(jnp.where((tl >> CHUNK_SHIFT) == c, v_t, 0.0), pr.kd,
                                           preferred_element_type=F32), 0.0)
                  for c in range(CHUNKS_PER_GROUP)]
        pr.out = [None] * CHUNKS_PER_GROUP

    for t in range(CHUNKS_PER_GROUP):
        for pr in probs:
            c = CHUNKS_PER_GROUP - 1 - t if pr.reverse else t
            rs = slice(c * CHUNK, (c + 1) * CHUNK)
            pr.out[c] = _dot_nt(pr.qd[rs], pr.state_t) + pr.intra[rs]
            cd = jnp.exp(pr.cedge[c * CHUNK:c * CHUNK + 1, :])
            pr.state_t = pr.state_t * cd + pr.upd[c]
    return [jnp.concatenate(pr.out, axis=0) for pr in probs]


def _hgrn_kernel(ff_ref, fb_ref, if_ref, ib_ref, qf_ref, qb_ref, lb_ref, of_ref, ob_ref, s_scr):
    n = pl.program_id(1)

    @pl.when(n == 0)
    def _():
        s_scr[...] = jnp.zeros_like(s_scr)

    pairs = C_HEADS // 2
    probs = []
    for d, (f_ref, i_ref, q_ref) in enumerate(((ff_ref, if_ref, qf_ref), (fb_ref, ib_ref, qb_ref))):
        masks = _scan_masks(d == 1)
        for p in range(pairs):
            sl = slice(p * LANES, (p + 1) * LANES)
            probs.append(_HgrnProblem(reverse=d == 1, masks=masks, slot=d * pairs + p, lb=lb_ref[:, sl],
                                      f_raw=f_ref[0, :, sl], v=i_ref[0, :, sl].astype(F32),
                                      q_raw=q_ref[0, :, sl].astype(F32), state_t=s_scr[d * pairs + p]))
    outs = _hgrn_group(probs)
    for pr, out in zip(probs, outs):
        o_ref = ob_ref if pr.reverse else of_ref
        p = pr.slot % pairs
        o_ref[0, :, p * LANES:(p + 1) * LANES] = out.astype(BF16)
        s_scr[pr.slot] = pr.state_t


def hgrn_scan(wide3, gates3, lb):
    b, seq_len, _ = wide3.shape
    n_groups = seq_len // GROUP
    fwd = lambda n: n
    bwd = lambda n: n_groups - 1 - n
    spec = lambda gidx, blk: pl.BlockSpec((1, GROUP, C_WIDTH), lambda i, n: (i, gidx(n), blk))
    f_blk = (COL_C_F - D_WIDE) // C_WIDTH
    out_sds = jax.ShapeDtypeStruct((b, seq_len, C_WIDTH), BF16)
    return pl.pallas_call(
        _hgrn_kernel,
        out_shape=(out_sds, out_sds),
        grid=(b, n_groups),
        in_specs=[spec(fwd, f_blk), spec(bwd, f_blk + 1),
                  spec(fwd, COL_C_I // C_WIDTH), spec(bwd, COL_C_I // C_WIDTH),
                  spec(fwd, COL_C_Q // C_WIDTH), spec(bwd, COL_C_Q // C_WIDTH),
                  pl.BlockSpec((1, C_WIDTH), lambda i, n: (0, 0))],
        out_specs=(spec(fwd, 0), spec(bwd, 0)),
        scratch_shapes=[pltpu.VMEM((C_HEADS, LANES, LANES), F32)],
        compiler_params=_cparams(("parallel", "arbitrary")),
        name="hgrn_scan",
    )(gates3, gates3, wide3, wide3, wide3, wide3, lb)


ATT_TQ = 512


def _norm_rope(x, w, cos, sin_signed, first_half16):
    lane = _iota2(x.shape, 1)
    h0 = lane < HEAD_DIM
    ms = _head_sum(x * x, h0) * (1.0 / HEAD_DIM)
    xn = x * lax.rsqrt(ms + RMS_EPS) * w
    partner = jnp.where(first_half16, pltpu.roll(xn, LANES - 16, 1), pltpu.roll(xn, 16, 1))
    return xn * cos + partner * sin_signed


def _attn_kernel(q_ref, k_ref, v_ref, cosq_ref, sinq_ref, cosk_ref, sink_ref, qw_ref, kw_ref, o_ref,
                 k_scr, v_scr):
    qi = pl.program_id(1)

    @pl.when(qi == 0)
    def _():
        kx = k_ref[0].astype(F32)
        lane_k = _iota2(kx.shape, 1)
        k_scr[...] = _norm_rope(kx, kw_ref[...], cosk_ref[...], sink_ref[...], (lane_k & 31) < 16).astype(BF16)
        v_scr[:, :LANES] = v_ref[0]
        v_scr[:, LANES:] = jnp.ones((v_scr.shape[0], LANES), BF16)

    lane = _iota2((ATT_TQ, LANES), 1)
    h0 = lane < HEAD_DIM
    fh = (lane & 31) < 16
    cos = cosq_ref[...]
    sin = sinq_ref[...]
    kmat = k_scr[...]
    vmat = v_scr[...]
    q_scale = (HEAD_DIM ** -0.5) * math.log2(math.e)
    for j in range(B_Q_HEADS // 2):
        qn = _norm_rope(q_ref[0, :, j * LANES:(j + 1) * LANES].astype(F32), qw_ref[...], cos, sin, fh) * q_scale
        outs = []
        for half in range(2):
            mh = h0 if half == 0 else jnp.logical_not(h0)
            s = _dot_nt(jnp.where(mh, qn, 0.0), kmat)
            m = jnp.max(s, axis=-1, keepdims=True)
            e = jnp.exp2((s - m).astype(BF16))
            pvl = jnp.dot(e, vmat, preferred_element_type=F32)
            outs.append(pvl[:, :LANES] / pvl[:, LANES:LANES + 1])
        o_ref[0, :, j * LANES:(j + 1) * LANES] = jnp.where(h0, outs[0], outs[1]).astype(BF16)


def gqa_attention(proj, cos_t, sin_t, q_w, k_w, seq_len):
    b = proj.shape[0]
    return pl.pallas_call(
        _attn_kernel,
        out_shape=jax.ShapeDtypeStruct((b, seq_len, B_WIDTH), BF16),
        grid=(b, seq_len // ATT_TQ),
        in_specs=[pl.BlockSpec((1, ATT_TQ, B_WIDTH), lambda i, t: (i, t, COL_B_Q // B_WIDTH)),
                  pl.BlockSpec((1, seq_len, LANES), lambda i, t: (i, 0, COL_B_K // LANES)),
                  pl.BlockSpec((1, seq_len, LANES), lambda i, t: (i, 0, COL_B_V // LANES)),
                  pl.BlockSpec((ATT_TQ, LANES), lambda i, t: (t, 0)),
                  pl.BlockSpec((ATT_TQ, LANES), lambda i, t: (t, 0)),
                  pl.BlockSpec((seq_len, LANES), lambda i, t: (0, 0)),
                  pl.BlockSpec((seq_len, LANES), lambda i, t: (0, 0)),
                  pl.BlockSpec((1, LANES), lambda i, t: (0, 0)),
                  pl.BlockSpec((1, LANES), lambda i, t: (0, 0))],
        out_specs=pl.BlockSpec((1, ATT_TQ, B_WIDTH), lambda i, t: (i, t, 0)),
        scratch_shapes=[pltpu.VMEM((seq_len, LANES), BF16), pltpu.VMEM((seq_len, 2 * LANES), BF16)],
        compiler_params=_cparams(("parallel", "arbitrary")),
        name="gqa_attention",
    )(proj, proj, proj, cos_t, sin_t, cos_t, sin_t, q_w, k_w)


def rope_tables(seq_len):
    rows = seq_len // GRID_W
    row = np.repeat(np.arange(rows, dtype=np.float64), GRID_W)
    col = np.tile(np.arange(GRID_W, dtype=np.float64), rows)
    inv_freq = ROPE_THETA ** (-np.arange(0, ROPE_AXIS_DIM, 2, dtype=np.float64) / ROPE_AXIS_DIM)
    ang_r = row[:, None] * inv_freq
    ang_c = col[:, None] * inv_freq
    cos64 = np.concatenate([np.cos(ang_r), np.cos(ang_r), np.cos(ang_c), np.cos(ang_c)], axis=1)
    sin64 = np.concatenate([-np.sin(ang_r), np.sin(ang_r), -np.sin(ang_c), np.sin(ang_c)], axis=1)
    return (jnp.asarray(np.tile(cos64, (1, 2)), dtype=F32), jnp.asarray(np.tile(sin64, (1, 2)), dtype=F32))


MIX_TM = 512


def _mixout_kernel(oaf_ref, oab_ref, z_ref, ob_ref, ocf_ref, ocb_ref, g_ref, x_ref, wa_ref, wc_ref,
                   wo_ref, lg_ref, lb_ref, xo_ref, xb_ref):
    lane = _iota2((MIX_TM, LANES), 1)
    h0 = lane < HEAD_DIM
    parts = []
    for j in range(A_WIDTH // LANES):
        sl = slice(j * LANES, (j + 1) * LANES)
        o = oaf_ref[:, sl].astype(F32) + oab_ref[:, sl].astype(F32)
        ms = _head_sum(o * o, h0) * (1.0 / HEAD_DIM)
        z = z_ref[:, sl].astype(F32)
        parts.append((o * lax.rsqrt(ms + RMS_EPS) * wa_ref[...] * (z * _sigmoid(z))).astype(BF16))
    parts.append(ob_ref[...])
    for j in range(C_WIDTH // LANES):
        sl = slice(j * LANES, (j + 1) * LANES)
        o = ocf_ref[:, sl].astype(F32) + ocb_ref[:, sl].astype(F32)
        ms = _head_sum(o * o, h0) * (1.0 / HEAD_DIM)
        parts.append((o * lax.rsqrt(ms + RMS_EPS) * wc_ref[...]
                      * _sigmoid(g_ref[:, sl].astype(F32))).astype(BF16))
    mixed = jnp.concatenate(parts, axis=1)
    h = jnp.dot(mixed, wo_ref[...], preferred_element_type=F32)
    y = _layer_norm_rows(DN_ALPHA * x_ref[...] + h, lg_ref[...], lb_ref[...])
    xo_ref[...] = y
    xb_ref[...] = y.astype(BF16)


def mixer_output(oa_f, oa_b, proj2, ob, oc_f, oc_b, x2, gdn_w, hgrn_w, w_out, ln_g, ln_b):
    m = x2.shape[0]
    row = lambda w: pl.BlockSpec((MIX_TM, w), lambda i: (i, 0))
    const = lambda r, c: pl.BlockSpec((r, c), lambda i: (0, 0))
    return pl.pallas_call(
        _mixout_kernel,
        out_shape=(jax.ShapeDtypeStruct((m, D_MODEL), F32), jax.ShapeDtypeStruct((m, D_MODEL), BF16)),
        grid=(m // MIX_TM,),
        in_specs=[row(A_WIDTH), row(A_WIDTH),
                  pl.BlockSpec((MIX_TM, A_WIDTH), lambda i: (i, COL_A_Z // A_WIDTH)),
                  row(B_WIDTH), row(C_WIDTH), row(C_WIDTH),
                  pl.BlockSpec((MIX_TM, C_WIDTH), lambda i: (i, COL_C_G // C_WIDTH)),
                  row(D_MODEL), const(1, LANES), const(1, LANES),
                  const(D_MIX, D_MODEL), const(1, D_MODEL), const(1, D_MODEL)],
        out_specs=(row(D_MODEL), row(D_MODEL)),
        compiler_params=_cparams(("parallel",)),
        name="mixer_output",
    )(oa_f, oa_b, proj2, ob, oc_f, oc_b, proj2, x2, gdn_w, hgrn_w, w_out, ln_g, ln_b)


XATT_TM = 512


def _xattn_kernel(xb_ref, x_ref, k_ref, v_ref, wq_ref, wo_ref, lg_ref, lb_ref, xo_ref, xob_ref):
    q = jnp.dot(xb_ref[0], wq_ref[...], preferred_element_type=F32) * (X_HEAD_DIM ** -0.5)
    outs = []
    for h in range(X_HEADS):
        sl = slice(h * X_HEAD_DIM, (h + 1) * X_HEAD_DIM)
        s = _dot_nt(q[:, sl], k_ref[0, :, sl])
        m = jnp.max(s, axis=-1, keepdims=True)
        e = jnp.exp(s - m)
        l = jnp.sum(e, axis=-1, keepdims=True)
        outs.append((jnp.dot(e.astype(BF16), v_ref[0, :, sl], preferred_element_type=F32) / l).astype(BF16))
    o = jnp.concatenate(outs, axis=1)
    c = jnp.dot(o, wo_ref[...], preferred_element_type=F32)
    y = _layer_norm_rows(DN_ALPHA * x_ref[0] + c, lg_ref[...], lb_ref[...])
    xo_ref[0] = y
    xob_ref[0] = y.astype(BF16)


def cross_attention(xb3, x3, k3, v3, wq, wo, ln_g, ln_b):
    b, s, _ = x3.shape
    mem = k3.shape[1]
    row = pl.BlockSpec((1, XATT_TM, D_MODEL), lambda i, t: (i, t, 0))
    const = lambda r, c: pl.BlockSpec((r, c), lambda i, t: (0, 0))
    kv = pl.BlockSpec((1, mem, D_MODEL), lambda i, t: (i, 0, 0))
    return pl.pallas_call(
        _xattn_kernel,
        out_shape=(jax.ShapeDtypeStruct((b, s, D_MODEL), F32), jax.ShapeDtypeStruct((b, s, D_MODEL), BF16)),
        grid=(b, s // XATT_TM),
        in_specs=[row, row, kv, kv, const(D_MODEL, D_MODEL), const(D_MODEL, D_MODEL),
                  const(1, D_MODEL), const(1, D_MODEL)],
        out_specs=(row, row),
        compiler_params=_cparams(("parallel", "parallel")),
        name="cross_attention",
    )(xb3, x3, k3, v3, wq, wo, ln_g, ln_b)


FFN_TM = 512
FFN_TF = 1408


def _ffn_kernel(xb_ref, x_ref, wg_ref, wu_ref, wd_ref, lg_ref, lb_ref, xo_ref, xob_ref):
    xb = xb_ref[...]
    acc = None
    for j in range(D_FF // FFN_TF):
        sl = slice(j * FFN_TF, (j + 1) * FFN_TF)
        g = jnp.dot(xb, wg_ref[:, sl], preferred_element_type=F32)
        u = jnp.dot(xb, wu_ref[:, sl], preferred_element_type=F32)
        h = (g * _sigmoid(g) * u).astype(BF16)
        part = jnp.dot(h, wd_ref[sl, :], preferred_element_type=F32)
        acc = part if acc is None else acc + part
    y = _layer_norm_rows(DN_ALPHA * x_ref[...] + acc, lg_ref[...], lb_ref[...])
    xo_ref[...] = y
    xob_ref[...] = y.astype(BF16)


def dense_ffn(xb2, x2, wg, wu, wd, ln_g, ln_b):
    m = x2.shape[0]
    row = pl.BlockSpec((FFN_TM, D_MODEL), lambda i: (i, 0))
    const = pl.BlockSpec((1, D_MODEL), lambda i: (0, 0))
    resident = lambda r, c: pl.BlockSpec((r, c), lambda i: (0, 0), pipeline_mode=pl.Buffered(1))
    return pl.pallas_call(
        _ffn_kernel,
        out_shape=(jax.ShapeDtypeStruct((m, D_MODEL), F32), jax.ShapeDtypeStruct((m, D_MODEL), BF16)),
        grid=(m // FFN_TM,),
        in_specs=[row, row, resident(D_MODEL, D_FF), resident(D_MODEL, D_FF), resident(D_FF, D_MODEL),
                  const, const],
        out_specs=(row, row),
        compiler_params=_cparams(("parallel",)),
        name="dense_ffn",
    )(xb2, x2, wg, wu, wd, ln_g, ln_b)


MOE_TM = 512
MOE_WIN = 512
MOE_TF = 1792
ROUTER_TM = 512


def _router_kernel(xb_ref, wr_ref, e_ref, w_ref, xp_ref):
    logits = jnp.dot(xb_ref[...], wr_ref[...], preferred_element_type=F32)
    lane = _iota2(logits.shape, 1)
    neg = jnp.float32(-jnp.inf)
    lane_f = lane.astype(F32)
    lg = jnp.where(lane < N_EXPERTS, logits, neg)
    m1 = jnp.max(lg, axis=-1, keepdims=True)
    i1 = jnp.min(jnp.where(lg == m1, lane_f, float(LANES)), axis=-1, keepdims=True)
    lg2 = jnp.where(lane_f == i1, neg, lg)
    m2 = jnp.max(lg2, axis=-1, keepdims=True)
    i2 = jnp.min(jnp.where(lg2 == m2, lane_f, float(LANES)), axis=-1, keepdims=True)
    t = jnp.exp(m2 - m1)
    w1 = 1.0 / (1.0 + t)
    w2 = t / (1.0 + t)
    e_ref[...] = jnp.where(lane == 0, i1, jnp.where(lane == 1, i2, 0.0)).astype(jnp.int32)[:, :N_EXPERTS]
    w_ref[...] = jnp.where(lane == 0, w1, jnp.where(lane == 1, w2, 0.0))[:, :N_EXPERTS]
    _store_packed(xp_ref, xb_ref[...].astype(F32))


def moe_router(xb2, w_router_p):
    m = xb2.shape[0]
    lane_out = pl.BlockSpec((ROUTER_TM, N_EXPERTS), lambda i: (i, 0))
    return pl.pallas_call(
        _router_kernel,
        out_shape=(jax.ShapeDtypeStruct((m, N_EXPERTS), jnp.int32), jax.ShapeDtypeStruct((m, N_EXPERTS), F32),
                   jax.ShapeDtypeStruct((SC_SPLIT, m, LANES), jnp.uint32)),
        grid=(m // ROUTER_TM,),
        in_specs=[pl.BlockSpec((ROUTER_TM, D_MODEL), lambda i: (i, 0)),
                  pl.BlockSpec((D_MODEL, LANES), lambda i: (0, 0))],
        out_specs=(lane_out, lane_out, pl.BlockSpec((SC_SPLIT, ROUTER_TM, LANES), lambda i: (0, i, 0))),
        compiler_params=_cparams(("parallel",)),
        name="moe_router",
    )(xb2, w_router_p)


def _moe_ffn_kernel(be_ref, bv_ref, x_ref, wg_ref, wu_ref, wd_ref, o_ref, acc_ref):
    i = pl.program_id(0)
    j = pl.program_id(1)
    valid = bv_ref[i] != 0

    @pl.when(j == 0)
    def _():
        acc_ref[...] = jnp.zeros_like(acc_ref)

    @pl.when(valid)
    def _():
        xb = _load_packed(x_ref).astype(BF16)
        g = jnp.dot(xb, wg_ref[0], preferred_element_type=F32)
        u = jnp.dot(xb, wu_ref[0], preferred_element_type=F32)
        h = (g * _sigmoid(g) * u).astype(BF16)
        acc_ref[...] += jnp.dot(h, wd_ref[0], preferred_element_type=F32)

    @pl.when(j == pl.num_programs(1) - 1)
    def _():
        _store_packed(o_ref, acc_ref[...])


def moe_expert_ffn(block_e, block_valid, xrows, wg, wu, wd):
    n_rows = xrows.shape[1]
    grid_spec = pltpu.PrefetchScalarGridSpec(
        num_scalar_prefetch=2,
        grid=(n_rows // MOE_TM, D_FF_EXPERT // MOE_TF),
        in_specs=[pl.BlockSpec((SC_SPLIT, MOE_TM, LANES), lambda i, j, be, bv: (0, i, 0)),
                  pl.BlockSpec((1, D_MODEL, MOE_TF), lambda i, j, be, bv: (be[i], 0, j)),
                  pl.BlockSpec((1, D_MODEL, MOE_TF), lambda i, j, be, bv: (be[i], 0, j)),
                  pl.BlockSpec((1, MOE_TF, D_MODEL), lambda i, j, be, bv: (be[i], j, 0))],
        out_specs=pl.BlockSpec((SC_SPLIT, MOE_TM, LANES), lambda i, j, be, bv: (0, i, 0)),
        scratch_shapes=[pltpu.VMEM((MOE_TM, D_MODEL), F32)],
    )
    return pl.pallas_call(
        _moe_ffn_kernel,
        out_shape=jax.ShapeDtypeStruct((SC_SPLIT, n_rows, LANES), jnp.uint32),
        grid_spec=grid_spec,
        compiler_params=_cparams(("arbitrary", "arbitrary")),
        name="moe_expert_ffn",
    )(block_e, block_valid, xrows, wg, wu, wd)


PACKED_WIDTH = D_MODEL // 2
SC_ROWS = 128
SC_SPLIT = 4


def _store_packed(o_ref, y):
    lo = lax.bitcast_convert_type(y[:, :PACKED_WIDTH].astype(BF16).astype(F32), jnp.uint32) >> 16
    hi = lax.bitcast_convert_type(y[:, PACKED_WIDTH:].astype(BF16).astype(F32), jnp.uint32)
    word = hi | lo
    for c in range(SC_SPLIT):
        o_ref[c] = word[:, c * LANES:(c + 1) * LANES]


def _load_packed(ref):
    w = jnp.concatenate([ref[c] for c in range(SC_SPLIT)], axis=1)
    lo = lax.bitcast_convert_type(w << 16, F32)
    hi = lax.bitcast_convert_type(w & jnp.uint32(0xFFFF0000), F32)
    return jnp.concatenate([lo, hi], axis=1)


def sc_gather_rows(data, idx):
    pieces, rows, width = data.shape
    idx = (jnp.arange(pieces, dtype=jnp.int32)[:, None] * rows + idx[None, :]).reshape(-1)
    return _sc_gather(data.reshape(pieces * rows, width), idx).reshape(pieces, -1, width)


def _sc_gather(data, idx):
    n = idx.shape[0]
    d = data.shape[1]
    sc = plsc.get_sparse_core_info()
    mesh = plsc.VectorSubcoreMesh(core_axis_name="core", subcore_axis_name="subcore")
    steps = n // (SC_ROWS * sc.num_cores)

    @pl.kernel(out_type=jax.ShapeDtypeStruct((n, d), data.dtype), mesh=mesh, scratch_types=[])
    def gather_kernel(x_hbm, i_hbm, o_hbm):
        def body(i_vmem, o_vmem):
            pltpu.sync_copy(x_hbm.at[i_vmem.at[0]], o_vmem)

        pltpu.emit_pipeline(
            body,
            grid=(sc.num_cores, steps),
            in_specs=[pl.BlockSpec((1, SC_ROWS), index_map=lambda c, i: (0, c * steps + i))],
            out_specs=[pl.BlockSpec((SC_ROWS, d), index_map=lambda c, i: (c * steps + i, 0))],
            core_axis_name=("core", "subcore"),
            dimension_semantics=(pltpu.PARALLEL, pltpu.PARALLEL),
        )(i_hbm, o_hbm)

    return gather_kernel(data, idx.reshape(1, n))


def sc_scatter_rows(data, idx0, idx1, n_out):
    pieces, n, width = data.shape
    off = jnp.arange(pieces, dtype=jnp.int32)[:, None] * n_out
    out = _sc_scatter(data.reshape(pieces * n, width), (off + idx0[None, :]).reshape(-1),
                      (off + idx1[None, :]).reshape(-1), pieces * n_out)
    return out.reshape(pieces, n_out, width)


def _sc_scatter(data, idx0, idx1, n_out):
    n, d = data.shape
    sc = plsc.get_sparse_core_info()
    mesh = plsc.VectorSubcoreMesh(core_axis_name="core", subcore_axis_name="subcore")
    steps = n // (SC_ROWS * sc.num_cores)
    idx_spec = pl.BlockSpec((1, SC_ROWS), index_map=lambda c, i: (0, c * steps + i))

    @pl.kernel(out_type=jax.ShapeDtypeStruct((n_out, d), data.dtype), mesh=mesh, scratch_types=[])
    def scatter_kernel(x_hbm, i0_hbm, i1_hbm, o_hbm):
        def body(x_vmem, i0_vmem, i1_vmem):
            pltpu.sync_copy(x_vmem, o_hbm.at[i0_vmem.at[0]])
            pltpu.sync_copy(x_vmem, o_hbm.at[i1_vmem.at[0]])

        pltpu.emit_pipeline(
            body,
            grid=(sc.num_cores, steps),
            in_specs=[pl.BlockSpec((SC_ROWS, d), index_map=lambda c, i: (c * steps + i, 0)), idx_spec, idx_spec],
            out_specs=[],
            core_axis_name=("core", "subcore"),
            dimension_semantics=(pltpu.PARALLEL, pltpu.PARALLEL),
        )(x_hbm, i0_hbm, i1_hbm)

    return scatter_kernel(data, idx0.reshape(1, n), idx1.reshape(1, n))


def _moe_finish_kernel(y0_ref, y1_ref, w_ref, x_ref, lg_ref, lb_ref, xo_ref):
    wt = w_ref[...]
    y = wt[:, 0:1] * _load_packed(y0_ref) + wt[:, 1:2] * _load_packed(y1_ref)
    xo_ref[...] = _layer_norm_rows(DN_ALPHA * x_ref[...] + y, lg_ref[...], lb_ref[...])


def moe_finish(ytok, top_w, x2, ln_g, ln_b):
    m = x2.shape[0]
    n_win = m // MOE_WIN
    packed = lambda off: pl.BlockSpec((SC_SPLIT, MOE_WIN, LANES), lambda i: (0, i + off, 0))
    row = pl.BlockSpec((MOE_WIN, D_MODEL), lambda i: (i, 0))
    const = pl.BlockSpec((1, D_MODEL), lambda i: (0, 0))
    return pl.pallas_call(
        _moe_finish_kernel,
        out_shape=jax.ShapeDtypeStruct((m, D_MODEL), F32),
        grid=(n_win,),
        in_specs=[packed(0), packed(n_win), pl.BlockSpec((MOE_WIN, TOP_K), lambda i: (i, 0)), row, const, const],
        out_specs=row,
        compiler_params=_cparams(("parallel",)),
        name="moe_finish",
    )(ytok, ytok, top_w, x2, ln_g, ln_b)


def moe_layer(xb2, x2, w_router, wg, wu, wd, ln_g, ln_b):
    n = x2.shape[0]
    n_rows = n * TOP_K + N_EXPERTS * MOE_TM
    n_blocks = n_rows // MOE_TM

    wr = jnp.pad(w_router, ((0, 0), (0, LANES - N_EXPERTS))).astype(BF16)
    top_e_t, top_w_t, x_packed = moe_router(xb2, wr)
    top_e = top_e_t[:, :TOP_K]
    top_w = top_w_t[:, :TOP_K]

    experts = jnp.arange(N_EXPERTS, dtype=jnp.int32)
    tok_onehot = ((top_e[:, 0:1] == experts[None, :]) | (top_e[:, 1:2] == experts[None, :])).astype(jnp.int32)
    csum = jnp.cumsum(tok_onehot, axis=0)
    counts = csum[-1]
    rank = csum - tok_onehot
    padded = (counts + MOE_TM - 1) // MOE_TM * MOE_TM
    pad_end = jnp.cumsum(padded)
    pad_start = pad_end - padded
    dest = pad_start[top_e] + jnp.take_along_axis(rank, top_e, axis=1)

    blk_row0 = jnp.arange(n_blocks, dtype=jnp.int32) * MOE_TM
    block_e = jnp.minimum(jnp.sum((pad_end[None, :] <= blk_row0[:, None]).astype(jnp.int32), axis=1),
                          N_EXPERTS - 1).astype(jnp.int32)
    block_valid = (blk_row0 < pad_end[-1]).astype(jnp.int32)

    n_pad = n_rows - n * TOP_K
    seg_lo = jnp.concatenate([pad_start + counts, pad_end[-1:]])
    seg_len = jnp.concatenate([padded - counts, n_rows - pad_end[-1:]])
    seg_end = jnp.cumsum(seg_len)
    k = jnp.arange(n_pad, dtype=jnp.int32)
    seg = jnp.sum((seg_end[None, :] <= k[:, None]).astype(jnp.int32), axis=1)
    pad_rows = (seg_lo[seg] + k - (seg_end - seg_len)[seg]).astype(jnp.int32)

    x_ext = jnp.pad(x_packed, ((0, 0), (0, n_pad // 2), (0, 0)))
    xrows = sc_scatter_rows(x_ext, jnp.concatenate([dest[:, 0], pad_rows[:n_pad // 2]]),
                            jnp.concatenate([dest[:, 1], pad_rows[n_pad // 2:]]), n_rows)
    yrows = moe_expert_ffn(block_e, block_valid, xrows, wg, wu, wd)

    ytok = sc_gather_rows(yrows, dest.T.reshape(-1))
    return moe_finish(ytok, top_w, x2, ln_g, ln_b)


def _w_in_perm():
    zero = 3480
    perm = np.full((D_IN_P,), zero, np.int64)
    o_qkv, o_z, o_beta, o_decay, o_bq, o_bk, o_bv, o_cf, o_ci, o_cq, o_cg = (
        0, 1152, 1536, 1548, 1560, 1944, 2072, 2200, 2712, 2968, 3224)
    perm[COL_A_QKV:COL_A_QKV + 1152] = np.arange(o_qkv, o_qkv + 1152)
    perm[COL_A_Z:COL_A_Z + 384] = np.arange(o_z, o_z + 384)
    for j in range(3):
        for a in range(2):
            dst = COL_B_Q + j * LANES + a * HEAD_DIM
            src = o_bq + (j + 3 * a) * HEAD_DIM
            perm[dst:dst + HEAD_DIM] = np.arange(src, src + HEAD_DIM)
    perm[COL_B_K:COL_B_K + 128] = np.arange(o_bk, o_bk + 128)
    perm[COL_C_G:COL_C_G + 256] = np.arange(o_cg, o_cg + 256)
    perm[COL_B_V:COL_B_V + 128] = np.arange(o_bv, o_bv + 128)
    perm[COL_C_F:COL_C_F + 512] = np.arange(o_cf, o_cf + 512)
    perm[COL_C_I:COL_C_I + 256] = np.arange(o_ci, o_ci + 256)
    perm[COL_C_Q:COL_C_Q + 256] = np.arange(o_cq, o_cq + 256)
    for d, col in enumerate((COL_GATES0, COL_GATES1)):
        for p in range(A_HEADS // 2):
            dst = col + 4 * p
            perm[dst + 0] = o_beta + d * A_HEADS + 2 * p
            perm[dst + 1] = o_beta + d * A_HEADS + 2 * p + 1
            perm[dst + 2] = o_decay + d * A_HEADS + 2 * p
            perm[dst + 3] = o_decay + d * A_HEADS + 2 * p + 1
    return perm


def _take_static(w, perm, axis, zero_index=None):
    pieces, start = [], 0
    for i in range(1, len(perm) + 1):
        is_zero = perm[start] == zero_index
        if i < len(perm) and ((perm[i] == zero_index) if is_zero
                              else (perm[i] == perm[i - 1] + 1 and perm[i] != zero_index)):
            continue
        if is_zero:
            shape = list(w.shape)
            shape[axis] = i - start
            pieces.append(jnp.zeros(shape, w.dtype))
        else:
            pieces.append(lax.slice_in_dim(w, int(perm[start]), int(perm[i - 1]) + 1, axis=axis))
        start = i
    return jnp.concatenate(pieces, axis=axis)


def _w_out_perm():
    perm = np.arange(D_MIX)
    for j in range(3):
        for a in range(2):
            dst = A_WIDTH + j * LANES + a * HEAD_DIM
            src = A_WIDTH + (j + 3 * a) * HEAD_DIM
            perm[dst:dst + HEAD_DIM] = np.arange(src, src + HEAD_DIM)
    return perm


def _gdn_params(a_log, dt_bias):
    rows = np.array([4 * (h // 2) + 2 + h % 2 for h in range(A_HEADS)])
    prm = jnp.zeros((2, GATE_ROWS, LANES), F32)
    prm = prm.at[:, rows, 0].set(a_log.astype(F32))
    prm = prm.at[:, rows, 1].set(dt_bias.astype(F32))
    return prm.at[:, rows, 2].set(1.0)


def kernel(x, mem, w_in, conv_w, gdn_a_log, gdn_dt_bias, gdn_norm_w, q_norm_w, k_norm_w, hgrn_lb_logits,
           hgrn_norm_w, w_out, ln1_g, ln1_b, xq, xk, xv, xo, ln2_g, ln2_b, ffn_wg, ffn_wu, ffn_wd,
           moe_router, moe_wg, moe_wu, moe_wd, ln3_g, ln3_b):
    b, s, d = x.shape
    m = b * s
    cos_t, sin_t = rope_tables(s)
    lb_p = jax.nn.softmax(hgrn_lb_logits.astype(F32), axis=0)
    lb_c = jnp.cumsum(lb_p, axis=0)
    lower_bounds = lb_c - lb_c[0]
    in_perm = _w_in_perm()
    out_perm = _w_out_perm()
    mem_b = mem.astype(BF16).reshape(b * mem.shape[1], d)
    tile2 = lambda w: jnp.tile(w.astype(F32), 2)[None, :]
    row = lambda v: v.astype(F32)[None, :]

    x2 = x.reshape(m, d)
    xb2 = x2.astype(BF16)
    for l in range(DEPTH):
        w_in_p = _take_static(w_in[l].astype(BF16), in_perm, 1, zero_index=w_in.shape[2])
        wide2 = matmul(xb2, w_in_p[:, :D_WIDE], BF16, min(1024, m), 768)
        gates2 = matmul(xb2, w_in_p[:, D_WIDE:], F32, min(1024, m), D_IN_P - D_WIDE)
        wide3 = wide2.reshape(b, s, D_WIDE)
        gates3 = gates2.reshape(b, s, D_IN_P - D_WIDE)

        oa_f, oa_b = gdn_scan(wide3, gates3, conv_w[l], _gdn_params(gdn_a_log[l], gdn_dt_bias[l]))
        ob = gqa_attention(wide3, cos_t, sin_t, tile2(q_norm_w[l]), tile2(k_norm_w[l]), s)
        lb = lower_bounds[l][None, :]
        oc_f, oc_b = hgrn_scan(wide3, gates3, lb)

        w_out_p = _take_static(w_out[l].astype(BF16), out_perm, 0)
        x2, xb2 = mixer_output(oa_f.reshape(m, A_WIDTH), oa_b.reshape(m, A_WIDTH), wide2,
                               ob.reshape(m, B_WIDTH), oc_f.reshape(m, C_WIDTH), oc_b.reshape(m, C_WIDTH),
                               x2, tile2(gdn_norm_w[l]), tile2(hgrn_norm_w[l]), w_out_p,
                               row(ln1_g[l]), row(ln1_b[l]))

        k3 = matmul(mem_b, xk[l].astype(BF16), BF16, 256, 512).reshape(b, -1, d)
        v3 = matmul(mem_b, xv[l].astype(BF16), BF16, 256, 512).reshape(b, -1, d)
        x3, xb3 = cross_attention(xb2.reshape(b, s, d), x2.reshape(b, s, d), k3, v3,
                                  xq[l].astype(BF16), xo[l].astype(BF16), row(ln2_g[l]), row(ln2_b[l]))
        x2, xb2 = x3.reshape(m, d), xb3.reshape(m, d)

        if l % 2 == 0:
            x2, xb2 = dense_ffn(xb2, x2, ffn_wg[l // 2].astype(BF16), ffn_wu[l // 2].astype(BF16),
                                ffn_wd[l // 2].astype(BF16), row(ln3_g[l]), row(ln3_b[l]))
        else:
            x2 = moe_layer(xb2, x2, moe_router[l // 2], moe_wg[l // 2].astype(BF16),
                           moe_wu[l // 2].astype(BF16), moe_wd[l // 2].astype(BF16),
                           row(ln3_g[l]), row(ln3_b[l]))
            xb2 = x2.astype(BF16)
    return x2.reshape(b, s, d)
```

```python
import math

import numpy as np
import jax
import jax.numpy as jnp
from jax import lax
from jax.experimental import pallas as pl
from jax.experimental.pallas import tpu as pltpu
from jax.experimental.pallas import tpu_sc as plsc

F32 = jnp.float32
BF16 = jnp.bfloat16

D_MODEL = 1024
DEPTH = 2
HEAD_DIM = 64
A_HEADS = 6
A_WIDTH = A_HEADS * HEAD_DIM
B_Q_HEADS = 6
B_WIDTH = B_Q_HEADS * HEAD_DIM
C_HEADS = 4
C_WIDTH = C_HEADS * HEAD_DIM
D_MIX = A_WIDTH + B_WIDTH + C_WIDTH
CONV_K = 5
CHUNK = 64
GRID_W = 64
ROPE_AXIS_DIM = HEAD_DIM // 2
ROPE_THETA = 10000.0
X_HEADS = 4
X_HEAD_DIM = D_MODEL // X_HEADS
D_FF = 2816
N_EXPERTS = 8
TOP_K = 2
D_FF_EXPERT = 3584
DN_ALPHA = (2 * DEPTH) ** 0.25
LN_EPS = 1e-5
RMS_EPS = 1e-6

LANES = 128
SUBLANES = 8
HALO = 16
GROUP = 256
CHUNKS_PER_GROUP = GROUP // CHUNK
VMEM_LIMIT = 56 * 1024 * 1024

COL_A_QKV = 0
COL_A_Z = 1152
COL_B_Q = 1536
COL_B_K = 1920
COL_C_G = 2048
COL_B_V = 2304
COL_C_I = 2560
COL_C_Q = 2816
D_WIDE = 3072
COL_C_F = D_WIDE
COL_GATES0 = D_WIDE + 512
COL_GATES1 = D_WIDE + 640
D_IN_P = 3840


def _cparams(sem):
    return pltpu.CompilerParams(dimension_semantics=sem, vmem_limit_bytes=VMEM_LIMIT)


def _sigmoid(x):
    return 1.0 / (1.0 + jnp.exp(-x))


def _softplus(x):
    return jnp.maximum(x, 0.0) + jnp.log1p(jnp.exp(-jnp.abs(x)))


def _dot_nt(a, b):
    return lax.dot_general(a.astype(BF16), b.astype(BF16), (((1,), (1,)), ((), ())),
                           preferred_element_type=F32)


def _dot_sel(m01, x):
    m = jnp.where(m01, 1.0, 0.0).astype(BF16)
    hi = x.astype(BF16)
    r1 = x - hi.astype(F32)
    mid = r1.astype(BF16)
    lo = (r1 - mid.astype(F32)).astype(BF16)
    out = jnp.dot(m, hi, preferred_element_type=F32)
    out = out + jnp.dot(m, mid, preferred_element_type=F32)
    return out + jnp.dot(m, lo, preferred_element_type=F32)


def _dot_sel_right(x, m01):
    m = jnp.where(m01, 1.0, 0.0).astype(BF16)
    hi = x.astype(BF16)
    r1 = x - hi.astype(F32)
    mid = r1.astype(BF16)
    lo = (r1 - mid.astype(F32)).astype(BF16)
    out = jnp.dot(hi, m, preferred_element_type=F32)
    out = out + jnp.dot(mid, m, preferred_element_type=F32)
    return out + jnp.dot(lo, m, preferred_element_type=F32)


def _layer_norm_rows(y, g, b):
    mu = jnp.mean(y, axis=-1, keepdims=True)
    yc = y - mu
    var = jnp.mean(yc * yc, axis=-1, keepdims=True)
    return yc * lax.rsqrt(var + LN_EPS) * g + b


def _iota2(shape, dim):
    return lax.broadcasted_iota(jnp.int32, shape, dim)


def _head_sum(x, first_head):
    s0 = jnp.sum(jnp.where(first_head, x, 0.0), axis=-1, keepdims=True)
    s1 = jnp.sum(jnp.where(first_head, 0.0, x), axis=-1, keepdims=True)
    return jnp.where(first_head, s0, s1)


def _block_rows(t, block, row_in_block):
    parts = [jnp.broadcast_to(t[b * block + row_in_block:b * block + row_in_block + 1, :], (block, t.shape[1]))
             for b in range(GROUP // block)]
    return jnp.concatenate(parts, axis=0)


def _mm_kernel(x_ref, w_ref, o_ref):
    o_ref[...] = jnp.dot(x_ref[...], w_ref[...], preferred_element_type=F32).astype(o_ref.dtype)


def matmul(x, w, out_dtype, tm, tn):
    m, k = x.shape
    n = w.shape[1]
    return pl.pallas_call(
        _mm_kernel,
        out_shape=jax.ShapeDtypeStruct((m, n), out_dtype),
        grid=(n // tn, m // tm),
        in_specs=[pl.BlockSpec((tm, k), lambda j, i: (i, 0)),
                  pl.BlockSpec((k, tn), lambda j, i: (0, j))],
        out_specs=pl.BlockSpec((tm, tn), lambda j, i: (i, j)),
        compiler_params=_cparams(("parallel", "arbitrary")),
        name="matmul",
    )(x, w)


CHUNK_SHIFT = CHUNK.bit_length() - 1
HEAD_SHIFT = HEAD_DIM.bit_length() - 1


def _scan_masks(reverse):
    ri = _iota2((GROUP, GROUP), 0)
    ci = _iota2((GROUP, GROUP), 1)
    same_chunk = (ri >> CHUNK_SHIFT) == (ci >> CHUNK_SHIFT)
    if reverse:
        return ri, ci, same_chunk, same_chunk & (ri <= ci), same_chunk & (ri < ci)
    return ri, ci, same_chunk, same_chunk & (ri >= ci), same_chunk & (ri > ci)


GATE_ROWS = 16


class _GdnProblem:
    def __init__(self, **kw):
        self.__dict__.update(kw)


def _gdn_group(probs, same_chunk, h0):
    nh0 = jnp.logical_not(h0)
    sr = _iota2((CHUNK, GROUP), 0)
    sc = _iota2((CHUNK, GROUP), 1)
    eye_side = jnp.where(sr == (sc & (CHUNK - 1)), 1.0, 0.0)
    tl = _iota2((LANES, GROUP), 1)
    br = _iota2((LANES, LANES), 0)
    bc = _iota2((LANES, LANES), 1)
    bd128 = (br >> HEAD_SHIFT) == (bc >> HEAD_SHIFT)

    def block_diag(side):
        return jnp.where(same_chunk, jnp.concatenate([side] * CHUNKS_PER_GROUP, axis=0), 0.0).astype(BF16)

    for pr in probs:
        pr.gedge = _block_rows(pr.gc, CHUNK, 0 if pr.reverse else CHUNK - 1)
        pr.exp_g = jnp.exp(pr.gc)
        pr.kb = pr.k * pr.beta_t
        pr.rhs = jnp.concatenate([pr.v * pr.beta_t, pr.kb * pr.exp_g], axis=1).astype(BF16)
        pr.k16 = pr.k.astype(BF16)
        pr.decay, pr.x, pr.pw, pr.bd = [], [], [], []
        for h in range(2):
            decay = jnp.exp(jnp.where(pr.incl, pr.g_col[h] - pr.g_row[h], -1e30))
            kk = _dot_nt(jnp.where(h0 if h == 0 else nh0, pr.kb, 0.0), pr.k16)
            low = jnp.where(pr.strict, kk * decay, 0.0)
            a = -(low[0:CHUNK] + low[CHUNK:2 * CHUNK] + low[2 * CHUNK:3 * CHUNK] + low[3 * CHUNK:4 * CHUNK])
            pr.decay.append(decay)
            pr.x.append(eye_side + a)
            pr.pw.append(a)
            pr.bd.append(block_diag(a))

    for _ in range(5):
        for pr in probs:
            for h in range(2):
                pr.pw[h] = jnp.dot(pr.pw[h].astype(BF16), pr.bd[h], preferred_element_type=F32)
        for pr in probs:
            for h in range(2):
                pr.bd[h] = block_diag(pr.pw[h])
        for pr in probs:
            for h in range(2):
                pr.x[h] = pr.x[h] + jnp.dot(pr.x[h].astype(BF16), pr.bd[h], preferred_element_type=F32)

    for pr in probs:
        sols = [jnp.dot(block_diag(pr.x[h]), pr.rhs, preferred_element_type=F32) for h in range(2)]
        u = jnp.where(h0, sols[0][:, :LANES], sols[1][:, :LANES])
        w = jnp.where(h0, sols[0][:, LANES:], sols[1][:, LANES:])
        wu = jnp.concatenate([w, u], axis=1)
        wu16 = wu.astype(BF16)
        aw = []
        for h in range(2):
            attn = jnp.where(pr.incl, _dot_nt(jnp.where(h0 if h == 0 else nh0, pr.q, 0.0), pr.k16) * pr.decay[h], 0.0)
            aw.append(jnp.dot(attn.astype(BF16), wu16, preferred_element_type=F32))
        pr.qp = (pr.q * pr.exp_g - jnp.where(h0, aw[0][:, :LANES], aw[1][:, :LANES])).astype(BF16)
        pr.op = jnp.where(h0, aw[0][:, LANES:], aw[1][:, LANES:])
        kd_t = (pr.k * jnp.exp(pr.gedge - pr.gc)).T
        pr.kmat, pr.nmat = [], []
        for c in range(CHUNKS_PER_GROUP):
            km = jnp.dot(jnp.where((tl >> CHUNK_SHIFT) == c, kd_t, 0.0), wu, preferred_element_type=F32)
            pr.kmat.append(jnp.where(bd128, km[:, :LANES], 0.0).astype(BF16))
            pr.nmat.append(jnp.where(bd128, km[:, LANES:], 0.0))
        pr.out = [None] * CHUNKS_PER_GROUP

    for t in range(CHUNKS_PER_GROUP):
        for pr in probs:
            c = CHUNKS_PER_GROUP - 1 - t if pr.reverse else t
            rs = slice(c * CHUNK, (c + 1) * CHUNK)
            s16 = pr.state.astype(BF16)
            pr.out[c] = jnp.dot(pr.qp[rs], s16, preferred_element_type=F32) + pr.op[rs]
            cd = jnp.exp(pr.gedge[c * CHUNK:c * CHUNK + 1, :])
            pr.state = pr.state * cd - jnp.dot(pr.kmat[c], s16, preferred_element_type=F32) + pr.nmat[c]
    return [jnp.concatenate(pr.out, axis=0) for pr in probs]


def _gdn_kernel(xf_ref, xfp_ref, xfn_ref, xb_ref, xbp_ref, xbn_ref, cw_ref, gf_ref, gb_ref, prm_ref,
                of_ref, ob_ref, s_scr, c_scr):
    n = pl.program_id(1)
    n_groups = pl.num_programs(1)

    @pl.when(n == 0)
    def _():
        s_scr[...] = jnp.zeros_like(s_scr)

    lane = _iota2((GROUP, LANES), 1)
    h0 = lane < HEAD_DIM
    ri = _iota2((GROUP, GROUP), 0)
    ci = _iota2((GROUP, GROUP), 1)
    same_chunk = (ri >> CHUNK_SHIFT) == (ci >> CHUNK_SHIFT)
    cw = cw_ref[...]
    base = SUBLANES - CONV_K // 2
    pairs = A_HEADS // 2

    dirs = ((n, xf_ref, xfp_ref, xfn_ref, gf_ref), (n_groups - 1 - n, xb_ref, xbp_ref, xbn_ref, gb_ref))

    @pl.when(2 * n < n_groups)
    def _():
        for gi, x_ref, xp_ref, xn_ref, _ in dirs:
            prev = jnp.where(gi > 0, xp_ref[0].astype(F32)[HALO - SUBLANES:], 0.0)
            nxt = jnp.where(gi < n_groups - 1, xn_ref[0].astype(F32)[:SUBLANES], 0.0)
            ext = jnp.concatenate([prev, x_ref[0].astype(F32), nxt], axis=0)
            acc = None
            for t in range(CONV_K):
                shift = (CONV_K // 2 - t) % ext.shape[0]
                tap = (pltpu.roll(ext, shift, 0) if shift else ext)[SUBLANES:SUBLANES + GROUP, :] * cw[t:t + 1, :]
                acc = tap if acc is None else acc + tap
            act = acc * _sigmoid(acc)
            rows = pl.ds(pl.multiple_of(gi * GROUP, GROUP), GROUP)
            for p in range(pairs):
                q_sl = slice(p * LANES, (p + 1) * LANES)
                k_sl = slice(A_WIDTH + p * LANES, A_WIDTH + (p + 1) * LANES)
                q = act[:, q_sl]
                k = act[:, k_sl]
                c_scr[rows, q_sl] = q * lax.rsqrt(_head_sum(q * q, h0) + RMS_EPS) * (HEAD_DIM ** -0.5)
                c_scr[rows, k_sl] = k * lax.rsqrt(_head_sum(k * k, h0) + RMS_EPS)
            c_scr[rows, 2 * A_WIDTH:] = act[:, 2 * A_WIDTH:]

    probs = []
    for d, (gi, _, _, _, g_ref) in enumerate(dirs):
        reverse = d == 1
        lower, upper = same_chunk & (ri >= ci), same_chunk & (ri <= ci)
        incl = upper if reverse else lower
        strict = same_chunk & ((ri < ci) if reverse else (ri > ci))
        qkv = c_scr[pl.ds(pl.multiple_of(gi * GROUP, GROUP), GROUP), :]

        gt_t = g_ref[0].T[0:GATE_ROWS, :]
        prm = prm_ref[d]
        e_rows = jnp.where(prm[:, 2:3] > 0.5,
                           -jnp.exp(prm[:, 0:1]) * _softplus(gt_t + prm[:, 1:2]), _sigmoid(gt_t))
        g_rows = _dot_sel_right(e_rows, lower if reverse else upper)
        slab = jnp.concatenate([e_rows, g_rows, jnp.zeros((LANES - 2 * GATE_ROWS, GROUP), F32)], axis=0)
        cols = slab.T
        for p in range(pairs):
            sl = lambda part: slice(part * A_WIDTH + p * LANES, part * A_WIDTH + (p + 1) * LANES)
            col = lambda r: cols[:, r:r + 1]
            probs.append(_GdnProblem(
                reverse=reverse, incl=incl, strict=strict, slot=d * pairs + p,
                q=qkv[:, sl(0)], k=qkv[:, sl(1)], v=qkv[:, sl(2)],
                beta_t=jnp.where(h0, col(4 * p), col(4 * p + 1)),
                gc=jnp.where(h0, col(GATE_ROWS + 4 * p + 2), col(GATE_ROWS + 4 * p + 3)),
                g_col=(col(GATE_ROWS + 4 * p + 2), col(GATE_ROWS + 4 * p + 3)),
                g_row=(g_rows[4 * p + 2:4 * p + 3, :], g_rows[4 * p + 3:4 * p + 4, :]),
                state=s_scr[d * pairs + p]))

    outs = _gdn_group(probs, same_chunk, h0)
    for pr, out in zip(probs, outs):
        o_ref = ob_ref if pr.reverse else of_ref
        p = pr.slot % pairs
        o_ref[0, :, p * LANES:(p + 1) * LANES] = out.astype(BF16)
        s_scr[pr.slot] = pr.state


def gdn_scan(wide3, gates3, conv_w, prm):
    b, seq_len, _ = wide3.shape
    n_groups = seq_len // GROUP
    halo_per_group = GROUP // HALO
    n_halo = seq_len // HALO
    width = 3 * A_WIDTH
    fwd = lambda n: n
    bwd = lambda n: n_groups - 1 - n

    def x_specs(gidx):
        return [pl.BlockSpec((1, GROUP, width), lambda i, n: (i, gidx(n), 0)),
                pl.BlockSpec((1, HALO, width),
                             lambda i, n: (i, jnp.maximum(gidx(n) * halo_per_group - 1, 0), 0)),
                pl.BlockSpec((1, HALO, width),
                             lambda i, n: (i, jnp.minimum((gidx(n) + 1) * halo_per_group, n_halo - 1), 0))]

    out_sds = jax.ShapeDtypeStruct((b, seq_len, A_WIDTH), BF16)
    return pl.pallas_call(
        _gdn_kernel,
        out_shape=(out_sds, out_sds),
        grid=(b, n_groups),
        in_specs=x_specs(fwd) + x_specs(bwd) + [
            pl.BlockSpec((CONV_K, width), lambda i, n: (0, 0)),
            pl.BlockSpec((1, GROUP, LANES), lambda i, n: (i, fwd(n), (COL_GATES0 - D_WIDE) // LANES)),
            pl.BlockSpec((1, GROUP, LANES), lambda i, n: (i, bwd(n), (COL_GATES1 - D_WIDE) // LANES)),
            pl.BlockSpec((2, GATE_ROWS, LANES), lambda i, n: (0, 0, 0))],
        out_specs=(pl.BlockSpec((1, GROUP, A_WIDTH), lambda i, n: (i, fwd(n), 0)),
                   pl.BlockSpec((1, GROUP, A_WIDTH), lambda i, n: (i, bwd(n), 0))),
        scratch_shapes=[pltpu.VMEM((A_HEADS, LANES, LANES), F32), pltpu.VMEM((seq_len, width), F32)],
        compiler_params=_cparams(("parallel", "arbitrary")),
        name="gdn_scan",
    )(wide3, wide3, wide3, wide3, wide3, wide3, conv_w, gates3, gates3, prm)


EXP_CLAMP = 60.0


class _HgrnProblem:
    def __init__(self, **kw):
        self.__dict__.update(kw)


def _hgrn_group(probs):
    lane = _iota2((GROUP, LANES), 1)
    h0 = lane < HEAD_DIM
    nh0 = jnp.logical_not(h0)
    tl = _iota2((LANES, GROUP), 1)
    br = _iota2((LANES, LANES), 0)
    bc = _iota2((LANES, LANES), 1)
    bd128 = (br >> HEAD_SHIFT) == (bc >> HEAD_SHIFT)

    for pr in probs:
        ri, ci, same_chunk, incl, _ = pr.masks
        f = pr.lb + (1.0 - pr.lb) * _sigmoid(pr.f_raw)
        kk = 1.0 - f
        q = pr.q_raw * _sigmoid(pr.q_raw) * (HEAD_DIM ** -0.5)
        cum = _dot_sel(incl, jnp.log(f))
        pr.cedge = _block_rows(cum, CHUNK, 0 if pr.reverse else CHUNK - 1)
        pr.qd = q * jnp.exp(cum)
        pr.kd = kk * jnp.exp(pr.cedge - cum)
        half, quarter = CHUNK // 2, CHUNK // 4
        pos_in_chunk = lambda t: t & (CHUNK - 1)
        pos_in_half = lambda t: t & (half - 1)
        same_half = (ri >> (CHUNK_SHIFT - 1)) == (ci >> (CHUNK_SHIFT - 1))
        same_quarter = (ri >> (CHUNK_SHIFT - 2)) == (ci >> (CHUNK_SHIFT - 2))
        if pr.reverse:
            m32 = same_chunk & (pos_in_chunk(ri) < half) & (pos_in_chunk(ci) >= half)
            m16 = same_half & (pos_in_half(ri) < quarter) & (pos_in_half(ci) >= quarter)
            mdg = same_quarter & (ri <= ci)
            ref32 = _block_rows(cum, CHUNK, half)
            ref16 = _block_rows(cum, half, quarter)
            refdg = _block_rows(cum, quarter, quarter // 2)
        else:
            m32 = same_chunk & (pos_in_chunk(ri) >= half) & (pos_in_chunk(ci) < half)
            m16 = same_half & (pos_in_half(ri) >= quarter) & (pos_in_half(ci) < quarter)
            mdg = same_quarter & (ri >= ci)
            ref32 = _block_rows(cum, CHUNK, half - 1)
            ref16 = _block_rows(cum, half, quarter - 1)
            refdg = _block_rows(cum, quarter, quarter // 2 - 1)
        pr.level_masks = (m32, m16, mdg)
        pr.qs = (q * jnp.exp(jnp.minimum(cum - ref32, 0.0)),
                 q * jnp.exp(jnp.minimum(cum - ref16, 0.0)),
                 q * jnp.exp(jnp.clip(cum - refdg, -EXP_CLAMP, EXP_CLAMP)))
        pr.ks = ((kk * jnp.exp(jnp.minimum(ref32 - cum, 0.0))).astype(BF16),
                 (kk * jnp.exp(jnp.minimum(ref16 - cum, 0.0))).astype(BF16),
                 (kk * jnp.exp(jnp.clip(refdg - cum, -EXP_CLAMP, EXP_CLAMP))).astype(BF16))

    for pr in probs:
        v16 = pr.v.astype(BF16)
        m32, m16, mdg = pr.level_masks
        intra = []
        for h in range(2):
            mh = h0 if h == 0 else nh0
            a = jnp.where(m32, _dot_nt(jnp.where(mh, pr.qs[0], 0.0), pr.ks[0]),
                          jnp.where(m16, _dot_nt(jnp.where(mh, pr.qs[1], 0.0), pr.ks[1]),
                                    jnp.where(mdg, _dot_nt(jnp.where(mh, pr.qs[2], 0.0), pr.ks[2]), 0.0)))
            intra.append(jnp.dot(a.astype(BF16), v16, preferred_element_type=F32))
        pr.intra = jnp.where(h0, intra[0], intra[1])
        v_t = pr.v.T
        pr.upd = [jnp.where(bd128, jnp.dot(jnp.where((tl >> CHUNK_SHIFT) == c, v_t, 0.0), pr.kd,
                                           preferred_element_type=F32), 0.0)
                  for c in range(CHUNKS_PER_GROUP)]
        pr.out = [None] * CHUNKS_PER_GROUP

    for t in range(CHUNKS_PER_GROUP):
        for pr in probs:
            c = CHUNKS_PER_GROUP - 1 - t if pr.reverse else t
            rs = slice(c * CHUNK, (c + 1) * CHUNK)
            pr.out[c] = _dot_nt(pr.qd[rs], pr.state_t) + pr.intra[rs]
            cd = jnp.exp(pr.cedge[c * CHUNK:c * CHUNK + 1, :])
            pr.state_t = pr.state_t * cd + pr.upd[c]
    return [jnp.concatenate(pr.out, axis=0) for pr in probs]


def _hgrn_kernel(ff_ref, fb_ref, if_ref, ib_ref, qf_ref, qb_ref, lb_ref, of_ref, ob_ref, s_scr):
    n = pl.program_id(1)

    @pl.when(n == 0)
    def _():
        s_scr[...] = jnp.zeros_like(s_scr)

    pairs = C_HEADS // 2
    probs = []
    for d, (f_ref, i_ref, q_ref) in enumerate(((ff_ref, if_ref, qf_ref), (fb_ref, ib_ref, qb_ref))):
        masks = _scan_masks(d == 1)
        for p in range(pairs):
            sl = slice(p * LANES, (p + 1) * LANES)
            probs.append(_HgrnProblem(reverse=d == 1, masks=masks, slot=d * pairs + p, lb=lb_ref[:, sl],
                                      f_raw=f_ref[0, :, sl], v=i_ref[0, :, sl].astype(F32),
                                      q_raw=q_ref[0, :, sl].astype(F32), state_t=s_scr[d * pairs + p]))
    outs = _hgrn_group(probs)
    for pr, out in zip(probs, outs):
        o_ref = ob_ref if pr.reverse else of_ref
        p = pr.slot % pairs
        o_ref[0, :, p * LANES:(p + 1) * LANES] = out.astype(BF16)
        s_scr[pr.slot] = pr.state_t


def hgrn_scan(wide3, gates3, lb):
    b, seq_len, _ = wide3.shape
    n_groups = seq_len // GROUP
    fwd = lambda n: n
    bwd = lambda n: n_groups - 1 - n
    spec = lambda gidx, blk: pl.BlockSpec((1, GROUP, C_WIDTH), lambda i, n: (i, gidx(n), blk))
    f_blk = (COL_C_F - D_WIDE) // C_WIDTH
    out_sds = jax.ShapeDtypeStruct((b, seq_len, C_WIDTH), BF16)
    return pl.pallas_call(
        _hgrn_kernel,
        out_shape=(out_sds, out_sds),
        grid=(b, n_groups),
        in_specs=[spec(fwd, f_blk), spec(bwd, f_blk + 1),
                  spec(fwd, COL_C_I // C_WIDTH), spec(bwd, COL_C_I // C_WIDTH),
                  spec(fwd, COL_C_Q // C_WIDTH), spec(bwd, COL_C_Q // C_WIDTH),
                  pl.BlockSpec((1, C_WIDTH), lambda i, n: (0, 0))],
        out_specs=(spec(fwd, 0), spec(bwd, 0)),
        scratch_shapes=[pltpu.VMEM((C_HEADS, LANES, LANES), F32)],
        compiler_params=_cparams(("parallel", "arbitrary")),
        name="hgrn_scan",
    )(gates3, gates3, wide3, wide3, wide3, wide3, lb)


ATT_TQ = 512


def _norm_rope(x, w, cos, sin_signed, first_half16):
    lane = _iota2(x.shape, 1)
    h0 = lane < HEAD_DIM
    ms = _head_sum(x * x, h0) * (1.0 / HEAD_DIM)
    xn = x * lax.rsqrt(ms + RMS_EPS) * w
    partner = jnp.where(first_half16, pltpu.roll(xn, LANES - 16, 1), pltpu.roll(xn, 16, 1))
    return xn * cos + partner * sin_signed


def _attn_kernel(q_ref, k_ref, v_ref, cosq_ref, sinq_ref, cosk_ref, sink_ref, qw_ref, kw_ref, o_ref,
                 k_scr, v_scr):
    qi = pl.program_id(1)

    @pl.when(qi == 0)
    def _():
        kx = k_ref[0].astype(F32)
        lane_k = _iota2(kx.shape, 1)
        k_scr[...] = _norm_rope(kx, kw_ref[...], cosk_ref[...], sink_ref[...], (lane_k & 31) < 16).astype(BF16)
        v_scr[:, :LANES] = v_ref[0]
        v_scr[:, LANES:] = jnp.ones((v_scr.shape[0], LANES), BF16)

    lane = _iota2((ATT_TQ, LANES), 1)
    h0 = lane < HEAD_DIM
    fh = (lane & 31) < 16
    cos = cosq_ref[...]
    sin = sinq_ref[...]
    kmat = k_scr[...]
    vmat = v_scr[...]
    q_scale = (HEAD_DIM ** -0.5) * math.log2(math.e)
    for j in range(B_Q_HEADS // 2):
        qn = _norm_rope(q_ref[0, :, j * LANES:(j + 1) * LANES].astype(F32), qw_ref[...], cos, sin, fh) * q_scale
        outs = []
        for half in range(2):
            mh = h0 if half == 0 else jnp.logical_not(h0)
            s = _dot_nt(jnp.where(mh, qn, 0.0), kmat)
            m = jnp.max(s, axis=-1, keepdims=True)
            e = jnp.exp2((s - m).astype(BF16))
            pvl = jnp.dot(e, vmat, preferred_element_type=F32)
            outs.append(pvl[:, :LANES] / pvl[:, LANES:LANES + 1])
        o_ref[0, :, j * LANES:(j + 1) * LANES] = jnp.where(h0, outs[0], outs[1]).astype(BF16)


def gqa_attention(proj, cos_t, sin_t, q_w, k_w, seq_len):
    b = proj.shape[0]
    return pl.pallas_call(
        _attn_kernel,
        out_shape=jax.ShapeDtypeStruct((b, seq_len, B_WIDTH), BF16),
        grid=(b, seq_len // ATT_TQ),
        in_specs=[pl.BlockSpec((1, ATT_TQ, B_WIDTH), lambda i, t: (i, t, COL_B_Q // B_WIDTH)),
                  pl.BlockSpec((1, seq_len, LANES), lambda i, t: (i, 0, COL_B_K // LANES)),
                  pl.BlockSpec((1, seq_len, LANES), lambda i, t: (i, 0, COL_B_V // LANES)),
                  pl.BlockSpec((ATT_TQ, LANES), lambda i, t: (t, 0)),
                  pl.BlockSpec((ATT_TQ, LANES), lambda i, t: (t, 0)),
                  pl.BlockSpec((seq_len, LANES), lambda i, t: (0, 0)),
                  pl.BlockSpec((seq_len, LANES), lambda i, t: (0, 0)),
                  pl.BlockSpec((1, LANES), lambda i, t: (0, 0)),
                  pl.BlockSpec((1, LANES), lambda i, t: (0, 0))],
        out_specs=pl.BlockSpec((1, ATT_TQ, B_WIDTH), lambda i, t: (i, t, 0)),
        scratch_shapes=[pltpu.VMEM((seq_len, LANES), BF16), pltpu.VMEM((seq_len, 2 * LANES), BF16)],
        compiler_params=_cparams(("parallel", "arbitrary")),
        name="gqa_attention",
    )(proj, proj, proj, cos_t, sin_t, cos_t, sin_t, q_w, k_w)


def rope_tables(seq_len):
    rows = seq_len // GRID_W
    row = np.repeat(np.arange(rows, dtype=np.float64), GRID_W)
    col = np.tile(np.arange(GRID_W, dtype=np.float64), rows)
    inv_freq = ROPE_THETA ** (-np.arange(0, ROPE_AXIS_DIM, 2, dtype=np.float64) / ROPE_AXIS_DIM)
    ang_r = row[:, None] * inv_freq
    ang_c = col[:, None] * inv_freq
    cos64 = np.concatenate([np.cos(ang_r), np.cos(ang_r), np.cos(ang_c), np.cos(ang_c)], axis=1)
    sin64 = np.concatenate([-np.sin(ang_r), np.sin(ang_r), -np.sin(ang_c), np.sin(ang_c)], axis=1)
    return (jnp.asarray(np.tile(cos64, (1, 2)), dtype=F32), jnp.asarray(np.tile(sin64, (1, 2)), dtype=F32))


MIX_TM = 512


def _mixout_kernel(oaf_ref, oab_ref, z_ref, ob_ref, ocf_ref, ocb_ref, g_ref, x_ref, wa_ref, wc_ref,
                   wo_ref, lg_ref, lb_ref, xo_ref, xb_ref):
    lane = _iota2((MIX_TM, LANES), 1)
    h0 = lane < HEAD_DIM
    parts = []
    for j in range(A_WIDTH // LANES):
        sl = slice(j * LANES, (j + 1) * LANES)
        o = oaf_ref[:, sl].astype(F32) + oab_ref[:, sl].astype(F32)
        ms = _head_sum(o * o, h0) * (1.0 / HEAD_DIM)
        z = z_ref[:, sl].astype(F32)
        parts.append((o * lax.rsqrt(ms + RMS_EPS) * wa_ref[...] * (z * _sigmoid(z))).astype(BF16))
    parts.append(ob_ref[...])
    for j in range(C_WIDTH // LANES):
        sl = slice(j * LANES, (j + 1) * LANES)
        o = ocf_ref[:, sl].astype(F32) + ocb_ref[:, sl].astype(F32)
        ms = _head_sum(o * o, h0) * (1.0 / HEAD_DIM)
        parts.append((o * lax.rsqrt(ms + RMS_EPS) * wc_ref[...]
                      * _sigmoid(g_ref[:, sl].astype(F32))).astype(BF16))
    mixed = jnp.concatenate(parts, axis=1)
    h = jnp.dot(mixed, wo_ref[...], preferred_element_type=F32)
    y = _layer_norm_rows(DN_ALPHA * x_ref[...] + h, lg_ref[...], lb_ref[...])
    xo_ref[...] = y
    xb_ref[...] = y.astype(BF16)


def mixer_output(oa_f, oa_b, proj2, ob, oc_f, oc_b, x2, gdn_w, hgrn_w, w_out, ln_g, ln_b):
    m = x2.shape[0]
    row = lambda w: pl.BlockSpec((MIX_TM, w), lambda i: (i, 0))
    const = lambda r, c: pl.BlockSpec((r, c), lambda i: (0, 0))
    return pl.pallas_call(
        _mixout_kernel,
        out_shape=(jax.ShapeDtypeStruct((m, D_MODEL), F32), jax.ShapeDtypeStruct((m, D_MODEL), BF16)),
        grid=(m // MIX_TM,),
        in_specs=[row(A_WIDTH), row(A_WIDTH),
                  pl.BlockSpec((MIX_TM, A_WIDTH), lambda i: (i, COL_A_Z // A_WIDTH)),
                  row(B_WIDTH), row(C_WIDTH), row(C_WIDTH),
                  pl.BlockSpec((MIX_TM, C_WIDTH), lambda i: (i, COL_C_G // C_WIDTH)),
                  row(D_MODEL), const(1, LANES), const(1, LANES),
                  const(D_MIX, D_MODEL), const(1, D_MODEL), const(1, D_MODEL)],
        out_specs=(row(D_MODEL), row(D_MODEL)),
        compiler_params=_cparams(("parallel",)),
        name="mixer_output",
    )(oa_f, oa_b, proj2, ob, oc_f, oc_b, proj2, x2, gdn_w, hgrn_w, w_out, ln_g, ln_b)


XATT_TM = 512


def _xattn_kernel(xb_ref, x_ref, k_ref, v_ref, wq_ref, wo_ref, lg_ref, lb_ref, xo_ref, xob_ref):
    q = jnp.dot(xb_ref[0], wq_ref[...], preferred_element_type=F32) * (X_HEAD_DIM ** -0.5)
    outs = []
    for h in range(X_HEADS):
        sl = slice(h * X_HEAD_DIM, (h + 1) * X_HEAD_DIM)
        s = _dot_nt(q[:, sl], k_ref[0, :, sl])
        m = jnp.max(s, axis=-1, keepdims=True)
        e = jnp.exp(s - m)
        l = jnp.sum(e, axis=-1, keepdims=True)
        outs.append((jnp.dot(e.astype(BF16), v_ref[0, :, sl], preferred_element_type=F32) / l).astype(BF16))
    o = jnp.concatenate(outs, axis=1)
    c = jnp.dot(o, wo_ref[...], preferred_element_type=F32)
    y = _layer_norm_rows(DN_ALPHA * x_ref[0] + c, lg_ref[...], lb_ref[...])
    xo_ref[0] = y
    xob_ref[0] = y.astype(BF16)


def cross_attention(xb3, x3, k3, v3, wq, wo, ln_g, ln_b):
    b, s, _ = x3.shape
    mem = k3.shape[1]
    row = pl.BlockSpec((1, XATT_TM, D_MODEL), lambda i, t: (i, t, 0))
    const = lambda r, c: pl.BlockSpec((r, c), lambda i, t: (0, 0))
    kv = pl.BlockSpec((1, mem, D_MODEL), lambda i, t: (i, 0, 0))
    return pl.pallas_call(
        _xattn_kernel,
        out_shape=(jax.ShapeDtypeStruct((b, s, D_MODEL), F32), jax.ShapeDtypeStruct((b, s, D_MODEL), BF16)),
        grid=(b, s // XATT_TM),
        in_specs=[row, row, kv, kv, const(D_MODEL, D_MODEL), const(D_MODEL, D_MODEL),
                  const(1, D_MODEL), const(1, D_MODEL)],
        out_specs=(row, row),
        compiler_params=_cparams(("parallel", "parallel")),
        name="cross_attention",
    )(xb3, x3, k3, v3, wq, wo, ln_g, ln_b)


FFN_TM = 512
FFN_TF = 1408


def _ffn_kernel(xb_ref, x_ref, wg_ref, wu_ref, wd_ref, lg_ref, lb_ref, xo_ref, xob_ref):
    xb = xb_ref[...]
    acc = None
    for j in range(D_FF // FFN_TF):
        sl = slice(j * FFN_TF, (j + 1) * FFN_TF)
        g = jnp.dot(xb, wg_ref[:, sl], preferred_element_type=F32)
        u = jnp.dot(xb, wu_ref[:, sl], preferred_element_type=F32)
        h = (g * _sigmoid(g) * u).astype(BF16)
        part = jnp.dot(h, wd_ref[sl, :], preferred_element_type=F32)
        acc = part if acc is None else acc + part
    y = _layer_norm_rows(DN_ALPHA * x_ref[...] + acc, lg_ref[...], lb_ref[...])
    xo_ref[...] = y
    xob_ref[...] = y.astype(BF16)


def dense_ffn(xb2, x2, wg, wu, wd, ln_g, ln_b):
    m = x2.shape[0]
    row = pl.BlockSpec((FFN_TM, D_MODEL), lambda i: (i, 0))
    const = pl.BlockSpec((1, D_MODEL), lambda i: (0, 0))
    resident = lambda r, c: pl.BlockSpec((r, c), lambda i: (0, 0), pipeline_mode=pl.Buffered(1))
    return pl.pallas_call(
        _ffn_kernel,
        out_shape=(jax.ShapeDtypeStruct((m, D_MODEL), F32), jax.ShapeDtypeStruct((m, D_MODEL), BF16)),
        grid=(m // FFN_TM,),
        in_specs=[row, row, resident(D_MODEL, D_FF), resident(D_MODEL, D_FF), resident(D_FF, D_MODEL),
                  const, const],
        out_specs=(row, row),
        compiler_params=_cparams(("parallel",)),
        name="dense_ffn",
    )(xb2, x2, wg, wu, wd, ln_g, ln_b)


MOE_TM = 512
MOE_WIN = 512
MOE_TF = 1792
ROUTER_TM = 512


def _router_kernel(xb_ref, wr_ref, e_ref, w_ref, xp_ref):
    logits = jnp.dot(xb_ref[...], wr_ref[...], preferred_element_type=F32)
    lane = _iota2(logits.shape, 1)
    neg = jnp.float32(-jnp.inf)
    lane_f = lane.astype(F32)
    lg = jnp.where(lane < N_EXPERTS, logits, neg)
    m1 = jnp.max(lg, axis=-1, keepdims=True)
    i1 = jnp.min(jnp.where(lg == m1, lane_f, float(LANES)), axis=-1, keepdims=True)
    lg2 = jnp.where(lane_f == i1, neg, lg)
    m2 = jnp.max(lg2, axis=-1, keepdims=True)
    i2 = jnp.min(jnp.where(lg2 == m2, lane_f, float(LANES)), axis=-1, keepdims=True)
    t = jnp.exp(m2 - m1)
    w1 = 1.0 / (1.0 + t)
    w2 = t / (1.0 + t)
    e_ref[...] = jnp.where(lane == 0, i1, jnp.where(lane == 1, i2, 0.0)).astype(jnp.int32)[:, :N_EXPERTS]
    w_ref[...] = jnp.where(lane == 0, w1, jnp.where(lane == 1, w2, 0.0))[:, :N_EXPERTS]
    _store_packed(xp_ref, xb_ref[...].astype(F32))


def moe_router(xb2, w_router_p):
    m = xb2.shape[0]
    lane_out = pl.BlockSpec((ROUTER_TM, N_EXPERTS), lambda i: (i, 0))
    return pl.pallas_call(
        _router_kernel,
        out_shape=(jax.ShapeDtypeStruct((m, N_EXPERTS), jnp.int32), jax.ShapeDtypeStruct((m, N_EXPERTS), F32),
                   jax.ShapeDtypeStruct((SC_SPLIT, m, LANES), jnp.uint32)),
        grid=(m // ROUTER_TM,),
        in_specs=[pl.BlockSpec((ROUTER_TM, D_MODEL), lambda i: (i, 0)),
                  pl.BlockSpec((D_MODEL, LANES), lambda i: (0, 0))],
        out_specs=(lane_out, lane_out, pl.BlockSpec((SC_SPLIT, ROUTER_TM, LANES), lambda i: (0, i, 0))),
        compiler_params=_cparams(("parallel",)),
        name="moe_router",
    )(xb2, w_router_p)


def _moe_ffn_kernel(be_ref, bv_ref, x_ref, wg_ref, wu_ref, wd_ref, o_ref, acc_ref):
    i = pl.program_id(0)
    j = pl.program_id(1)
    valid = bv_ref[i] != 0

    @pl.when(j == 0)
    def _():
        acc_ref[...] = jnp.zeros_like(acc_ref)

    @pl.when(valid)
    def _():
        xb = _load_packed(x_ref).astype(BF16)
        g = jnp.dot(xb, wg_ref[0], preferred_element_type=F32)
        u = jnp.dot(xb, wu_ref[0], preferred_element_type=F32)
        h = (g * _sigmoid(g) * u).astype(BF16)
        acc_ref[...] += jnp.dot(h, wd_ref[0], preferred_element_type=F32)

    @pl.when(j == pl.num_programs(1) - 1)
    def _():
        _store_packed(o_ref, acc_ref[...])


def moe_expert_ffn(block_e, block_valid, xrows, wg, wu, wd):
    n_rows = xrows.shape[1]
    grid_spec = pltpu.PrefetchScalarGridSpec(
        num_scalar_prefetch=2,
        grid=(n_rows // MOE_TM, D_FF_EXPERT // MOE_TF),
        in_specs=[pl.BlockSpec((SC_SPLIT, MOE_TM, LANES), lambda i, j, be, bv: (0, i, 0)),
                  pl.BlockSpec((1, D_MODEL, MOE_TF), lambda i, j, be, bv: (be[i], 0, j)),
                  pl.BlockSpec((1, D_MODEL, MOE_TF), lambda i, j, be, bv: (be[i], 0, j)),
                  pl.BlockSpec((1, MOE_TF, D_MODEL), lambda i, j, be, bv: (be[i], j, 0))],
        out_specs=pl.BlockSpec((SC_SPLIT, MOE_TM, LANES), lambda i, j, be, bv: (0, i, 0)),
        scratch_shapes=[pltpu.VMEM((MOE_TM, D_MODEL), F32)],
    )
    return pl.pallas_call(
        _moe_ffn_kernel,
        out_shape=jax.ShapeDtypeStruct((SC_SPLIT, n_rows, LANES), jnp.uint32),
        grid_spec=grid_spec,
        compiler_params=_cparams(("arbitrary", "arbitrary")),
        name="moe_expert_ffn",
    )(block_e, block_valid, xrows, wg, wu, wd)


PACKED_WIDTH = D_MODEL // 2
SC_ROWS = 128
SC_SPLIT = 4


def _store_packed(o_ref, y):
    lo = lax.bitcast_convert_type(y[:, :PACKED_WIDTH].astype(BF16).astype(F32), jnp.uint32) >> 16
    hi = lax.bitcast_convert_type(y[:, PACKED_WIDTH:].astype(BF16).astype(F32), jnp.uint32)
    word = hi | lo
    for c in range(SC_SPLIT):
        o_ref[c] = word[:, c * LANES:(c + 1) * LANES]


def _load_packed(ref):
    w = jnp.concatenate([ref[c] for c in range(SC_SPLIT)], axis=1)
    lo = lax.bitcast_convert_type(w << 16, F32)
    hi = lax.bitcast_convert_type(w & jnp.uint32(0xFFFF0000), F32)
    return jnp.concatenate([lo, hi], axis=1)


def sc_gather_rows(data, idx):
    pieces, rows, width = data.shape
    idx = (jnp.arange(pieces, dtype=jnp.int32)[:, None] * rows + idx[None, :]).reshape(-1)
    return _sc_gather(data.reshape(pieces * rows, width), idx).reshape(pieces, -1, width)


def _sc_gather(data, idx):
    n = idx.shape[0]
    d = data.shape[1]
    sc = plsc.get_sparse_core_info()
    mesh = plsc.VectorSubcoreMesh(core_axis_name="core", subcore_axis_name="subcore")
    steps = n // (SC_ROWS * sc.num_cores)

    @pl.kernel(out_type=jax.ShapeDtypeStruct((n, d), data.dtype), mesh=mesh, scratch_types=[])
    def gather_kernel(x_hbm, i_hbm, o_hbm):
        def body(i_vmem, o_vmem):
            pltpu.sync_copy(x_hbm.at[i_vmem.at[0]], o_vmem)

        pltpu.emit_pipeline(
            body,
            grid=(sc.num_cores, steps),
            in_specs=[pl.BlockSpec((1, SC_ROWS), index_map=lambda c, i: (0, c * steps + i))],
            out_specs=[pl.BlockSpec((SC_ROWS, d), index_map=lambda c, i: (c * steps + i, 0))],
            core_axis_name=("core", "subcore"),
            dimension_semantics=(pltpu.PARALLEL, pltpu.PARALLEL),
        )(i_hbm, o_hbm)

    return gather_kernel(data, idx.reshape(1, n))


def sc_scatter_rows(data, idx0, idx1, zero_idx0, zero_idx1, n_out):
    pieces, n, width = data.shape
    off = jnp.arange(pieces, dtype=jnp.int32)[:, None] * n_out
    flat = lambda idx: (off + idx[None, :]).reshape(-1)
    zeros = jnp.zeros((pieces * zero_idx0.shape[0], width), data.dtype)
    out = _sc_scatter(data.reshape(pieces * n, width), flat(idx0), flat(idx1),
                      zeros, flat(zero_idx0), flat(zero_idx1), pieces * n_out)
    return out.reshape(pieces, n_out, width)


def _sc_scatter(data, idx0, idx1, zdata, zidx0, zidx1, n_out):
    d = data.shape[1]
    sc = plsc.get_sparse_core_info()
    mesh = plsc.VectorSubcoreMesh(core_axis_name="core", subcore_axis_name="subcore")

    @pl.kernel(out_type=jax.ShapeDtypeStruct((n_out, d), data.dtype), mesh=mesh, scratch_types=[])
    def scatter_kernel(x_hbm, i0_hbm, i1_hbm, z_hbm, zi0_hbm, zi1_hbm, o_hbm):
        def body(x_vmem, i0_vmem, i1_vmem):
            pltpu.sync_copy(x_vmem, o_hbm.at[i0_vmem.at[0]])
            pltpu.sync_copy(x_vmem, o_hbm.at[i1_vmem.at[0]])

        for src, a, b in ((x_hbm, i0_hbm, i1_hbm), (z_hbm, zi0_hbm, zi1_hbm)):
            steps = src.shape[0] // (SC_ROWS * sc.num_cores)
            idx_spec = pl.BlockSpec((1, SC_ROWS), index_map=lambda c, i, steps=steps: (0, c * steps + i))
            pltpu.emit_pipeline(
                body,
                grid=(sc.num_cores, steps),
                in_specs=[pl.BlockSpec((SC_ROWS, d), index_map=lambda c, i, steps=steps: (c * steps + i, 0)),
                          idx_spec, idx_spec],
                out_specs=[],
                core_axis_name=("core", "subcore"),
                dimension_semantics=(pltpu.PARALLEL, pltpu.PARALLEL),
            )(src, a, b)

    row = lambda idx: idx.reshape(1, -1)
    return scatter_kernel(data, row(idx0), row(idx1), zdata, row(zidx0), row(zidx1))


def _moe_finish_kernel(y0_ref, y1_ref, w_ref, x_ref, lg_ref, lb_ref, xo_ref):
    wt = w_ref[...]
    y = wt[:, 0:1] * _load_packed(y0_ref) + wt[:, 1:2] * _load_packed(y1_ref)
    xo_ref[...] = _layer_norm_rows(DN_ALPHA * x_ref[...] + y, lg_ref[...], lb_ref[...])


def moe_finish(ytok, top_w, x2, ln_g, ln_b):
    m = x2.shape[0]
    n_win = m // MOE_WIN
    packed = lambda off: pl.BlockSpec((SC_SPLIT, MOE_WIN, LANES), lambda i: (0, i + off, 0))
    row = pl.BlockSpec((MOE_WIN, D_MODEL), lambda i: (i, 0))
    const = pl.BlockSpec((1, D_MODEL), lambda i: (0, 0))
    return pl.pallas_call(
        _moe_finish_kernel,
        out_shape=jax.ShapeDtypeStruct((m, D_MODEL), F32),
        grid=(n_win,),
        in_specs=[packed(0), packed(n_win), pl.BlockSpec((MOE_WIN, TOP_K), lambda i: (i, 0)), row, const, const],
        out_specs=row,
        compiler_params=_cparams(("parallel",)),
        name="moe_finish",
    )(ytok, ytok, top_w, x2, ln_g, ln_b)


def moe_layer(xb2, x2, w_router, wg, wu, wd, ln_g, ln_b):
    n = x2.shape[0]
    n_rows = n * TOP_K + N_EXPERTS * MOE_TM
    n_blocks = n_rows // MOE_TM

    wr = jnp.pad(w_router, ((0, 0), (0, LANES - N_EXPERTS))).astype(BF16)
    top_e_t, top_w_t, x_packed = moe_router(xb2, wr)
    top_e = top_e_t[:, :TOP_K]
    top_w = top_w_t[:, :TOP_K]

    experts = jnp.arange(N_EXPERTS, dtype=jnp.int32)
    tok_onehot = ((top_e[:, 0:1] == experts[None, :]) | (top_e[:, 1:2] == experts[None, :])).astype(jnp.int32)
    csum = jnp.cumsum(tok_onehot, axis=0)
    counts = csum[-1]
    rank = csum - tok_onehot
    padded = (counts + MOE_TM - 1) // MOE_TM * MOE_TM
    pad_end = jnp.cumsum(padded)
    pad_start = pad_end - padded
    dest = pad_start[top_e] + jnp.take_along_axis(rank, top_e, axis=1)

    blk_row0 = jnp.arange(n_blocks, dtype=jnp.int32) * MOE_TM
    block_e = jnp.minimum(jnp.sum((pad_end[None, :] <= blk_row0[:, None]).astype(jnp.int32), axis=1),
                          N_EXPERTS - 1).astype(jnp.int32)
    block_valid = (blk_row0 < pad_end[-1]).astype(jnp.int32)

    n_pad = n_rows - n * TOP_K
    seg_lo = jnp.concatenate([pad_start + counts, pad_end[-1:]])
    seg_len = jnp.concatenate([padded - counts, n_rows - pad_end[-1:]])
    seg_end = jnp.cumsum(seg_len)
    k = jnp.arange(n_pad, dtype=jnp.int32)
    seg = jnp.sum((seg_end[None, :] <= k[:, None]).astype(jnp.int32), axis=1)
    pad_rows = (seg_lo[seg] + k - (seg_end - seg_len)[seg]).astype(jnp.int32)

    xrows = sc_scatter_rows(x_packed, dest[:, 0], dest[:, 1], pad_rows[:n_pad // 2], pad_rows[n_pad // 2:], n_rows)
    yrows = moe_expert_ffn(block_e, block_valid, xrows, wg, wu, wd)

    ytok = sc_gather_rows(yrows, dest.T.reshape(-1))
    return moe_finish(ytok, top_w, x2, ln_g, ln_b)


def _w_in_perm():
    zero = 3480
    perm = np.full((D_IN_P,), zero, np.int64)
    o_qkv, o_z, o_beta, o_decay, o_bq, o_bk, o_bv, o_cf, o_ci, o_cq, o_cg = (
        0, 1152, 1536, 1548, 1560, 1944, 2072, 2200, 2712, 2968, 3224)
    perm[COL_A_QKV:COL_A_QKV + 1152] = np.arange(o_qkv, o_qkv + 1152)
    perm[COL_A_Z:COL_A_Z + 384] = np.arange(o_z, o_z + 384)
    for j in range(3):
        for a in range(2):
            dst = COL_B_Q + j * LANES + a * HEAD_DIM
            src = o_bq + (j + 3 * a) * HEAD_DIM
            perm[dst:dst + HEAD_DIM] = np.arange(src, src + HEAD_DIM)
    perm[COL_B_K:COL_B_K + 128] = np.arange(o_bk, o_bk + 128)
    perm[COL_C_G:COL_C_G + 256] = np.arange(o_cg, o_cg + 256)
    perm[COL_B_V:COL_B_V + 128] = np.arange(o_bv, o_bv + 128)
    perm[COL_C_F:COL_C_F + 512] = np.arange(o_cf, o_cf + 512)
    perm[COL_C_I:COL_C_I + 256] = np.arange(o_ci, o_ci + 256)
    perm[COL_C_Q:COL_C_Q + 256] = np.arange(o_cq, o_cq + 256)
    for d, col in enumerate((COL_GATES0, COL_GATES1)):
        for p in range(A_HEADS // 2):
            dst = col + 4 * p
            perm[dst + 0] = o_beta + d * A_HEADS + 2 * p
            perm[dst + 1] = o_beta + d * A_HEADS + 2 * p + 1
            perm[dst + 2] = o_decay + d * A_HEADS + 2 * p
            perm[dst + 3] = o_decay + d * A_HEADS + 2 * p + 1
    return perm


def _take_static(w, perm, axis, zero_index=None):
    pieces, start = [], 0
    for i in range(1, len(perm) + 1):
        is_zero = perm[start] == zero_index
        if i < len(perm) and ((perm[i] == zero_index) if is_zero
                              else (perm[i] == perm[i - 1] + 1 and perm[i] != zero_index)):
            continue
        if is_zero:
            shape = list(w.shape)
            shape[axis] = i - start
            pieces.append(jnp.zeros(shape, w.dtype))
        else:
            pieces.append(lax.slice_in_dim(w, int(perm[start]), int(perm[i - 1]) + 1, axis=axis))
        start = i
    return jnp.concatenate(pieces, axis=axis)


def _w_out_perm():
    perm = np.arange(D_MIX)
    for j in range(3):
        for a in range(2):
            dst = A_WIDTH + j * LANES + a * HEAD_DIM
            src = A_WIDTH + (j + 3 * a) * HEAD_DIM
            perm[dst:dst + HEAD_DIM] = np.arange(src, src + HEAD_DIM)
    return perm


def _gdn_params(a_log, dt_bias):
    rows = np.array([4 * (h // 2) + 2 + h % 2 for h in range(A_HEADS)])
    prm = jnp.zeros((2, GATE_ROWS, LANES), F32)
    prm = prm.at[:, rows, 0].set(a_log.astype(F32))
    prm = prm.at[:, rows, 1].set(dt_bias.astype(F32))
    return prm.at[:, rows, 2].set(1.0)


def kernel(x, mem, w_in, conv_w, gdn_a_log, gdn_dt_bias, gdn_norm_w, q_norm_w, k_norm_w, hgrn_lb_logits,
           hgrn_norm_w, w_out, ln1_g, ln1_b, xq, xk, xv, xo, ln2_g, ln2_b, ffn_wg, ffn_wu, ffn_wd,
           moe_router, moe_wg, moe_wu, moe_wd, ln3_g, ln3_b):
    b, s, d = x.shape
    m = b * s
    cos_t, sin_t = rope_tables(s)
    lb_p = jax.nn.softmax(hgrn_lb_logits.astype(F32), axis=0)
    lb_c = jnp.cumsum(lb_p, axis=0)
    lower_bounds = lb_c - lb_c[0]
    in_perm = _w_in_perm()
    out_perm = _w_out_perm()
    mem_b = mem.astype(BF16).reshape(b * mem.shape[1], d)
    tile2 = lambda w: jnp.tile(w.astype(F32), 2)[None, :]
    row = lambda v: v.astype(F32)[None, :]

    x2 = x.reshape(m, d)
    xb2 = x2.astype(BF16)
    for l in range(DEPTH):
        w_in_p = _take_static(w_in[l].astype(BF16), in_perm, 1, zero_index=w_in.shape[2])
        wide2 = matmul(xb2, w_in_p[:, :D_WIDE], BF16, min(1024, m), 768)
        gates2 = matmul(xb2, w_in_p[:, D_WIDE:], F32, min(1024, m), D_IN_P - D_WIDE)
        wide3 = wide2.reshape(b, s, D_WIDE)
        gates3 = gates2.reshape(b, s, D_IN_P - D_WIDE)

        oa_f, oa_b = gdn_scan(wide3, gates3, conv_w[l], _gdn_params(gdn_a_log[l], gdn_dt_bias[l]))
        ob = gqa_attention(wide3, cos_t, sin_t, tile2(q_norm_w[l]), tile2(k_norm_w[l]), s)
        lb = lower_bounds[l][None, :]
        oc_f, oc_b = hgrn_scan(wide3, gates3, lb)

        w_out_p = _take_static(w_out[l].astype(BF16), out_perm, 0)
        x2, xb2 = mixer_output(oa_f.reshape(m, A_WIDTH), oa_b.reshape(m, A_WIDTH), wide2,
                               ob.reshape(m, B_WIDTH), oc_f.reshape(m, C_WIDTH), oc_b.reshape(m, C_WIDTH),
                               x2, tile2(gdn_norm_w[l]), tile2(hgrn_norm_w[l]), w_out_p,
                               row(ln1_g[l]), row(ln1_b[l]))

        k3 = matmul(mem_b, xk[l].astype(BF16), BF16, 256, 512).reshape(b, -1, d)
        v3 = matmul(mem_b, xv[l].astype(BF16), BF16, 256, 512).reshape(b, -1, d)
        x3, xb3 = cross_attention(xb2.reshape(b, s, d), x2.reshape(b, s, d), k3, v3,
                                  xq[l].astype(BF16), xo[l].astype(BF16), row(ln2_g[l]), row(ln2_b[l]))
        x2, xb2 = x3.reshape(m, d), xb3.reshape(m, d)

        if l % 2 == 0:
            x2, xb2 = dense_ffn(xb2, x2, ffn_wg[l // 2].astype(BF16), ffn_wu[l // 2].astype(BF16),
                                ffn_wd[l // 2].astype(BF16), row(ln3_g[l]), row(ln3_b[l]))
        else:
            x2 = moe_layer(xb2, x2, moe_router[l // 2], moe_wg[l // 2].astype(BF16),
                           moe_wu[l // 2].astype(BF16), moe_wd[l // 2].astype(BF16),
                           row(ln3_g[l]), row(ln3_b[l]))
            xb2 = x2.astype(BF16)
    return x2.reshape(b, s, d)
```

```python
import math

import numpy as np
import jax
import jax.numpy as jnp
from jax import lax
from jax.experimental import pallas as pl
from jax.experimental.pallas import tpu as pltpu
from jax.experimental.pallas import tpu_sc as plsc

F32 = jnp.float32
BF16 = jnp.bfloat16

D_MODEL = 1024
DEPTH = 2
HEAD_DIM = 64
A_HEADS = 6
A_WIDTH = A_HEADS * HEAD_DIM
B_Q_HEADS = 6
B_WIDTH = B_Q_HEADS * HEAD_DIM
C_HEADS = 4
C_WIDTH = C_HEADS * HEAD_DIM
D_MIX = A_WIDTH + B_WIDTH + C_WIDTH
CONV_K = 5
CHUNK = 64
GRID_W = 64
ROPE_AXIS_DIM = HEAD_DIM // 2
ROPE_THETA = 10000.0
X_HEADS = 4
X_HEAD_DIM = D_MODEL // X_HEADS
D_FF = 2816
N_EXPERTS = 8
TOP_K = 2
D_FF_EXPERT = 3584
DN_ALPHA = (2 * DEPTH) ** 0.25
LN_EPS = 1e-5
RMS_EPS = 1e-6

LANES = 128
SUBLANES = 8
HALO = 16
GROUP = 256
CHUNKS_PER_GROUP = GROUP // CHUNK
VMEM_LIMIT = 56 * 1024 * 1024

COL_A_QKV = 0
COL_A_Z = 1152
COL_B_Q = 1536
COL_B_K = 1920
COL_C_G = 2048
COL_B_V = 2304
COL_C_I = 2560
COL_C_Q = 2816
D_WIDE = 3072
COL_C_F = D_WIDE
COL_GATES0 = D_WIDE + 512
COL_GATES1 = D_WIDE + 640
D_IN_P = 3840


def _cparams(sem):
    return pltpu.CompilerParams(dimension_semantics=sem, vmem_limit_bytes=VMEM_LIMIT)


def _sigmoid(x):
    return 1.0 / (1.0 + jnp.exp(-x))


def _softplus(x):
    return jnp.maximum(x, 0.0) + jnp.log1p(jnp.exp(-jnp.abs(x)))


def _dot_nt(a, b):
    return lax.dot_general(a.astype(BF16), b.astype(BF16), (((1,), (1,)), ((), ())),
                           preferred_element_type=F32)


def _dot_sel(m01, x):
    m = jnp.where(m01, 1.0, 0.0).astype(BF16)
    hi = x.astype(BF16)
    r1 = x - hi.astype(F32)
    mid = r1.astype(BF16)
    lo = (r1 - mid.astype(F32)).astype(BF16)
    out = jnp.dot(m, hi, preferred_element_type=F32)
    out = out + jnp.dot(m, mid, preferred_element_type=F32)
    return out + jnp.dot(m, lo, preferred_element_type=F32)


def _dot_sel_right(x, m01):
    m = jnp.where(m01, 1.0, 0.0).astype(BF16)
    hi = x.astype(BF16)
    r1 = x - hi.astype(F32)
    mid = r1.astype(BF16)
    lo = (r1 - mid.astype(F32)).astype(BF16)
    out = jnp.dot(hi, m, preferred_element_type=F32)
    out = out + jnp.dot(mid, m, preferred_element_type=F32)
    return out + jnp.dot(lo, m, preferred_element_type=F32)


def _layer_norm_rows(y, g, b):
    mu = jnp.mean(y, axis=-1, keepdims=True)
    yc = y - mu
    var = jnp.mean(yc * yc, axis=-1, keepdims=True)
    return yc * lax.rsqrt(var + LN_EPS) * g + b


def _iota2(shape, dim):
    return lax.broadcasted_iota(jnp.int32, shape, dim)


def _head_sum(x, first_head):
    s0 = jnp.sum(jnp.where(first_head, x, 0.0), axis=-1, keepdims=True)
    s1 = jnp.sum(jnp.where(first_head, 0.0, x), axis=-1, keepdims=True)
    return jnp.where(first_head, s0, s1)


def _block_rows(t, block, row_in_block):
    parts = [jnp.broadcast_to(t[b * block + row_in_block:b * block + row_in_block + 1, :], (block, t.shape[1]))
             for b in range(GROUP // block)]
    return jnp.concatenate(parts, axis=0)


def _mm_kernel(x_ref, w_ref, o_ref):
    o_ref[...] = jnp.dot(x_ref[...], w_ref[...], preferred_element_type=F32).astype(o_ref.dtype)


def matmul(x, w, out_dtype, tm, tn):
    m, k = x.shape
    n = w.shape[1]
    return pl.pallas_call(
        _mm_kernel,
        out_shape=jax.ShapeDtypeStruct((m, n), out_dtype),
        grid=(n // tn, m // tm),
        in_specs=[pl.BlockSpec((tm, k), lambda j, i: (i, 0)),
                  pl.BlockSpec((k, tn), lambda j, i: (0, j))],
        out_specs=pl.BlockSpec((tm, tn), lambda j, i: (i, j)),
        compiler_params=_cparams(("parallel", "arbitrary")),
        name="matmul",
    )(x, w)


CHUNK_SHIFT = CHUNK.bit_length() - 1
HEAD_SHIFT = HEAD_DIM.bit_length() - 1


def _scan_masks(reverse):
    ri = _iota2((GROUP, GROUP), 0)
    ci = _iota2((GROUP, GROUP), 1)
    same_chunk = (ri >> CHUNK_SHIFT) == (ci >> CHUNK_SHIFT)
    if reverse:
        return ri, ci, same_chunk, same_chunk & (ri <= ci), same_chunk & (ri < ci)
    return ri, ci, same_chunk, same_chunk & (ri >= ci), same_chunk & (ri > ci)


GATE_ROWS = 16


class _GdnProblem:
    def __init__(self, **kw):
        self.__dict__.update(kw)


def _gdn_group(probs, same_chunk, h0):
    nh0 = jnp.logical_not(h0)
    sr = _iota2((CHUNK, GROUP), 0)
    sc = _iota2((CHUNK, GROUP), 1)
    eye_side = jnp.where(sr == (sc & (CHUNK - 1)), 1.0, 0.0)
    tl = _iota2((LANES, GROUP), 1)
    br = _iota2((LANES, LANES), 0)
    bc = _iota2((LANES, LANES), 1)
    bd128 = (br >> HEAD_SHIFT) == (bc >> HEAD_SHIFT)

    def block_diag(side):
        return jnp.where(same_chunk, jnp.concatenate([side] * CHUNKS_PER_GROUP, axis=0), 0.0).astype(BF16)

    for pr in probs:
        pr.gedge = _block_rows(pr.gc, CHUNK, 0 if pr.reverse else CHUNK - 1)
        pr.exp_g = jnp.exp(pr.gc)
        pr.kb = pr.k * pr.beta_t
        pr.rhs = jnp.concatenate([pr.v * pr.beta_t, pr.kb * pr.exp_g], axis=1).astype(BF16)
        pr.k16 = pr.k.astype(BF16)
        pr.decay, pr.x, pr.pw, pr.bd = [], [], [], []
        for h in range(2):
            decay = jnp.exp(jnp.where(pr.incl, pr.g_col[h] - pr.g_row[h], -1e30))
            kk = _dot_nt(jnp.where(h0 if h == 0 else nh0, pr.kb, 0.0), pr.k16)
            low = jnp.where(pr.strict, kk * decay, 0.0)
            a = -(low[0:CHUNK] + low[CHUNK:2 * CHUNK] + low[2 * CHUNK:3 * CHUNK] + low[3 * CHUNK:4 * CHUNK])
            pr.decay.append(decay)
            pr.x.append(eye_side + a)
            pr.pw.append(a)
            pr.bd.append(block_diag(a))

    for _ in range(5):
        for pr in probs:
            for h in range(2):
                pr.pw[h] = jnp.dot(pr.pw[h].astype(BF16), pr.bd[h], preferred_element_type=F32)
        for pr in probs:
            for h in range(2):
                pr.bd[h] = block_diag(pr.pw[h])
        for pr in probs:
            for h in range(2):
                pr.x[h] = pr.x[h] + jnp.dot(pr.x[h].astype(BF16), pr.bd[h], preferred_element_type=F32)

    for pr in probs:
        sols = [jnp.dot(block_diag(pr.x[h]), pr.rhs, preferred_element_type=F32) for h in range(2)]
        u = jnp.where(h0, sols[0][:, :LANES], sols[1][:, :LANES])
        w = jnp.where(h0, sols[0][:, LANES:], sols[1][:, LANES:])
        wu = jnp.concatenate([w, u], axis=1)
        wu16 = wu.astype(BF16)
        aw = []
        for h in range(2):
            attn = jnp.where(pr.incl, _dot_nt(jnp.where(h0 if h == 0 else nh0, pr.q, 0.0), pr.k16) * pr.decay[h], 0.0)
            aw.append(jnp.dot(attn.astype(BF16), wu16, preferred_element_type=F32))
        pr.qp = (pr.q * pr.exp_g - jnp.where(h0, aw[0][:, :LANES], aw[1][:, :LANES])).astype(BF16)
        pr.op = jnp.where(h0, aw[0][:, LANES:], aw[1][:, LANES:])
        kd_t = (pr.k * jnp.exp(pr.gedge - pr.gc)).T
        pr.kmat, pr.nmat = [], []
        for c in range(CHUNKS_PER_GROUP):
            km = jnp.dot(jnp.where((tl >> CHUNK_SHIFT) == c, kd_t, 0.0), wu, preferred_element_type=F32)
            pr.kmat.append(jnp.where(bd128, km[:, :LANES], 0.0).astype(BF16))
            pr.nmat.append(jnp.where(bd128, km[:, LANES:], 0.0))
        pr.out = [None] * CHUNKS_PER_GROUP

    for t in range(CHUNKS_PER_GROUP):
        for pr in probs:
            c = CHUNKS_PER_GROUP - 1 - t if pr.reverse else t
            rs = slice(c * CHUNK, (c + 1) * CHUNK)
            s16 = pr.state.astype(BF16)
            pr.out[c] = jnp.dot(pr.qp[rs], s16, preferred_element_type=F32) + pr.op[rs]
            cd = jnp.exp(pr.gedge[c * CHUNK:c * CHUNK + 1, :])
            pr.state = pr.state * cd - jnp.dot(pr.kmat[c], s16, preferred_element_type=F32) + pr.nmat[c]
    return [jnp.concatenate(pr.out, axis=0) for pr in probs]


def _gdn_kernel(xf_ref, xfp_ref, xfn_ref, xb_ref, xbp_ref, xbn_ref, cw_ref, gf_ref, gb_ref, prm_ref,
                of_ref, ob_ref, s_scr, c_scr):
    n = pl.program_id(1)
    n_groups = pl.num_programs(1)

    @pl.when(n == 0)
    def _():
        s_scr[...] = jnp.zeros_like(s_scr)

    lane = _iota2((GROUP, LANES), 1)
    h0 = lane < HEAD_DIM
    ri = _iota2((GROUP, GROUP), 0)
    ci = _iota2((GROUP, GROUP), 1)
    same_chunk = (ri >> CHUNK_SHIFT) == (ci >> CHUNK_SHIFT)
    cw = cw_ref[...]
    base = SUBLANES - CONV_K // 2
    pairs = A_HEADS // 2

    dirs = ((n, xf_ref, xfp_ref, xfn_ref, gf_ref), (n_groups - 1 - n, xb_ref, xbp_ref, xbn_ref, gb_ref))

    @pl.when(2 * n < n_groups)
    def _():
        for gi, x_ref, xp_ref, xn_ref, _ in dirs:
            prev = jnp.where(gi > 0, xp_ref[0].astype(F32)[HALO - SUBLANES:], 0.0)
            nxt = jnp.where(gi < n_groups - 1, xn_ref[0].astype(F32)[:SUBLANES], 0.0)
            ext = jnp.concatenate([prev, x_ref[0].astype(F32), nxt], axis=0)
            acc = None
            for t in range(CONV_K):
                shift = (CONV_K // 2 - t) % ext.shape[0]
                tap = (pltpu.roll(ext, shift, 0) if shift else ext)[SUBLANES:SUBLANES + GROUP, :] * cw[t:t + 1, :]
                acc = tap if acc is None else acc + tap
            act = acc * _sigmoid(acc)
            rows = pl.ds(pl.multiple_of(gi * GROUP, GROUP), GROUP)
            for p in range(pairs):
                q_sl = slice(p * LANES, (p + 1) * LANES)
                k_sl = slice(A_WIDTH + p * LANES, A_WIDTH + (p + 1) * LANES)
                q = act[:, q_sl]
                k = act[:, k_sl]
                c_scr[rows, q_sl] = q * lax.rsqrt(_head_sum(q * q, h0) + RMS_EPS) * (HEAD_DIM ** -0.5)
                c_scr[rows, k_sl] = k * lax.rsqrt(_head_sum(k * k, h0) + RMS_EPS)
            c_scr[rows, 2 * A_WIDTH:] = act[:, 2 * A_WIDTH:]

    probs = []
    for d, (gi, _, _, _, g_ref) in enumerate(dirs):
        reverse = d == 1
        lower, upper = same_chunk & (ri >= ci), same_chunk & (ri <= ci)
        incl = upper if reverse else lower
        strict = same_chunk & ((ri < ci) if reverse else (ri > ci))
        qkv = c_scr[pl.ds(pl.multiple_of(gi * GROUP, GROUP), GROUP), :]

        gt_t = g_ref[0].T[0:GATE_ROWS, :]
        prm = prm_ref[d]
        e_rows = jnp.where(prm[:, 2:3] > 0.5,
                           -jnp.exp(prm[:, 0:1]) * _softplus(gt_t + prm[:, 1:2]), _sigmoid(gt_t))
        g_rows = _dot_sel_right(e_rows, lower if reverse else upper)
        slab = jnp.concatenate([e_rows, g_rows, jnp.zeros((LANES - 2 * GATE_ROWS, GROUP), F32)], axis=0)
        cols = slab.T
        for p in range(pairs):
            sl = lambda part: slice(part * A_WIDTH + p * LANES, part * A_WIDTH + (p + 1) * LANES)
            col = lambda r: cols[:, r:r + 1]
            probs.append(_GdnProblem(
                reverse=reverse, incl=incl, strict=strict, slot=d * pairs + p,
                q=qkv[:, sl(0)], k=qkv[:, sl(1)], v=qkv[:, sl(2)],
                beta_t=jnp.where(h0, col(4 * p), col(4 * p + 1)),
                gc=jnp.where(h0, col(GATE_ROWS + 4 * p + 2), col(GATE_ROWS + 4 * p + 3)),
                g_col=(col(GATE_ROWS + 4 * p + 2), col(GATE_ROWS + 4 * p + 3)),
                g_row=(g_rows[4 * p + 2:4 * p + 3, :], g_rows[4 * p + 3:4 * p + 4, :]),
                state=s_scr[d * pairs + p]))

    outs = _gdn_group(probs, same_chunk, h0)
    for pr, out in zip(probs, outs):
        o_ref = ob_ref if pr.reverse else of_ref
        p = pr.slot % pairs
        o_ref[0, :, p * LANES:(p + 1) * LANES] = out.astype(BF16)
        s_scr[pr.slot] = pr.state


def gdn_scan(wide3, gates3, conv_w, prm):
    b, seq_len, _ = wide3.shape
    n_groups = seq_len // GROUP
    halo_per_group = GROUP // HALO
    n_halo = seq_len // HALO
    width = 3 * A_WIDTH
    fwd = lambda n: n
    bwd = lambda n: n_groups - 1 - n

    def x_specs(gidx):
        return [pl.BlockSpec((1, GROUP, width), lambda i, n: (i, gidx(n), 0)),
                pl.BlockSpec((1, HALO, width),
                             lambda i, n: (i, jnp.maximum(gidx(n) * halo_per_group - 1, 0), 0)),
                pl.BlockSpec((1, HALO, width),
                             lambda i, n: (i, jnp.minimum((gidx(n) + 1) * halo_per_group, n_halo - 1), 0))]

    out_sds = jax.ShapeDtypeStruct((b, seq_len, A_WIDTH), BF16)
    return pl.pallas_call(
        _gdn_kernel,
        out_shape=(out_sds, out_sds),
        grid=(b, n_groups),
        in_specs=x_specs(fwd) + x_specs(bwd) + [
            pl.BlockSpec((CONV_K, width), lambda i, n: (0, 0)),
            pl.BlockSpec((1, GROUP, LANES), lambda i, n: (i, fwd(n), (COL_GATES0 - D_WIDE) // LANES)),
            pl.BlockSpec((1, GROUP, LANES), lambda i, n: (i, bwd(n), (COL_GATES1 - D_WIDE) // LANES)),
            pl.BlockSpec((2, GATE_ROWS, LANES), lambda i, n: (0, 0, 0))],
        out_specs=(pl.BlockSpec((1, GROUP, A_WIDTH), lambda i, n: (i, fwd(n), 0)),
                   pl.BlockSpec((1, GROUP, A_WIDTH), lambda i, n: (i, bwd(n), 0))),
        scratch_shapes=[pltpu.VMEM((A_HEADS, LANES, LANES), F32), pltpu.VMEM((seq_len, width), F32)],
        compiler_params=_cparams(("parallel", "arbitrary")),
        name="gdn_scan",
    )(wide3, wide3, wide3, wide3, wide3, wide3, conv_w, gates3, gates3, prm)


EXP_CLAMP = 60.0


class _HgrnProblem:
    def __init__(self, **kw):
        self.__dict__.update(kw)


def _hgrn_group(probs):
    lane = _iota2((GROUP, LANES), 1)
    h0 = lane < HEAD_DIM
    nh0 = jnp.logical_not(h0)
    tl = _iota2((LANES, GROUP), 1)
    br = _iota2((LANES, LANES), 0)
    bc = _iota2((LANES, LANES), 1)
    bd128 = (br >> HEAD_SHIFT) == (bc >> HEAD_SHIFT)

    for pr in probs:
        ri, ci, same_chunk, incl, _ = pr.masks
        f = pr.lb + (1.0 - pr.lb) * _sigmoid(pr.f_raw)
        kk = 1.0 - f
        q = pr.q_raw * _sigmoid(pr.q_raw) * (HEAD_DIM ** -0.5)
        cum = _dot_sel(incl, jnp.log(f))
        pr.cedge = _block_rows(cum, CHUNK, 0 if pr.reverse else CHUNK - 1)
        pr.qd = q * jnp.exp(cum)
        pr.kd = kk * jnp.exp(pr.cedge - cum)
        half, quarter = CHUNK // 2, CHUNK // 4
        pos_in_chunk = lambda t: t & (CHUNK - 1)
        pos_in_half = lambda t: t & (half - 1)
        same_half = (ri >> (CHUNK_SHIFT - 1)) == (ci >> (CHUNK_SHIFT - 1))
        same_quarter = (ri >> (CHUNK_SHIFT - 2)) == (ci >> (CHUNK_SHIFT - 2))
        if pr.reverse:
            m32 = same_chunk & (pos_in_chunk(ri) < half) & (pos_in_chunk(ci) >= half)
            m16 = same_half & (pos_in_half(ri) < quarter) & (pos_in_half(ci) >= quarter)
            mdg = same_quarter & (ri <= ci)
            ref32 = _block_rows(cum, CHUNK, half)
            ref16 = _block_rows(cum, half, quarter)
            refdg = _block_rows(cum, quarter, quarter // 2)
        else:
            m32 = same_chunk & (pos_in_chunk(ri) >= half) & (pos_in_chunk(ci) < half)
            m16 = same_half & (pos_in_half(ri) >= quarter) & (pos_in_half(ci) < quarter)
            mdg = same_quarter & (ri >= ci)
            ref32 = _block_rows(cum, CHUNK, half - 1)
            ref16 = _block_rows(cum, half, quarter - 1)
            refdg = _block_rows(cum, quarter, quarter // 2 - 1)
        pr.level_masks = (m32, m16, mdg)
        pr.qs = (q * jnp.exp(jnp.minimum(cum - ref32, 0.0)),
                 q * jnp.exp(jnp.minimum(cum - ref16, 0.0)),
                 q * jnp.exp(jnp.clip(cum - refdg, -EXP_CLAMP, EXP_CLAMP)))
        pr.ks = ((kk * jnp.exp(jnp.minimum(ref32 - cum, 0.0))).astype(BF16),
                 (kk * jnp.exp(jnp.minimum(ref16 - cum, 0.0))).astype(BF16),
                 (kk * jnp.exp(jnp.clip(refdg - cum, -EXP_CLAMP, EXP_CLAMP))).astype(BF16))

    for pr in probs:
        v16 = pr.v.astype(BF16)
        m32, m16, mdg = pr.level_masks
        intra = []
        for h in range(2):
            mh = h0 if h == 0 else nh0
            a = jnp.where(m32, _dot_nt(jnp.where(mh, pr.qs[0], 0.0), pr.ks[0]),
                          jnp.where(m16, _dot_nt(jnp.where(mh, pr.qs[1], 0.0), pr.ks[1]),
                                    jnp.where(mdg, _dot_nt(jnp.where(mh, pr.qs[2], 0.0), pr.ks[2]), 0.0)))
            intra.append(jnp.dot(a.astype(BF16), v16, preferred_element_type=F32))
        pr.intra = jnp.where(h0, intra[0], intra[1])
        v_t = pr.v.T
        pr.upd = [jnp.where(bd128, jnp.dot(jnp.where((tl >> CHUNK_SHIFT) == c, v_t, 0.0), pr.kd,
                                           preferred_element_type=F32), 0.0)
                  for c in range(CHUNKS_PER_GROUP)]
        pr.out = [None] * CHUNKS_PER_GROUP

    for t in range(CHUNKS_PER_GROUP):
        for pr in probs:
            c = CHUNKS_PER_GROUP - 1 - t if pr.reverse else t
            rs = slice(c * CHUNK, (c + 1) * CHUNK)
            pr.out[c] = _dot_nt(pr.qd[rs], pr.state_t) + pr.intra[rs]
            cd = jnp.exp(pr.cedge[c * CHUNK:c * CHUNK + 1, :])
            pr.state_t = pr.state_t * cd + pr.upd[c]
    return [jnp.concatenate(pr.out, axis=0) for pr in probs]


def _hgrn_kernel(ff_ref, fb_ref, if_ref, ib_ref, qf_ref, qb_ref, lb_ref, of_ref, ob_ref, s_scr):
    n = pl.program_id(1)

    @pl.when(n == 0)
    def _():
        s_scr[...] = jnp.zeros_like(s_scr)

    pairs = C_HEADS // 2
    probs = []
    for d, (f_ref, i_ref, q_ref) in enumerate(((ff_ref, if_ref, qf_ref), (fb_ref, ib_ref, qb_ref))):
        masks = _scan_masks(d == 1)
        for p in range(pairs):
            sl = slice(p * LANES, (p + 1) * LANES)
            probs.append(_HgrnProblem(reverse=d == 1, masks=masks, slot=d * pairs + p, lb=lb_ref[:, sl],
                                      f_raw=f_ref[0, :, sl], v=i_ref[0, :, sl].astype(F32),
                                      q_raw=q_ref[0, :, sl].astype(F32), state_t=s_scr[d * pairs + p]))
    outs = _hgrn_group(probs)
    for pr, out in zip(probs, outs):
        o_ref = ob_ref if pr.reverse else of_ref
        p = pr.slot % pairs
        o_ref[0, :, p * LANES:(p + 1) * LANES] = out.astype(BF16)
        s_scr[pr.slot] = pr.state_t


def hgrn_scan(wide3, gates3, lb):
    b, seq_len, _ = wide3.shape
    n_groups = seq_len // GROUP
    fwd = lambda n: n
    bwd = lambda n: n_groups - 1 - n
    spec = lambda gidx, blk: pl.BlockSpec((1, GROUP, C_WIDTH), lambda i, n: (i, gidx(n), blk))
    f_blk = (COL_C_F - D_WIDE) // C_WIDTH
    out_sds = jax.ShapeDtypeStruct((b, seq_len, C_WIDTH), BF16)
    return pl.pallas_call(
        _hgrn_kernel,
        out_shape=(out_sds, out_sds),
        grid=(b, n_groups),
        in_specs=[spec(fwd, f_blk), spec(bwd, f_blk + 1),
                  spec(fwd, COL_C_I // C_WIDTH), spec(bwd, COL_C_I // C_WIDTH),
                  spec(fwd, COL_C_Q // C_WIDTH), spec(bwd, COL_C_Q // C_WIDTH),
                  pl.BlockSpec((1, C_WIDTH), lambda i, n: (0, 0))],
        out_specs=(spec(fwd, 0), spec(bwd, 0)),
        scratch_shapes=[pltpu.VMEM((C_HEADS, LANES, LANES), F32)],
        compiler_params=_cparams(("parallel", "arbitrary")),
        name="hgrn_scan",
    )(gates3, gates3, wide3, wide3, wide3, wide3, lb)


ATT_TQ = 1024


def _norm_rope(x, w, cos, sin_signed, first_half16):
    lane = _iota2(x.shape, 1)
    h0 = lane < HEAD_DIM
    ms = _head_sum(x * x, h0) * (1.0 / HEAD_DIM)
    xn = x * lax.rsqrt(ms + RMS_EPS) * w
    partner = jnp.where(first_half16, pltpu.roll(xn, LANES - 16, 1), pltpu.roll(xn, 16, 1))
    return xn * cos + partner * sin_signed


def _attn_kernel(q_ref, k_ref, v_ref, cosq_ref, sinq_ref, cosk_ref, sink_ref, qw_ref, kw_ref, o_ref,
                 k_scr, v_scr):
    qi = pl.program_id(1)

    @pl.when(qi == 0)
    def _():
        kx = k_ref[0].astype(F32)
        lane_k = _iota2(kx.shape, 1)
        k_scr[...] = _norm_rope(kx, kw_ref[...], cosk_ref[...], sink_ref[...], (lane_k & 31) < 16).astype(BF16)
        v_scr[:, :LANES] = v_ref[0]
        v_scr[:, LANES:] = jnp.ones((v_scr.shape[0], LANES), BF16)

    lane = _iota2((ATT_TQ, LANES), 1)
    h0 = lane < HEAD_DIM
    fh = (lane & 31) < 16
    cos = cosq_ref[...]
    sin = sinq_ref[...]
    kmat = k_scr[...]
    vmat = v_scr[...]
    q_scale = (HEAD_DIM ** -0.5) * math.log2(math.e)
    for j in range(B_Q_HEADS // 2):
        qn = _norm_rope(q_ref[0, :, j * LANES:(j + 1) * LANES].astype(F32), qw_ref[...], cos, sin, fh) * q_scale
        outs = []
        for half in range(2):
            mh = h0 if half == 0 else jnp.logical_not(h0)
            s = _dot_nt(jnp.where(mh, qn, 0.0), kmat)
            m = jnp.max(s, axis=-1, keepdims=True)
            e = jnp.exp2((s - m).astype(BF16))
            pvl = jnp.dot(e, vmat, preferred_element_type=F32)
            outs.append(pvl[:, :LANES] / pvl[:, LANES:LANES + 1])
        o_ref[0, :, j * LANES:(j + 1) * LANES] = jnp.where(h0, outs[0], outs[1]).astype(BF16)


def gqa_attention(proj, cos_t, sin_t, q_w, k_w, seq_len):
    b = proj.shape[0]
    return pl.pallas_call(
        _attn_kernel,
        out_shape=jax.ShapeDtypeStruct((b, seq_len, B_WIDTH), BF16),
        grid=(b, seq_len // ATT_TQ),
        in_specs=[pl.BlockSpec((1, ATT_TQ, B_WIDTH), lambda i, t: (i, t, COL_B_Q // B_WIDTH)),
                  pl.BlockSpec((1, seq_len, LANES), lambda i, t: (i, 0, COL_B_K // LANES)),
                  pl.BlockSpec((1, seq_len, LANES), lambda i, t: (i, 0, COL_B_V // LANES)),
                  pl.BlockSpec((ATT_TQ, LANES), lambda i, t: (t, 0)),
                  pl.BlockSpec((ATT_TQ, LANES), lambda i, t: (t, 0)),
                  pl.BlockSpec((seq_len, LANES), lambda i, t: (0, 0)),
                  pl.BlockSpec((seq_len, LANES), lambda i, t: (0, 0)),
                  pl.BlockSpec((1, LANES), lambda i, t: (0, 0)),
                  pl.BlockSpec((1, LANES), lambda i, t: (0, 0))],
        out_specs=pl.BlockSpec((1, ATT_TQ, B_WIDTH), lambda i, t: (i, t, 0)),
        scratch_shapes=[pltpu.VMEM((seq_len, LANES), BF16), pltpu.VMEM((seq_len, 2 * LANES), BF16)],
        compiler_params=_cparams(("parallel", "arbitrary")),
        name="gqa_attention",
    )(proj, proj, proj, cos_t, sin_t, cos_t, sin_t, q_w, k_w)


def rope_tables(seq_len):
    rows = seq_len // GRID_W
    row = np.repeat(np.arange(rows, dtype=np.float64), GRID_W)
    col = np.tile(np.arange(GRID_W, dtype=np.float64), rows)
    inv_freq = ROPE_THETA ** (-np.arange(0, ROPE_AXIS_DIM, 2, dtype=np.float64) / ROPE_AXIS_DIM)
    ang_r = row[:, None] * inv_freq
    ang_c = col[:, None] * inv_freq
    cos64 = np.concatenate([np.cos(ang_r), np.cos(ang_r), np.cos(ang_c), np.cos(ang_c)], axis=1)
    sin64 = np.concatenate([-np.sin(ang_r), np.sin(ang_r), -np.sin(ang_c), np.sin(ang_c)], axis=1)
    return (jnp.asarray(np.tile(cos64, (1, 2)), dtype=F32), jnp.asarray(np.tile(sin64, (1, 2)), dtype=F32))


MIX_TM = 512


def _mixout_kernel(oaf_ref, oab_ref, z_ref, ob_ref, ocf_ref, ocb_ref, g_ref, x_ref, wa_ref, wc_ref,
                   wo_ref, lg_ref, lb_ref, xo_ref, xb_ref):
    lane = _iota2((MIX_TM, LANES), 1)
    h0 = lane < HEAD_DIM
    parts = []
    for j in range(A_WIDTH // LANES):
        sl = slice(j * LANES, (j + 1) * LANES)
        o = oaf_ref[:, sl].astype(F32) + oab_ref[:, sl].astype(F32)
        ms = _head_sum(o * o, h0) * (1.0 / HEAD_DIM)
        z = z_ref[:, sl].astype(F32)
        parts.append((o * lax.rsqrt(ms + RMS_EPS) * wa_ref[...] * (z * _sigmoid(z))).astype(BF16))
    parts.append(ob_ref[...])
    for j in range(C_WIDTH // LANES):
        sl = slice(j * LANES, (j + 1) * LANES)
        o = ocf_ref[:, sl].astype(F32) + ocb_ref[:, sl].astype(F32)
        ms = _head_sum(o * o, h0) * (1.0 / HEAD_DIM)
        parts.append((o * lax.rsqrt(ms + RMS_EPS) * wc_ref[...]
                      * _sigmoid(g_ref[:, sl].astype(F32))).astype(BF16))
    mixed = jnp.concatenate(parts, axis=1)
    h = jnp.dot(mixed, wo_ref[...], preferred_element_type=F32)
    y = _layer_norm_rows(DN_ALPHA * x_ref[...] + h, lg_ref[...], lb_ref[...])
    xo_ref[...] = y
    xb_ref[...] = y.astype(BF16)


def mixer_output(oa_f, oa_b, proj2, ob, oc_f, oc_b, x2, gdn_w, hgrn_w, w_out, ln_g, ln_b):
    m = x2.shape[0]
    row = lambda w: pl.BlockSpec((MIX_TM, w), lambda i: (i, 0))
    const = lambda r, c: pl.BlockSpec((r, c), lambda i: (0, 0))
    return pl.pallas_call(
        _mixout_kernel,
        out_shape=(jax.ShapeDtypeStruct((m, D_MODEL), F32), jax.ShapeDtypeStruct((m, D_MODEL), BF16)),
        grid=(m // MIX_TM,),
        in_specs=[row(A_WIDTH), row(A_WIDTH),
                  pl.BlockSpec((MIX_TM, A_WIDTH), lambda i: (i, COL_A_Z // A_WIDTH)),
                  row(B_WIDTH), row(C_WIDTH), row(C_WIDTH),
                  pl.BlockSpec((MIX_TM, C_WIDTH), lambda i: (i, COL_C_G // C_WIDTH)),
                  row(D_MODEL), const(1, LANES), const(1, LANES),
                  const(D_MIX, D_MODEL), const(1, D_MODEL), const(1, D_MODEL)],
        out_specs=(row(D_MODEL), row(D_MODEL)),
        compiler_params=_cparams(("parallel",)),
        name="mixer_output",
    )(oa_f, oa_b, proj2, ob, oc_f, oc_b, proj2, x2, gdn_w, hgrn_w, w_out, ln_g, ln_b)


XATT_TM = 512


def _xattn_kernel(xb_ref, x_ref, k_ref, v_ref, wq_ref, wo_ref, lg_ref, lb_ref, xo_ref, xob_ref):
    q = jnp.dot(xb_ref[0], wq_ref[...], preferred_element_type=F32) * (X_HEAD_DIM ** -0.5)
    outs = []
    for h in range(X_HEADS):
        sl = slice(h * X_HEAD_DIM, (h + 1) * X_HEAD_DIM)
        s = _dot_nt(q[:, sl], k_ref[0, :, sl])
        m = jnp.max(s, axis=-1, keepdims=True)
        e = jnp.exp(s - m)
        l = jnp.sum(e, axis=-1, keepdims=True)
        outs.append((jnp.dot(e.astype(BF16), v_ref[0, :, sl], preferred_element_type=F32) / l).astype(BF16))
    o = jnp.concatenate(outs, axis=1)
    c = jnp.dot(o, wo_ref[...], preferred_element_type=F32)
    y = _layer_norm_rows(DN_ALPHA * x_ref[0] + c, lg_ref[...], lb_ref[...])
    xo_ref[0] = y
    xob_ref[0] = y.astype(BF16)


def cross_attention(xb3, x3, k3, v3, wq, wo, ln_g, ln_b):
    b, s, _ = x3.shape
    mem = k3.shape[1]
    row = pl.BlockSpec((1, XATT_TM, D_MODEL), lambda i, t: (i, t, 0))
    const = lambda r, c: pl.BlockSpec((r, c), lambda i, t: (0, 0))
    kv = pl.BlockSpec((1, mem, D_MODEL), lambda i, t: (i, 0, 0))
    return pl.pallas_call(
        _xattn_kernel,
        out_shape=(jax.ShapeDtypeStruct((b, s, D_MODEL), F32), jax.ShapeDtypeStruct((b, s, D_MODEL), BF16)),
        grid=(b, s // XATT_TM),
        in_specs=[row, row, kv, kv, const(D_MODEL, D_MODEL), const(D_MODEL, D_MODEL),
                  const(1, D_MODEL), const(1, D_MODEL)],
        out_specs=(row, row),
        compiler_params=_cparams(("parallel", "parallel")),
        name="cross_attention",
    )(xb3, x3, k3, v3, wq, wo, ln_g, ln_b)


FFN_TM = 512
FFN_TF = 1408


def _ffn_kernel(xb_ref, x_ref, wg_ref, wu_ref, wd_ref, lg_ref, lb_ref, xo_ref, xob_ref):
    xb = xb_ref[...]
    acc = None
    for j in range(D_FF // FFN_TF):
        sl = slice(j * FFN_TF, (j + 1) * FFN_TF)
        g = jnp.dot(xb, wg_ref[:, sl], preferred_element_type=F32)
        u = jnp.dot(xb, wu_ref[:, sl], preferred_element_type=F32)
        h = (g * _sigmoid(g) * u).astype(BF16)
        part = jnp.dot(h, wd_ref[sl, :], preferred_element_type=F32)
        acc = part if acc is None else acc + part
    y = _layer_norm_rows(DN_ALPHA * x_ref[...] + acc, lg_ref[...], lb_ref[...])
    xo_ref[...] = y
    xob_ref[...] = y.astype(BF16)


def dense_ffn(xb2, x2, wg, wu, wd, ln_g, ln_b):
    m = x2.shape[0]
    row = pl.BlockSpec((FFN_TM, D_MODEL), lambda i: (i, 0))
    const = pl.BlockSpec((1, D_MODEL), lambda i: (0, 0))
    resident = lambda r, c: pl.BlockSpec((r, c), lambda i: (0, 0), pipeline_mode=pl.Buffered(1))
    return pl.pallas_call(
        _ffn_kernel,
        out_shape=(jax.ShapeDtypeStruct((m, D_MODEL), F32), jax.ShapeDtypeStruct((m, D_MODEL), BF16)),
        grid=(m // FFN_TM,),
        in_specs=[row, row, resident(D_MODEL, D_FF), resident(D_MODEL, D_FF), resident(D_FF, D_MODEL),
                  const, const],
        out_specs=(row, row),
        compiler_params=_cparams(("parallel",)),
        name="dense_ffn",
    )(xb2, x2, wg, wu, wd, ln_g, ln_b)


MOE_TM = 512
MOE_WIN = 512
MOE_TF = 1792
ROUTER_TM = 512


def _router_kernel(xb_ref, wr_ref, e_ref, w_ref, xp_ref):
    logits = jnp.dot(xb_ref[...], wr_ref[...], preferred_element_type=F32)
    lane = _iota2(logits.shape, 1)
    neg = jnp.float32(-jnp.inf)
    lane_f = lane.astype(F32)
    lg = jnp.where(lane < N_EXPERTS, logits, neg)
    m1 = jnp.max(lg, axis=-1, keepdims=True)
    i1 = jnp.min(jnp.where(lg == m1, lane_f, float(LANES)), axis=-1, keepdims=True)
    lg2 = jnp.where(lane_f == i1, neg, lg)
    m2 = jnp.max(lg2, axis=-1, keepdims=True)
    i2 = jnp.min(jnp.where(lg2 == m2, lane_f, float(LANES)), axis=-1, keepdims=True)
    t = jnp.exp(m2 - m1)
    w1 = 1.0 / (1.0 + t)
    w2 = t / (1.0 + t)
    e_ref[...] = jnp.where(lane == 0, i1, jnp.where(lane == 1, i2, 0.0)).astype(jnp.int32)[:, :N_EXPERTS]
    w_ref[...] = jnp.where(lane == 0, w1, jnp.where(lane == 1, w2, 0.0))[:, :N_EXPERTS]
    _store_packed(xp_ref, xb_ref[...].astype(F32))


def moe_router(xb2, w_router_p):
    m = xb2.shape[0]
    lane_out = pl.BlockSpec((ROUTER_TM, N_EXPERTS), lambda i: (i, 0))
    return pl.pallas_call(
        _router_kernel,
        out_shape=(jax.ShapeDtypeStruct((m, N_EXPERTS), jnp.int32), jax.ShapeDtypeStruct((m, N_EXPERTS), F32),
                   jax.ShapeDtypeStruct((SC_SPLIT, m, LANES), jnp.uint32)),
        grid=(m // ROUTER_TM,),
        in_specs=[pl.BlockSpec((ROUTER_TM, D_MODEL), lambda i: (i, 0)),
                  pl.BlockSpec((D_MODEL, LANES), lambda i: (0, 0))],
        out_specs=(lane_out, lane_out, pl.BlockSpec((SC_SPLIT, ROUTER_TM, LANES), lambda i: (0, i, 0))),
        compiler_params=_cparams(("parallel",)),
        name="moe_router",
    )(xb2, w_router_p)


def _moe_ffn_kernel(be_ref, bv_ref, x_ref, wg_ref, wu_ref, wd_ref, o_ref, acc_ref):
    i = pl.program_id(0)
    j = pl.program_id(1)
    valid = bv_ref[i] != 0

    @pl.when(j == 0)
    def _():
        acc_ref[...] = jnp.zeros_like(acc_ref)

    @pl.when(valid)
    def _():
        xb = _load_packed(x_ref).astype(BF16)
        g = jnp.dot(xb, wg_ref[0], preferred_element_type=F32)
        u = jnp.dot(xb, wu_ref[0], preferred_element_type=F32)
        h = (g * _sigmoid(g) * u).astype(BF16)
        acc_ref[...] += jnp.dot(h, wd_ref[0], preferred_element_type=F32)

    @pl.when(j == pl.num_programs(1) - 1)
    def _():
        _store_packed(o_ref, acc_ref[...])


def moe_expert_ffn(block_e, block_valid, xrows, wg, wu, wd):
    n_rows = xrows.shape[1]
    grid_spec = pltpu.PrefetchScalarGridSpec(
        num_scalar_prefetch=2,
        grid=(n_rows // MOE_TM, D_FF_EXPERT // MOE_TF),
        in_specs=[pl.BlockSpec((SC_SPLIT, MOE_TM, LANES), lambda i, j, be, bv: (0, i, 0)),
                  pl.BlockSpec((1, D_MODEL, MOE_TF), lambda i, j, be, bv: (be[i], 0, j)),
                  pl.BlockSpec((1, D_MODEL, MOE_TF), lambda i, j, be, bv: (be[i], 0, j)),
                  pl.BlockSpec((1, MOE_TF, D_MODEL), lambda i, j, be, bv: (be[i], j, 0))],
        out_specs=pl.BlockSpec((SC_SPLIT, MOE_TM, LANES), lambda i, j, be, bv: (0, i, 0)),
        scratch_shapes=[pltpu.VMEM((MOE_TM, D_MODEL), F32)],
    )
    return pl.pallas_call(
        _moe_ffn_kernel,
        out_shape=jax.ShapeDtypeStruct((SC_SPLIT, n_rows, LANES), jnp.uint32),
        grid_spec=grid_spec,
        compiler_params=_cparams(("arbitrary", "arbitrary")),
        name="moe_expert_ffn",
    )(block_e, block_valid, xrows, wg, wu, wd)


PACKED_WIDTH = D_MODEL // 2
SC_ROWS = 128
SC_SPLIT = 4


def _store_packed(o_ref, y):
    lo = lax.bitcast_convert_type(y[:, :PACKED_WIDTH].astype(BF16).astype(F32), jnp.uint32) >> 16
    hi = lax.bitcast_convert_type(y[:, PACKED_WIDTH:].astype(BF16).astype(F32), jnp.uint32)
    word = hi | lo
    for c in range(SC_SPLIT):
        o_ref[c] = word[:, c * LANES:(c + 1) * LANES]


def _load_packed(ref):
    w = jnp.concatenate([ref[c] for c in range(SC_SPLIT)], axis=1)
    lo = lax.bitcast_convert_type(w << 16, F32)
    hi = lax.bitcast_convert_type(w & jnp.uint32(0xFFFF0000), F32)
    return jnp.concatenate([lo, hi], axis=1)


def sc_gather_rows(data, idx):
    pieces, rows, width = data.shape
    idx = (jnp.arange(pieces, dtype=jnp.int32)[:, None] * rows + idx[None, :]).reshape(-1)
    return _sc_gather(data.reshape(pieces * rows, width), idx).reshape(pieces, -1, width)


def _sc_gather(data, idx):
    n = idx.shape[0]
    d = data.shape[1]
    sc = plsc.get_sparse_core_info()
    mesh = plsc.VectorSubcoreMesh(core_axis_name="core", subcore_axis_name="subcore")
    steps = n // (SC_ROWS * sc.num_cores)

    @pl.kernel(out_type=jax.ShapeDtypeStruct((n, d), data.dtype), mesh=mesh, scratch_types=[])
    def gather_kernel(x_hbm, i_hbm, o_hbm):
        def body(i_vmem, o_vmem):
            pltpu.sync_copy(x_hbm.at[i_vmem.at[0]], o_vmem)

        pltpu.emit_pipeline(
            body,
            grid=(sc.num_cores, steps),
            in_specs=[pl.BlockSpec((1, SC_ROWS), index_map=lambda c, i: (0, c * steps + i))],
            out_specs=[pl.BlockSpec((SC_ROWS, d), index_map=lambda c, i: (c * steps + i, 0))],
            core_axis_name=("core", "subcore"),
            dimension_semantics=(pltpu.PARALLEL, pltpu.PARALLEL),
        )(i_hbm, o_hbm)

    return gather_kernel(data, idx.reshape(1, n))


def sc_scatter_rows(data, idx0, idx1, zero_idx0, zero_idx1, n_out):
    pieces, n, width = data.shape
    off = jnp.arange(pieces, dtype=jnp.int32)[:, None] * n_out
    flat = lambda idx: (off + idx[None, :]).reshape(-1)
    zeros = jnp.zeros((pieces * zero_idx0.shape[0], width), data.dtype)
    out = _sc_scatter(data.reshape(pieces * n, width), flat(idx0), flat(idx1),
                      zeros, flat(zero_idx0), flat(zero_idx1), pieces * n_out)
    return out.reshape(pieces, n_out, width)


def _sc_scatter(data, idx0, idx1, zdata, zidx0, zidx1, n_out):
    d = data.shape[1]
    sc = plsc.get_sparse_core_info()
    mesh = plsc.VectorSubcoreMesh(core_axis_name="core", subcore_axis_name="subcore")

    @pl.kernel(out_type=jax.ShapeDtypeStruct((n_out, d), data.dtype), mesh=mesh, scratch_types=[])
    def scatter_kernel(x_hbm, i0_hbm, i1_hbm, z_hbm, zi0_hbm, zi1_hbm, o_hbm):
        def body(x_vmem, i0_vmem, i1_vmem):
            pltpu.sync_copy(x_vmem, o_hbm.at[i0_vmem.at[0]])
            pltpu.sync_copy(x_vmem, o_hbm.at[i1_vmem.at[0]])

        for src, a, b in ((x_hbm, i0_hbm, i1_hbm), (z_hbm, zi0_hbm, zi1_hbm)):
            steps = src.shape[0] // (SC_ROWS * sc.num_cores)
            idx_spec = pl.BlockSpec((1, SC_ROWS), index_map=lambda c, i, steps=steps: (0, c * steps + i))
            pltpu.emit_pipeline(
                body,
                grid=(sc.num_cores, steps),
                in_specs=[pl.BlockSpec((SC_ROWS, d), index_map=lambda c, i, steps=steps: (c * steps + i, 0)),
                          idx_spec, idx_spec],
                out_specs=[],
                core_axis_name=("core", "subcore"),
                dimension_semantics=(pltpu.PARALLEL, pltpu.PARALLEL),
            )(src, a, b)

    row = lambda idx: idx.reshape(1, -1)
    return scatter_kernel(data, row(idx0), row(idx1), zdata, row(zidx0), row(zidx1))


def _moe_finish_kernel(y0_ref, y1_ref, w_ref, x_ref, lg_ref, lb_ref, xo_ref):
    wt = w_ref[...]
    y = wt[:, 0:1] * _load_packed(y0_ref) + wt[:, 1:2] * _load_packed(y1_ref)
    xo_ref[...] = _layer_norm_rows(DN_ALPHA * x_ref[...] + y, lg_ref[...], lb_ref[...])


def moe_finish(ytok, top_w, x2, ln_g, ln_b):
    m = x2.shape[0]
    n_win = m // MOE_WIN
    packed = lambda off: pl.BlockSpec((SC_SPLIT, MOE_WIN, LANES), lambda i: (0, i + off, 0))
    row = pl.BlockSpec((MOE_WIN, D_MODEL), lambda i: (i, 0))
    const = pl.BlockSpec((1, D_MODEL), lambda i: (0, 0))
    return pl.pallas_call(
        _moe_finish_kernel,
        out_shape=jax.ShapeDtypeStruct((m, D_MODEL), F32),
        grid=(n_win,),
        in_specs=[packed(0), packed(n_win), pl.BlockSpec((MOE_WIN, TOP_K), lambda i: (i, 0)), row, const, const],
        out_specs=row,
        compiler_params=_cparams(("parallel",)),
        name="moe_finish",
    )(ytok, ytok, top_w, x2, ln_g, ln_b)


def moe_layer(xb2, x2, w_router, wg, wu, wd, ln_g, ln_b):
    n = x2.shape[0]
    n_rows = n * TOP_K + N_EXPERTS * MOE_TM
    n_blocks = n_rows // MOE_TM

    wr = jnp.pad(w_router, ((0, 0), (0, LANES - N_EXPERTS))).astype(BF16)
    top_e_t, top_w_t, x_packed = moe_router(xb2, wr)
    top_e = top_e_t[:, :TOP_K]
    top_w = top_w_t[:, :TOP_K]

    experts = jnp.arange(N_EXPERTS, dtype=jnp.int32)
    tok_onehot = ((top_e[:, 0:1] == experts[None, :]) | (top_e[:, 1:2] == experts[None, :])).astype(jnp.int32)
    csum = jnp.cumsum(tok_onehot, axis=0)
    counts = csum[-1]
    rank = csum - tok_onehot
    padded = (counts + MOE_TM - 1) // MOE_TM * MOE_TM
    pad_end = jnp.cumsum(padded)
    pad_start = pad_end - padded
    dest = pad_start[top_e] + jnp.take_along_axis(rank, top_e, axis=1)

    blk_row0 = jnp.arange(n_blocks, dtype=jnp.int32) * MOE_TM
    block_e = jnp.minimum(jnp.sum((pad_end[None, :] <= blk_row0[:, None]).astype(jnp.int32), axis=1),
                          N_EXPERTS - 1).astype(jnp.int32)
    block_valid = (blk_row0 < pad_end[-1]).astype(jnp.int32)

    n_pad = n_rows - n * TOP_K
    seg_lo = jnp.concatenate([pad_start + counts, pad_end[-1:]])
    seg_len = jnp.concatenate([padded - counts, n_rows - pad_end[-1:]])
    seg_end = jnp.cumsum(seg_len)
    k = jnp.arange(n_pad, dtype=jnp.int32)
    seg = jnp.sum((seg_end[None, :] <= k[:, None]).astype(jnp.int32), axis=1)
    pad_rows = (seg_lo[seg] + k - (seg_end - seg_len)[seg]).astype(jnp.int32)

    xrows = sc_scatter_rows(x_packed, dest[:, 0], dest[:, 1], pad_rows[:n_pad // 2], pad_rows[n_pad // 2:], n_rows)
    yrows = moe_expert_ffn(block_e, block_valid, xrows, wg, wu, wd)

    ytok = sc_gather_rows(yrows, dest.T.reshape(-1))
    return moe_finish(ytok, top_w, x2, ln_g, ln_b)


def _w_in_perm():
    zero = 3480
    perm = np.full((D_IN_P,), zero, np.int64)
    o_qkv, o_z, o_beta, o_decay, o_bq, o_bk, o_bv, o_cf, o_ci, o_cq, o_cg = (
        0, 1152, 1536, 1548, 1560, 1944, 2072, 2200, 2712, 2968, 3224)
    perm[COL_A_QKV:COL_A_QKV + 1152] = np.arange(o_qkv, o_qkv + 1152)
    perm[COL_A_Z:COL_A_Z + 384] = np.arange(o_z, o_z + 384)
    for j in range(3):
        for a in range(2):
            dst = COL_B_Q + j * LANES + a * HEAD_DIM
            src = o_bq + (j + 3 * a) * HEAD_DIM
            perm[dst:dst + HEAD_DIM] = np.arange(src, src + HEAD_DIM)
    perm[COL_B_K:COL_B_K + 128] = np.arange(o_bk, o_bk + 128)
    perm[COL_C_G:COL_C_G + 256] = np.arange(o_cg, o_cg + 256)
    perm[COL_B_V:COL_B_V + 128] = np.arange(o_bv, o_bv + 128)
    perm[COL_C_F:COL_C_F + 512] = np.arange(o_cf, o_cf + 512)
    perm[COL_C_I:COL_C_I + 256] = np.arange(o_ci, o_ci + 256)
    perm[COL_C_Q:COL_C_Q + 256] = np.arange(o_cq, o_cq + 256)
    for d, col in enumerate((COL_GATES0, COL_GATES1)):
        for p in range(A_HEADS // 2):
            dst = col + 4 * p
            perm[dst + 0] = o_beta + d * A_HEADS + 2 * p
            perm[dst + 1] = o_beta + d * A_HEADS + 2 * p + 1
            perm[dst + 2] = o_decay + d * A_HEADS + 2 * p
            perm[dst + 3] = o_decay + d * A_HEADS + 2 * p + 1
    return perm


def _take_static(w, perm, axis, zero_index=None):
    pieces, start = [], 0
    for i in range(1, len(perm) + 1):
        is_zero = perm[start] == zero_index
        if i < len(perm) and ((perm[i] == zero_index) if is_zero
                              else (perm[i] == perm[i - 1] + 1 and perm[i] != zero_index)):
            continue
        if is_zero:
            shape = list(w.shape)
            shape[axis] = i - start
            pieces.append(jnp.zeros(shape, w.dtype))
        else:
            pieces.append(lax.slice_in_dim(w, int(perm[start]), int(perm[i - 1]) + 1, axis=axis))
        start = i
    return jnp.concatenate(pieces, axis=axis)


def _w_out_perm():
    perm = np.arange(D_MIX)
    for j in range(3):
        for a in range(2):
            dst = A_WIDTH + j * LANES + a * HEAD_DIM
            src = A_WIDTH + (j + 3 * a) * HEAD_DIM
            perm[dst:dst + HEAD_DIM] = np.arange(src, src + HEAD_DIM)
    return perm


def _gdn_params(a_log, dt_bias):
    rows = np.array([4 * (h // 2) + 2 + h % 2 for h in range(A_HEADS)])
    prm = jnp.zeros((2, GATE_ROWS, LANES), F32)
    prm = prm.at[:, rows, 0].set(a_log.astype(F32))
    prm = prm.at[:, rows, 1].set(dt_bias.astype(F32))
    return prm.at[:, rows, 2].set(1.0)


def kernel(x, mem, w_in, conv_w, gdn_a_log, gdn_dt_bias, gdn_norm_w, q_norm_w, k_norm_w, hgrn_lb_logits,
           hgrn_norm_w, w_out, ln1_g, ln1_b, xq, xk, xv, xo, ln2_g, ln2_b, ffn_wg, ffn_wu, ffn_wd,
           moe_router, moe_wg, moe_wu, moe_wd, ln3_g, ln3_b):
    b, s, d = x.shape
    m = b * s
    cos_t, sin_t = rope_tables(s)
    lb_p = jax.nn.softmax(hgrn_lb_logits.astype(F32), axis=0)
    lb_c = jnp.cumsum(lb_p, axis=0)
    lower_bounds = lb_c - lb_c[0]
    in_perm = _w_in_perm()
    out_perm = _w_out_perm()
    mem_b = mem.astype(BF16).reshape(b * mem.shape[1], d)
    tile2 = lambda w: jnp.tile(w.astype(F32), 2)[None, :]
    row = lambda v: v.astype(F32)[None, :]

    x2 = x.reshape(m, d)
    xb2 = x2.astype(BF16)
    for l in range(DEPTH):
        w_in_p = _take_static(w_in[l].astype(BF16), in_perm, 1, zero_index=w_in.shape[2])
        wide2 = matmul(xb2, w_in_p[:, :D_WIDE], BF16, min(1024, m), 768)
        gates2 = matmul(xb2, w_in_p[:, D_WIDE:], F32, min(1024, m), D_IN_P - D_WIDE)
        wide3 = wide2.reshape(b, s, D_WIDE)
        gates3 = gates2.reshape(b, s, D_IN_P - D_WIDE)

        oa_f, oa_b = gdn_scan(wide3, gates3, conv_w[l], _gdn_params(gdn_a_log[l], gdn_dt_bias[l]))
        ob = gqa_attention(wide3, cos_t, sin_t, tile2(q_norm_w[l]), tile2(k_norm_w[l]), s)
        lb = lower_bounds[l][None, :]
        oc_f, oc_b = hgrn_scan(wide3, gates3, lb)

        w_out_p = _take_static(w_out[l].astype(BF16), out_perm, 0)
        x2, xb2 = mixer_output(oa_f.reshape(m, A_WIDTH), oa_b.reshape(m, A_WIDTH), wide2,
                               ob.reshape(m, B_WIDTH), oc_f.reshape(m, C_WIDTH), oc_b.reshape(m, C_WIDTH),
                               x2, tile2(gdn_norm_w[l]), tile2(hgrn_norm_w[l]), w_out_p,
                               row(ln1_g[l]), row(ln1_b[l]))

        k3 = matmul(mem_b, xk[l].astype(BF16), BF16, 256, 512).reshape(b, -1, d)
        v3 = matmul(mem_b, xv[l].astype(BF16), BF16, 256, 512).reshape(b, -1, d)
        x3, xb3 = cross_attention(xb2.reshape(b, s, d), x2.reshape(b, s, d), k3, v3,
                                  xq[l].astype(BF16), xo[l].astype(BF16), row(ln2_g[l]), row(ln2_b[l]))
        x2, xb2 = x3.reshape(m, d), xb3.reshape(m, d)

        if l % 2 == 0:
            x2, xb2 = dense_ffn(xb2, x2, ffn_wg[l // 2].astype(BF16), ffn_wu[l // 2].astype(BF16),
                                ffn_wd[l // 2].astype(BF16), row(ln3_g[l]), row(ln3_b[l]))
        else:
            x2 = moe_layer(xb2, x2, moe_router[l // 2], moe_wg[l // 2].astype(BF16),
                           moe_wu[l // 2].astype(BF16), moe_wd[l // 2].astype(BF16),
                           row(ln3_g[l]), row(ln3_b[l]))
            xb2 = x2.astype(BF16)
    return x2.reshape(b, s, d)
```
